```python
import jax, jax.numpy as jnp
from jax import lax
import numpy as np

D_MODEL = 1024
BATCH = 8
SEQ = 2048
DEPTH = 1

CHUNK = 64
D_MIX = D_MODEL
A_WIDTH = 512
A_GROUPS = 4
A_GROUP_DIM = A_WIDTH // A_GROUPS
A_BLOCK = 128
B_HEADS = 8
B_HEAD_DIM = 64
B_WIDTH = B_HEADS * B_HEAD_DIM
B_KV_DIM = B_HEAD_DIM
IDX_HEADS = 8
IDX_DIM = 64
TOPK_MAX = 256
Q_BLOCK = 128
NORM_EPS = 1e-6
NEG = -1e30
IN_SIZES = (A_WIDTH, A_WIDTH, A_WIDTH,
            B_WIDTH, B_KV_DIM, B_KV_DIM, B_WIDTH,
            IDX_HEADS * IDX_DIM, IDX_DIM, IDX_HEADS)
D_IN = 3272

kernel_name = "hybrid_sgu_dsa_chunk_causal_block"


def rmsnorm(x, g):
    x32 = x.astype(jnp.float32)
    y = x32 * lax.rsqrt(jnp.mean(x32 * x32, axis=-1, keepdims=True) + NORM_EPS)
    return (y * g.astype(jnp.float32)).astype(x.dtype)


def layernorm(x, g):
    x32 = x.astype(jnp.float32)
    mu = jnp.mean(x32, axis=-1, keepdims=True)
    var = jnp.mean(jnp.square(x32 - mu), axis=-1, keepdims=True)
    y = (x32 - mu) * lax.rsqrt(var + NORM_EPS)
    return (y * g.astype(jnp.float32)).astype(x.dtype)


def split_cols(h, sizes):
    outs, off = [], 0
    for s in sizes:
        outs.append(h[..., off:off + s])
        off += s
    return outs


def spatial_gating(u, v, v_gain, w_s, b_s):
    bsz, seq, _ = u.shape
    nb = seq // A_BLOCK
    u = jax.nn.gelu(u)
    v = jax.nn.gelu(v).reshape(bsz, nb, A_BLOCK, A_GROUPS, A_GROUP_DIM)
    v = layernorm(v, v_gain.reshape(A_GROUPS, A_GROUP_DIM))
    pos = jnp.arange(A_BLOCK)
    mask = (pos[None, :] // CHUNK) <= (pos[:, None] // CHUNK)
    w = jnp.where(mask[None], w_s, jnp.zeros_like(w_s))
    s = jnp.einsum('gij,bnjgc->bnigc', w, v) + b_s.T[None, None, :, :, None]
    return u * s.reshape(bsz, seq, A_WIDTH)


def sparse_attention(q, k, v, qi, ki, wi):
    bsz, seq = q.shape[:2]
    topk = min(TOPK_MAX, seq // 4)
    nb = seq // Q_BLOCK
    key_chunk = jnp.arange(seq) // CHUNK
    slopes = 2.0 ** (-8.0 * jnp.arange(1, B_HEADS + 1, dtype=jnp.float32) / B_HEADS)
    scale = B_HEAD_DIM ** -0.5
    ki32 = ki.astype(jnp.float32)
    gather = jax.vmap(lambda a, i: a[i])

    def to_blocks(a):
        return a.reshape(bsz, nb, Q_BLOCK, *a.shape[2:]).swapaxes(0, 1)

    def block(args):
        qb, qib, wib, qpos = args
        logits = jnp.einsum('bqhd,bsd->bqhs', qib.astype(jnp.float32), ki32)
        idx_score = jnp.einsum('bqhs,bqh->bqs', jax.nn.relu(logits), wib.astype(jnp.float32))
        qchunk = qpos // CHUNK
        admissible = key_chunk[None, :] <= qchunk[:, None]
        idx_score = jnp.where(admissible[None], idx_score, NEG)
        _, sel = lax.top_k(idx_score, topk)
        valid = (sel // CHUNK) <= qchunk[None, :, None]
        k_sel = gather(k, sel)
        v_sel = gather(v, sel)
        scores = jnp.einsum('bqhd,bqkd->bhqk', qb, k_sel).astype(jnp.float32) * scale
        dist = jnp.abs(qpos[None, :, None] - sel).astype(jnp.float32)
        scores = scores - slopes[None, :, None, None] * dist[:, None]
        scores = jnp.where(valid[:, None], scores, NEG)
        p = jax.nn.softmax(scores, axis=-1).astype(v.dtype)
        return jnp.einsum('bhqk,bqkd->bqhd', p, v_sel)

    qpos_blocks = jnp.arange(seq).reshape(nb, Q_BLOCK)
    out = lax.map(block, (to_blocks(q), to_blocks(qi), to_blocks(wi), qpos_blocks))
    return out.swapaxes(0, 1).reshape(bsz, seq, B_HEADS * B_KV_DIM)


def setup_inputs(seed: int = 0) -> dict:
    key = jax.random.key(seed)
    ks = jax.random.split(key, 11)
    f32 = jnp.float32
    return {
        "x": jax.random.normal(ks[0], (BATCH, SEQ, D_MODEL), f32),
        "norm_gain": 1.0 + 0.02 * jax.random.normal(ks[1], (DEPTH, D_MODEL), f32),
        "w_in": jax.random.normal(ks[2], (DEPTH, D_MODEL, D_IN), f32) * D_MODEL ** -0.5,
        "sgu_norm_gain": 1.0 + 0.02 * jax.random.normal(ks[3], (DEPTH, A_WIDTH), f32),
        "sgu_w": jax.random.normal(ks[4], (DEPTH, A_GROUPS, A_BLOCK, A_BLOCK), f32) * A_BLOCK ** -0.5,
        "sgu_b": 1.0 + 0.1 * jax.random.normal(ks[5], (DEPTH, A_GROUPS, A_BLOCK), f32),
        "q_norm_gain": 1.0 + 0.02 * jax.random.normal(ks[6], (DEPTH, B_HEAD_DIM), f32),
        "k_norm_gain": 1.0 + 0.02 * jax.random.normal(ks[7], (DEPTH, B_HEAD_DIM), f32),
        "idx_k_norm_gain": 1.0 + 0.02 * jax.random.normal(ks[8], (DEPTH, IDX_DIM), f32),
        "branch_norm_gain": 1.0 + 0.02 * jax.random.normal(ks[9], (DEPTH, D_MIX), f32),
        "w_out": jax.random.normal(ks[10], (DEPTH, D_MIX, D_MODEL), f32) * D_MIX ** -0.5,
    }


def reference(x, norm_gain, w_in, sgu_norm_gain, sgu_w, sgu_b, q_norm_gain, k_norm_gain,
              idx_k_norm_gain, branch_norm_gain, w_out):
    bsz, seq, _ = x.shape
    idx_w_scale = (IDX_HEADS ** -0.5) * (IDX_DIM ** -0.5)
    for l in range(DEPTH):
        h = rmsnorm(x, norm_gain[l])
        proj = jnp.einsum('bsd,de->bse', h, w_in[l])
        a_u, a_v, a_z, b_q, b_k, b_v, b_z, i_q, i_k, i_w = split_cols(proj, IN_SIZES)
        ya = spatial_gating(a_u, a_v, sgu_norm_gain[l], sgu_w[l], sgu_b[l])
        q = rmsnorm(b_q.reshape(bsz, seq, B_HEADS, B_HEAD_DIM), q_norm_gain[l])
        k = rmsnorm(b_k, k_norm_gain[l])
        qi = i_q.reshape(bsz, seq, IDX_HEADS, IDX_DIM)
        ki = layernorm(i_k, idx_k_norm_gain[l])
        wi = i_w * idx_w_scale
        yb = sparse_attention(q, k, b_v, qi, ki, wi)
        ga = branch_norm_gain[l, :A_WIDTH].reshape(A_GROUPS, A_GROUP_DIM)
        gb = branch_norm_gain[l, A_WIDTH:].reshape(B_HEADS, B_KV_DIM)
        oa = rmsnorm(ya.reshape(bsz, seq, A_GROUPS, A_GROUP_DIM), ga).reshape(bsz, seq, A_WIDTH)
        ob = rmsnorm(yb.reshape(bsz, seq, B_HEADS, B_KV_DIM), gb).reshape(bsz, seq, B_WIDTH)
        mixed = jnp.concatenate([oa * jax.nn.silu(a_z), ob * jax.nn.silu(b_z)], axis=-1)
        x = x + jnp.einsum('bse,ed->bsd', mixed, w_out[l])
    return x
```

```python
import functools

import jax
import jax.numpy as jnp
from jax import lax
from jax.experimental import pallas as pl
from jax.experimental.pallas import tpu as pltpu

F32 = jnp.float32
BF16 = jnp.bfloat16

D_MODEL = 1024
CHUNK = 64
A_WIDTH = 512
A_GROUPS = 4
A_BLOCK = 128
HEADS = 8
HEAD_DIM = 64
B_WIDTH = HEADS * HEAD_DIM
IDX_HEADS = 8
IDX_DIM = 64
TOPK_MAX = 256
Q_BLOCK = 128
NORM_EPS = 1e-6
NEG = -1e30
LANES = 128
HALF = LANES // 2
PROJ_ROWS = 256
IDX_ROWS = 256
BISECT_STEPS = 32
VMEM_LIMIT = 48 * 1024 * 1024

_OFF_U, _OFF_V, _OFF_Z, _OFF_Q, _OFF_G, _OFF_QI, _OFF_K = 0, 512, 1024, 1536, 2048, 2560, 3072
_PACKED_COLS = 3200

_NT = (((1,), (1,)), ((), ()))


def _gelu(x):
    c = 0.7978845608028654
    return 0.5 * x * (1.0 + jnp.tanh(c * (x + 0.044715 * (x * x * x))))


def _silu(x):
    return x / (1.0 + jnp.exp(-x))


def _half_mean_sq(x2, lo_half):
    tot = jnp.sum(x2, axis=-1, keepdims=True)
    lo = jnp.sum(jnp.where(lo_half, x2, 0.0), axis=-1, keepdims=True)
    return jnp.where(lo_half, lo, tot - lo) * (1.0 / HALF)


def _proj_kernel(x_ref, ng_ref, w_ref, wt_ref, sgn_ref, sw_ref, sb_ref, qg_ref, kg_ref, ga_ref,
                 ma_ref, gate_ref, q_ref, qsw_ref, qi_ref, kaug_ref, kid_ref, vdt_ref, wit_ref,
                 *, tiles_per_seq, idx_w_scale):
    tm = x_ref.shape[0]
    i = pl.program_id(0)
    x = x_ref[...]
    ms = jnp.mean(x * x, axis=-1, keepdims=True)
    h = (x * lax.rsqrt(ms + NORM_EPS) * ng_ref[...]).astype(BF16)

    lane = lax.broadcasted_iota(jnp.int32, (tm, LANES), 1)
    lo_half = lane < HALF

    def proj(off, width):
        return jnp.dot(h, w_ref[:, off:off + width], preferred_element_type=F32)

    gu = _gelu(proj(_OFF_U, A_WIDTH))
    gv = _gelu(proj(_OFF_V, A_WIDTH))
    pz = proj(_OFF_Z, A_WIDTH)
    r_i = lax.broadcasted_iota(jnp.int32, (A_BLOCK, A_BLOCK), 0)
    c_j = lax.broadcasted_iota(jnp.int32, (A_BLOCK, A_BLOCK), 1)
    causal = lax.shift_right_logical(c_j, 6) <= lax.shift_right_logical(r_i, 6)
    for g in range(A_GROUPS):
        cols = slice(g * LANES, (g + 1) * LANES)
        vg = gv[:, cols]
        mu = jnp.mean(vg, axis=-1, keepdims=True)
        d = vg - mu
        var = jnp.mean(d * d, axis=-1, keepdims=True)
        vn = (d * lax.rsqrt(var + NORM_EPS) * sgn_ref[:, cols]).astype(BF16)
        wg = jnp.where(causal, sw_ref[g], 0.0).astype(BF16)
        for blk in range(tm // A_BLOCK):
            rows = slice(blk * A_BLOCK, (blk + 1) * A_BLOCK)
            s = jnp.dot(wg, vn[rows, :], preferred_element_type=F32) + sb_ref[g]
            ya = gu[rows, cols] * s
            oa = ya * lax.rsqrt(jnp.mean(ya * ya, axis=-1, keepdims=True) + NORM_EPS) * ga_ref[:, cols]
            ma_ref[rows, cols] = (oa * _silu(pz[rows, cols])).astype(BF16)

    pq = proj(_OFF_Q, B_WIDTH)
    for t in range(B_WIDTH // LANES):
        cols = slice(t * LANES, (t + 1) * LANES)
        xq = pq[:, cols]
        msq = _half_mean_sq(xq * xq, lo_half)
        y = xq * lax.rsqrt(msq + NORM_EPS) * qg_ref[:, cols] * (HEAD_DIM ** -0.5)
        q_ref[:, cols] = y.astype(BF16)
        qsw_ref[:, cols] = pltpu.roll(y, HALF, axis=1).astype(BF16)

    gate_ref[...] = _silu(proj(_OFF_G, B_WIDTH)).astype(BF16)
    qi_ref[...] = proj(_OFF_QI, IDX_HEADS * IDX_DIM).astype(BF16)

    pk = proj(_OFF_K, LANES)
    k_ms = jnp.sum(jnp.where(lo_half, pk * pk, 0.0), axis=-1, keepdims=True) * (1.0 / HALF)
    kn = pk * lax.rsqrt(k_ms + NORM_EPS)
    ik_mu = jnp.sum(jnp.where(lo_half, 0.0, pk), axis=-1, keepdims=True) * (1.0 / HALF)
    dk = pk - ik_mu
    ik_var = jnp.sum(jnp.where(lo_half, 0.0, dk * dk), axis=-1, keepdims=True) * (1.0 / HALF)
    kin = dk * lax.rsqrt(ik_var + NORM_EPS)
    tile = jnp.where(lo_half, kn, kin) * kg_ref[...]
    swapped = pltpu.roll(tile, HALF, axis=1)
    row = lax.broadcasted_iota(jnp.int32, (tm, LANES), 0)
    pos = (i % tiles_per_seq) * tm + row
    pos_hi = lax.shift_right_logical(pos, 6).astype(F32)
    pos_lo = (pos & (CHUNK - 1)).astype(F32)
    posfeat = jnp.where(lane == HALF, pos_hi, jnp.where(lane == HALF + 1, pos_lo, 0.0))
    kaug_ref[...] = jnp.where(lo_half, tile, posfeat).astype(BF16)
    kid_ref[:, 0:LANES] = jnp.where(lo_half, swapped, 0.0).astype(BF16)
    kid_ref[:, LANES:2 * LANES] = jnp.where(lo_half, 0.0, tile).astype(BF16)

    pt = lax.dot_general(wt_ref[...], h, _NT, preferred_element_type=F32)
    vt = pt[0:HEAD_DIM, :].astype(BF16)
    zeros = jnp.zeros((HEAD_DIM, tm), BF16)
    vdt_ref[0:HEAD_DIM, :] = vt
    vdt_ref[HEAD_DIM:2 * HEAD_DIM, :] = zeros
    vdt_ref[2 * HEAD_DIM:3 * HEAD_DIM, :] = zeros
    vdt_ref[3 * HEAD_DIM:4 * HEAD_DIM, :] = vt
    wit_ref[...] = pt[HEAD_DIM:HEAD_DIM + IDX_HEADS, :] * idx_w_scale


def _attn_kernel(q_ref, qsw_ref, qi_ref, wit_ref, kaug_ref, kid_ref, vdt_ref, ma_ref, gate_ref,
                 x_ref, gb_ref, wo_ref, y_ref, idx_ref, rhs_ref, s_ref, *, n_keys, topk):
    j = pl.program_id(1)
    kf = float(topk)

    @pl.when(j == 0)
    def _():
        rhs_ref[:, 0:LANES] = kaug_ref[...]

    lane_q = lax.broadcasted_iota(jnp.int32, (1, Q_BLOCK), 1)
    n_adm = j * Q_BLOCK + CHUNK + CHUNK * (lane_q >= CHUNK).astype(jnp.int32)
    wi = wit_ref[...]

    def idx_chunk(c, carry):
        mn, mx = carry
        r0 = pl.multiple_of(c * IDX_ROWS, IDX_ROWS)
        rows = pl.ds(r0, IDX_ROWS)
        acc = jnp.zeros((IDX_ROWS, Q_BLOCK), F32)
        for hh in range(IDX_HEADS):
            t, half = hh // 2, hh % 2
            lg = lax.dot_general(kid_ref[rows, half * LANES:(half + 1) * LANES],
                                 qi_ref[:, t * LANES:(t + 1) * LANES], _NT,
                                 preferred_element_type=F32)
            acc = acc + jnp.maximum(lg, 0.0) * wi[hh:hh + 1, :]
        krow = r0 + lax.broadcasted_iota(jnp.int32, (IDX_ROWS, Q_BLOCK), 0)
        adm = krow < n_adm
        idx_ref[rows, :] = jnp.where(adm, acc, -jnp.inf)
        mn = jnp.minimum(mn, jnp.min(jnp.where(adm, acc, jnp.inf), axis=0, keepdims=True))
        mx = jnp.maximum(mx, jnp.max(jnp.where(adm, acc, -jnp.inf), axis=0, keepdims=True))
        return mn, mx

    mn0 = jnp.full((1, Q_BLOCK), jnp.inf, F32)
    mx0 = jnp.full((1, Q_BLOCK), -jnp.inf, F32)
    lo, hi = lax.fori_loop(0, n_keys // IDX_ROWS, idx_chunk, (mn0, mx0))

    def count_ge(th):
        return jnp.sum(jnp.where(idx_ref[0:n_keys, :] >= th, 1.0, 0.0), axis=0, keepdims=True)

    def bisect(_, carry):
        lo, hi = carry
        mid = 0.5 * lo + 0.5 * hi
        ge = count_ge(mid) >= kf
        return jnp.where(ge, mid, lo), jnp.where(ge, hi, mid)

    lo, hi = lax.fori_loop(0, BISECT_STEPS, bisect, (lo, hi))

    sm = idx_ref[0:n_keys, :]
    thr = jnp.min(jnp.where(sm >= lo, sm, jnp.inf), axis=0, keepdims=True)
    c_gt = jnp.sum(jnp.where(sm > thr, 1.0, 0.0), axis=0, keepdims=True)
    c_eq = jnp.sum(jnp.where(sm == thr, 1.0, 0.0), axis=0, keepdims=True)
    need = (c_gt + c_eq) > kf
    rhs_ref[0:n_keys, LANES:2 * LANES] = jnp.where(sm >= thr, 0.0, NEG).astype(BF16)

    @pl.when(jnp.max(jnp.where(need, 1.0, 0.0)) > 0.0)
    def _():
        smt = idx_ref[0:n_keys, :]
        rowf = lax.broadcasted_iota(jnp.int32, (n_keys, Q_BLOCK), 0).astype(F32)
        eq = smt == thr

        def step(_, carry):
            lo_i, hi_i = carry
            mid_i = jnp.floor((lo_i + hi_i) * 0.5)
            c = c_gt + jnp.sum(jnp.where(eq, jnp.where(rowf <= mid_i, 1.0, 0.0), 0.0),
                               axis=0, keepdims=True)
            ok = c >= kf
            return jnp.where(ok, lo_i, mid_i), jnp.where(ok, mid_i, hi_i)

        lo_i = jnp.full((1, Q_BLOCK), -1.0, F32)
        hi_i = jnp.full((1, Q_BLOCK), float(n_keys - 1), F32)
        _, cut = lax.fori_loop(0, max(1, (n_keys - 1).bit_length()) + 1, step, (lo_i, hi_i))
        cut = jnp.where(need, cut, float(n_keys))
        bias = jnp.where(eq, jnp.where(rowf <= cut, 0.0, NEG), jnp.where(smt > thr, 0.0, NEG))
        rhs_ref[0:n_keys, LANES:2 * LANES] = bias.astype(BF16)

    r_k = lax.broadcasted_iota(jnp.int32, (Q_BLOCK, Q_BLOCK), 0)
    c_q = lax.broadcasted_iota(jnp.int32, (Q_BLOCK, Q_BLOCK), 1)
    ident = jnp.where(r_k == c_q, 1.0, 0.0).astype(BF16)
    later = 2.0 * jnp.maximum(r_k - c_q, 0).astype(F32)
    lane = lax.broadcasted_iota(jnp.int32, (Q_BLOCK, LANES), 1)
    lo_half = lane < HALF
    diag = pl.ds(pl.multiple_of(j * Q_BLOCK, Q_BLOCK), Q_BLOCK)

    y = x_ref[...] + jnp.dot(ma_ref[...], wo_ref[0:A_WIDTH, :], preferred_element_type=F32)
    for t in range(HEADS // 2):
        cols = slice(t * LANES, (t + 1) * LANES)
        o_pair = jnp.zeros((LANES, Q_BLOCK), F32)
        for half in range(2):
            slope = 2.0 ** (-(2 * t + half + 1))
            src = q_ref if half == 0 else qsw_ref
            alibi = jnp.where(lane == HALF, CHUNK * slope, jnp.where(lane == HALF + 1, slope, 0.0))
            lhs_q = jnp.where(lo_half, src[:, cols], alibi.astype(BF16))
            lhs = jnp.concatenate([lhs_q, ident], axis=1)
            s_ref[0:n_keys, :] = lax.dot_general(rhs_ref[0:n_keys, :], lhs, _NT,
                                                 preferred_element_type=F32)
            s_ref[diag, :] = s_ref[diag, :] - slope * later
            s = s_ref[0:n_keys, :]
            m = jnp.max(s, axis=0, keepdims=True)
            p = jnp.exp(s - m)
            l = jnp.sum(p, axis=0, keepdims=True)
            vt = vdt_ref[half * LANES:(half + 1) * LANES, 0:n_keys]
            o_pair = o_pair + jnp.dot(vt, p.astype(BF16), preferred_element_type=F32) * (1.0 / l)
        o = o_pair.T
        msq = _half_mean_sq(o * o, lo_half)
        ob = o * lax.rsqrt(msq + NORM_EPS) * gb_ref[:, cols] * gate_ref[:, cols].astype(F32)
        y = y + jnp.dot(ob.astype(BF16), wo_ref[A_WIDTH + t * LANES:A_WIDTH + (t + 1) * LANES, :],
                        preferred_element_type=F32)
    y_ref[...] = y


def kernel(x, norm_gain, w_in, sgu_norm_gain, sgu_w, sgu_b, q_norm_gain, k_norm_gain,
           idx_k_norm_gain, branch_norm_gain, w_out):
    bsz, seq, d_model = x.shape
    assert d_model == D_MODEL and norm_gain.shape[0] == 1
    assert seq % PROJ_ROWS == 0 and seq % Q_BLOCK == 0 and seq % IDX_ROWS == 0
    tokens = bsz * seq
    topk = min(TOPK_MAX, seq // 4)
    idx_w_scale = (IDX_HEADS ** -0.5) * (IDX_DIM ** -0.5)

    w = w_in[0]
    a3 = 3 * A_WIDTH
    w_q = w[:, a3:a3 + B_WIDTH]
    w_k = w[:, a3 + B_WIDTH:a3 + B_WIDTH + HEAD_DIM]
    w_v = w[:, a3 + B_WIDTH + HEAD_DIM:a3 + B_WIDTH + 2 * HEAD_DIM]
    o_g = a3 + B_WIDTH + 2 * HEAD_DIM
    w_g = w[:, o_g:o_g + B_WIDTH]
    o_i = o_g + B_WIDTH
    w_iq = w[:, o_i:o_i + IDX_HEADS * IDX_DIM]
    w_ik = w[:, o_i + IDX_HEADS * IDX_DIM:o_i + IDX_HEADS * IDX_DIM + IDX_DIM]
    w_iw = w[:, o_i + IDX_HEADS * IDX_DIM + IDX_DIM:]
    w_main = jnp.concatenate([w[:, :a3], w_q, w_g, w_iq, w_k, w_ik], axis=1).astype(BF16)
    assert w_main.shape[1] == _PACKED_COLS
    w_t = jnp.concatenate([w_v.T, w_iw.T, jnp.zeros((LANES - HEAD_DIM - IDX_HEADS, D_MODEL), F32)],
                          axis=0).astype(BF16)
    x2 = x.reshape(tokens, D_MODEL)
    ng = norm_gain[0].reshape(1, D_MODEL)
    sgn = sgu_norm_gain[0].reshape(1, A_WIDTH)
    sw = sgu_w[0]
    sb = sgu_b[0].reshape(A_GROUPS, A_BLOCK, 1)
    qg = jnp.tile(q_norm_gain[0], HEADS).reshape(1, B_WIDTH)
    kg = jnp.concatenate([k_norm_gain[0], idx_k_norm_gain[0]]).reshape(1, LANES)
    ga = branch_norm_gain[0, :A_WIDTH].reshape(1, A_WIDTH)
    gb = branch_norm_gain[0, A_WIDTH:].reshape(1, B_WIDTH)
    wo = w_out[0].astype(BF16)

    tm = PROJ_ROWS
    full = lambda shape: pl.BlockSpec(shape, lambda i: (0,) * len(shape))
    rows = lambda width: pl.BlockSpec((tm, width), lambda i: (i, 0))
    outs = pl.pallas_call(
        functools.partial(_proj_kernel, tiles_per_seq=seq // tm, idx_w_scale=idx_w_scale),
        grid=(tokens // tm,),
        in_specs=[rows(D_MODEL), full((1, D_MODEL)), full((D_MODEL, _PACKED_COLS)),
                  full((LANES, D_MODEL)), full((1, A_WIDTH)), full((A_GROUPS, A_BLOCK, A_BLOCK)),
                  full((A_GROUPS, A_BLOCK, 1)), full((1, B_WIDTH)), full((1, LANES)),
                  full((1, A_WIDTH))],
        out_specs=[rows(A_WIDTH), rows(B_WIDTH), rows(B_WIDTH), rows(B_WIDTH), rows(B_WIDTH),
                   rows(LANES), rows(2 * LANES),
                   pl.BlockSpec((2 * LANES, tm), lambda i: (0, i)),
                   pl.BlockSpec((IDX_HEADS, tm), lambda i: (0, i))],
        out_shape=[jax.ShapeDtypeStruct((tokens, A_WIDTH), BF16),
                   jax.ShapeDtypeStruct((tokens, B_WIDTH), BF16),
                   jax.ShapeDtypeStruct((tokens, B_WIDTH), BF16),
                   jax.ShapeDtypeStruct((tokens, B_WIDTH), BF16),
                   jax.ShapeDtypeStruct((tokens, B_WIDTH), BF16),
                   jax.ShapeDtypeStruct((tokens, LANES), BF16),
                   jax.ShapeDtypeStruct((tokens, 2 * LANES), BF16),
                   jax.ShapeDtypeStruct((2 * LANES, tokens), BF16),
                   jax.ShapeDtypeStruct((IDX_HEADS, tokens), F32)],
        compiler_params=pltpu.CompilerParams(dimension_semantics=("arbitrary",),
                                             vmem_limit_bytes=VMEM_LIMIT),
        name="proj_sgu",
    )(x2, ng, w_main, w_t, sgn, sw, sb, qg, kg, ga)
    ma, gate, q, qsw, qi, kaug, kid, vdt, wit = outs

    nb = seq // Q_BLOCK
    qblk = lambda width: pl.BlockSpec((Q_BLOCK, width), lambda b, j: (b * nb + j, 0))
    const = lambda shape: pl.BlockSpec(shape, lambda b, j: (0,) * len(shape))
    y = pl.pallas_call(
        functools.partial(_attn_kernel, n_keys=seq, topk=topk),
        grid=(bsz, nb),
        in_specs=[qblk(B_WIDTH), qblk(B_WIDTH), qblk(B_WIDTH),
                  pl.BlockSpec((IDX_HEADS, Q_BLOCK), lambda b, j: (0, b * nb + j)),
                  pl.BlockSpec((seq, LANES), lambda b, j: (b, 0)),
                  pl.BlockSpec((seq, 2 * LANES), lambda b, j: (b, 0)),
                  pl.BlockSpec((2 * LANES, seq), lambda b, j: (0, b)),
                  qblk(A_WIDTH), qblk(B_WIDTH), qblk(D_MODEL),
                  const((1, B_WIDTH)), const((D_MODEL, D_MODEL))],
        out_specs=qblk(D_MODEL),
        out_shape=jax.ShapeDtypeStruct((tokens, D_MODEL), F32),
        scratch_shapes=[pltpu.VMEM((seq, Q_BLOCK), F32),
                        pltpu.VMEM((seq, 2 * LANES), BF16),
                        pltpu.VMEM((seq, Q_BLOCK), F32)],
        compiler_params=pltpu.CompilerParams(dimension_semantics=("arbitrary", "arbitrary"),
                                             vmem_limit_bytes=VMEM_LIMIT),
        name="dsa_attn_out",
    )(q, qsw, qi, wit, kaug, kid, vdt, ma, gate, x2, gb, wo)
    return y.reshape(bsz, seq, D_MODEL)
```

```python
import functools

import jax
import jax.numpy as jnp
from jax import lax
from jax.experimental import pallas as pl
from jax.experimental.pallas import tpu as pltpu

F32 = jnp.float32
BF16 = jnp.bfloat16

D_MODEL = 1024
CHUNK = 64
A_WIDTH = 512
A_GROUPS = 4
A_BLOCK = 128
HEADS = 8
HEAD_DIM = 64
B_WIDTH = HEADS * HEAD_DIM
IDX_HEADS = 8
IDX_DIM = 64
TOPK_MAX = 256
Q_BLOCK = 128
NORM_EPS = 1e-6
NEG = -1e30
LANES = 128
HALF = LANES // 2
PROJ_ROWS = 256
IDX_ROWS = 256
RED_ROWS = 128
KEY_BUCKET = 512
FAST_TRIPS = 8
SLOW_TRIPS = 70
VMEM_LIMIT = 48 * 1024 * 1024

_OFF_U, _OFF_V, _OFF_Z, _OFF_G, _OFF_K = 0, 512, 1024, 1536, 2048
_PACKED_COLS = 2176
_ROW_Q, _ROW_QI, _ROW_V, _ROW_W, _PACKED_ROWS = 0, 512, 1024, 1088, 1152

_NT = (((1,), (1,)), ((), ()))


def _gelu(x):
    c = 0.7978845608028654
    return 0.5 * x * (1.0 + jnp.tanh(c * (x + 0.044715 * (x * x * x))))


def _silu(x):
    return x / (1.0 + jnp.exp(-x))


def _half_mean_sq(x2, lo_half):
    tot = jnp.sum(x2, axis=-1, keepdims=True)
    lo = jnp.sum(jnp.where(lo_half, x2, 0.0), axis=-1, keepdims=True)
    return jnp.where(lo_half, lo, tot - lo) * (1.0 / HALF)


def _proj_kernel(x_ref, ng_ref, w_ref, wt_ref, sgn_ref, sw_ref, sb_ref, qg_ref, kg_ref, ga_ref,
                 ma_ref, gate_ref, kaug_ref, kk_ref, qt_ref, qit_ref, vt_ref, wit_ref,
                 *, tiles_per_seq, idx_w_scale):
    tm = x_ref.shape[0]
    i = pl.program_id(0)
    x = x_ref[...]
    ms = jnp.mean(x * x, axis=-1, keepdims=True)
    h = (x * lax.rsqrt(ms + NORM_EPS) * ng_ref[...]).astype(BF16)

    lane = lax.broadcasted_iota(jnp.int32, (tm, LANES), 1)
    lo_half = lane < HALF

    def proj(off, width):
        return jnp.dot(h, w_ref[:, off:off + width], preferred_element_type=F32)

    gu = _gelu(proj(_OFF_U, A_WIDTH))
    gv = _gelu(proj(_OFF_V, A_WIDTH))
    pz = proj(_OFF_Z, A_WIDTH)
    r_i = lax.broadcasted_iota(jnp.int32, (A_BLOCK, A_BLOCK), 0)
    c_j = lax.broadcasted_iota(jnp.int32, (A_BLOCK, A_BLOCK), 1)
    causal = lax.shift_right_logical(c_j, 6) <= lax.shift_right_logical(r_i, 6)
    for g in range(A_GROUPS):
        cols = slice(g * LANES, (g + 1) * LANES)
        vg = gv[:, cols]
        mu = jnp.mean(vg, axis=-1, keepdims=True)
        d = vg - mu
        var = jnp.mean(d * d, axis=-1, keepdims=True)
        vn = (d * lax.rsqrt(var + NORM_EPS) * sgn_ref[:, cols]).astype(BF16)
        wg = jnp.where(causal, sw_ref[g], 0.0).astype(BF16)
        for blk in range(tm // A_BLOCK):
            rows = slice(blk * A_BLOCK, (blk + 1) * A_BLOCK)
            s = jnp.dot(wg, vn[rows, :], preferred_element_type=F32) + sb_ref[g]
            ya = gu[rows, cols] * s
            oa = ya * lax.rsqrt(jnp.mean(ya * ya, axis=-1, keepdims=True) + NORM_EPS) * ga_ref[:, cols]
            ma_ref[rows, cols] = (oa * _silu(pz[rows, cols])).astype(BF16)

    gate_ref[...] = _silu(proj(_OFF_G, B_WIDTH)).astype(BF16)

    pk = proj(_OFF_K, LANES)
    k_ms = jnp.sum(jnp.where(lo_half, pk * pk, 0.0), axis=-1, keepdims=True) * (1.0 / HALF)
    kn = pk * lax.rsqrt(k_ms + NORM_EPS)
    ik_mu = jnp.sum(jnp.where(lo_half, 0.0, pk), axis=-1, keepdims=True) * (1.0 / HALF)
    dk = pk - ik_mu
    ik_var = jnp.sum(jnp.where(lo_half, 0.0, dk * dk), axis=-1, keepdims=True) * (1.0 / HALF)
    kin = dk * lax.rsqrt(ik_var + NORM_EPS)
    tile = jnp.where(lo_half, kn, kin) * kg_ref[...]
    swapped = pltpu.roll(tile, HALF, axis=1)
    row = lax.broadcasted_iota(jnp.int32, (tm, LANES), 0)
    pos = (i % tiles_per_seq) * tm + row
    pos_hi = lax.shift_right_logical(pos, 6).astype(F32)
    pos_lo = (pos & (CHUNK - 1)).astype(F32)
    posfeat = jnp.where(lane == HALF, pos_hi, jnp.where(lane == HALF + 1, pos_lo, 0.0))
    kaug_ref[...] = jnp.where(lo_half, tile, posfeat).astype(BF16)
    kk_ref[...] = jnp.where(lo_half, swapped, tile).astype(BF16)

    pt = lax.dot_general(wt_ref[...], h, _NT, preferred_element_type=F32)
    for hh in range(HEADS):
        rows = slice(_ROW_Q + hh * HEAD_DIM, _ROW_Q + (hh + 1) * HEAD_DIM)
        xq = pt[rows, :]
        msq = jnp.mean(xq * xq, axis=0, keepdims=True)
        qt_ref[rows, :] = (xq * lax.rsqrt(msq + NORM_EPS) * qg_ref[...] * (HEAD_DIM ** -0.5)).astype(BF16)
    qit_ref[...] = pt[_ROW_QI:_ROW_QI + IDX_HEADS * IDX_DIM, :].astype(BF16)
    vt_ref[...] = pt[_ROW_V:_ROW_V + HEAD_DIM, :].astype(BF16)
    wit_ref[...] = pt[_ROW_W:_ROW_W + IDX_HEADS, :] * idx_w_scale


def _attn_block(nk, topk, j, qt_ref, qit_ref, wit_ref, kk_ref, vt_ref, ma_ref, gate_ref,
                x_ref, gb_ref, wo_ref, y_ref, idx_ref, rhs_ref, s_ref, p_ref):
    kf = float(topk)
    lane_q = lax.broadcasted_iota(jnp.int32, (1, Q_BLOCK), 1)
    n_adm = j * Q_BLOCK + CHUNK + CHUNK * (lane_q >= CHUNK).astype(jnp.int32)
    wi = wit_ref[...]
    top_rows = lax.broadcasted_iota(jnp.int32, (LANES, Q_BLOCK), 0) < HALF

    wts = []
    for t in range(IDX_HEADS // 2):
        qit = qit_ref[t * LANES:(t + 1) * LANES, :]
        zero = jnp.zeros_like(qit)
        wts.append(jnp.concatenate([jnp.where(top_rows, qit, zero), jnp.where(top_rows, zero, qit)], axis=1))

    def idx_chunk(c, carry):
        mn, mx = carry
        r0 = pl.multiple_of(c * IDX_ROWS, IDX_ROWS)
        rows = pl.ds(r0, IDX_ROWS)
        kk = kk_ref[rows, :]
        acc = jnp.zeros((IDX_ROWS, Q_BLOCK), F32)
        for t in range(IDX_HEADS // 2):
            lg = jnp.dot(kk, wts[t], preferred_element_type=F32)
            acc = acc + jnp.maximum(lg[:, 0:Q_BLOCK], 0.0) * wi[2 * t:2 * t + 1, :]
            acc = acc + jnp.maximum(lg[:, Q_BLOCK:2 * Q_BLOCK], 0.0) * wi[2 * t + 1:2 * t + 2, :]
        krow = r0 + lax.broadcasted_iota(jnp.int32, (IDX_ROWS, Q_BLOCK), 0)
        adm = krow < n_adm
        idx_ref[rows, :] = jnp.where(adm, acc, -jnp.inf)
        mn = jnp.minimum(mn, jnp.min(jnp.where(adm, acc, jnp.inf), axis=0, keepdims=True))
        mx = jnp.maximum(mx, jnp.max(jnp.where(adm, acc, -jnp.inf), axis=0, keepdims=True))
        return mn, mx

    mn0 = jnp.full((1, Q_BLOCK), jnp.inf, F32)
    mx0 = jnp.full((1, Q_BLOCK), -jnp.inf, F32)
    lo, hi = lax.fori_loop(0, nk // IDX_ROWS, idx_chunk, (mn0, mx0))

    def count(pred):
        acc = jnp.zeros((RED_ROWS, Q_BLOCK), F32)
        for r in range(nk // RED_ROWS):
            acc = acc + jnp.where(pred(idx_ref[r * RED_ROWS:(r + 1) * RED_ROWS, :]), 1.0, 0.0)
        return jnp.sum(acc, axis=0, keepdims=True)

    def probe(t, state):
        lo, hi, c_lo, c_hi = state
        c = count(lambda s: s >= t)
        ge = c >= kf
        return (jnp.where(ge, t, lo), jnp.where(ge, hi, t), jnp.where(ge, c, c_lo), jnp.where(ge, c_hi, c))

    def unsettled(c_lo):
        return jnp.max(jnp.where(c_lo != kf, 1.0, 0.0)) > 0.0

    def fast_cond(carry):
        it, _, _, c_lo, _ = carry
        return jnp.logical_and(it < FAST_TRIPS, unsettled(c_lo))

    def fast_body(carry):
        it, lo, hi, c_lo, c_hi = carry
        frac = (c_lo - kf) / jnp.maximum(c_lo - c_hi, 1.0)
        guess = jnp.minimum(jnp.maximum(lo + (hi - lo) * frac, lo), hi)
        state = probe(guess, (lo, hi, c_lo, c_hi))
        state = probe(0.5 * state[0] + 0.5 * state[1], state)
        return (it + 1,) + state

    n_adm_f = n_adm.astype(F32)
    c_lo0 = jnp.where(n_adm_f <= kf, kf, n_adm_f)
    c_hi0 = jnp.zeros((1, Q_BLOCK), F32)
    _, lo, hi, c_lo, _ = lax.while_loop(fast_cond, fast_body, (jnp.int32(0), lo, hi, c_lo0, c_hi0))

    def write_bias(bias_fn):
        for r in range(nk // RED_ROWS):
            rows = slice(r * RED_ROWS, (r + 1) * RED_ROWS)
            rhs_ref[rows, LANES:2 * LANES] = bias_fn(idx_ref[rows, :], r * RED_ROWS).astype(BF16)

    write_bias(lambda s, r0: jnp.where(s >= lo, 0.0, NEG))

    @pl.when(unsettled(c_lo))
    def _():
        def kth(lo):
            acc = jnp.full((RED_ROWS, Q_BLOCK), jnp.inf, F32)
            for r in range(nk // RED_ROWS):
                s = idx_ref[r * RED_ROWS:(r + 1) * RED_ROWS, :]
                acc = jnp.minimum(acc, jnp.where(s >= lo, s, jnp.inf))
            thr = jnp.min(acc, axis=0, keepdims=True)
            return thr, count(lambda s: s > thr)

        def slow_cond(carry):
            it, _, _, _, c_gt = carry
            return jnp.logical_and(it < SLOW_TRIPS, jnp.max(jnp.where(c_gt >= kf, 1.0, 0.0)) > 0.0)

        def slow_body(carry):
            it, lo, hi, _, _ = carry
            for _ in range(4):
                mid = 0.5 * lo + 0.5 * hi
                ge = count(lambda s: s >= mid) >= kf
                lo, hi = jnp.where(ge, mid, lo), jnp.where(ge, hi, mid)
            thr, c_gt = kth(lo)
            return it + 1, lo, hi, thr, c_gt

        thr0, c_gt0 = kth(lo)
        _, _, _, thr, c_gt = lax.while_loop(slow_cond, slow_body, (jnp.int32(0), lo, hi, thr0, c_gt0))
        c_eq = count(lambda s: s == thr)
        need = (c_gt + c_eq) > kf

        def tie_count(cut):
            acc = jnp.zeros((RED_ROWS, Q_BLOCK), F32)
            for r in range(nk // RED_ROWS):
                s = idx_ref[r * RED_ROWS:(r + 1) * RED_ROWS, :]
                rowf = (r * RED_ROWS + lax.broadcasted_iota(jnp.int32, (RED_ROWS, Q_BLOCK), 0)).astype(F32)
                acc = acc + jnp.where(s == thr, jnp.where(rowf <= cut, 1.0, 0.0), 0.0)
            return jnp.sum(acc, axis=0, keepdims=True)

        def tie_step(_, carry):
            lo_i, hi_i = carry
            mid_i = jnp.floor((lo_i + hi_i) * 0.5)
            ok = (c_gt + tie_count(mid_i)) >= kf
            return jnp.where(ok, lo_i, mid_i), jnp.where(ok, mid_i, hi_i)

        lo_i = jnp.full((1, Q_BLOCK), -1.0, F32)
        hi_i = jnp.full((1, Q_BLOCK), float(nk - 1), F32)
        _, cut = lax.fori_loop(0, (nk - 1).bit_length() + 1, tie_step, (lo_i, hi_i))
        cut = jnp.where(need, cut, float(nk))

        def exact_bias(s, r0):
            rowf = (r0 + lax.broadcasted_iota(jnp.int32, (RED_ROWS, Q_BLOCK), 0)).astype(F32)
            return jnp.where(s == thr, jnp.where(rowf <= cut, 0.0, NEG), jnp.where(s > thr, 0.0, NEG))

        write_bias(exact_bias)

    r_k = lax.broadcasted_iota(jnp.int32, (Q_BLOCK, Q_BLOCK), 0)
    c_q = lax.broadcasted_iota(jnp.int32, (Q_BLOCK, Q_BLOCK), 1)
    ident = jnp.where(r_k == c_q, 1.0, 0.0).astype(BF16)
    later = 2.0 * jnp.maximum(r_k - c_q, 0).astype(F32)
    feat = lax.broadcasted_iota(jnp.int32, (HALF, Q_BLOCK), 0)
    lane = lax.broadcasted_iota(jnp.int32, (Q_BLOCK, LANES), 1)
    lo_half = lane < HALF
    diag = pl.ds(pl.multiple_of(j * Q_BLOCK, Q_BLOCK), Q_BLOCK)
    vt = vt_ref[:, 0:nk]

    y = x_ref[...] + jnp.dot(ma_ref[...], wo_ref[0:A_WIDTH, :], preferred_element_type=F32)
    for t in range(HEADS // 2):
        cols = slice(t * LANES, (t + 1) * LANES)
        sbuf = s_ref.at[t % 2]
        pbuf = p_ref.at[t % 2]
        halves = []
        slopes = []
        for half in range(2):
            hh = 2 * t + half
            slope = 2.0 ** (-(hh + 1))
            slopes.append(slope)
            alibi = jnp.where(feat == 0, CHUNK * slope, jnp.where(feat == 1, slope, 0.0)).astype(BF16)
            halves.append(jnp.concatenate([qt_ref[hh * HEAD_DIM:(hh + 1) * HEAD_DIM, :], alibi, ident], axis=0))
        lhs_t = jnp.concatenate(halves, axis=1)
        for r in range(nk // IDX_ROWS):
            rows = slice(r * IDX_ROWS, (r + 1) * IDX_ROWS)
            sbuf[rows, :] = jnp.dot(rhs_ref[rows, :], lhs_t, preferred_element_type=F32)
        corr = jnp.concatenate([slopes[0] * later, slopes[1] * later], axis=1)
        sbuf[diag, :] = sbuf[diag, :] - corr
        m_acc = jnp.full((RED_ROWS, 2 * Q_BLOCK), -jnp.inf, F32)
        for r in range(nk // RED_ROWS):
            m_acc = jnp.maximum(m_acc, sbuf[r * RED_ROWS:(r + 1) * RED_ROWS, :])
        m = jnp.max(m_acc, axis=0, keepdims=True)
        l_acc = jnp.zeros((RED_ROWS, 2 * Q_BLOCK), F32)
        for r in range(nk // RED_ROWS):
            rows = slice(r * RED_ROWS, (r + 1) * RED_ROWS)
            p = jnp.exp(sbuf[rows, :] - m)
            l_acc = l_acc + p
            pbuf[rows, :] = p.astype(BF16)
        l = jnp.sum(l_acc, axis=0, keepdims=True)
        o_t = jnp.dot(vt, pbuf[0:nk, :], preferred_element_type=F32) * (1.0 / l)
        o = jnp.concatenate([o_t[:, 0:Q_BLOCK], o_t[:, Q_BLOCK:2 * Q_BLOCK]], axis=0).T
        msq = _half_mean_sq(o * o, lo_half)
        ob = o * lax.rsqrt(msq + NORM_EPS) * gb_ref[:, cols] * gate_ref[:, cols].astype(F32)
        y = y + jnp.dot(ob.astype(BF16), wo_ref[A_WIDTH + t * LANES:A_WIDTH + (t + 1) * LANES, :],
                        preferred_element_type=F32)
    y_ref[...] = y


def _attn_kernel(qt_ref, qit_ref, wit_ref, kaug_ref, kk_ref, vt_ref, ma_ref, gate_ref, x_ref, gb_ref,
                 wo_ref, y_ref, idx_ref, rhs_ref, s_ref, p_ref, *, seq, topk):
    j = pl.program_id(1)

    @pl.when(j == 0)
    def _():
        rhs_ref[:, 0:LANES] = kaug_ref[...]

    blocks_per_bucket = KEY_BUCKET // Q_BLOCK
    for nk in range(KEY_BUCKET, seq + 1, KEY_BUCKET):
        first = nk // Q_BLOCK - blocks_per_bucket

        @pl.when(jnp.logical_and(j >= first, j < first + blocks_per_bucket))
        def _(nk=nk):
            _attn_block(nk, topk, j, qt_ref, qit_ref, wit_ref, kk_ref, vt_ref, ma_ref, gate_ref,
                        x_ref, gb_ref, wo_ref, y_ref, idx_ref, rhs_ref, s_ref, p_ref)


def kernel(x, norm_gain, w_in, sgu_norm_gain, sgu_w, sgu_b, q_norm_gain, k_norm_gain,
           idx_k_norm_gain, branch_norm_gain, w_out):
    bsz, seq, d_model = x.shape
    assert d_model == D_MODEL and norm_gain.shape[0] == 1
    assert seq % PROJ_ROWS == 0 and seq % KEY_BUCKET == 0
    tokens = bsz * seq
    topk = min(TOPK_MAX, seq // 4)
    idx_w_scale = (IDX_HEADS ** -0.5) * (IDX_DIM ** -0.5)

    w = w_in[0]
    a3 = 3 * A_WIDTH
    w_q = w[:, a3:a3 + B_WIDTH]
    w_k = w[:, a3 + B_WIDTH:a3 + B_WIDTH + HEAD_DIM]
    w_v = w[:, a3 + B_WIDTH + HEAD_DIM:a3 + B_WIDTH + 2 * HEAD_DIM]
    o_g = a3 + B_WIDTH + 2 * HEAD_DIM
    w_g = w[:, o_g:o_g + B_WIDTH]
    o_i = o_g + B_WIDTH
    w_iq = w[:, o_i:o_i + IDX_HEADS * IDX_DIM]
    w_ik = w[:, o_i + IDX_HEADS * IDX_DIM:o_i + IDX_HEADS * IDX_DIM + IDX_DIM]
    w_iw = w[:, o_i + IDX_HEADS * IDX_DIM + IDX_DIM:]
    w_main = jnp.concatenate([w[:, :a3], w_g, w_k, w_ik], axis=1).astype(BF16)
    assert w_main.shape[1] == _PACKED_COLS
    w_t = jnp.concatenate([w_q.T, w_iq.T, w_v.T, w_iw.T,
                           jnp.zeros((_PACKED_ROWS - _ROW_W - IDX_HEADS, D_MODEL), F32)], axis=0).astype(BF16)
    x2 = x.reshape(tokens, D_MODEL)
    ng = norm_gain[0].reshape(1, D_MODEL)
    sgn = sgu_norm_gain[0].reshape(1, A_WIDTH)
    sw = sgu_w[0]
    sb = sgu_b[0].reshape(A_GROUPS, A_BLOCK, 1)
    qg = q_norm_gain[0].reshape(HEAD_DIM, 1)
    kg = jnp.concatenate([k_norm_gain[0], idx_k_norm_gain[0]]).reshape(1, LANES)
    ga = branch_norm_gain[0, :A_WIDTH].reshape(1, A_WIDTH)
    gb = branch_norm_gain[0, A_WIDTH:].reshape(1, B_WIDTH)
    wo = w_out[0].astype(BF16)

    tm = PROJ_ROWS
    full = lambda shape: pl.BlockSpec(shape, lambda i: (0,) * len(shape))
    rows = lambda width: pl.BlockSpec((tm, width), lambda i: (i, 0))
    colsT = lambda height: pl.BlockSpec((height, tm), lambda i: (0, i))
    outs = pl.pallas_call(
        functools.partial(_proj_kernel, tiles_per_seq=seq // tm, idx_w_scale=idx_w_scale),
        grid=(tokens // tm,),
        in_specs=[rows(D_MODEL), full((1, D_MODEL)), full((D_MODEL, _PACKED_COLS)),
                  full((_PACKED_ROWS, D_MODEL)), full((1, A_WIDTH)), full((A_GROUPS, A_BLOCK, A_BLOCK)),
                  full((A_GROUPS, A_BLOCK, 1)), full((HEAD_DIM, 1)), full((1, LANES)),
                  full((1, A_WIDTH))],
        out_specs=[rows(A_WIDTH), rows(B_WIDTH), rows(LANES), rows(LANES),
                   colsT(B_WIDTH), colsT(IDX_HEADS * IDX_DIM), colsT(HEAD_DIM), colsT(IDX_HEADS)],
        out_shape=[jax.ShapeDtypeStruct((tokens, A_WIDTH), BF16),
                   jax.ShapeDtypeStruct((tokens, B_WIDTH), BF16),
                   jax.ShapeDtypeStruct((tokens, LANES), BF16),
                   jax.ShapeDtypeStruct((tokens, LANES), BF16),
                   jax.ShapeDtypeStruct((B_WIDTH, tokens), BF16),
                   jax.ShapeDtypeStruct((IDX_HEADS * IDX_DIM, tokens), BF16),
                   jax.ShapeDtypeStruct((HEAD_DIM, tokens), BF16),
                   jax.ShapeDtypeStruct((IDX_HEADS, tokens), F32)],
        compiler_params=pltpu.CompilerParams(dimension_semantics=("arbitrary",),
                                             vmem_limit_bytes=VMEM_LIMIT),
        name="proj_sgu",
    )(x2, ng, w_main, w_t, sgn, sw, sb, qg, kg, ga)
    ma, gate, kaug, kk, qt, qit, vt, wit = outs

    nb = seq // Q_BLOCK
    qblk = lambda width: pl.BlockSpec((Q_BLOCK, width), lambda b, j: (b * nb + j, 0))
    qblkT = lambda height: pl.BlockSpec((height, Q_BLOCK), lambda b, j: (0, b * nb + j))
    const = lambda shape: pl.BlockSpec(shape, lambda b, j: (0,) * len(shape))
    y = pl.pallas_call(
        functools.partial(_attn_kernel, seq=seq, topk=topk),
        grid=(bsz, nb),
        in_specs=[qblkT(B_WIDTH), qblkT(IDX_HEADS * IDX_DIM), qblkT(IDX_HEADS),
                  pl.BlockSpec((seq, LANES), lambda b, j: (b, 0)),
                  pl.BlockSpec((seq, LANES), lambda b, j: (b, 0)),
                  pl.BlockSpec((HEAD_DIM, seq), lambda b, j: (0, b)),
                  qblk(A_WIDTH), qblk(B_WIDTH), qblk(D_MODEL),
                  const((1, B_WIDTH)), const((D_MODEL, D_MODEL))],
        out_specs=qblk(D_MODEL),
        out_shape=jax.ShapeDtypeStruct((tokens, D_MODEL), F32),
        scratch_shapes=[pltpu.VMEM((seq, Q_BLOCK), F32),
                        pltpu.VMEM((seq, 2 * LANES), BF16),
                        pltpu.VMEM((2, seq, 2 * Q_BLOCK), F32),
                        pltpu.VMEM((2, seq, 2 * Q_BLOCK), BF16)],
        compiler_params=pltpu.CompilerParams(dimension_semantics=("arbitrary", "arbitrary"),
                                             vmem_limit_bytes=VMEM_LIMIT),
        name="dsa_attn_out",
    )(qt, qit, wit, kaug, kk, vt, ma, gate, x2, gb, wo)
    return y.reshape(bsz, seq, D_MODEL)
```

```python
import functools

import jax
import jax.numpy as jnp
from jax import lax
from jax.experimental import pallas as pl
from jax.experimental.pallas import tpu as pltpu

F32 = jnp.float32
BF16 = jnp.bfloat16

D_MODEL = 1024
CHUNK = 64
A_WIDTH = 512
A_GROUPS = 4
A_BLOCK = 128
HEADS = 8
HEAD_DIM = 64
B_WIDTH = HEADS * HEAD_DIM
IDX_HEADS = 8
IDX_DIM = 64
TOPK_MAX = 256
Q_BLOCK = 128
NORM_EPS = 1e-6
NEG = -1e30
LANES = 128
HALF = LANES // 2
PROJ_ROWS = 256
IDX_ROWS = 256
RED_ROWS = 128
KEY_BUCKET = 512
FAST_TRIPS = 24
INTERP_MARGIN = 0.02
SLOW_TRIPS = 70
VMEM_LIMIT = 48 * 1024 * 1024

_OFF_U, _OFF_V, _OFF_Z, _OFF_G, _OFF_K = 0, 512, 1024, 1536, 2048
_PACKED_COLS = 2176
_ROW_Q, _ROW_QI, _ROW_V, _ROW_W, _PACKED_ROWS = 0, 512, 1024, 1088, 1152

_NT = (((1,), (1,)), ((), ()))


def _gelu(x):
    c = 0.7978845608028654
    return 0.5 * x * (1.0 + jnp.tanh(c * (x + 0.044715 * (x * x * x))))


def _silu(x):
    return x / (1.0 + jnp.exp(-x))


def _half_mean_sq(x2, lo_half):
    tot = jnp.sum(x2, axis=-1, keepdims=True)
    lo = jnp.sum(jnp.where(lo_half, x2, 0.0), axis=-1, keepdims=True)
    return jnp.where(lo_half, lo, tot - lo) * (1.0 / HALF)


def _proj_kernel(x_ref, ng_ref, w_ref, wt_ref, sgn_ref, sw_ref, sb_ref, qg_ref, kg_ref, ga_ref,
                 ma_ref, gate_ref, kaug_ref, kk_ref, qt_ref, qit_ref, vt_ref, wit_ref,
                 *, tiles_per_seq, idx_w_scale):
    tm = x_ref.shape[0]
    i = pl.program_id(0)
    x = x_ref[...]
    ms = jnp.mean(x * x, axis=-1, keepdims=True)
    h = (x * lax.rsqrt(ms + NORM_EPS) * ng_ref[...]).astype(BF16)

    lane = lax.broadcasted_iota(jnp.int32, (tm, LANES), 1)
    lo_half = lane < HALF

    def proj(off, width):
        return jnp.dot(h, w_ref[:, off:off + width], preferred_element_type=F32)

    gu = _gelu(proj(_OFF_U, A_WIDTH))
    gv = _gelu(proj(_OFF_V, A_WIDTH))
    pz = proj(_OFF_Z, A_WIDTH)
    r_i = lax.broadcasted_iota(jnp.int32, (A_BLOCK, A_BLOCK), 0)
    c_j = lax.broadcasted_iota(jnp.int32, (A_BLOCK, A_BLOCK), 1)
    causal = lax.shift_right_logical(c_j, 6) <= lax.shift_right_logical(r_i, 6)
    for g in range(A_GROUPS):
        cols = slice(g * LANES, (g + 1) * LANES)
        vg = gv[:, cols]
        mu = jnp.mean(vg, axis=-1, keepdims=True)
        d = vg - mu
        var = jnp.mean(d * d, axis=-1, keepdims=True)
        vn = (d * lax.rsqrt(var + NORM_EPS) * sgn_ref[:, cols]).astype(BF16)
        wg = jnp.where(causal, sw_ref[g], 0.0).astype(BF16)
        for blk in range(tm // A_BLOCK):
            rows = slice(blk * A_BLOCK, (blk + 1) * A_BLOCK)
            s = jnp.dot(wg, vn[rows, :], preferred_element_type=F32) + sb_ref[g]
            ya = gu[rows, cols] * s
            oa = ya * lax.rsqrt(jnp.mean(ya * ya, axis=-1, keepdims=True) + NORM_EPS) * ga_ref[:, cols]
            ma_ref[rows, cols] = (oa * _silu(pz[rows, cols])).astype(BF16)

    gate_ref[...] = _silu(proj(_OFF_G, B_WIDTH)).astype(BF16)

    pk = proj(_OFF_K, LANES)
    k_ms = jnp.sum(jnp.where(lo_half, pk * pk, 0.0), axis=-1, keepdims=True) * (1.0 / HALF)
    kn = pk * lax.rsqrt(k_ms + NORM_EPS)
    ik_mu = jnp.sum(jnp.where(lo_half, 0.0, pk), axis=-1, keepdims=True) * (1.0 / HALF)
    dk = pk - ik_mu
    ik_var = jnp.sum(jnp.where(lo_half, 0.0, dk * dk), axis=-1, keepdims=True) * (1.0 / HALF)
    kin = dk * lax.rsqrt(ik_var + NORM_EPS)
    tile = jnp.where(lo_half, kn, kin) * kg_ref[...]
    swapped = pltpu.roll(tile, HALF, axis=1)
    row = lax.broadcasted_iota(jnp.int32, (tm, LANES), 0)
    pos = (i % tiles_per_seq) * tm + row
    pos_hi = lax.shift_right_logical(pos, 6).astype(F32)
    pos_lo = (pos & (CHUNK - 1)).astype(F32)
    posfeat = jnp.where(lane == HALF, pos_hi, jnp.where(lane == HALF + 1, pos_lo, 0.0))
    kaug_ref[...] = jnp.where(lo_half, tile, posfeat).astype(BF16)
    kk_ref[...] = jnp.where(lo_half, swapped, tile).astype(BF16)

    pt = lax.dot_general(wt_ref[...], h, _NT, preferred_element_type=F32)
    for hh in range(HEADS):
        rows = slice(_ROW_Q + hh * HEAD_DIM, _ROW_Q + (hh + 1) * HEAD_DIM)
        xq = pt[rows, :]
        msq = jnp.mean(xq * xq, axis=0, keepdims=True)
        qt_ref[rows, :] = (xq * lax.rsqrt(msq + NORM_EPS) * qg_ref[...] * (HEAD_DIM ** -0.5)).astype(BF16)
    qit_ref[...] = pt[_ROW_QI:_ROW_QI + IDX_HEADS * IDX_DIM, :].astype(BF16)
    vt_ref[...] = pt[_ROW_V:_ROW_V + HEAD_DIM, :].astype(BF16)
    wit_ref[...] = pt[_ROW_W:_ROW_W + IDX_HEADS, :] * idx_w_scale


def _attn_block(nk, topk, j, qt_ref, qit_ref, wit_ref, kk_ref, vt_ref, ma_ref, gate_ref,
                x_ref, gb_ref, wo_ref, y_ref, idx_ref, rhs_ref, s_ref, p_ref):
    kf = float(topk)
    lane_q = lax.broadcasted_iota(jnp.int32, (1, Q_BLOCK), 1)
    n_adm = j * Q_BLOCK + CHUNK + CHUNK * (lane_q >= CHUNK).astype(jnp.int32)
    wi = wit_ref[...]
    top_rows = lax.broadcasted_iota(jnp.int32, (LANES, Q_BLOCK), 0) < HALF

    wts = []
    for t in range(IDX_HEADS // 2):
        qit = qit_ref[t * LANES:(t + 1) * LANES, :]
        zero = jnp.zeros_like(qit)
        wts.append(jnp.concatenate([jnp.where(top_rows, qit, zero), jnp.where(top_rows, zero, qit)], axis=1))

    def idx_chunk(c, carry):
        mn, mx, tiny = carry
        r0 = pl.multiple_of(c * IDX_ROWS, IDX_ROWS)
        rows = pl.ds(r0, IDX_ROWS)
        kk = kk_ref[rows, :]
        acc = jnp.zeros((IDX_ROWS, Q_BLOCK), F32)
        for t in range(IDX_HEADS // 2):
            lg = jnp.dot(kk, wts[t], preferred_element_type=F32)
            acc = acc + jnp.maximum(lg[:, 0:Q_BLOCK], 0.0) * wi[2 * t:2 * t + 1, :]
            acc = acc + jnp.maximum(lg[:, Q_BLOCK:2 * Q_BLOCK], 0.0) * wi[2 * t + 1:2 * t + 2, :]
        krow = r0 + lax.broadcasted_iota(jnp.int32, (IDX_ROWS, Q_BLOCK), 0)
        adm = krow < n_adm
        idx_ref[rows, :] = jnp.where(adm, acc, -jnp.inf)
        hi_fill = jnp.where(adm, acc, jnp.inf)
        mag = jnp.abs(hi_fill)
        mn = jnp.minimum(mn, jnp.min(hi_fill, axis=0, keepdims=True))
        mx = jnp.maximum(mx, jnp.max(jnp.where(adm, acc, -jnp.inf), axis=0, keepdims=True))
        tiny = jnp.minimum(tiny, jnp.min(jnp.where(mag == 0.0, jnp.inf, mag), axis=0, keepdims=True))
        return mn, mx, tiny

    inf0 = jnp.full((1, Q_BLOCK), jnp.inf, F32)
    lo, hi, tiny = lax.fori_loop(0, nk // IDX_ROWS, idx_chunk, (inf0, -inf0, inf0))

    eps = jnp.where(tiny < jnp.inf, tiny, 1.0) * (0.5 / nk)
    for r in range(nk // RED_ROWS):
        rows = slice(r * RED_ROWS, (r + 1) * RED_ROWS)
        s = idx_ref[rows, :]
        rank = (r * RED_ROWS + 1 + lax.broadcasted_iota(jnp.int32, (RED_ROWS, Q_BLOCK), 0)).astype(F32)
        idx_ref[rows, :] = jnp.where(s == 0.0, -(rank * eps), s)
    lo = jnp.minimum(lo, -0.5 * jnp.where(tiny < jnp.inf, tiny, 1.0))

    def count(pred):
        acc = jnp.zeros((RED_ROWS, Q_BLOCK), F32)
        for r in range(nk // RED_ROWS):
            acc = acc + jnp.where(pred(idx_ref[r * RED_ROWS:(r + 1) * RED_ROWS, :]), 1.0, 0.0)
        return jnp.sum(acc, axis=0, keepdims=True)

    def probe(t, state):
        lo, hi, c_lo, c_hi = state
        c = count(lambda s: s >= t)
        ge = c >= kf
        return (jnp.where(ge, t, lo), jnp.where(ge, hi, t), jnp.where(ge, c, c_lo), jnp.where(ge, c_hi, c))

    def unsettled(c_lo):
        return jnp.max(jnp.where(c_lo != kf, 1.0, 0.0)) > 0.0

    def fast_cond(carry):
        it, _, _, c_lo, _ = carry
        return jnp.logical_and(it < FAST_TRIPS, unsettled(c_lo))

    def fast_body(carry):
        it, lo, hi, c_lo, c_hi = carry
        state = (lo, hi, c_lo, c_hi)
        for _ in range(2):
            lo, hi, c_lo, c_hi = state
            frac = (c_lo - kf) / jnp.maximum(c_lo - c_hi, 1.0)
            frac = jnp.minimum(jnp.maximum(frac, INTERP_MARGIN), 1.0 - INTERP_MARGIN)
            state = probe(lo + (hi - lo) * frac, state)
        return (it + 1,) + state

    n_adm_f = n_adm.astype(F32)
    c_lo0 = jnp.where(n_adm_f <= kf, kf, n_adm_f)
    c_hi0 = jnp.zeros((1, Q_BLOCK), F32)
    _, lo, hi, c_lo, _ = lax.while_loop(fast_cond, fast_body, (jnp.int32(0), lo, hi, c_lo0, c_hi0))

    def write_bias(bias_fn):
        for r in range(nk // RED_ROWS):
            rows = slice(r * RED_ROWS, (r + 1) * RED_ROWS)
            rhs_ref[rows, LANES:2 * LANES] = bias_fn(idx_ref[rows, :], r * RED_ROWS).astype(BF16)

    write_bias(lambda s, r0: jnp.where(s >= lo, 0.0, NEG))

    @pl.when(unsettled(c_lo))
    def _():
        def kth(lo):
            acc = jnp.full((RED_ROWS, Q_BLOCK), jnp.inf, F32)
            for r in range(nk // RED_ROWS):
                s = idx_ref[r * RED_ROWS:(r + 1) * RED_ROWS, :]
                acc = jnp.minimum(acc, jnp.where(s >= lo, s, jnp.inf))
            thr = jnp.min(acc, axis=0, keepdims=True)
            return thr, count(lambda s: s > thr)

        def slow_cond(carry):
            it, _, _, _, c_gt = carry
            return jnp.logical_and(it < SLOW_TRIPS, jnp.max(jnp.where(c_gt >= kf, 1.0, 0.0)) > 0.0)

        def slow_body(carry):
            it, lo, hi, _, _ = carry
            for _ in range(4):
                mid = 0.5 * lo + 0.5 * hi
                ge = count(lambda s: s >= mid) >= kf
                lo, hi = jnp.where(ge, mid, lo), jnp.where(ge, hi, mid)
            thr, c_gt = kth(lo)
            return it + 1, lo, hi, thr, c_gt

        thr0, c_gt0 = kth(lo)
        _, _, _, thr, c_gt = lax.while_loop(slow_cond, slow_body, (jnp.int32(0), lo, hi, thr0, c_gt0))
        c_eq = count(lambda s: s == thr)
        need = (c_gt + c_eq) > kf

        def tie_count(cut):
            acc = jnp.zeros((RED_ROWS, Q_BLOCK), F32)
            for r in range(nk // RED_ROWS):
                s = idx_ref[r * RED_ROWS:(r + 1) * RED_ROWS, :]
                rowf = (r * RED_ROWS + lax.broadcasted_iota(jnp.int32, (RED_ROWS, Q_BLOCK), 0)).astype(F32)
                acc = acc + jnp.where(s == thr, jnp.where(rowf <= cut, 1.0, 0.0), 0.0)
            return jnp.sum(acc, axis=0, keepdims=True)

        def tie_step(_, carry):
            lo_i, hi_i = carry
            mid_i = jnp.floor((lo_i + hi_i) * 0.5)
            ok = (c_gt + tie_count(mid_i)) >= kf
            return jnp.where(ok, lo_i, mid_i), jnp.where(ok, mid_i, hi_i)

        lo_i = jnp.full((1, Q_BLOCK), -1.0, F32)
        hi_i = jnp.full((1, Q_BLOCK), float(nk - 1), F32)
        _, cut = lax.fori_loop(0, (nk - 1).bit_length() + 1, tie_step, (lo_i, hi_i))
        cut = jnp.where(need, cut, float(nk))

        def exact_bias(s, r0):
            rowf = (r0 + lax.broadcasted_iota(jnp.int32, (RED_ROWS, Q_BLOCK), 0)).astype(F32)
            return jnp.where(s == thr, jnp.where(rowf <= cut, 0.0, NEG), jnp.where(s > thr, 0.0, NEG))

        write_bias(exact_bias)

    r_k = lax.broadcasted_iota(jnp.int32, (Q_BLOCK, Q_BLOCK), 0)
    c_q = lax.broadcasted_iota(jnp.int32, (Q_BLOCK, Q_BLOCK), 1)
    ident = jnp.where(r_k == c_q, 1.0, 0.0).astype(BF16)
    later = 2.0 * jnp.maximum(r_k - c_q, 0).astype(F32)
    feat = lax.broadcasted_iota(jnp.int32, (HALF, Q_BLOCK), 0)
    lane = lax.broadcasted_iota(jnp.int32, (Q_BLOCK, LANES), 1)
    lo_half = lane < HALF
    diag = pl.ds(pl.multiple_of(j * Q_BLOCK, Q_BLOCK), Q_BLOCK)
    vt = vt_ref[:, 0:nk]

    y = x_ref[...] + jnp.dot(ma_ref[...], wo_ref[0:A_WIDTH, :], preferred_element_type=F32)
    for t in range(HEADS // 2):
        cols = slice(t * LANES, (t + 1) * LANES)
        sbuf = s_ref.at[t % 2]
        pbuf = p_ref.at[t % 2]
        halves = []
        slopes = []
        for half in range(2):
            hh = 2 * t + half
            slope = 2.0 ** (-(hh + 1))
            slopes.append(slope)
            alibi = jnp.where(feat == 0, CHUNK * slope, jnp.where(feat == 1, slope, 0.0)).astype(BF16)
            halves.append(jnp.concatenate([qt_ref[hh * HEAD_DIM:(hh + 1) * HEAD_DIM, :], alibi, ident], axis=0))
        lhs_t = jnp.concatenate(halves, axis=1)
        for r in range(nk // IDX_ROWS):
            rows = slice(r * IDX_ROWS, (r + 1) * IDX_ROWS)
            sbuf[rows, :] = jnp.dot(rhs_ref[rows, :], lhs_t, preferred_element_type=F32)
        corr = jnp.concatenate([slopes[0] * later, slopes[1] * later], axis=1)
        sbuf[diag, :] = sbuf[diag, :] - corr
        m_acc = jnp.full((RED_ROWS, 2 * Q_BLOCK), -jnp.inf, F32)
        for r in range(nk // RED_ROWS):
            m_acc = jnp.maximum(m_acc, sbuf[r * RED_ROWS:(r + 1) * RED_ROWS, :])
        m = jnp.max(m_acc, axis=0, keepdims=True)
        l_acc = jnp.zeros((RED_ROWS, 2 * Q_BLOCK), F32)
        for r in range(nk // RED_ROWS):
            rows = slice(r * RED_ROWS, (r + 1) * RED_ROWS)
            p = jnp.exp(sbuf[rows, :] - m)
            l_acc = l_acc + p
            pbuf[rows, :] = p.astype(BF16)
        l = jnp.sum(l_acc, axis=0, keepdims=True)
        o_t = jnp.dot(vt, pbuf[0:nk, :], preferred_element_type=F32) * (1.0 / l)
        o = jnp.concatenate([o_t[:, 0:Q_BLOCK], o_t[:, Q_BLOCK:2 * Q_BLOCK]], axis=0).T
        msq = _half_mean_sq(o * o, lo_half)
        ob = o * lax.rsqrt(msq + NORM_EPS) * gb_ref[:, cols] * gate_ref[:, cols].astype(F32)
        y = y + jnp.dot(ob.astype(BF16), wo_ref[A_WIDTH + t * LANES:A_WIDTH + (t + 1) * LANES, :],
                        preferred_element_type=F32)
    y_ref[...] = y


def _attn_kernel(qt_ref, qit_ref, wit_ref, kaug_ref, kk_ref, vt_ref, ma_ref, gate_ref, x_ref, gb_ref,
                 wo_ref, y_ref, idx_ref, rhs_ref, s_ref, p_ref, *, seq, topk):
    j = pl.program_id(1)

    @pl.when(j == 0)
    def _():
        rhs_ref[:, 0:LANES] = kaug_ref[...]

    blocks_per_bucket = KEY_BUCKET // Q_BLOCK
    for nk in range(KEY_BUCKET, seq + 1, KEY_BUCKET):
        first = nk // Q_BLOCK - blocks_per_bucket

        @pl.when(jnp.logical_and(j >= first, j < first + blocks_per_bucket))
        def _(nk=nk):
            _attn_block(nk, topk, j, qt_ref, qit_ref, wit_ref, kk_ref, vt_ref, ma_ref, gate_ref,
                        x_ref, gb_ref, wo_ref, y_ref, idx_ref, rhs_ref, s_ref, p_ref)


def kernel(x, norm_gain, w_in, sgu_norm_gain, sgu_w, sgu_b, q_norm_gain, k_norm_gain,
           idx_k_norm_gain, branch_norm_gain, w_out):
    bsz, seq, d_model = x.shape
    assert d_model == D_MODEL and norm_gain.shape[0] == 1
    assert seq % PROJ_ROWS == 0 and seq % KEY_BUCKET == 0
    tokens = bsz * seq
    topk = min(TOPK_MAX, seq // 4)
    idx_w_scale = (IDX_HEADS ** -0.5) * (IDX_DIM ** -0.5)

    w = w_in[0]
    a3 = 3 * A_WIDTH
    w_q = w[:, a3:a3 + B_WIDTH]
    w_k = w[:, a3 + B_WIDTH:a3 + B_WIDTH + HEAD_DIM]
    w_v = w[:, a3 + B_WIDTH + HEAD_DIM:a3 + B_WIDTH + 2 * HEAD_DIM]
    o_g = a3 + B_WIDTH + 2 * HEAD_DIM
    w_g = w[:, o_g:o_g + B_WIDTH]
    o_i = o_g + B_WIDTH
    w_iq = w[:, o_i:o_i + IDX_HEADS * IDX_DIM]
    w_ik = w[:, o_i + IDX_HEADS * IDX_DIM:o_i + IDX_HEADS * IDX_DIM + IDX_DIM]
    w_iw = w[:, o_i + IDX_HEADS * IDX_DIM + IDX_DIM:]
    w_main = jnp.concatenate([w[:, :a3], w_g, w_k, w_ik], axis=1).astype(BF16)
    assert w_main.shape[1] == _PACKED_COLS
    w_t = jnp.concatenate([w_q.T, w_iq.T, w_v.T, w_iw.T,
                           jnp.zeros((_PACKED_ROWS - _ROW_W - IDX_HEADS, D_MODEL), F32)], axis=0).astype(BF16)
    x2 = x.reshape(tokens, D_MODEL)
    ng = norm_gain[0].reshape(1, D_MODEL)
    sgn = sgu_norm_gain[0].reshape(1, A_WIDTH)
    sw = sgu_w[0]
    sb = sgu_b[0].reshape(A_GROUPS, A_BLOCK, 1)
    qg = q_norm_gain[0].reshape(HEAD_DIM, 1)
    kg = jnp.concatenate([k_norm_gain[0], idx_k_norm_gain[0]]).reshape(1, LANES)
    ga = branch_norm_gain[0, :A_WIDTH].reshape(1, A_WIDTH)
    gb = branch_norm_gain[0, A_WIDTH:].reshape(1, B_WIDTH)
    wo = w_out[0].astype(BF16)

    tm = PROJ_ROWS
    full = lambda shape: pl.BlockSpec(shape, lambda i: (0,) * len(shape))
    rows = lambda width: pl.BlockSpec((tm, width), lambda i: (i, 0))
    colsT = lambda height: pl.BlockSpec((height, tm), lambda i: (0, i))
    outs = pl.pallas_call(
        functools.partial(_proj_kernel, tiles_per_seq=seq // tm, idx_w_scale=idx_w_scale),
        grid=(tokens // tm,),
        in_specs=[rows(D_MODEL), full((1, D_MODEL)), full((D_MODEL, _PACKED_COLS)),
                  full((_PACKED_ROWS, D_MODEL)), full((1, A_WIDTH)), full((A_GROUPS, A_BLOCK, A_BLOCK)),
                  full((A_GROUPS, A_BLOCK, 1)), full((HEAD_DIM, 1)), full((1, LANES)),
                  full((1, A_WIDTH))],
        out_specs=[rows(A_WIDTH), rows(B_WIDTH), rows(LANES), rows(LANES),
                   colsT(B_WIDTH), colsT(IDX_HEADS * IDX_DIM), colsT(HEAD_DIM), colsT(IDX_HEADS)],
        out_shape=[jax.ShapeDtypeStruct((tokens, A_WIDTH), BF16),
                   jax.ShapeDtypeStruct((tokens, B_WIDTH), BF16),
                   jax.ShapeDtypeStruct((tokens, LANES), BF16),
                   jax.ShapeDtypeStruct((tokens, LANES), BF16),
                   jax.ShapeDtypeStruct((B_WIDTH, tokens), BF16),
                   jax.ShapeDtypeStruct((IDX_HEADS * IDX_DIM, tokens), BF16),
                   jax.ShapeDtypeStruct((HEAD_DIM, tokens), BF16),
                   jax.ShapeDtypeStruct((IDX_HEADS, tokens), F32)],
        compiler_params=pltpu.CompilerParams(dimension_semantics=("arbitrary",),
                                             vmem_limit_bytes=VMEM_LIMIT),
        name="proj_sgu",
    )(x2, ng, w_main, w_t, sgn, sw, sb, qg, kg, ga)
    ma, gate, kaug, kk, qt, qit, vt, wit = outs

    nb = seq // Q_BLOCK
    qblk = lambda width: pl.BlockSpec((Q_BLOCK, width), lambda b, j: (b * nb + j, 0))
    qblkT = lambda height: pl.BlockSpec((height, Q_BLOCK), lambda b, j: (0, b * nb + j))
    const = lambda shape: pl.BlockSpec(shape, lambda b, j: (0,) * len(shape))
    y = pl.pallas_call(
        functools.partial(_attn_kernel, seq=seq, topk=topk),
        grid=(bsz, nb),
        in_specs=[qblkT(B_WIDTH), qblkT(IDX_HEADS * IDX_DIM), qblkT(IDX_HEADS),
                  pl.BlockSpec((seq, LANES), lambda b, j: (b, 0)),
                  pl.BlockSpec((seq, LANES), lambda b, j: (b, 0)),
                  pl.BlockSpec((HEAD_DIM, seq), lambda b, j: (0, b)),
                  qblk(A_WIDTH), qblk(B_WIDTH), qblk(D_MODEL),
                  const((1, B_WIDTH)), const((D_MODEL, D_MODEL))],
        out_specs=qblk(D_MODEL),
        out_shape=jax.ShapeDtypeStruct((tokens, D_MODEL), F32),
        scratch_shapes=[pltpu.VMEM((seq, Q_BLOCK), F32),
                        pltpu.VMEM((seq, 2 * LANES), BF16),
                        pltpu.VMEM((2, seq, 2 * Q_BLOCK), F32),
                        pltpu.VMEM((2, seq, 2 * Q_BLOCK), BF16)],
        compiler_params=pltpu.CompilerParams(dimension_semantics=("arbitrary", "arbitrary"),
                                             vmem_limit_bytes=VMEM_LIMIT),
        name="dsa_attn_out",
    )(qt, qit, wit, kaug, kk, vt, ma, gate, x2, gb, wo)
    return y.reshape(bsz, seq, D_MODEL)
```

```python
import functools

import jax
import jax.numpy as jnp
from jax import lax
from jax.experimental import pallas as pl
from jax.experimental.pallas import tpu as pltpu

F32 = jnp.float32
BF16 = jnp.bfloat16

D_MODEL = 1024
CHUNK = 64
A_WIDTH = 512
A_GROUPS = 4
A_BLOCK = 128
HEADS = 8
HEAD_DIM = 64
B_WIDTH = HEADS * HEAD_DIM
IDX_HEADS = 8
IDX_DIM = 64
TOPK_MAX = 256
Q_BLOCK = 128
NORM_EPS = 1e-6
MASK_OFF = 1e32
LANES = 128
HALF = LANES // 2
PROJ_ROWS = 256
DOT_ROWS = 512
RED_ROWS = 128
KEY_BUCKET = 512
FAST_TRIPS = 24
INTERP_MARGIN = 0.02
SLOW_TRIPS = 70
VMEM_LIMIT = 48 * 1024 * 1024

_OFF_U, _OFF_V, _OFF_Z, _OFF_G, _OFF_K = 0, 512, 1024, 1536, 2048
_PACKED_COLS = 2176
_ROW_Q, _ROW_QI, _ROW_V, _ROW_W, _PACKED_ROWS = 0, 512, 1024, 1088, 1152

_NT = (((1,), (1,)), ((), ()))


def _gelu(x):
    c = 0.7978845608028654
    return 0.5 * x * (1.0 + jnp.tanh(c * (x + 0.044715 * (x * x * x))))


def _silu(x):
    return x / (1.0 + jnp.exp(-x))


def _half_mean_sq(x2, lo_half):
    tot = jnp.sum(x2, axis=-1, keepdims=True)
    lo = jnp.sum(jnp.where(lo_half, x2, 0.0), axis=-1, keepdims=True)
    return jnp.where(lo_half, lo, tot - lo) * (1.0 / HALF)


def _proj_kernel(x_ref, ng_ref, w_ref, wt_ref, sgn_ref, sw_ref, sb_ref, qg_ref, kg_ref, ga_ref,
                 ma_ref, gate_ref, kaug_ref, kk_ref, qt_ref, qit_ref, vt_ref, wit_ref,
                 *, tiles_per_seq, idx_w_scale):
    tm = x_ref.shape[0]
    i = pl.program_id(0)
    x = x_ref[...]
    ms = jnp.mean(x * x, axis=-1, keepdims=True)
    h = (x * lax.rsqrt(ms + NORM_EPS) * ng_ref[...]).astype(BF16)

    lane = lax.broadcasted_iota(jnp.int32, (tm, LANES), 1)
    lo_half = lane < HALF

    def proj(off, width):
        return jnp.dot(h, w_ref[:, off:off + width], preferred_element_type=F32)

    gu = _gelu(proj(_OFF_U, A_WIDTH))
    gv = _gelu(proj(_OFF_V, A_WIDTH))
    pz = proj(_OFF_Z, A_WIDTH)
    r_i = lax.broadcasted_iota(jnp.int32, (A_BLOCK, A_BLOCK), 0)
    c_j = lax.broadcasted_iota(jnp.int32, (A_BLOCK, A_BLOCK), 1)
    causal = lax.shift_right_logical(c_j, 6) <= lax.shift_right_logical(r_i, 6)
    for g in range(A_GROUPS):
        cols = slice(g * LANES, (g + 1) * LANES)
        vg = gv[:, cols]
        mu = jnp.mean(vg, axis=-1, keepdims=True)
        d = vg - mu
        var = jnp.mean(d * d, axis=-1, keepdims=True)
        vn = (d * lax.rsqrt(var + NORM_EPS) * sgn_ref[:, cols]).astype(BF16)
        wg = jnp.where(causal, sw_ref[g], 0.0).astype(BF16)
        for blk in range(tm // A_BLOCK):
            rows = slice(blk * A_BLOCK, (blk + 1) * A_BLOCK)
            s = jnp.dot(wg, vn[rows, :], preferred_element_type=F32) + sb_ref[g]
            ya = gu[rows, cols] * s
            oa = ya * lax.rsqrt(jnp.mean(ya * ya, axis=-1, keepdims=True) + NORM_EPS) * ga_ref[:, cols]
            ma_ref[rows, cols] = (oa * _silu(pz[rows, cols])).astype(BF16)

    gate_ref[...] = _silu(proj(_OFF_G, B_WIDTH)).astype(BF16)

    pk = proj(_OFF_K, LANES)
    k_ms = jnp.sum(jnp.where(lo_half, pk * pk, 0.0), axis=-1, keepdims=True) * (1.0 / HALF)
    kn = pk * lax.rsqrt(k_ms + NORM_EPS)
    ik_mu = jnp.sum(jnp.where(lo_half, 0.0, pk), axis=-1, keepdims=True) * (1.0 / HALF)
    dk = pk - ik_mu
    ik_var = jnp.sum(jnp.where(lo_half, 0.0, dk * dk), axis=-1, keepdims=True) * (1.0 / HALF)
    kin = dk * lax.rsqrt(ik_var + NORM_EPS)
    tile = jnp.where(lo_half, kn, kin) * kg_ref[...]
    swapped = pltpu.roll(tile, HALF, axis=1)
    row = lax.broadcasted_iota(jnp.int32, (tm, LANES), 0)
    pos = (i % tiles_per_seq) * tm + row
    pos_hi = lax.shift_right_logical(pos, 6).astype(F32)
    pos_lo = (pos & (CHUNK - 1)).astype(F32)
    posfeat = jnp.where(lane == HALF, pos_hi, jnp.where(lane == HALF + 1, pos_lo, 0.0))
    kaug_ref[...] = jnp.where(lo_half, tile, posfeat).astype(BF16)
    kk_ref[...] = jnp.where(lo_half, swapped, tile).astype(BF16)

    pt = lax.dot_general(wt_ref[...], h, _NT, preferred_element_type=F32)
    for hh in range(HEADS):
        rows = slice(_ROW_Q + hh * HEAD_DIM, _ROW_Q + (hh + 1) * HEAD_DIM)
        xq = pt[rows, :]
        msq = jnp.mean(xq * xq, axis=0, keepdims=True)
        qt_ref[rows, :] = (xq * lax.rsqrt(msq + NORM_EPS) * qg_ref[...] * (HEAD_DIM ** -0.5)).astype(BF16)
    qit_ref[...] = pt[_ROW_QI:_ROW_QI + IDX_HEADS * IDX_DIM, :].astype(BF16)
    vt_ref[...] = pt[_ROW_V:_ROW_V + HEAD_DIM, :].astype(BF16)
    wit_ref[...] = pt[_ROW_W:_ROW_W + IDX_HEADS, :] * idx_w_scale


def _attn_block(nk, topk, j, qt_ref, qit_ref, wit_ref, kk_ref, vt_ref, ma_ref, gate_ref,
                x_ref, gb_ref, wo_ref, y_ref, idx_ref, rhs_ref, s_ref, p_ref):
    kf = float(topk)
    lane_q = lax.broadcasted_iota(jnp.int32, (1, Q_BLOCK), 1)
    n_adm = j * Q_BLOCK + CHUNK + CHUNK * (lane_q >= CHUNK).astype(jnp.int32)
    wi = wit_ref[...]
    top_rows = lax.broadcasted_iota(jnp.int32, (LANES, Q_BLOCK), 0) < HALF

    per_head = []
    for t in range(IDX_HEADS // 2):
        qit = qit_ref[t * LANES:(t + 1) * LANES, :]
        zero = jnp.zeros_like(qit)
        per_head += [jnp.where(top_rows, qit, zero), jnp.where(top_rows, zero, qit)]
    half_heads = IDX_HEADS // 2
    for g in range(2):
        wg = jnp.concatenate(per_head[g * half_heads:(g + 1) * half_heads], axis=1)
        for c in range(nk // DOT_ROWS):
            rows = slice(c * DOT_ROWS, (c + 1) * DOT_ROWS)
            s_ref[g, rows, :] = jnp.dot(kk_ref[rows, :], wg, preferred_element_type=F32)

    def weighted_relu(g, rows):
        acc = None
        for i in range(half_heads):
            hh = g * half_heads + i
            term = jnp.maximum(s_ref[g, rows, i * Q_BLOCK:(i + 1) * Q_BLOCK], 0.0) * wi[hh:hh + 1, :]
            acc = term if acc is None else acc + term
        return acc

    mn_acc = jnp.full((RED_ROWS, Q_BLOCK), jnp.inf, F32)
    mx_acc = jnp.full((RED_ROWS, Q_BLOCK), -jnp.inf, F32)
    tiny_acc = jnp.full((RED_ROWS, Q_BLOCK), jnp.inf, F32)
    for r in range(nk // RED_ROWS):
        rows = slice(r * RED_ROWS, (r + 1) * RED_ROWS)
        acc = weighted_relu(0, rows) + weighted_relu(1, rows)
        krow = r * RED_ROWS + lax.broadcasted_iota(jnp.int32, (RED_ROWS, Q_BLOCK), 0)
        adm = krow < n_adm
        lo_fill = jnp.where(adm, acc, -jnp.inf)
        hi_fill = jnp.where(adm, acc, jnp.inf)
        mag = jnp.abs(hi_fill)
        idx_ref[rows, :] = lo_fill
        mn_acc = jnp.minimum(mn_acc, hi_fill)
        mx_acc = jnp.maximum(mx_acc, lo_fill)
        tiny_acc = jnp.minimum(tiny_acc, jnp.where(mag == 0.0, jnp.inf, mag))
    lo = jnp.min(mn_acc, axis=0, keepdims=True)
    hi = jnp.max(mx_acc, axis=0, keepdims=True)
    tiny = jnp.min(tiny_acc, axis=0, keepdims=True)

    eps = jnp.where(tiny < jnp.inf, tiny, 1.0) * (0.5 / nk)
    for r in range(nk // RED_ROWS):
        rows = slice(r * RED_ROWS, (r + 1) * RED_ROWS)
        s = idx_ref[rows, :]
        rank = (r * RED_ROWS + 1 + lax.broadcasted_iota(jnp.int32, (RED_ROWS, Q_BLOCK), 0)).astype(F32)
        idx_ref[rows, :] = jnp.where(s == 0.0, -(rank * eps), s)
    lo = jnp.minimum(lo, -0.5 * jnp.where(tiny < jnp.inf, tiny, 1.0))

    def count(pred):
        acc = jnp.zeros((RED_ROWS, Q_BLOCK), F32)
        for r in range(nk // RED_ROWS):
            acc = acc + jnp.where(pred(idx_ref[r * RED_ROWS:(r + 1) * RED_ROWS, :]), 1.0, 0.0)
        return jnp.sum(acc, axis=0, keepdims=True)

    def probe(t, state):
        lo, hi, c_lo, c_hi = state
        c = count(lambda s: s >= t)
        ge = c >= kf
        return (jnp.where(ge, t, lo), jnp.where(ge, hi, t), jnp.where(ge, c, c_lo), jnp.where(ge, c_hi, c))

    def unsettled(c_lo):
        return jnp.max(jnp.where(c_lo != kf, 1.0, 0.0)) > 0.0

    def fast_cond(carry):
        it, _, _, c_lo, _ = carry
        return jnp.logical_and(it < FAST_TRIPS, unsettled(c_lo))

    def fast_body(carry):
        it, lo, hi, c_lo, c_hi = carry
        state = (lo, hi, c_lo, c_hi)
        for _ in range(2):
            lo, hi, c_lo, c_hi = state
            frac = (c_lo - kf) / jnp.maximum(c_lo - c_hi, 1.0)
            frac = jnp.minimum(jnp.maximum(frac, INTERP_MARGIN), 1.0 - INTERP_MARGIN)
            state = probe(lo + (hi - lo) * frac, state)
        return (it + 1,) + state

    n_adm_f = n_adm.astype(F32)
    c_lo0 = jnp.where(n_adm_f <= kf, kf, n_adm_f)
    c_hi0 = jnp.zeros((1, Q_BLOCK), F32)
    _, lo, hi, c_lo, _ = lax.while_loop(fast_cond, fast_body, (jnp.int32(0), lo, hi, c_lo0, c_hi0))

    r_k = lax.broadcasted_iota(jnp.int32, (Q_BLOCK, Q_BLOCK), 0)
    c_q = lax.broadcasted_iota(jnp.int32, (Q_BLOCK, Q_BLOCK), 1)
    later = 2.0 * jnp.maximum(r_k - c_q, 0).astype(F32)
    diag = pl.ds(pl.multiple_of(j * Q_BLOCK, Q_BLOCK), Q_BLOCK)

    def write_bias(sel_fn):
        for r in range(nk // RED_ROWS):
            rows = slice(r * RED_ROWS, (r + 1) * RED_ROWS)
            sel = sel_fn(idx_ref[rows, :], r * RED_ROWS + r_k)
            rhs_ref[rows, LANES:2 * LANES] = jnp.where(sel, 0.0, -MASK_OFF).astype(BF16)
        sel = sel_fn(idx_ref[diag, :], j * Q_BLOCK + r_k)
        rhs_ref[diag, LANES:2 * LANES] = jnp.where(sel, -later, -MASK_OFF).astype(BF16)

    write_bias(lambda s, krow: s >= lo)

    @pl.when(unsettled(c_lo))
    def _():
        def kth(lo):
            acc = jnp.full((RED_ROWS, Q_BLOCK), jnp.inf, F32)
            for r in range(nk // RED_ROWS):
                s = idx_ref[r * RED_ROWS:(r + 1) * RED_ROWS, :]
                acc = jnp.minimum(acc, jnp.where(s >= lo, s, jnp.inf))
            thr = jnp.min(acc, axis=0, keepdims=True)
            return thr, count(lambda s: s > thr)

        def slow_cond(carry):
            it, _, _, _, c_gt = carry
            return jnp.logical_and(it < SLOW_TRIPS, jnp.max(jnp.where(c_gt >= kf, 1.0, 0.0)) > 0.0)

        def slow_body(carry):
            it, lo, hi, _, _ = carry
            for _ in range(4):
                mid = 0.5 * lo + 0.5 * hi
                ge = count(lambda s: s >= mid) >= kf
                lo, hi = jnp.where(ge, mid, lo), jnp.where(ge, hi, mid)
            thr, c_gt = kth(lo)
            return it + 1, lo, hi, thr, c_gt

        thr0, c_gt0 = kth(lo)
        _, _, _, thr, c_gt = lax.while_loop(slow_cond, slow_body, (jnp.int32(0), lo, hi, thr0, c_gt0))
        c_eq = count(lambda s: s == thr)
        need = (c_gt + c_eq) > kf

        def tie_count(cut):
            acc = jnp.zeros((RED_ROWS, Q_BLOCK), F32)
            for r in range(nk // RED_ROWS):
                s = idx_ref[r * RED_ROWS:(r + 1) * RED_ROWS, :]
                rowf = (r * RED_ROWS + lax.broadcasted_iota(jnp.int32, (RED_ROWS, Q_BLOCK), 0)).astype(F32)
                acc = acc + jnp.where(s == thr, jnp.where(rowf <= cut, 1.0, 0.0), 0.0)
            return jnp.sum(acc, axis=0, keepdims=True)

        def tie_step(_, carry):
            lo_i, hi_i = carry
            mid_i = jnp.floor((lo_i + hi_i) * 0.5)
            ok = (c_gt + tie_count(mid_i)) >= kf
            return jnp.where(ok, lo_i, mid_i), jnp.where(ok, mid_i, hi_i)

        lo_i = jnp.full((1, Q_BLOCK), -1.0, F32)
        hi_i = jnp.full((1, Q_BLOCK), float(nk - 1), F32)
        _, cut = lax.fori_loop(0, (nk - 1).bit_length() + 1, tie_step, (lo_i, hi_i))
        cut = jnp.where(need, cut, float(nk))

        cut = cut.astype(jnp.int32)
        write_bias(lambda s, krow: jnp.logical_or(s > thr, jnp.logical_and(s == thr, krow <= cut)))

    feat = lax.broadcasted_iota(jnp.int32, (HALF, Q_BLOCK), 0)
    lane = lax.broadcasted_iota(jnp.int32, (Q_BLOCK, LANES), 1)
    lo_half = lane < HALF
    group = HEADS // 2
    width = group * Q_BLOCK

    def score_operand(hh):
        slope = 2.0 ** (-(hh + 1))
        alibi = jnp.where(feat == 0, CHUNK * slope, jnp.where(feat == 1, slope, 0.0)).astype(BF16)
        scaled_ident = jnp.where(r_k == c_q, slope, 0.0).astype(BF16)
        return jnp.concatenate([qt_ref[hh * HEAD_DIM:(hh + 1) * HEAD_DIM, :], alibi, scaled_ident], axis=0)

    def scores(g):
        lhs_t = jnp.concatenate([score_operand(g * group + i) for i in range(group)], axis=1)
        m_acc = jnp.full((RED_ROWS, width), -jnp.inf, F32)
        for c in range(nk // DOT_ROWS):
            blk = jnp.dot(rhs_ref[c * DOT_ROWS:(c + 1) * DOT_ROWS, :], lhs_t, preferred_element_type=F32)
            s_ref[g, c * DOT_ROWS:(c + 1) * DOT_ROWS, :] = blk
            for sub in range(DOT_ROWS // RED_ROWS):
                m_acc = jnp.maximum(m_acc, blk[sub * RED_ROWS:(sub + 1) * RED_ROWS, :])
        return jnp.max(m_acc, axis=0, keepdims=True)

    def probabilities(g, m):
        l_acc = jnp.zeros((RED_ROWS, width), F32)
        for r in range(nk // RED_ROWS):
            rows = slice(r * RED_ROWS, (r + 1) * RED_ROWS)
            p = jnp.exp(s_ref[g, rows, :] - m)
            l_acc = l_acc + p
            p_ref[g, rows, :] = p.astype(BF16)
        return jnp.sum(l_acc, axis=0, keepdims=True)

    def values(g, l):
        return jnp.dot(vt_ref[:, 0:nk], p_ref[g, 0:nk, :], preferred_element_type=F32) * (1.0 / l)

    m0 = scores(0)
    m1 = scores(1)
    y = x_ref[...] + jnp.dot(ma_ref[...], wo_ref[0:A_WIDTH, :], preferred_element_type=F32)
    l0 = probabilities(0, m0)
    o0 = values(0, l0)
    l1 = probabilities(1, m1)
    o1 = values(1, l1)
    for t in range(HEADS // 2):
        cols = slice(t * LANES, (t + 1) * LANES)
        o_g = o0 if t < group // 2 else o1
        a = (2 * t) % group
        o = jnp.concatenate([o_g[:, a * Q_BLOCK:(a + 1) * Q_BLOCK],
                             o_g[:, (a + 1) * Q_BLOCK:(a + 2) * Q_BLOCK]], axis=0).T
        msq = _half_mean_sq(o * o, lo_half)
        ob = o * lax.rsqrt(msq + NORM_EPS) * gb_ref[:, cols] * gate_ref[:, cols].astype(F32)
        y = y + jnp.dot(ob.astype(BF16), wo_ref[A_WIDTH + t * LANES:A_WIDTH + (t + 1) * LANES, :],
                        preferred_element_type=F32)
    y_ref[...] = y


def _attn_kernel(qt_ref, qit_ref, wit_ref, kaug_ref, kk_ref, vt_ref, ma_ref, gate_ref, x_ref, gb_ref,
                 wo_ref, y_ref, idx_ref, rhs_ref, s_ref, p_ref, *, seq, topk):
    j = pl.program_id(1)

    @pl.when(j == 0)
    def _():
        rhs_ref[:, 0:LANES] = kaug_ref[...]

    blocks_per_bucket = KEY_BUCKET // Q_BLOCK
    for nk in range(KEY_BUCKET, seq + 1, KEY_BUCKET):
        first = nk // Q_BLOCK - blocks_per_bucket

        @pl.when(jnp.logical_and(j >= first, j < first + blocks_per_bucket))
        def _(nk=nk):
            _attn_block(nk, topk, j, qt_ref, qit_ref, wit_ref, kk_ref, vt_ref, ma_ref, gate_ref,
                        x_ref, gb_ref, wo_ref, y_ref, idx_ref, rhs_ref, s_ref, p_ref)


def kernel(x, norm_gain, w_in, sgu_norm_gain, sgu_w, sgu_b, q_norm_gain, k_norm_gain,
           idx_k_norm_gain, branch_norm_gain, w_out):
    bsz, seq, d_model = x.shape
    assert d_model == D_MODEL and norm_gain.shape[0] == 1
    assert seq % PROJ_ROWS == 0 and seq % KEY_BUCKET == 0
    tokens = bsz * seq
    topk = min(TOPK_MAX, seq // 4)
    idx_w_scale = (IDX_HEADS ** -0.5) * (IDX_DIM ** -0.5)

    w = w_in[0]
    a3 = 3 * A_WIDTH
    w_q = w[:, a3:a3 + B_WIDTH]
    w_k = w[:, a3 + B_WIDTH:a3 + B_WIDTH + HEAD_DIM]
    w_v = w[:, a3 + B_WIDTH + HEAD_DIM:a3 + B_WIDTH + 2 * HEAD_DIM]
    o_g = a3 + B_WIDTH + 2 * HEAD_DIM
    w_g = w[:, o_g:o_g + B_WIDTH]
    o_i = o_g + B_WIDTH
    w_iq = w[:, o_i:o_i + IDX_HEADS * IDX_DIM]
    w_ik = w[:, o_i + IDX_HEADS * IDX_DIM:o_i + IDX_HEADS * IDX_DIM + IDX_DIM]
    w_iw = w[:, o_i + IDX_HEADS * IDX_DIM + IDX_DIM:]
    w_main = jnp.concatenate([w[:, :a3], w_g, w_k, w_ik], axis=1).astype(BF16)
    assert w_main.shape[1] == _PACKED_COLS
    w_t = jnp.concatenate([w_q.T, w_iq.T, w_v.T, w_iw.T,
                           jnp.zeros((_PACKED_ROWS - _ROW_W - IDX_HEADS, D_MODEL), F32)], axis=0).astype(BF16)
    x2 = x.reshape(tokens, D_MODEL)
    ng = norm_gain[0].reshape(1, D_MODEL)
    sgn = sgu_norm_gain[0].reshape(1, A_WIDTH)
    sw = sgu_w[0]
    sb = sgu_b[0].reshape(A_GROUPS, A_BLOCK, 1)
    qg = q_norm_gain[0].reshape(HEAD_DIM, 1)
    kg = jnp.concatenate([k_norm_gain[0], idx_k_norm_gain[0]]).reshape(1, LANES)
    ga = branch_norm_gain[0, :A_WIDTH].reshape(1, A_WIDTH)
    gb = branch_norm_gain[0, A_WIDTH:].reshape(1, B_WIDTH)
    wo = w_out[0].astype(BF16)

    tm = PROJ_ROWS
    full = lambda shape: pl.BlockSpec(shape, lambda i: (0,) * len(shape))
    rows = lambda width: pl.BlockSpec((tm, width), lambda i: (i, 0))
    colsT = lambda height: pl.BlockSpec((height, tm), lambda i: (0, i))
    outs = pl.pallas_call(
        functools.partial(_proj_kernel, tiles_per_seq=seq // tm, idx_w_scale=idx_w_scale),
        grid=(tokens // tm,),
        in_specs=[rows(D_MODEL), full((1, D_MODEL)), full((D_MODEL, _PACKED_COLS)),
                  full((_PACKED_ROWS, D_MODEL)), full((1, A_WIDTH)), full((A_GROUPS, A_BLOCK, A_BLOCK)),
                  full((A_GROUPS, A_BLOCK, 1)), full((HEAD_DIM, 1)), full((1, LANES)),
                  full((1, A_WIDTH))],
        out_specs=[rows(A_WIDTH), rows(B_WIDTH), rows(LANES), rows(LANES),
                   colsT(B_WIDTH), colsT(IDX_HEADS * IDX_DIM), colsT(HEAD_DIM), colsT(IDX_HEADS)],
        out_shape=[jax.ShapeDtypeStruct((tokens, A_WIDTH), BF16),
                   jax.ShapeDtypeStruct((tokens, B_WIDTH), BF16),
                   jax.ShapeDtypeStruct((tokens, LANES), BF16),
                   jax.ShapeDtypeStruct((tokens, LANES), BF16),
                   jax.ShapeDtypeStruct((B_WIDTH, tokens), BF16),
                   jax.ShapeDtypeStruct((IDX_HEADS * IDX_DIM, tokens), BF16),
                   jax.ShapeDtypeStruct((HEAD_DIM, tokens), BF16),
                   jax.ShapeDtypeStruct((IDX_HEADS, tokens), F32)],
        compiler_params=pltpu.CompilerParams(dimension_semantics=("arbitrary",),
                                             vmem_limit_bytes=VMEM_LIMIT),
        name="proj_sgu",
    )(x2, ng, w_main, w_t, sgn, sw, sb, qg, kg, ga)
    ma, gate, kaug, kk, qt, qit, vt, wit = outs

    nb = seq // Q_BLOCK
    qblk = lambda width: pl.BlockSpec((Q_BLOCK, width), lambda b, j: (b * nb + j, 0))
    qblkT = lambda height: pl.BlockSpec((height, Q_BLOCK), lambda b, j: (0, b * nb + j))
    const = lambda shape: pl.BlockSpec(shape, lambda b, j: (0,) * len(shape))
    y = pl.pallas_call(
        functools.partial(_attn_kernel, seq=seq, topk=topk),
        grid=(bsz, nb),
        in_specs=[qblkT(B_WIDTH), qblkT(IDX_HEADS * IDX_DIM), qblkT(IDX_HEADS),
                  pl.BlockSpec((seq, LANES), lambda b, j: (b, 0)),
                  pl.BlockSpec((seq, LANES), lambda b, j: (b, 0)),
                  pl.BlockSpec((HEAD_DIM, seq), lambda b, j: (0, b)),
                  qblk(A_WIDTH), qblk(B_WIDTH), qblk(D_MODEL),
                  const((1, B_WIDTH)), const((D_MODEL, D_MODEL))],
        out_specs=qblk(D_MODEL),
        out_shape=jax.ShapeDtypeStruct((tokens, D_MODEL), F32),
        scratch_shapes=[pltpu.VMEM((seq, Q_BLOCK), F32),
                        pltpu.VMEM((seq, 2 * LANES), BF16),
                        pltpu.VMEM((2, seq, (HEADS // 2) * Q_BLOCK), F32),
                        pltpu.VMEM((2, seq, (HEADS // 2) * Q_BLOCK), BF16)],
        compiler_params=pltpu.CompilerParams(dimension_semantics=("arbitrary", "arbitrary"),
                                             vmem_limit_bytes=VMEM_LIMIT),
        name="dsa_attn_out",
    )(qt, qit, wit, kaug, kk, vt, ma, gate, x2, gb, wo)
    return y.reshape(bsz, seq, D_MODEL)
```

```python
import functools

import jax
import jax.numpy as jnp
from jax import lax
from jax.experimental import pallas as pl
from jax.experimental.pallas import tpu as pltpu

F32 = jnp.float32
BF16 = jnp.bfloat16

D_MODEL = 1024
CHUNK = 64
A_WIDTH = 512
A_GROUPS = 4
A_BLOCK = 128
HEADS = 8
HEAD_DIM = 64
B_WIDTH = HEADS * HEAD_DIM
IDX_HEADS = 8
IDX_DIM = 64
TOPK_MAX = 256
Q_BLOCK = 128
NORM_EPS = 1e-6
MASK_OFF = 1e32
LANES = 128
HALF = LANES // 2
PROJ_ROWS = 512
DOT_ROWS = 512
RED_ROWS = 128
KEY_BUCKET = 256
FAST_TRIPS = 24
INTERP_MARGIN = 0.02
SLOW_TRIPS = 70
VMEM_LIMIT = 48 * 1024 * 1024

_OFF_U, _OFF_V, _OFF_Z, _OFF_G, _OFF_K = 0, 512, 1024, 1536, 2048
_PACKED_COLS = 2176
_ROW_Q, _ROW_QI, _ROW_V, _ROW_W, _PACKED_ROWS = 0, 512, 1024, 1088, 1152

_NT = (((1,), (1,)), ((), ()))


def _gelu(x):
    c = 0.7978845608028654
    return 0.5 * x * (1.0 + jnp.tanh(c * (x + 0.044715 * (x * x * x))))


def _silu(x):
    return x / (1.0 + jnp.exp(-x))


def _row_blocks(total, size):
    return [slice(r, min(r + size, total)) for r in range(0, total, size)]


def _half_mean_sq(x2, lo_half):
    tot = jnp.sum(x2, axis=-1, keepdims=True)
    lo = jnp.sum(jnp.where(lo_half, x2, 0.0), axis=-1, keepdims=True)
    return jnp.where(lo_half, lo, tot - lo) * (1.0 / HALF)


def _proj_kernel(x_ref, ng_ref, w_ref, wt_ref, sgn_ref, sw_ref, sb_ref, qg_ref, kg_ref, ga_ref,
                 ma_ref, gate_ref, kaug_ref, kk_ref, qt_ref, qit_ref, vt_ref, wit_ref,
                 *, tiles_per_seq, idx_w_scale):
    tm = x_ref.shape[0]
    i = pl.program_id(0)
    x = x_ref[...]
    ms = jnp.mean(x * x, axis=-1, keepdims=True)
    h = (x * lax.rsqrt(ms + NORM_EPS) * ng_ref[...]).astype(BF16)

    lane = lax.broadcasted_iota(jnp.int32, (tm, LANES), 1)
    lo_half = lane < HALF

    def proj(off, width):
        return jnp.dot(h, w_ref[:, off:off + width], preferred_element_type=F32)

    gu = _gelu(proj(_OFF_U, A_WIDTH))
    gv = _gelu(proj(_OFF_V, A_WIDTH))
    pz = proj(_OFF_Z, A_WIDTH)
    r_i = lax.broadcasted_iota(jnp.int32, (A_BLOCK, A_BLOCK), 0)
    c_j = lax.broadcasted_iota(jnp.int32, (A_BLOCK, A_BLOCK), 1)
    causal = lax.shift_right_logical(c_j, 6) <= lax.shift_right_logical(r_i, 6)
    for g in range(A_GROUPS):
        cols = slice(g * LANES, (g + 1) * LANES)
        vg = gv[:, cols]
        mu = jnp.mean(vg, axis=-1, keepdims=True)
        d = vg - mu
        var = jnp.mean(d * d, axis=-1, keepdims=True)
        vn = (d * lax.rsqrt(var + NORM_EPS) * sgn_ref[:, cols]).astype(BF16)
        wg = jnp.where(causal, sw_ref[g], 0.0).astype(BF16)
        for blk in range(tm // A_BLOCK):
            rows = slice(blk * A_BLOCK, (blk + 1) * A_BLOCK)
            s = jnp.dot(wg, vn[rows, :], preferred_element_type=F32) + sb_ref[g]
            ya = gu[rows, cols] * s
            oa = ya * lax.rsqrt(jnp.mean(ya * ya, axis=-1, keepdims=True) + NORM_EPS) * ga_ref[:, cols]
            ma_ref[rows, cols] = (oa * _silu(pz[rows, cols])).astype(BF16)

    gate_ref[...] = _silu(proj(_OFF_G, B_WIDTH)).astype(BF16)

    pk = proj(_OFF_K, LANES)
    k_ms = jnp.sum(jnp.where(lo_half, pk * pk, 0.0), axis=-1, keepdims=True) * (1.0 / HALF)
    kn = pk * lax.rsqrt(k_ms + NORM_EPS)
    ik_mu = jnp.sum(jnp.where(lo_half, 0.0, pk), axis=-1, keepdims=True) * (1.0 / HALF)
    dk = pk - ik_mu
    ik_var = jnp.sum(jnp.where(lo_half, 0.0, dk * dk), axis=-1, keepdims=True) * (1.0 / HALF)
    kin = dk * lax.rsqrt(ik_var + NORM_EPS)
    tile = jnp.where(lo_half, kn, kin) * kg_ref[...]
    swapped = pltpu.roll(tile, HALF, axis=1)
    row = lax.broadcasted_iota(jnp.int32, (tm, LANES), 0)
    pos = (i % tiles_per_seq) * tm + row
    pos_hi = lax.shift_right_logical(pos, 6).astype(F32)
    pos_lo = (pos & (CHUNK - 1)).astype(F32)
    posfeat = jnp.where(lane == HALF, pos_hi, jnp.where(lane == HALF + 1, pos_lo, 0.0))
    kaug_ref[...] = jnp.where(lo_half, tile, posfeat).astype(BF16)
    kk_ref[...] = jnp.where(lo_half, swapped, tile).astype(BF16)

    pt = lax.dot_general(wt_ref[...], h, _NT, preferred_element_type=F32)
    for hh in range(HEADS):
        rows = slice(_ROW_Q + hh * HEAD_DIM, _ROW_Q + (hh + 1) * HEAD_DIM)
        xq = pt[rows, :]
        msq = jnp.mean(xq * xq, axis=0, keepdims=True)
        qt_ref[rows, :] = (xq * lax.rsqrt(msq + NORM_EPS) * qg_ref[...] * (HEAD_DIM ** -0.5)).astype(BF16)
    qit_ref[...] = pt[_ROW_QI:_ROW_QI + IDX_HEADS * IDX_DIM, :].astype(BF16)
    vt_ref[...] = pt[_ROW_V:_ROW_V + HEAD_DIM, :].astype(BF16)
    wit_ref[...] = pt[_ROW_W:_ROW_W + IDX_HEADS, :] * idx_w_scale


def _attn_block(nk, topk, j, qt_ref, qit_ref, wit_ref, kk_ref, vt_ref, ma_ref, gate_ref,
                x_ref, gb_ref, wo_ref, y_ref, idx_ref, rhs_ref, s_ref, p_ref):
    kf = float(topk)
    lane_q = lax.broadcasted_iota(jnp.int32, (1, Q_BLOCK), 1)
    n_adm = j * Q_BLOCK + CHUNK + CHUNK * (lane_q >= CHUNK).astype(jnp.int32)
    wi = wit_ref[...]
    top_rows = lax.broadcasted_iota(jnp.int32, (LANES, Q_BLOCK), 0) < HALF

    per_head = []
    for t in range(IDX_HEADS // 2):
        qit = qit_ref[t * LANES:(t + 1) * LANES, :]
        zero = jnp.zeros_like(qit)
        per_head += [jnp.where(top_rows, qit, zero), jnp.where(top_rows, zero, qit)]
    half_heads = IDX_HEADS // 2
    for g in range(2):
        wg = jnp.concatenate(per_head[g * half_heads:(g + 1) * half_heads], axis=1)
        for rows in _row_blocks(nk, DOT_ROWS):
            s_ref[g, rows, :] = jnp.dot(kk_ref[rows, :], wg, preferred_element_type=F32)

    def weighted_relu(g, rows):
        acc = None
        for i in range(half_heads):
            hh = g * half_heads + i
            term = jnp.maximum(s_ref[g, rows, i * Q_BLOCK:(i + 1) * Q_BLOCK], 0.0) * wi[hh:hh + 1, :]
            acc = term if acc is None else acc + term
        return acc

    mn_acc = jnp.full((RED_ROWS, Q_BLOCK), jnp.inf, F32)
    mx_acc = jnp.full((RED_ROWS, Q_BLOCK), -jnp.inf, F32)
    tiny_acc = jnp.full((RED_ROWS, Q_BLOCK), jnp.inf, F32)
    for r in range(nk // RED_ROWS):
        rows = slice(r * RED_ROWS, (r + 1) * RED_ROWS)
        acc = weighted_relu(0, rows) + weighted_relu(1, rows)
        krow = r * RED_ROWS + lax.broadcasted_iota(jnp.int32, (RED_ROWS, Q_BLOCK), 0)
        adm = krow < n_adm
        lo_fill = jnp.where(adm, acc, -jnp.inf)
        hi_fill = jnp.where(adm, acc, jnp.inf)
        mag = jnp.abs(hi_fill)
        idx_ref[rows, :] = lo_fill
        mn_acc = jnp.minimum(mn_acc, hi_fill)
        mx_acc = jnp.maximum(mx_acc, lo_fill)
        tiny_acc = jnp.minimum(tiny_acc, jnp.where(mag == 0.0, jnp.inf, mag))
    lo = jnp.min(mn_acc, axis=0, keepdims=True)
    hi = jnp.max(mx_acc, axis=0, keepdims=True)
    tiny = jnp.min(tiny_acc, axis=0, keepdims=True)

    eps = jnp.where(tiny < jnp.inf, tiny, 1.0) * (0.5 / nk)
    for r in range(nk // RED_ROWS):
        rows = slice(r * RED_ROWS, (r + 1) * RED_ROWS)
        s = idx_ref[rows, :]
        rank = (r * RED_ROWS + 1 + lax.broadcasted_iota(jnp.int32, (RED_ROWS, Q_BLOCK), 0)).astype(F32)
        idx_ref[rows, :] = jnp.where(s == 0.0, -(rank * eps), s)
    lo = jnp.minimum(lo, -0.5 * jnp.where(tiny < jnp.inf, tiny, 1.0))

    def count(pred):
        acc = jnp.zeros((RED_ROWS, Q_BLOCK), F32)
        for r in range(nk // RED_ROWS):
            acc = acc + jnp.where(pred(idx_ref[r * RED_ROWS:(r + 1) * RED_ROWS, :]), 1.0, 0.0)
        return jnp.sum(acc, axis=0, keepdims=True)

    def probe(t, state):
        lo, hi, c_lo, c_hi = state
        c = count(lambda s: s >= t)
        ge = c >= kf
        return (jnp.where(ge, t, lo), jnp.where(ge, hi, t), jnp.where(ge, c, c_lo), jnp.where(ge, c_hi, c))

    def unsettled(c_lo):
        return jnp.max(jnp.where(c_lo != kf, 1.0, 0.0)) > 0.0

    def fast_cond(carry):
        it, _, _, c_lo, _ = carry
        return jnp.logical_and(it < FAST_TRIPS, unsettled(c_lo))

    def fast_body(carry):
        it, lo, hi, c_lo, c_hi = carry
        state = (lo, hi, c_lo, c_hi)
        for _ in range(2):
            lo, hi, c_lo, c_hi = state
            frac = (c_lo - kf) / jnp.maximum(c_lo - c_hi, 1.0)
            frac = jnp.minimum(jnp.maximum(frac, INTERP_MARGIN), 1.0 - INTERP_MARGIN)
            state = probe(lo + (hi - lo) * frac, state)
        return (it + 1,) + state

    n_adm_f = n_adm.astype(F32)
    c_lo0 = jnp.where(n_adm_f <= kf, kf, n_adm_f)
    c_hi0 = jnp.zeros((1, Q_BLOCK), F32)
    _, lo, hi, c_lo, _ = lax.while_loop(fast_cond, fast_body, (jnp.int32(0), lo, hi, c_lo0, c_hi0))

    r_k = lax.broadcasted_iota(jnp.int32, (Q_BLOCK, Q_BLOCK), 0)
    c_q = lax.broadcasted_iota(jnp.int32, (Q_BLOCK, Q_BLOCK), 1)
    later = 2.0 * jnp.maximum(r_k - c_q, 0).astype(F32)
    diag = pl.ds(pl.multiple_of(j * Q_BLOCK, Q_BLOCK), Q_BLOCK)

    def write_bias(sel_fn):
        for r in range(nk // RED_ROWS):
            rows = slice(r * RED_ROWS, (r + 1) * RED_ROWS)
            sel = sel_fn(idx_ref[rows, :], r * RED_ROWS + r_k)
            rhs_ref[rows, LANES:2 * LANES] = jnp.where(sel, 0.0, -MASK_OFF).astype(BF16)
        sel = sel_fn(idx_ref[diag, :], j * Q_BLOCK + r_k)
        rhs_ref[diag, LANES:2 * LANES] = jnp.where(sel, -later, -MASK_OFF).astype(BF16)

    write_bias(lambda s, krow: s >= lo)

    @pl.when(unsettled(c_lo))
    def _():
        def kth(lo):
            acc = jnp.full((RED_ROWS, Q_BLOCK), jnp.inf, F32)
            for r in range(nk // RED_ROWS):
                s = idx_ref[r * RED_ROWS:(r + 1) * RED_ROWS, :]
                acc = jnp.minimum(acc, jnp.where(s >= lo, s, jnp.inf))
            thr = jnp.min(acc, axis=0, keepdims=True)
            return thr, count(lambda s: s > thr)

        def slow_cond(carry):
            it, _, _, _, c_gt = carry
            return jnp.logical_and(it < SLOW_TRIPS, jnp.max(jnp.where(c_gt >= kf, 1.0, 0.0)) > 0.0)

        def slow_body(carry):
            it, lo, hi, _, _ = carry
            for _ in range(4):
                mid = 0.5 * lo + 0.5 * hi
                ge = count(lambda s: s >= mid) >= kf
                lo, hi = jnp.where(ge, mid, lo), jnp.where(ge, hi, mid)
            thr, c_gt = kth(lo)
            return it + 1, lo, hi, thr, c_gt

        thr0, c_gt0 = kth(lo)
        _, _, _, thr, c_gt = lax.while_loop(slow_cond, slow_body, (jnp.int32(0), lo, hi, thr0, c_gt0))
        c_eq = count(lambda s: s == thr)
        need = (c_gt + c_eq) > kf

        def tie_count(cut):
            acc = jnp.zeros((RED_ROWS, Q_BLOCK), F32)
            for r in range(nk // RED_ROWS):
                s = idx_ref[r * RED_ROWS:(r + 1) * RED_ROWS, :]
                rowf = (r * RED_ROWS + lax.broadcasted_iota(jnp.int32, (RED_ROWS, Q_BLOCK), 0)).astype(F32)
                acc = acc + jnp.where(s == thr, jnp.where(rowf <= cut, 1.0, 0.0), 0.0)
            return jnp.sum(acc, axis=0, keepdims=True)

        def tie_step(_, carry):
            lo_i, hi_i = carry
            mid_i = jnp.floor((lo_i + hi_i) * 0.5)
            ok = (c_gt + tie_count(mid_i)) >= kf
            return jnp.where(ok, lo_i, mid_i), jnp.where(ok, mid_i, hi_i)

        lo_i = jnp.full((1, Q_BLOCK), -1.0, F32)
        hi_i = jnp.full((1, Q_BLOCK), float(nk - 1), F32)
        _, cut = lax.fori_loop(0, (nk - 1).bit_length() + 1, tie_step, (lo_i, hi_i))
        cut = jnp.where(need, cut, float(nk))

        cut = cut.astype(jnp.int32)
        write_bias(lambda s, krow: jnp.logical_or(s > thr, jnp.logical_and(s == thr, krow <= cut)))

    feat = lax.broadcasted_iota(jnp.int32, (HALF, Q_BLOCK), 0)
    lane = lax.broadcasted_iota(jnp.int32, (Q_BLOCK, LANES), 1)
    lo_half = lane < HALF
    group = HEADS // 2
    width = group * Q_BLOCK

    def score_operand(hh):
        slope = 2.0 ** (-(hh + 1))
        alibi = jnp.where(feat == 0, CHUNK * slope, jnp.where(feat == 1, slope, 0.0)).astype(BF16)
        scaled_ident = jnp.where(r_k == c_q, slope, 0.0).astype(BF16)
        return jnp.concatenate([qt_ref[hh * HEAD_DIM:(hh + 1) * HEAD_DIM, :], alibi, scaled_ident], axis=0)

    def scores(g):
        lhs_t = jnp.concatenate([score_operand(g * group + i) for i in range(group)], axis=1)
        m_acc = jnp.full((RED_ROWS, width), -jnp.inf, F32)
        for rows in _row_blocks(nk, DOT_ROWS):
            blk = jnp.dot(rhs_ref[rows, :], lhs_t, preferred_element_type=F32)
            s_ref[g, rows, :] = blk
            for sub in range((rows.stop - rows.start) // RED_ROWS):
                m_acc = jnp.maximum(m_acc, blk[sub * RED_ROWS:(sub + 1) * RED_ROWS, :])
        return jnp.max(m_acc, axis=0, keepdims=True)

    def probabilities(g, m):
        l_acc = jnp.zeros((RED_ROWS, width), F32)
        for r in range(nk // RED_ROWS):
            rows = slice(r * RED_ROWS, (r + 1) * RED_ROWS)
            p = jnp.exp(s_ref[g, rows, :] - m)
            l_acc = l_acc + p
            p_ref[g, rows, :] = p.astype(BF16)
        return jnp.sum(l_acc, axis=0, keepdims=True)

    def values(g, l):
        return jnp.dot(vt_ref[:, 0:nk], p_ref[g, 0:nk, :], preferred_element_type=F32) * (1.0 / l)

    m0 = scores(0)
    m1 = scores(1)
    y = x_ref[...] + jnp.dot(ma_ref[...], wo_ref[0:A_WIDTH, :], preferred_element_type=F32)
    l0 = probabilities(0, m0)
    o0 = values(0, l0)
    l1 = probabilities(1, m1)
    o1 = values(1, l1)
    for t in range(HEADS // 2):
        cols = slice(t * LANES, (t + 1) * LANES)
        o_g = o0 if t < group // 2 else o1
        a = (2 * t) % group
        o = jnp.concatenate([o_g[:, a * Q_BLOCK:(a + 1) * Q_BLOCK],
                             o_g[:, (a + 1) * Q_BLOCK:(a + 2) * Q_BLOCK]], axis=0).T
        msq = _half_mean_sq(o * o, lo_half)
        ob = o * lax.rsqrt(msq + NORM_EPS) * gb_ref[:, cols] * gate_ref[:, cols].astype(F32)
        y = y + jnp.dot(ob.astype(BF16), wo_ref[A_WIDTH + t * LANES:A_WIDTH + (t + 1) * LANES, :],
                        preferred_element_type=F32)
    y_ref[...] = y


def _attn_kernel(qt_ref, qit_ref, wit_ref, kaug_ref, kk_ref, vt_ref, ma_ref, gate_ref, x_ref, gb_ref,
                 wo_ref, y_ref, idx_ref, rhs_ref, s_ref, p_ref, *, seq, topk):
    j = pl.program_id(1)

    @pl.when(j == 0)
    def _():
        rhs_ref[:, 0:LANES] = kaug_ref[...]

    blocks_per_bucket = KEY_BUCKET // Q_BLOCK
    for nk in range(KEY_BUCKET, seq + 1, KEY_BUCKET):
        first = nk // Q_BLOCK - blocks_per_bucket

        @pl.when(jnp.logical_and(j >= first, j < first + blocks_per_bucket))
        def _(nk=nk):
            _attn_block(nk, topk, j, qt_ref, qit_ref, wit_ref, kk_ref, vt_ref, ma_ref, gate_ref,
                        x_ref, gb_ref, wo_ref, y_ref, idx_ref, rhs_ref, s_ref, p_ref)


def kernel(x, norm_gain, w_in, sgu_norm_gain, sgu_w, sgu_b, q_norm_gain, k_norm_gain,
           idx_k_norm_gain, branch_norm_gain, w_out):
    bsz, seq, d_model = x.shape
    assert d_model == D_MODEL and norm_gain.shape[0] == 1
    assert seq % PROJ_ROWS == 0 and seq % KEY_BUCKET == 0
    tokens = bsz * seq
    topk = min(TOPK_MAX, seq // 4)
    idx_w_scale = (IDX_HEADS ** -0.5) * (IDX_DIM ** -0.5)

    w = w_in[0]
    a3 = 3 * A_WIDTH
    w_q = w[:, a3:a3 + B_WIDTH]
    w_k = w[:, a3 + B_WIDTH:a3 + B_WIDTH + HEAD_DIM]
    w_v = w[:, a3 + B_WIDTH + HEAD_DIM:a3 + B_WIDTH + 2 * HEAD_DIM]
    o_g = a3 + B_WIDTH + 2 * HEAD_DIM
    w_g = w[:, o_g:o_g + B_WIDTH]
    o_i = o_g + B_WIDTH
    w_iq = w[:, o_i:o_i + IDX_HEADS * IDX_DIM]
    w_ik = w[:, o_i + IDX_HEADS * IDX_DIM:o_i + IDX_HEADS * IDX_DIM + IDX_DIM]
    w_iw = w[:, o_i + IDX_HEADS * IDX_DIM + IDX_DIM:]
    w_main = jnp.concatenate([w[:, :a3], w_g, w_k, w_ik], axis=1).astype(BF16)
    assert w_main.shape[1] == _PACKED_COLS
    w_t = jnp.concatenate([w_q.T, w_iq.T, w_v.T, w_iw.T,
                           jnp.zeros((_PACKED_ROWS - _ROW_W - IDX_HEADS, D_MODEL), F32)], axis=0).astype(BF16)
    x2 = x.reshape(tokens, D_MODEL)
    ng = norm_gain[0].reshape(1, D_MODEL)
    sgn = sgu_norm_gain[0].reshape(1, A_WIDTH)
    sw = sgu_w[0]
    sb = sgu_b[0].reshape(A_GROUPS, A_BLOCK, 1)
    qg = q_norm_gain[0].reshape(HEAD_DIM, 1)
    kg = jnp.concatenate([k_norm_gain[0], idx_k_norm_gain[0]]).reshape(1, LANES)
    ga = branch_norm_gain[0, :A_WIDTH].reshape(1, A_WIDTH)
    gb = branch_norm_gain[0, A_WIDTH:].reshape(1, B_WIDTH)
    wo = w_out[0].astype(BF16)

    tm = PROJ_ROWS
    full = lambda shape: pl.BlockSpec(shape, lambda i: (0,) * len(shape))
    rows = lambda width: pl.BlockSpec((tm, width), lambda i: (i, 0))
    colsT = lambda height: pl.BlockSpec((height, tm), lambda i: (0, i))
    outs = pl.pallas_call(
        functools.partial(_proj_kernel, tiles_per_seq=seq // tm, idx_w_scale=idx_w_scale),
        grid=(tokens // tm,),
        in_specs=[rows(D_MODEL), full((1, D_MODEL)), full((D_MODEL, _PACKED_COLS)),
                  full((_PACKED_ROWS, D_MODEL)), full((1, A_WIDTH)), full((A_GROUPS, A_BLOCK, A_BLOCK)),
                  full((A_GROUPS, A_BLOCK, 1)), full((HEAD_DIM, 1)), full((1, LANES)),
                  full((1, A_WIDTH))],
        out_specs=[rows(A_WIDTH), rows(B_WIDTH), rows(LANES), rows(LANES),
                   colsT(B_WIDTH), colsT(IDX_HEADS * IDX_DIM), colsT(HEAD_DIM), colsT(IDX_HEADS)],
        out_shape=[jax.ShapeDtypeStruct((tokens, A_WIDTH), BF16),
                   jax.ShapeDtypeStruct((tokens, B_WIDTH), BF16),
                   jax.ShapeDtypeStruct((tokens, LANES), BF16),
                   jax.ShapeDtypeStruct((tokens, LANES), BF16),
                   jax.ShapeDtypeStruct((B_WIDTH, tokens), BF16),
                   jax.ShapeDtypeStruct((IDX_HEADS * IDX_DIM, tokens), BF16),
                   jax.ShapeDtypeStruct((HEAD_DIM, tokens), BF16),
                   jax.ShapeDtypeStruct((IDX_HEADS, tokens), F32)],
        compiler_params=pltpu.CompilerParams(dimension_semantics=("arbitrary",),
                                             vmem_limit_bytes=VMEM_LIMIT),
        name="proj_sgu",
    )(x2, ng, w_main, w_t, sgn, sw, sb, qg, kg, ga)
    ma, gate, kaug, kk, qt, qit, vt, wit = outs

    nb = seq // Q_BLOCK
    qblk = lambda width: pl.BlockSpec((Q_BLOCK, width), lambda b, j: (b * nb + j, 0))
    qblkT = lambda height: pl.BlockSpec((height, Q_BLOCK), lambda b, j: (0, b * nb + j))
    const = lambda shape: pl.BlockSpec(shape, lambda b, j: (0,) * len(shape))
    y = pl.pallas_call(
        functools.partial(_attn_kernel, seq=seq, topk=topk),
        grid=(bsz, nb),
        in_specs=[qblkT(B_WIDTH), qblkT(IDX_HEADS * IDX_DIM), qblkT(IDX_HEADS),
                  pl.BlockSpec((seq, LANES), lambda b, j: (b, 0)),
                  pl.BlockSpec((seq, LANES), lambda b, j: (b, 0)),
                  pl.BlockSpec((HEAD_DIM, seq), lambda b, j: (0, b)),
                  qblk(A_WIDTH), qblk(B_WIDTH), qblk(D_MODEL),
                  const((1, B_WIDTH)), const((D_MODEL, D_MODEL))],
        out_specs=qblk(D_MODEL),
        out_shape=jax.ShapeDtypeStruct((tokens, D_MODEL), F32),
        scratch_shapes=[pltpu.VMEM((seq, Q_BLOCK), F32),
                        pltpu.VMEM((seq, 2 * LANES), BF16),
                        pltpu.VMEM((2, seq, (HEADS // 2) * Q_BLOCK), F32),
                        pltpu.VMEM((2, seq, (HEADS // 2) * Q_BLOCK), BF16)],
        compiler_params=pltpu.CompilerParams(dimension_semantics=("arbitrary", "arbitrary"),
                                             vmem_limit_bytes=VMEM_LIMIT),
        name="dsa_attn_out",
    )(qt, qit, wit, kaug, kk, vt, ma, gate, x2, gb, wo)
    return y.reshape(bsz, seq, D_MODEL)
```

```python
import functools

import jax
import jax.numpy as jnp
from jax import lax
from jax.experimental import pallas as pl
from jax.experimental.pallas import tpu as pltpu

F32 = jnp.float32
BF16 = jnp.bfloat16

D_MODEL = 1024
CHUNK = 64
A_WIDTH = 512
A_GROUPS = 4
A_BLOCK = 128
HEADS = 8
HEAD_DIM = 64
B_WIDTH = HEADS * HEAD_DIM
IDX_HEADS = 8
IDX_DIM = 64
TOPK_MAX = 256
Q_BLOCK = 128
PAIR = 2
NORM_EPS = 1e-6
MASK_OFF = 1e32
LANES = 128
HALF = LANES // 2
PROJ_ROWS = 512
DOT_ROWS = 512
RED_ROWS = 128
FAST_TRIPS = 24
INTERP_MARGIN = 0.02
SLOW_TRIPS = 70
VMEM_LIMIT = 48 * 1024 * 1024

_OFF_U, _OFF_V, _OFF_Z, _OFF_G, _OFF_K = 0, 512, 1024, 1536, 2048
_PACKED_COLS = 2176
_ROW_Q, _ROW_QI, _ROW_V, _ROW_W, _PACKED_ROWS = 0, 512, 1024, 1088, 1152

_NT = (((1,), (1,)), ((), ()))


def _gelu(x):
    c = 0.7978845608028654
    return 0.5 * x * (1.0 + jnp.tanh(c * (x + 0.044715 * (x * x * x))))


def _silu(x):
    return x / (1.0 + jnp.exp(-x))


def _row_blocks(total, size):
    return [slice(r, min(r + size, total)) for r in range(0, total, size)]


def _half_mean_sq(x2, lo_half):
    tot = jnp.sum(x2, axis=-1, keepdims=True)
    lo = jnp.sum(jnp.where(lo_half, x2, 0.0), axis=-1, keepdims=True)
    return jnp.where(lo_half, lo, tot - lo) * (1.0 / HALF)


def _proj_kernel(x_ref, ng_ref, w_ref, wt_ref, sgn_ref, sw_ref, sb_ref, qg_ref, kg_ref, ga_ref,
                 ma_ref, gate_ref, kaug_ref, kk_ref, qt_ref, qit_ref, vt_ref, wit_ref,
                 *, tiles_per_seq, idx_w_scale):
    tm = x_ref.shape[0]
    i = pl.program_id(0)
    x = x_ref[...]
    ms = jnp.mean(x * x, axis=-1, keepdims=True)
    h = (x * lax.rsqrt(ms + NORM_EPS) * ng_ref[...]).astype(BF16)

    lane = lax.broadcasted_iota(jnp.int32, (tm, LANES), 1)
    lo_half = lane < HALF

    def proj(off, width):
        return jnp.dot(h, w_ref[:, off:off + width], preferred_element_type=F32)

    gu = _gelu(proj(_OFF_U, A_WIDTH))
    gv = _gelu(proj(_OFF_V, A_WIDTH))
    pz = proj(_OFF_Z, A_WIDTH)
    r_i = lax.broadcasted_iota(jnp.int32, (A_BLOCK, A_BLOCK), 0)
    c_j = lax.broadcasted_iota(jnp.int32, (A_BLOCK, A_BLOCK), 1)
    causal = lax.shift_right_logical(c_j, 6) <= lax.shift_right_logical(r_i, 6)
    for g in range(A_GROUPS):
        cols = slice(g * LANES, (g + 1) * LANES)
        vg = gv[:, cols]
        mu = jnp.mean(vg, axis=-1, keepdims=True)
        d = vg - mu
        var = jnp.mean(d * d, axis=-1, keepdims=True)
        vn = (d * lax.rsqrt(var + NORM_EPS) * sgn_ref[:, cols]).astype(BF16)
        wg = jnp.where(causal, sw_ref[g], 0.0).astype(BF16)
        for blk in range(tm // A_BLOCK):
            rows = slice(blk * A_BLOCK, (blk + 1) * A_BLOCK)
            s = jnp.dot(wg, vn[rows, :], preferred_element_type=F32) + sb_ref[g]
            ya = gu[rows, cols] * s
            oa = ya * lax.rsqrt(jnp.mean(ya * ya, axis=-1, keepdims=True) + NORM_EPS) * ga_ref[:, cols]
            ma_ref[rows, cols] = (oa * _silu(pz[rows, cols])).astype(BF16)

    gate_ref[...] = _silu(proj(_OFF_G, B_WIDTH)).astype(BF16)

    pk = proj(_OFF_K, LANES)
    k_ms = jnp.sum(jnp.where(lo_half, pk * pk, 0.0), axis=-1, keepdims=True) * (1.0 / HALF)
    kn = pk * lax.rsqrt(k_ms + NORM_EPS)
    ik_mu = jnp.sum(jnp.where(lo_half, 0.0, pk), axis=-1, keepdims=True) * (1.0 / HALF)
    dk = pk - ik_mu
    ik_var = jnp.sum(jnp.where(lo_half, 0.0, dk * dk), axis=-1, keepdims=True) * (1.0 / HALF)
    kin = dk * lax.rsqrt(ik_var + NORM_EPS)
    tile = jnp.where(lo_half, kn, kin) * kg_ref[...]
    swapped = pltpu.roll(tile, HALF, axis=1)
    row = lax.broadcasted_iota(jnp.int32, (tm, LANES), 0)
    pos = (i % tiles_per_seq) * tm + row
    pos_hi = lax.shift_right_logical(pos, 6).astype(F32)
    pos_lo = (pos & (CHUNK - 1)).astype(F32)
    posfeat = jnp.where(lane == HALF, pos_hi, jnp.where(lane == HALF + 1, pos_lo, 0.0))
    kaug_ref[...] = jnp.where(lo_half, tile, posfeat).astype(BF16)
    kk_ref[...] = jnp.where(lo_half, swapped, tile).astype(BF16)

    pt = lax.dot_general(wt_ref[...], h, _NT, preferred_element_type=F32)
    for hh in range(HEADS):
        rows = slice(_ROW_Q + hh * HEAD_DIM, _ROW_Q + (hh + 1) * HEAD_DIM)
        xq = pt[rows, :]
        msq = jnp.mean(xq * xq, axis=0, keepdims=True)
        qt_ref[rows, :] = (xq * lax.rsqrt(msq + NORM_EPS) * qg_ref[...] * (HEAD_DIM ** -0.5)).astype(BF16)
    qit_ref[...] = pt[_ROW_QI:_ROW_QI + IDX_HEADS * IDX_DIM, :].astype(BF16)
    vt_ref[...] = pt[_ROW_V:_ROW_V + HEAD_DIM, :].astype(BF16)
    wit_ref[...] = pt[_ROW_W:_ROW_W + IDX_HEADS, :] * idx_w_scale


def _tile_iotas():
    r_k = lax.broadcasted_iota(jnp.int32, (Q_BLOCK, Q_BLOCK), 0)
    c_q = lax.broadcasted_iota(jnp.int32, (Q_BLOCK, Q_BLOCK), 1)
    return r_k, c_q


def _colsum(x):
    return jnp.sum(x, axis=0, keepdims=True)


def _any_lane(pred):
    return jnp.max(jnp.where(pred, 1.0, 0.0)) > 0.0


def _index_scores(nk, qit_ref, qcols, wi, kk_ref, s_ref, idx_ref):
    r_k, c_q = _tile_iotas()
    top_rows = r_k < HALF
    per_head = []
    for t in range(IDX_HEADS // 2):
        qit = qit_ref[t * LANES:(t + 1) * LANES, qcols]
        zero = jnp.zeros_like(qit)
        per_head += [jnp.where(top_rows, qit, zero), jnp.where(top_rows, zero, qit)]
    half_heads = IDX_HEADS // 2
    for g in range(2):
        wg = jnp.concatenate(per_head[g * half_heads:(g + 1) * half_heads], axis=1)
        for rows in _row_blocks(nk, DOT_ROWS):
            s_ref[g, rows, :] = jnp.dot(kk_ref[rows, :], wg, preferred_element_type=F32)

    def weighted_relu(g, rows):
        acc = None
        for i in range(half_heads):
            hh = g * half_heads + i
            term = jnp.maximum(s_ref[g, rows, i * Q_BLOCK:(i + 1) * Q_BLOCK], 0.0) * wi[hh:hh + 1, :]
            acc = term if acc is None else acc + term
        return acc

    last_adm = jnp.logical_or(r_k < CHUNK, c_q >= CHUNK)
    slabs = nk // RED_ROWS
    mn_acc = jnp.full((RED_ROWS, Q_BLOCK), jnp.inf, F32)
    mx_acc = jnp.full((RED_ROWS, Q_BLOCK), -jnp.inf, F32)
    tiny_acc = jnp.full((RED_ROWS, Q_BLOCK), jnp.inf, F32)
    for r in range(slabs):
        rows = slice(r * RED_ROWS, (r + 1) * RED_ROWS)
        acc = weighted_relu(0, rows) + weighted_relu(1, rows)
        if r == slabs - 1:
            lo_fill = jnp.where(last_adm, acc, -jnp.inf)
            hi_fill = jnp.where(last_adm, acc, jnp.inf)
        else:
            lo_fill = hi_fill = acc
        mag = jnp.abs(hi_fill)
        idx_ref[rows, :] = lo_fill
        mn_acc = jnp.minimum(mn_acc, hi_fill)
        mx_acc = jnp.maximum(mx_acc, lo_fill)
        tiny_acc = jnp.minimum(tiny_acc, jnp.where(mag == 0.0, jnp.inf, mag))
    lo = jnp.min(mn_acc, axis=0, keepdims=True)
    hi = jnp.max(mx_acc, axis=0, keepdims=True)
    tiny = jnp.min(tiny_acc, axis=0, keepdims=True)

    unit = jnp.where(tiny < jnp.inf, tiny, 1.0)
    eps = unit * (0.5 / nk)
    for r in range(slabs):
        rows = slice(r * RED_ROWS, (r + 1) * RED_ROWS)
        s = idx_ref[rows, :]
        rank = (r * RED_ROWS + 1 + r_k).astype(F32)
        idx_ref[rows, :] = jnp.where(s == 0.0, -(rank * eps), s)
    return jnp.minimum(lo, -0.5 * unit), hi


def _probe(nk, kf, idx_ref, state):
    lo, hi, c_lo, c_hi = state
    frac = (c_lo - kf) / jnp.maximum(c_lo - c_hi, 1.0)
    frac = jnp.minimum(jnp.maximum(frac, INTERP_MARGIN), 1.0 - INTERP_MARGIN)
    t = lo + (hi - lo) * frac
    acc = jnp.zeros((RED_ROWS, Q_BLOCK), F32)
    for r in range(nk // RED_ROWS):
        acc = acc + jnp.where(idx_ref[r * RED_ROWS:(r + 1) * RED_ROWS, :] >= t, 1.0, 0.0)
    c = _colsum(acc)
    ge = c >= kf
    return (jnp.where(ge, t, lo), jnp.where(ge, hi, t), jnp.where(ge, c, c_lo), jnp.where(ge, c_hi, c))


def _write_selection(nk, sel_fn, idx_ref, rhs_ref):
    r_k, c_q = _tile_iotas()
    later = 2.0 * jnp.maximum(r_k - c_q, 0).astype(F32)
    slabs = nk // RED_ROWS
    for r in range(slabs):
        rows = slice(r * RED_ROWS, (r + 1) * RED_ROWS)
        sel = sel_fn(idx_ref[rows, :], r * RED_ROWS + r_k)
        on = -later if r == slabs - 1 else 0.0
        rhs_ref[rows, LANES:2 * LANES] = jnp.where(sel, on, -MASK_OFF).astype(BF16)


def _exact_fallback(slabs, kf, lo, hi, idx_ref, rhs_ref):
    r_k, c_q = _tile_iotas()
    later = 2.0 * jnp.maximum(r_k - c_q, 0).astype(F32)

    def slab(r):
        return idx_ref[pl.ds(pl.multiple_of(r * RED_ROWS, RED_ROWS), RED_ROWS), :]

    def count(pred):
        def body(r, acc):
            return acc + jnp.where(pred(slab(r), r), 1.0, 0.0)
        return _colsum(lax.fori_loop(0, slabs, body, jnp.zeros((RED_ROWS, Q_BLOCK), F32)))

    def kth(lo):
        def body(r, acc):
            s = slab(r)
            return jnp.minimum(acc, jnp.where(s >= lo, s, jnp.inf))
        acc = lax.fori_loop(0, slabs, body, jnp.full((RED_ROWS, Q_BLOCK), jnp.inf, F32))
        thr = jnp.min(acc, axis=0, keepdims=True)
        return thr, count(lambda s, r: s > thr)

    def slow_cond(carry):
        it, _, _, _, c_gt = carry
        return jnp.logical_and(it < SLOW_TRIPS, _any_lane(c_gt >= kf))

    def slow_body(carry):
        it, lo, hi, _, _ = carry
        for _ in range(4):
            mid = 0.5 * lo + 0.5 * hi
            ge = count(lambda s, r: s >= mid) >= kf
            lo, hi = jnp.where(ge, mid, lo), jnp.where(ge, hi, mid)
        thr, c_gt = kth(lo)
        return it + 1, lo, hi, thr, c_gt

    thr0, c_gt0 = kth(lo)
    _, _, _, thr, c_gt = lax.while_loop(slow_cond, slow_body, (jnp.int32(0), lo, hi, thr0, c_gt0))
    c_eq = count(lambda s, r: s == thr)
    need = (c_gt + c_eq) > kf

    def tie_step(_, carry):
        lo_i, hi_i = carry
        mid_i = jnp.floor((lo_i + hi_i) * 0.5)
        below = count(lambda s, r: jnp.logical_and(s == thr, (r * RED_ROWS + r_k).astype(F32) <= mid_i))
        ok = (c_gt + below) >= kf
        return jnp.where(ok, lo_i, mid_i), jnp.where(ok, mid_i, hi_i)

    last = (slabs * RED_ROWS - 1).astype(F32)
    lo_i = jnp.full((1, Q_BLOCK), -1.0, F32)
    hi_i = jnp.zeros((1, Q_BLOCK), F32) + last
    _, cut = lax.fori_loop(0, 13, tie_step, (lo_i, hi_i))
    cut = jnp.where(need, cut, last + 1.0).astype(jnp.int32)

    def write(r, carry):
        s = slab(r)
        krow = r * RED_ROWS + r_k
        sel = jnp.logical_or(s > thr, jnp.logical_and(s == thr, krow <= cut))
        on = jnp.where(r == slabs - 1, -later, 0.0)
        rhs_ref[pl.ds(pl.multiple_of(r * RED_ROWS, RED_ROWS), RED_ROWS), LANES:2 * LANES] = (
            jnp.where(sel, on, -MASK_OFF).astype(BF16))
        return carry

    lax.fori_loop(0, slabs, write, 0)


def _attend(nk, qt_ref, qcols, vt_ref, rhs_ref, s_ref, p_ref):
    r_k, c_q = _tile_iotas()
    feat = lax.broadcasted_iota(jnp.int32, (HALF, Q_BLOCK), 0)
    group = HEADS // 2
    width = group * Q_BLOCK

    def score_operand(hh):
        slope = 2.0 ** (-(hh + 1))
        alibi = jnp.where(feat == 0, CHUNK * slope, jnp.where(feat == 1, slope, 0.0)).astype(BF16)
        scaled_ident = jnp.where(r_k == c_q, slope, 0.0).astype(BF16)
        return jnp.concatenate([qt_ref[hh * HEAD_DIM:(hh + 1) * HEAD_DIM, qcols], alibi, scaled_ident], axis=0)

    def scores(g):
        lhs_t = jnp.concatenate([score_operand(g * group + i) for i in range(group)], axis=1)
        m_acc = jnp.full((RED_ROWS, width), -jnp.inf, F32)
        for rows in _row_blocks(nk, DOT_ROWS):
            blk = jnp.dot(rhs_ref[rows, :], lhs_t, preferred_element_type=F32)
            s_ref[g, rows, :] = blk
            for sub in range((rows.stop - rows.start) // RED_ROWS):
                m_acc = jnp.maximum(m_acc, blk[sub * RED_ROWS:(sub + 1) * RED_ROWS, :])
        return jnp.max(m_acc, axis=0, keepdims=True)

    def probabilities(g, m):
        l_acc = jnp.zeros((RED_ROWS, width), F32)
        for r in range(nk // RED_ROWS):
            rows = slice(r * RED_ROWS, (r + 1) * RED_ROWS)
            p = jnp.exp(s_ref[g, rows, :] - m)
            l_acc = l_acc + p
            p_ref[g, rows, :] = p.astype(BF16)
        return _colsum(l_acc)

    def values(g, l):
        return jnp.dot(vt_ref[:, 0:nk], p_ref[g, 0:nk, :], preferred_element_type=F32) * (1.0 / l)

    m0 = scores(0)
    m1 = scores(1)
    outs = [values(0, probabilities(0, m0)), values(1, probabilities(1, m1))]
    tiles = []
    for t in range(HEADS // 2):
        o_g = outs[(2 * t) // group]
        a = (2 * t) % group
        tiles.append(jnp.concatenate([o_g[:, a * Q_BLOCK:(a + 1) * Q_BLOCK],
                                      o_g[:, (a + 1) * Q_BLOCK:(a + 2) * Q_BLOCK]], axis=0).T)
    return tiles


def _attn_kernel(qt_ref, qit_ref, wit_ref, kaug_ref, kk_ref, vt_ref, ma_ref, gate_ref, x_ref, gb_ref,
                 wo_ref, y_ref, idx_ref, rhs_ref, s_ref, p_ref, st_ref, *, seq, topk):
    jj = pl.program_id(1)
    kf = float(topk)
    steps = seq // (PAIR * Q_BLOCK)
    qcols = [slice(u * Q_BLOCK, (u + 1) * Q_BLOCK) for u in range(PAIR)]

    @pl.when(jj == 0)
    def _():
        for u in range(PAIR):
            rhs_ref[u, :, 0:LANES] = kaug_ref[...]

    def extent(m, u):
        return (PAIR * m + u + 1) * Q_BLOCK

    for m in range(steps):
        @pl.when(jj == m)
        def _(m=m):
            lane_q = lax.broadcasted_iota(jnp.int32, (1, Q_BLOCK), 1)
            states = []
            for u in range(PAIR):
                nk = extent(m, u)
                lo, hi = _index_scores(nk, qit_ref, qcols[u], wit_ref[:, qcols[u]], kk_ref, s_ref, idx_ref.at[u])
                n_adm = jnp.where(lane_q >= CHUNK, float(nk), float(nk - CHUNK))
                c_lo = jnp.where(n_adm <= kf, kf, n_adm)
                states.append((lo, hi, c_lo, jnp.zeros((1, Q_BLOCK), F32)))

            def cond(carry):
                it, flat = carry[0], carry[1:]
                still = jnp.logical_or(flat[2] != kf, flat[6] != kf)
                return jnp.logical_and(it < FAST_TRIPS, _any_lane(still))

            def body(carry):
                it, flat = carry[0], carry[1:]
                st = [flat[0:4], flat[4:8]]
                for _ in range(2):
                    st = [_probe(extent(m, u), kf, idx_ref.at[u], st[u]) for u in range(PAIR)]
                return (it + 1,) + tuple(st[0]) + tuple(st[1])

            final = lax.while_loop(cond, body, (jnp.int32(0),) + tuple(states[0]) + tuple(states[1]))[1:]
            for u in range(PAIR):
                lo, hi, c_lo, _ = final[4 * u:4 * u + 4]
                _write_selection(extent(m, u), lambda s, krow, lo=lo: s >= lo, idx_ref.at[u], rhs_ref.at[u])
                st_ref[4 * u + 0:4 * u + 1, :] = lo
                st_ref[4 * u + 1:4 * u + 2, :] = hi
                st_ref[4 * u + 2:4 * u + 3, :] = c_lo

    for u in range(PAIR):
        lo = st_ref[4 * u + 0:4 * u + 1, :]
        hi = st_ref[4 * u + 1:4 * u + 2, :]
        c_lo = st_ref[4 * u + 2:4 * u + 3, :]

        @pl.when(_any_lane(c_lo != kf))
        def _(u=u, lo=lo, hi=hi):
            _exact_fallback(PAIR * jj + u + 1, kf, lo, hi, idx_ref.at[u], rhs_ref.at[u])

    lane = lax.broadcasted_iota(jnp.int32, (PAIR * Q_BLOCK, LANES), 1)
    lo_half = lane < HALF
    for m in range(steps):
        @pl.when(jj == m)
        def _(m=m):
            per_block = [_attend(extent(m, u), qt_ref, qcols[u], vt_ref, rhs_ref.at[u], s_ref, p_ref)
                         for u in range(PAIR)]
            y = x_ref[...] + jnp.dot(ma_ref[...], wo_ref[0:A_WIDTH, :], preferred_element_type=F32)
            for t in range(HEADS // 2):
                cols = slice(t * LANES, (t + 1) * LANES)
                o = jnp.concatenate([per_block[u][t] for u in range(PAIR)], axis=0)
                msq = _half_mean_sq(o * o, lo_half)
                ob = o * lax.rsqrt(msq + NORM_EPS) * gb_ref[:, cols] * gate_ref[:, cols].astype(F32)
                y = y + jnp.dot(ob.astype(BF16), wo_ref[A_WIDTH + t * LANES:A_WIDTH + (t + 1) * LANES, :],
                                preferred_element_type=F32)
            y_ref[...] = y


def kernel(x, norm_gain, w_in, sgu_norm_gain, sgu_w, sgu_b, q_norm_gain, k_norm_gain,
           idx_k_norm_gain, branch_norm_gain, w_out):
    bsz, seq, d_model = x.shape
    assert d_model == D_MODEL and norm_gain.shape[0] == 1
    assert seq % PROJ_ROWS == 0 and seq % (PAIR * Q_BLOCK) == 0
    tokens = bsz * seq
    topk = min(TOPK_MAX, seq // 4)
    idx_w_scale = (IDX_HEADS ** -0.5) * (IDX_DIM ** -0.5)

    w = w_in[0]
    a3 = 3 * A_WIDTH
    w_q = w[:, a3:a3 + B_WIDTH]
    w_k = w[:, a3 + B_WIDTH:a3 + B_WIDTH + HEAD_DIM]
    w_v = w[:, a3 + B_WIDTH + HEAD_DIM:a3 + B_WIDTH + 2 * HEAD_DIM]
    o_g = a3 + B_WIDTH + 2 * HEAD_DIM
    w_g = w[:, o_g:o_g + B_WIDTH]
    o_i = o_g + B_WIDTH
    w_iq = w[:, o_i:o_i + IDX_HEADS * IDX_DIM]
    w_ik = w[:, o_i + IDX_HEADS * IDX_DIM:o_i + IDX_HEADS * IDX_DIM + IDX_DIM]
    w_iw = w[:, o_i + IDX_HEADS * IDX_DIM + IDX_DIM:]
    w_main = jnp.concatenate([w[:, :a3], w_g, w_k, w_ik], axis=1).astype(BF16)
    assert w_main.shape[1] == _PACKED_COLS
    w_t = jnp.concatenate([w_q.T, w_iq.T, w_v.T, w_iw.T,
                           jnp.zeros((_PACKED_ROWS - _ROW_W - IDX_HEADS, D_MODEL), F32)], axis=0).astype(BF16)
    x2 = x.reshape(tokens, D_MODEL)
    ng = norm_gain[0].reshape(1, D_MODEL)
    sgn = sgu_norm_gain[0].reshape(1, A_WIDTH)
    sw = sgu_w[0]
    sb = sgu_b[0].reshape(A_GROUPS, A_BLOCK, 1)
    qg = q_norm_gain[0].reshape(HEAD_DIM, 1)
    kg = jnp.concatenate([k_norm_gain[0], idx_k_norm_gain[0]]).reshape(1, LANES)
    ga = branch_norm_gain[0, :A_WIDTH].reshape(1, A_WIDTH)
    gb = branch_norm_gain[0, A_WIDTH:].reshape(1, B_WIDTH)
    wo = w_out[0].astype(BF16)

    tm = PROJ_ROWS
    full = lambda shape: pl.BlockSpec(shape, lambda i: (0,) * len(shape))
    rows = lambda width: pl.BlockSpec((tm, width), lambda i: (i, 0))
    colsT = lambda height: pl.BlockSpec((height, tm), lambda i: (0, i))
    outs = pl.pallas_call(
        functools.partial(_proj_kernel, tiles_per_seq=seq // tm, idx_w_scale=idx_w_scale),
        grid=(tokens // tm,),
        in_specs=[rows(D_MODEL), full((1, D_MODEL)), full((D_MODEL, _PACKED_COLS)),
                  full((_PACKED_ROWS, D_MODEL)), full((1, A_WIDTH)), full((A_GROUPS, A_BLOCK, A_BLOCK)),
                  full((A_GROUPS, A_BLOCK, 1)), full((HEAD_DIM, 1)), full((1, LANES)),
                  full((1, A_WIDTH))],
        out_specs=[rows(A_WIDTH), rows(B_WIDTH), rows(LANES), rows(LANES),
                   colsT(B_WIDTH), colsT(IDX_HEADS * IDX_DIM), colsT(HEAD_DIM), colsT(IDX_HEADS)],
        out_shape=[jax.ShapeDtypeStruct((tokens, A_WIDTH), BF16),
                   jax.ShapeDtypeStruct((tokens, B_WIDTH), BF16),
                   jax.ShapeDtypeStruct((tokens, LANES), BF16),
                   jax.ShapeDtypeStruct((tokens, LANES), BF16),
                   jax.ShapeDtypeStruct((B_WIDTH, tokens), BF16),
                   jax.ShapeDtypeStruct((IDX_HEADS * IDX_DIM, tokens), BF16),
                   jax.ShapeDtypeStruct((HEAD_DIM, tokens), BF16),
                   jax.ShapeDtypeStruct((IDX_HEADS, tokens), F32)],
        compiler_params=pltpu.CompilerParams(dimension_semantics=("arbitrary",),
                                             vmem_limit_bytes=VMEM_LIMIT),
        name="proj_sgu",
    )(x2, ng, w_main, w_t, sgn, sw, sb, qg, kg, ga)
    ma, gate, kaug, kk, qt, qit, vt, wit = outs

    qrows = PAIR * Q_BLOCK
    steps = seq // qrows
    qblk = lambda width: pl.BlockSpec((qrows, width), lambda b, j: (b * steps + j, 0))
    qblkT = lambda height: pl.BlockSpec((height, qrows), lambda b, j: (0, b * steps + j))
    const = lambda shape: pl.BlockSpec(shape, lambda b, j: (0,) * len(shape))
    y = pl.pallas_call(
        functools.partial(_attn_kernel, seq=seq, topk=topk),
        grid=(bsz, steps),
        in_specs=[qblkT(B_WIDTH), qblkT(IDX_HEADS * IDX_DIM), qblkT(IDX_HEADS),
                  pl.BlockSpec((seq, LANES), lambda b, j: (b, 0)),
                  pl.BlockSpec((seq, LANES), lambda b, j: (b, 0)),
                  pl.BlockSpec((HEAD_DIM, seq), lambda b, j: (0, b)),
                  qblk(A_WIDTH), qblk(B_WIDTH), qblk(D_MODEL),
                  const((1, B_WIDTH)), const((D_MODEL, D_MODEL))],
        out_specs=qblk(D_MODEL),
        out_shape=jax.ShapeDtypeStruct((tokens, D_MODEL), F32),
        scratch_shapes=[pltpu.VMEM((PAIR, seq, Q_BLOCK), F32),
                        pltpu.VMEM((PAIR, seq, 2 * LANES), BF16),
                        pltpu.VMEM((2, seq, (HEADS // 2) * Q_BLOCK), F32),
                        pltpu.VMEM((2, seq, (HEADS // 2) * Q_BLOCK), BF16),
                        pltpu.VMEM((8, Q_BLOCK), F32)],
        compiler_params=pltpu.CompilerParams(dimension_semantics=("arbitrary", "arbitrary"),
                                             vmem_limit_bytes=VMEM_LIMIT),
        name="dsa_attn_out",
    )(qt, qit, wit, kaug, kk, vt, ma, gate, x2, gb, wo)
    return y.reshape(bsz, seq, D_MODEL)
```

```python
import functools

import jax
import jax.numpy as jnp
from jax import lax
from jax.experimental import pallas as pl
from jax.experimental.pallas import tpu as pltpu

F32 = jnp.float32
BF16 = jnp.bfloat16

D_MODEL = 1024
CHUNK = 64
A_WIDTH = 512
A_GROUPS = 4
A_BLOCK = 128
HEADS = 8
HEAD_DIM = 64
B_WIDTH = HEADS * HEAD_DIM
IDX_HEADS = 8
IDX_DIM = 64
TOPK_MAX = 256
Q_BLOCK = 128
PAIR = 2
NORM_EPS = 1e-6
MASK_OFF = 1e32
LANES = 128
HALF = LANES // 2
PROJ_ROWS = 512
DOT_ROWS = 512
RED_ROWS = 128
KEY_BUCKET = 512
FAST_TRIPS = 24
INTERP_MARGIN = 0.02
SLOW_TRIPS = 70
VMEM_LIMIT = 48 * 1024 * 1024

_OFF_U, _OFF_V, _OFF_Z, _OFF_G, _OFF_K = 0, 512, 1024, 1536, 2048
_PACKED_COLS = 2176
_ROW_Q, _ROW_QI, _ROW_V, _ROW_W, _PACKED_ROWS = 0, 512, 1024, 1088, 1152

_NT = (((1,), (1,)), ((), ()))


def _gelu(x):
    c = 0.7978845608028654
    return 0.5 * x * (1.0 + jnp.tanh(c * (x + 0.044715 * (x * x * x))))


def _silu(x):
    return x / (1.0 + jnp.exp(-x))


def _row_blocks(total, size):
    return [slice(r, min(r + size, total)) for r in range(0, total, size)]


def _half_mean_sq(x2, lo_half):
    tot = jnp.sum(x2, axis=-1, keepdims=True)
    lo = jnp.sum(jnp.where(lo_half, x2, 0.0), axis=-1, keepdims=True)
    return jnp.where(lo_half, lo, tot - lo) * (1.0 / HALF)


def _proj_kernel(x_ref, ng_ref, w_ref, wt_ref, sgn_ref, sw_ref, sb_ref, qg_ref, kg_ref, ga_ref,
                 ma_ref, gate_ref, kaug_ref, kk_ref, qt_ref, qit_ref, vt_ref, wit_ref,
                 *, tiles_per_seq, idx_w_scale):
    tm = x_ref.shape[0]
    i = pl.program_id(0)
    x = x_ref[...]
    ms = jnp.mean(x * x, axis=-1, keepdims=True)
    h = (x * lax.rsqrt(ms + NORM_EPS) * ng_ref[...]).astype(BF16)

    lane = lax.broadcasted_iota(jnp.int32, (tm, LANES), 1)
    lo_half = lane < HALF

    def proj(off, width):
        return jnp.dot(h, w_ref[:, off:off + width], preferred_element_type=F32)

    gu = _gelu(proj(_OFF_U, A_WIDTH))
    gv = _gelu(proj(_OFF_V, A_WIDTH))
    pz = proj(_OFF_Z, A_WIDTH)
    r_i = lax.broadcasted_iota(jnp.int32, (A_BLOCK, A_BLOCK), 0)
    c_j = lax.broadcasted_iota(jnp.int32, (A_BLOCK, A_BLOCK), 1)
    causal = lax.shift_right_logical(c_j, 6) <= lax.shift_right_logical(r_i, 6)
    for g in range(A_GROUPS):
        cols = slice(g * LANES, (g + 1) * LANES)
        vg = gv[:, cols]
        mu = jnp.mean(vg, axis=-1, keepdims=True)
        d = vg - mu
        var = jnp.mean(d * d, axis=-1, keepdims=True)
        vn = (d * lax.rsqrt(var + NORM_EPS) * sgn_ref[:, cols]).astype(BF16)
        wg = jnp.where(causal, sw_ref[g], 0.0).astype(BF16)
        for blk in range(tm // A_BLOCK):
            rows = slice(blk * A_BLOCK, (blk + 1) * A_BLOCK)
            s = jnp.dot(wg, vn[rows, :], preferred_element_type=F32) + sb_ref[g]
            ya = gu[rows, cols] * s
            oa = ya * lax.rsqrt(jnp.mean(ya * ya, axis=-1, keepdims=True) + NORM_EPS) * ga_ref[:, cols]
            ma_ref[rows, cols] = (oa * _silu(pz[rows, cols])).astype(BF16)

    gate_ref[...] = _silu(proj(_OFF_G, B_WIDTH)).astype(BF16)

    pk = proj(_OFF_K, LANES)
    k_ms = jnp.sum(jnp.where(lo_half, pk * pk, 0.0), axis=-1, keepdims=True) * (1.0 / HALF)
    kn = pk * lax.rsqrt(k_ms + NORM_EPS)
    ik_mu = jnp.sum(jnp.where(lo_half, 0.0, pk), axis=-1, keepdims=True) * (1.0 / HALF)
    dk = pk - ik_mu
    ik_var = jnp.sum(jnp.where(lo_half, 0.0, dk * dk), axis=-1, keepdims=True) * (1.0 / HALF)
    kin = dk * lax.rsqrt(ik_var + NORM_EPS)
    tile = jnp.where(lo_half, kn, kin) * kg_ref[...]
    swapped = pltpu.roll(tile, HALF, axis=1)
    row = lax.broadcasted_iota(jnp.int32, (tm, LANES), 0)
    pos = (i % tiles_per_seq) * tm + row
    pos_hi = lax.shift_right_logical(pos, 6).astype(F32)
    pos_lo = (pos & (CHUNK - 1)).astype(F32)
    posfeat = jnp.where(lane == HALF, pos_hi, jnp.where(lane == HALF + 1, pos_lo, 0.0))
    kaug_ref[...] = jnp.where(lo_half, tile, posfeat).astype(BF16)
    kk_ref[...] = jnp.where(lo_half, swapped, tile).astype(BF16)

    pt = lax.dot_general(wt_ref[...], h, _NT, preferred_element_type=F32)
    for hh in range(HEADS):
        rows = slice(_ROW_Q + hh * HEAD_DIM, _ROW_Q + (hh + 1) * HEAD_DIM)
        xq = pt[rows, :]
        msq = jnp.mean(xq * xq, axis=0, keepdims=True)
        qt_ref[rows, :] = (xq * lax.rsqrt(msq + NORM_EPS) * qg_ref[...] * (HEAD_DIM ** -0.5)).astype(BF16)
    qit_ref[...] = pt[_ROW_QI:_ROW_QI + IDX_HEADS * IDX_DIM, :].astype(BF16)
    vt_ref[...] = pt[_ROW_V:_ROW_V + HEAD_DIM, :].astype(BF16)
    wit_ref[...] = pt[_ROW_W:_ROW_W + IDX_HEADS, :] * idx_w_scale


def _tile_iotas():
    r_k = lax.broadcasted_iota(jnp.int32, (Q_BLOCK, Q_BLOCK), 0)
    c_q = lax.broadcasted_iota(jnp.int32, (Q_BLOCK, Q_BLOCK), 1)
    return r_k, c_q


def _colsum(x):
    return jnp.sum(x, axis=0, keepdims=True)


def _any_lane(pred):
    return jnp.max(jnp.where(pred, 1.0, 0.0)) > 0.0


def _index_scores(nk, n_adm, qit_ref, qcols, wi, kk_ref, s_ref, idx_ref):
    r_k, c_q = _tile_iotas()
    top_rows = r_k < HALF
    per_head = []
    for t in range(IDX_HEADS // 2):
        qit = qit_ref[t * LANES:(t + 1) * LANES, qcols]
        zero = jnp.zeros_like(qit)
        per_head += [jnp.where(top_rows, qit, zero), jnp.where(top_rows, zero, qit)]
    half_heads = IDX_HEADS // 2
    for g in range(2):
        wg = jnp.concatenate(per_head[g * half_heads:(g + 1) * half_heads], axis=1)
        for rows in _row_blocks(nk, DOT_ROWS):
            s_ref[g, rows, :] = jnp.dot(kk_ref[rows, :], wg, preferred_element_type=F32)

    def weighted_relu(g, rows):
        acc = None
        for i in range(half_heads):
            hh = g * half_heads + i
            term = jnp.maximum(s_ref[g, rows, i * Q_BLOCK:(i + 1) * Q_BLOCK], 0.0) * wi[hh:hh + 1, :]
            acc = term if acc is None else acc + term
        return acc

    slabs = nk // RED_ROWS
    mn_acc = jnp.full((RED_ROWS, Q_BLOCK), jnp.inf, F32)
    mx_acc = jnp.full((RED_ROWS, Q_BLOCK), -jnp.inf, F32)
    tiny_acc = jnp.full((RED_ROWS, Q_BLOCK), jnp.inf, F32)
    for r in range(slabs):
        rows = slice(r * RED_ROWS, (r + 1) * RED_ROWS)
        acc = weighted_relu(0, rows) + weighted_relu(1, rows)
        if r >= slabs - KEY_BUCKET // RED_ROWS:
            adm = (r * RED_ROWS + r_k) < n_adm
            lo_fill = jnp.where(adm, acc, -jnp.inf)
            hi_fill = jnp.where(adm, acc, jnp.inf)
        else:
            lo_fill = hi_fill = acc
        mag = jnp.abs(hi_fill)
        idx_ref[rows, :] = lo_fill
        mn_acc = jnp.minimum(mn_acc, hi_fill)
        mx_acc = jnp.maximum(mx_acc, lo_fill)
        tiny_acc = jnp.minimum(tiny_acc, jnp.where(mag == 0.0, jnp.inf, mag))
    lo = jnp.min(mn_acc, axis=0, keepdims=True)
    hi = jnp.max(mx_acc, axis=0, keepdims=True)
    tiny = jnp.min(tiny_acc, axis=0, keepdims=True)

    unit = jnp.where(tiny < jnp.inf, tiny, 1.0)
    eps = unit * (0.5 / nk)
    rank0 = (1 + r_k).astype(F32)

    def spread(r, carry):
        rows = pl.ds(pl.multiple_of(r * RED_ROWS, RED_ROWS), RED_ROWS)
        s = idx_ref[rows, :]
        rank = rank0 + (r * RED_ROWS).astype(F32)
        idx_ref[rows, :] = jnp.where(s == 0.0, -(rank * eps), s)
        return carry

    lax.fori_loop(0, slabs, spread, 0)
    return jnp.minimum(lo, -0.5 * unit), hi


def _probe(nk, kf, idx_ref, state):
    lo, hi, c_lo, c_hi = state
    frac = (c_lo - kf) / jnp.maximum(c_lo - c_hi, 1.0)
    frac = jnp.minimum(jnp.maximum(frac, INTERP_MARGIN), 1.0 - INTERP_MARGIN)
    t = lo + (hi - lo) * frac
    acc = jnp.zeros((RED_ROWS, Q_BLOCK), F32)
    for r in range(nk // RED_ROWS):
        acc = acc + jnp.where(idx_ref[r * RED_ROWS:(r + 1) * RED_ROWS, :] >= t, 1.0, 0.0)
    c = _colsum(acc)
    ge = c >= kf
    return (jnp.where(ge, t, lo), jnp.where(ge, hi, t), jnp.where(ge, c, c_lo), jnp.where(ge, c_hi, c))


def _write_selection(nk, j_blk, lo, idx_ref, rhs_ref):
    r_k, c_q = _tile_iotas()
    later = 2.0 * jnp.maximum(r_k - c_q, 0).astype(F32)

    def write(r, carry):
        rows = pl.ds(pl.multiple_of(r * RED_ROWS, RED_ROWS), RED_ROWS)
        on = jnp.where(r == j_blk, -later, 0.0)
        rhs_ref[rows, LANES:2 * LANES] = jnp.where(idx_ref[rows, :] >= lo, on, -MASK_OFF).astype(BF16)
        return carry

    lax.fori_loop(0, nk // RED_ROWS, write, 0)


def _exact_fallback(slabs, kf, lo, hi, idx_ref, rhs_ref):
    r_k, c_q = _tile_iotas()
    later = 2.0 * jnp.maximum(r_k - c_q, 0).astype(F32)

    def slab(r):
        return idx_ref[pl.ds(pl.multiple_of(r * RED_ROWS, RED_ROWS), RED_ROWS), :]

    def count(pred):
        def body(r, acc):
            return acc + jnp.where(pred(slab(r), r), 1.0, 0.0)
        return _colsum(lax.fori_loop(0, slabs, body, jnp.zeros((RED_ROWS, Q_BLOCK), F32)))

    def kth(lo):
        def body(r, acc):
            s = slab(r)
            return jnp.minimum(acc, jnp.where(s >= lo, s, jnp.inf))
        acc = lax.fori_loop(0, slabs, body, jnp.full((RED_ROWS, Q_BLOCK), jnp.inf, F32))
        thr = jnp.min(acc, axis=0, keepdims=True)
        return thr, count(lambda s, r: s > thr)

    def slow_cond(carry):
        it, _, _, _, c_gt = carry
        return jnp.logical_and(it < SLOW_TRIPS, _any_lane(c_gt >= kf))

    def slow_body(carry):
        it, lo, hi, _, _ = carry
        for _ in range(4):
            mid = 0.5 * lo + 0.5 * hi
            ge = count(lambda s, r: s >= mid) >= kf
            lo, hi = jnp.where(ge, mid, lo), jnp.where(ge, hi, mid)
        thr, c_gt = kth(lo)
        return it + 1, lo, hi, thr, c_gt

    thr0, c_gt0 = kth(lo)
    _, _, _, thr, c_gt = lax.while_loop(slow_cond, slow_body, (jnp.int32(0), lo, hi, thr0, c_gt0))
    c_eq = count(lambda s, r: s == thr)
    need = (c_gt + c_eq) > kf

    def tie_step(_, carry):
        lo_i, hi_i = carry
        mid_i = jnp.floor((lo_i + hi_i) * 0.5)
        below = count(lambda s, r: jnp.logical_and(s == thr, (r * RED_ROWS + r_k).astype(F32) <= mid_i))
        ok = (c_gt + below) >= kf
        return jnp.where(ok, lo_i, mid_i), jnp.where(ok, mid_i, hi_i)

    last = (slabs * RED_ROWS - 1).astype(F32)
    lo_i = jnp.full((1, Q_BLOCK), -1.0, F32)
    hi_i = jnp.zeros((1, Q_BLOCK), F32) + last
    _, cut = lax.fori_loop(0, 13, tie_step, (lo_i, hi_i))
    cut = jnp.where(need, cut, last + 1.0).astype(jnp.int32)

    def write(r, carry):
        s = slab(r)
        krow = r * RED_ROWS + r_k
        sel = jnp.logical_or(s > thr, jnp.logical_and(s == thr, krow <= cut))
        on = jnp.where(r == slabs - 1, -later, 0.0)
        rhs_ref[pl.ds(pl.multiple_of(r * RED_ROWS, RED_ROWS), RED_ROWS), LANES:2 * LANES] = (
            jnp.where(sel, on, -MASK_OFF).astype(BF16))
        return carry

    lax.fori_loop(0, slabs, write, 0)


def _attend(nk, qt_ref, qcols, vt_ref, rhs_ref, s_ref, p_ref):
    r_k, c_q = _tile_iotas()
    feat = lax.broadcasted_iota(jnp.int32, (HALF, Q_BLOCK), 0)
    group = HEADS // 2
    width = group * Q_BLOCK

    def score_operand(hh):
        slope = 2.0 ** (-(hh + 1))
        alibi = jnp.where(feat == 0, CHUNK * slope, jnp.where(feat == 1, slope, 0.0)).astype(BF16)
        scaled_ident = jnp.where(r_k == c_q, slope, 0.0).astype(BF16)
        return jnp.concatenate([qt_ref[hh * HEAD_DIM:(hh + 1) * HEAD_DIM, qcols], alibi, scaled_ident], axis=0)

    def scores(g):
        lhs_t = jnp.concatenate([score_operand(g * group + i) for i in range(group)], axis=1)
        m_acc = jnp.full((RED_ROWS, width), -jnp.inf, F32)
        for rows in _row_blocks(nk, DOT_ROWS):
            blk = jnp.dot(rhs_ref[rows, :], lhs_t, preferred_element_type=F32)
            s_ref[g, rows, :] = blk
            for sub in range((rows.stop - rows.start) // RED_ROWS):
                m_acc = jnp.maximum(m_acc, blk[sub * RED_ROWS:(sub + 1) * RED_ROWS, :])
        return jnp.max(m_acc, axis=0, keepdims=True)

    def probabilities(g, m):
        step = 2 * RED_ROWS

        def body(r, l_acc):
            rows = pl.ds(pl.multiple_of(r * step, step), step)
            p = jnp.exp(s_ref[g, rows, :] - m)
            p_ref[g, rows, :] = p.astype(BF16)
            parts = [p[i * 8:(i + 1) * 8, :] for i in range(step // 8)]
            while len(parts) > 1:
                parts = [parts[i] + parts[i + 1] for i in range(0, len(parts), 2)]
            return l_acc + parts[0]

        return _colsum(lax.fori_loop(0, nk // step, body, jnp.zeros((8, width), F32)))

    def values(g, l):
        return jnp.dot(vt_ref[:, 0:nk], p_ref[g, 0:nk, :], preferred_element_type=F32) * (1.0 / l)

    m0 = scores(0)
    m1 = scores(1)
    outs = [values(0, probabilities(0, m0)), values(1, probabilities(1, m1))]
    tiles = []
    for t in range(HEADS // 2):
        o_g = outs[(2 * t) // group]
        a = (2 * t) % group
        tiles.append(jnp.concatenate([o_g[:, a * Q_BLOCK:(a + 1) * Q_BLOCK],
                                      o_g[:, (a + 1) * Q_BLOCK:(a + 2) * Q_BLOCK]], axis=0).T)
    return tiles


def _attn_kernel(qt_ref, qit_ref, wit_ref, kaug_ref, kk_ref, vt_ref, ma_ref, gate_ref, x_ref, gb_ref,
                 wo_ref, y_ref, idx_ref, rhs_ref, s_ref, p_ref, st_ref, *, seq, topk):
    jj = pl.program_id(1)
    kf = float(topk)
    qcols =[slice(u * Q_BLOCK, (u + 1) * Q_BLOCK) for u in range(PAIR)]

    @pl.when(jj == 0)
    def _():
        for u in range(PAIR):
            rhs_ref[u, :, 0:LANES] = kaug_ref[...]

    steps_per_bucket = KEY_BUCKET // (PAIR * Q_BLOCK)
    buckets = [((n + 1) * KEY_BUCKET, jnp.logical_and(jj >= n * steps_per_bucket, jj < (n + 1) * steps_per_bucket))
               for n in range(seq // KEY_BUCKET)]
    blocks = [PAIR * jj + u for u in range(PAIR)]

    for nk, here in buckets:
        @pl.when(here)
        def _(nk=nk):
            lane_q = lax.broadcasted_iota(jnp.int32, (1, Q_BLOCK), 1)
            states = []
            for u in range(PAIR):
                n_adm = blocks[u] * Q_BLOCK + CHUNK + CHUNK * (lane_q >= CHUNK).astype(jnp.int32)
                lo, hi = _index_scores(nk, n_adm, qit_ref, qcols[u], wit_ref[:, qcols[u]], kk_ref, s_ref,
                                       idx_ref.at[u])
                n_adm_f = n_adm.astype(F32)
                c_lo = jnp.where(n_adm_f <= kf, kf, n_adm_f)
                states.append((lo, hi, c_lo, jnp.zeros((1, Q_BLOCK), F32)))

            def cond(carry):
                it, flat = carry[0], carry[1:]
                still = jnp.logical_or(flat[2] != kf, flat[6] != kf)
                return jnp.logical_and(it < FAST_TRIPS, _any_lane(still))

            def body(carry):
                it, flat = carry[0], carry[1:]
                st = [flat[0:4], flat[4:8]]
                for _ in range(2):
                    st = [_probe(nk, kf, idx_ref.at[u], st[u]) for u in range(PAIR)]
                return (it + 1,) + tuple(st[0]) + tuple(st[1])

            final = lax.while_loop(cond, body, (jnp.int32(0),) + tuple(states[0]) + tuple(states[1]))[1:]
            for u in range(PAIR):
                lo, hi, c_lo, _ = final[4 * u:4 * u + 4]
                _write_selection(nk, blocks[u], lo, idx_ref.at[u], rhs_ref.at[u])
                st_ref[4 * u + 0:4 * u + 1, :] = lo
                st_ref[4 * u + 1:4 * u + 2, :] = hi
                st_ref[4 * u + 2:4 * u + 3, :] = c_lo

    for u in range(PAIR):
        lo = st_ref[4 * u + 0:4 * u + 1, :]
        hi = st_ref[4 * u + 1:4 * u + 2, :]
        c_lo = st_ref[4 * u + 2:4 * u + 3, :]

        @pl.when(_any_lane(c_lo != kf))
        def _(u=u, lo=lo, hi=hi):
            _exact_fallback(blocks[u] + 1, kf, lo, hi, idx_ref.at[u], rhs_ref.at[u])

    lane = lax.broadcasted_iota(jnp.int32, (PAIR * Q_BLOCK, LANES), 1)
    lo_half = lane < HALF
    for nk, here in buckets:
        @pl.when(here)
        def _(nk=nk):
            per_block = [_attend(nk, qt_ref, qcols[u], vt_ref, rhs_ref.at[u], s_ref, p_ref)
                         for u in range(PAIR)]
            y = x_ref[...] + jnp.dot(ma_ref[...], wo_ref[0:A_WIDTH, :], preferred_element_type=F32)
            for t in range(HEADS // 2):
                cols = slice(t * LANES, (t + 1) * LANES)
                o = jnp.concatenate([per_block[u][t] for u in range(PAIR)], axis=0)
                msq = _half_mean_sq(o * o, lo_half)
                ob = o * lax.rsqrt(msq + NORM_EPS) * gb_ref[:, cols] * gate_ref[:, cols].astype(F32)
                y = y + jnp.dot(ob.astype(BF16), wo_ref[A_WIDTH + t * LANES:A_WIDTH + (t + 1) * LANES, :],
                                preferred_element_type=F32)
            y_ref[...] = y


def kernel(x, norm_gain, w_in, sgu_norm_gain, sgu_w, sgu_b, q_norm_gain, k_norm_gain,
           idx_k_norm_gain, branch_norm_gain, w_out):
    bsz, seq, d_model = x.shape
    assert d_model == D_MODEL and norm_gain.shape[0] == 1
    assert seq % PROJ_ROWS == 0 and seq % KEY_BUCKET == 0 and KEY_BUCKET % (PAIR * Q_BLOCK) == 0
    tokens = bsz * seq
    topk = min(TOPK_MAX, seq // 4)
    idx_w_scale = (IDX_HEADS ** -0.5) * (IDX_DIM ** -0.5)

    w = w_in[0]
    a3 = 3 * A_WIDTH
    w_q = w[:, a3:a3 + B_WIDTH]
    w_k = w[:, a3 + B_WIDTH:a3 + B_WIDTH + HEAD_DIM]
    w_v = w[:, a3 + B_WIDTH + HEAD_DIM:a3 + B_WIDTH + 2 * HEAD_DIM]
    o_g = a3 + B_WIDTH + 2 * HEAD_DIM
    w_g = w[:, o_g:o_g + B_WIDTH]
    o_i = o_g + B_WIDTH
    w_iq = w[:, o_i:o_i + IDX_HEADS * IDX_DIM]
    w_ik = w[:, o_i + IDX_HEADS * IDX_DIM:o_i + IDX_HEADS * IDX_DIM + IDX_DIM]
    w_iw = w[:, o_i + IDX_HEADS * IDX_DIM + IDX_DIM:]
    w_main = jnp.concatenate([w[:, :a3], w_g, w_k, w_ik], axis=1).astype(BF16)
    assert w_main.shape[1] == _PACKED_COLS
    w_t = jnp.concatenate([w_q.T, w_iq.T, w_v.T, w_iw.T,
                           jnp.zeros((_PACKED_ROWS - _ROW_W - IDX_HEADS, D_MODEL), F32)], axis=0).astype(BF16)
    x2 = x.reshape(tokens, D_MODEL)
    ng = norm_gain[0].reshape(1, D_MODEL)
    sgn = sgu_norm_gain[0].reshape(1, A_WIDTH)
    sw = sgu_w[0]
    sb = sgu_b[0].reshape(A_GROUPS, A_BLOCK, 1)
    qg = q_norm_gain[0].reshape(HEAD_DIM, 1)
    kg = jnp.concatenate([k_norm_gain[0], idx_k_norm_gain[0]]).reshape(1, LANES)
    ga = branch_norm_gain[0, :A_WIDTH].reshape(1, A_WIDTH)
    gb = branch_norm_gain[0, A_WIDTH:].reshape(1, B_WIDTH)
    wo = w_out[0].astype(BF16)

    tm = PROJ_ROWS
    full = lambda shape: pl.BlockSpec(shape, lambda i: (0,) * len(shape))
    rows = lambda width: pl.BlockSpec((tm, width), lambda i: (i, 0))
    colsT = lambda height: pl.BlockSpec((height, tm), lambda i: (0, i))
    outs = pl.pallas_call(
        functools.partial(_proj_kernel, tiles_per_seq=seq // tm, idx_w_scale=idx_w_scale),
        grid=(tokens // tm,),
        in_specs=[rows(D_MODEL), full((1, D_MODEL)), full((D_MODEL, _PACKED_COLS)),
                  full((_PACKED_ROWS, D_MODEL)), full((1, A_WIDTH)), full((A_GROUPS, A_BLOCK, A_BLOCK)),
                  full((A_GROUPS, A_BLOCK, 1)), full((HEAD_DIM, 1)), full((1, LANES)),
                  full((1, A_WIDTH))],
        out_specs=[rows(A_WIDTH), rows(B_WIDTH), rows(LANES), rows(LANES),
                   colsT(B_WIDTH), colsT(IDX_HEADS * IDX_DIM), colsT(HEAD_DIM), colsT(IDX_HEADS)],
        out_shape=[jax.ShapeDtypeStruct((tokens, A_WIDTH), BF16),
                   jax.ShapeDtypeStruct((tokens, B_WIDTH), BF16),
                   jax.ShapeDtypeStruct((tokens, LANES), BF16),
                   jax.ShapeDtypeStruct((tokens, LANES), BF16),
                   jax.ShapeDtypeStruct((B_WIDTH, tokens), BF16),
                   jax.ShapeDtypeStruct((IDX_HEADS * IDX_DIM, tokens), BF16),
                   jax.ShapeDtypeStruct((HEAD_DIM, tokens), BF16),
                   jax.ShapeDtypeStruct((IDX_HEADS, tokens), F32)],
        compiler_params=pltpu.CompilerParams(dimension_semantics=("arbitrary",),
                                             vmem_limit_bytes=VMEM_LIMIT),
        name="proj_sgu",
    )(x2, ng, w_main, w_t, sgn, sw, sb, qg, kg, ga)
    ma, gate, kaug, kk, qt, qit, vt, wit = outs

    qrows = PAIR * Q_BLOCK
    steps = seq // qrows
    qblk = lambda width: pl.BlockSpec((qrows, width), lambda b, j: (b * steps + j, 0))
    qblkT = lambda height: pl.BlockSpec((height, qrows), lambda b, j: (0, b * steps + j))
    const = lambda shape: pl.BlockSpec(shape, lambda b, j: (0,) * len(shape))
    y = pl.pallas_call(
        functools.partial(_attn_kernel, seq=seq, topk=topk),
        grid=(bsz, steps),
        in_specs=[qblkT(B_WIDTH), qblkT(IDX_HEADS * IDX_DIM), qblkT(IDX_HEADS),
                  pl.BlockSpec((seq, LANES), lambda b, j: (b, 0)),
                  pl.BlockSpec((seq, LANES), lambda b, j: (b, 0)),
                  pl.BlockSpec((HEAD_DIM, seq), lambda b, j: (0, b)),
                  qblk(A_WIDTH), qblk(B_WIDTH), qblk(D_MODEL),
                  const((1, B_WIDTH)), const((D_MODEL, D_MODEL))],
        out_specs=qblk(D_MODEL),
        out_shape=jax.ShapeDtypeStruct((tokens, D_MODEL), F32),
        scratch_shapes=[pltpu.VMEM((PAIR, seq, Q_BLOCK), F32),
                        pltpu.VMEM((PAIR, seq, 2 * LANES), BF16),
                        pltpu.VMEM((2, seq, (HEADS // 2) * Q_BLOCK), F32),
                        pltpu.VMEM((2, seq, (HEADS // 2) * Q_BLOCK), BF16),
                        pltpu.VMEM((8, Q_BLOCK), F32)],
        compiler_params=pltpu.CompilerParams(dimension_semantics=("arbitrary", "arbitrary"),
                                             vmem_limit_bytes=VMEM_LIMIT),
        name="dsa_attn_out",
    )(qt, qit, wit, kaug, kk, vt, ma, gate, x2, gb, wo)
    return y.reshape(bsz, seq, D_MODEL)
```

```python
import functools

import jax
import jax.numpy as jnp
from jax import lax
from jax.experimental import pallas as pl
from jax.experimental.pallas import tpu as pltpu

F32 = jnp.float32
BF16 = jnp.bfloat16

D_MODEL = 1024
CHUNK = 64
A_WIDTH = 512
A_GROUPS = 4
A_BLOCK = 128
HEADS = 8
HEAD_DIM = 64
B_WIDTH = HEADS * HEAD_DIM
IDX_HEADS = 8
IDX_DIM = 64
TOPK_MAX = 256
Q_BLOCK = 128
PAIR = 2
NORM_EPS = 1e-6
MASK_OFF = 1e32
LANES = 128
HALF = LANES // 2
PROJ_ROWS = 512
DOT_ROWS = 512
RED_ROWS = 128
KEY_BUCKET = 512
FAST_TRIPS = 24
INTERP_MARGIN = 0.02
SLOW_TRIPS = 70
VMEM_LIMIT = 48 * 1024 * 1024

_OFF_U, _OFF_V, _OFF_Z, _OFF_G, _OFF_K = 0, 512, 1024, 1536, 2048
_PACKED_COLS = 2176
_ROW_Q, _ROW_QI, _ROW_V, _ROW_W, _PACKED_ROWS = 0, 512, 1024, 1088, 1152

_NT = (((1,), (1,)), ((), ()))


def _gelu(x):
    c = 0.7978845608028654
    return 0.5 * x * (1.0 + jnp.tanh(c * (x + 0.044715 * (x * x * x))))


def _silu(x):
    return x / (1.0 + jnp.exp(-x))


def _row_blocks(total, size):
    return [slice(r, min(r + size, total)) for r in range(0, total, size)]


def _half_mean_sq(x2, lo_half):
    tot = jnp.sum(x2, axis=-1, keepdims=True)
    lo = jnp.sum(jnp.where(lo_half, x2, 0.0), axis=-1, keepdims=True)
    return jnp.where(lo_half, lo, tot - lo) * (1.0 / HALF)


def _proj_kernel(x_ref, ng_ref, w_ref, wt_ref, sgn_ref, sw_ref, sb_ref, qg_ref, kg_ref, ga_ref,
                 ma_ref, gate_ref, kaug_ref, kk_ref, qt_ref, qit_ref, vt_ref, wit_ref,
                 *, tiles_per_seq, idx_w_scale):
    tm = x_ref.shape[0]
    i = pl.program_id(0)
    x = x_ref[...]
    ms = jnp.mean(x * x, axis=-1, keepdims=True)
    h = (x * lax.rsqrt(ms + NORM_EPS) * ng_ref[...]).astype(BF16)

    lane = lax.broadcasted_iota(jnp.int32, (tm, LANES), 1)
    lo_half = lane < HALF

    def proj(off, width):
        return jnp.dot(h, w_ref[:, off:off + width], preferred_element_type=F32)

    gu = _gelu(proj(_OFF_U, A_WIDTH))
    gv = _gelu(proj(_OFF_V, A_WIDTH))
    pz = proj(_OFF_Z, A_WIDTH)
    r_i = lax.broadcasted_iota(jnp.int32, (A_BLOCK, A_BLOCK), 0)
    c_j = lax.broadcasted_iota(jnp.int32, (A_BLOCK, A_BLOCK), 1)
    causal = lax.shift_right_logical(c_j, 6) <= lax.shift_right_logical(r_i, 6)
    for g in range(A_GROUPS):
        cols = slice(g * LANES, (g + 1) * LANES)
        vg = gv[:, cols]
        mu = jnp.mean(vg, axis=-1, keepdims=True)
        d = vg - mu
        var = jnp.mean(d * d, axis=-1, keepdims=True)
        vn = (d * lax.rsqrt(var + NORM_EPS) * sgn_ref[:, cols]).astype(BF16)
        wg = jnp.where(causal, sw_ref[g], 0.0).astype(BF16)
        for blk in range(tm // A_BLOCK):
            rows = slice(blk * A_BLOCK, (blk + 1) * A_BLOCK)
            s = jnp.dot(wg, vn[rows, :], preferred_element_type=F32) + sb_ref[g]
            ya = gu[rows, cols] * s
            oa = ya * lax.rsqrt(jnp.mean(ya * ya, axis=-1, keepdims=True) + NORM_EPS) * ga_ref[:, cols]
            ma_ref[rows, cols] = (oa * _silu(pz[rows, cols])).astype(BF16)

    gate_ref[...] = _silu(proj(_OFF_G, B_WIDTH)).astype(BF16)

    pk = proj(_OFF_K, LANES)
    k_ms = jnp.sum(jnp.where(lo_half, pk * pk, 0.0), axis=-1, keepdims=True) * (1.0 / HALF)
    kn = pk * lax.rsqrt(k_ms + NORM_EPS)
    ik_mu = jnp.sum(jnp.where(lo_half, 0.0, pk), axis=-1, keepdims=True) * (1.0 / HALF)
    dk = pk - ik_mu
    ik_var = jnp.sum(jnp.where(lo_half, 0.0, dk * dk), axis=-1, keepdims=True) * (1.0 / HALF)
    kin = dk * lax.rsqrt(ik_var + NORM_EPS)
    tile = jnp.where(lo_half, kn, kin) * kg_ref[...]
    swapped = pltpu.roll(tile, HALF, axis=1)
    row = lax.broadcasted_iota(jnp.int32, (tm, LANES), 0)
    pos = (i % tiles_per_seq) * tm + row
    pos_hi = lax.shift_right_logical(pos, 6).astype(F32)
    pos_lo = (pos & (CHUNK - 1)).astype(F32)
    posfeat = jnp.where(lane == HALF, pos_hi, jnp.where(lane == HALF + 1, pos_lo, 0.0))
    kaug_ref[...] = jnp.where(lo_half, tile, posfeat).astype(BF16)
    kk_ref[...] = jnp.where(lo_half, swapped, tile).astype(BF16)

    pt = lax.dot_general(wt_ref[...], h, _NT, preferred_element_type=F32)
    for hh in range(HEADS):
        rows = slice(_ROW_Q + hh * HEAD_DIM, _ROW_Q + (hh + 1) * HEAD_DIM)
        xq = pt[rows, :]
        msq = jnp.mean(xq * xq, axis=0, keepdims=True)
        qt_ref[rows, :] = (xq * lax.rsqrt(msq + NORM_EPS) * qg_ref[...] * (HEAD_DIM ** -0.5)).astype(BF16)
    qit_ref[...] = pt[_ROW_QI:_ROW_QI + IDX_HEADS * IDX_DIM, :].astype(BF16)
    vt_ref[...] = pt[_ROW_V:_ROW_V + HEAD_DIM, :].astype(BF16)
    wit_ref[...] = pt[_ROW_W:_ROW_W + IDX_HEADS, :] * idx_w_scale


def _tile_iotas():
    r_k = lax.broadcasted_iota(jnp.int32, (Q_BLOCK, Q_BLOCK), 0)
    c_q = lax.broadcasted_iota(jnp.int32, (Q_BLOCK, Q_BLOCK), 1)
    return r_k, c_q


def _colsum(x):
    return jnp.sum(x, axis=0, keepdims=True)


def _any_lane(pred):
    return jnp.max(jnp.where(pred, 1.0, 0.0)) > 0.0


def _index_scores(nk, n_adm, qit_ref, qcols, wi, kk_ref, s_ref, idx_ref):
    r_k, c_q = _tile_iotas()
    top_rows = r_k < HALF
    per_head = []
    for t in range(IDX_HEADS // 2):
        qit = qit_ref[t * LANES:(t + 1) * LANES, qcols]
        zero = jnp.zeros_like(qit)
        per_head += [jnp.where(top_rows, qit, zero), jnp.where(top_rows, zero, qit)]
    half_heads = IDX_HEADS // 2
    for g in range(2):
        wg = jnp.concatenate(per_head[g * half_heads:(g + 1) * half_heads], axis=1)
        for rows in _row_blocks(nk, DOT_ROWS):
            s_ref[g, rows, :] = jnp.dot(kk_ref[rows, :], wg, preferred_element_type=F32)

    def weighted_relu(g, rows):
        acc = None
        for i in range(half_heads):
            hh = g * half_heads + i
            term = jnp.maximum(s_ref[g, rows, i * Q_BLOCK:(i + 1) * Q_BLOCK], 0.0) * wi[hh:hh + 1, :]
            acc = term if acc is None else acc + term
        return acc

    slabs = nk // RED_ROWS
    mn_acc = jnp.full((RED_ROWS, Q_BLOCK), jnp.inf, F32)
    mx_acc = jnp.full((RED_ROWS, Q_BLOCK), -jnp.inf, F32)
    tiny_acc = jnp.full((RED_ROWS, Q_BLOCK), jnp.inf, F32)
    for r in range(slabs):
        rows = slice(r * RED_ROWS, (r + 1) * RED_ROWS)
        acc = weighted_relu(0, rows) + weighted_relu(1, rows)
        if r >= slabs - KEY_BUCKET // RED_ROWS:
            adm = (r * RED_ROWS + r_k) < n_adm
            lo_fill = jnp.where(adm, acc, -jnp.inf)
            hi_fill = jnp.where(adm, acc, jnp.inf)
        else:
            lo_fill = hi_fill = acc
        mag = jnp.abs(hi_fill)
        idx_ref[rows, :] = lo_fill
        mn_acc = jnp.minimum(mn_acc, hi_fill)
        mx_acc = jnp.maximum(mx_acc, lo_fill)
        tiny_acc = jnp.minimum(tiny_acc, jnp.where(mag == 0.0, jnp.inf, mag))
    lo = jnp.min(mn_acc, axis=0, keepdims=True)
    hi = jnp.max(mx_acc, axis=0, keepdims=True)
    tiny = jnp.min(tiny_acc, axis=0, keepdims=True)

    unit = jnp.where(tiny < jnp.inf, tiny, 1.0)
    eps = unit * (0.5 / nk)
    rank0 = (1 + r_k).astype(F32)

    def spread(r, carry):
        rows = pl.ds(pl.multiple_of(r * RED_ROWS, RED_ROWS), RED_ROWS)
        s = idx_ref[rows, :]
        rank = rank0 + jnp.asarray(r * RED_ROWS, F32)
        idx_ref[rows, :] = jnp.where(s == 0.0, -(rank * eps), s)
        return carry

    lax.fori_loop(0, slabs, spread, 0)
    return jnp.minimum(lo, -0.5 * unit), hi


def _probe(nk, kf, idx_ref, state):
    lo, hi, c_lo, c_hi = state
    frac = (c_lo - kf) / jnp.maximum(c_lo - c_hi, 1.0)
    frac = jnp.minimum(jnp.maximum(frac, INTERP_MARGIN), 1.0 - INTERP_MARGIN)
    t = lo + (hi - lo) * frac
    acc = jnp.zeros((RED_ROWS, Q_BLOCK), F32)
    for r in range(nk // RED_ROWS):
        acc = acc + jnp.where(idx_ref[r * RED_ROWS:(r + 1) * RED_ROWS, :] >= t, 1.0, 0.0)
    c = _colsum(acc)
    ge = c >= kf
    return (jnp.where(ge, t, lo), jnp.where(ge, hi, t), jnp.where(ge, c, c_lo), jnp.where(ge, c_hi, c))


def _write_selection(nk, j_blk, lo, idx_ref, rhs_ref):
    r_k, c_q = _tile_iotas()
    later = 2.0 * jnp.maximum(r_k - c_q, 0).astype(F32)

    def write(r, carry):
        rows = pl.ds(pl.multiple_of(r * RED_ROWS, RED_ROWS), RED_ROWS)
        on = jnp.where(r == j_blk, -later, 0.0)
        rhs_ref[rows, LANES:2 * LANES] = jnp.where(idx_ref[rows, :] >= lo, on, -MASK_OFF).astype(BF16)
        return carry

    lax.fori_loop(0, nk // RED_ROWS, write, 0)


def _exact_fallback(slabs, kf, lo, hi, idx_ref, rhs_ref):
    r_k, c_q = _tile_iotas()
    later = 2.0 * jnp.maximum(r_k - c_q, 0).astype(F32)

    def slab(r):
        return idx_ref[pl.ds(pl.multiple_of(r * RED_ROWS, RED_ROWS), RED_ROWS), :]

    def count(pred):
        def body(r, acc):
            return acc + jnp.where(pred(slab(r), r), 1.0, 0.0)
        return _colsum(lax.fori_loop(0, slabs, body, jnp.zeros((RED_ROWS, Q_BLOCK), F32)))

    def kth(lo):
        def body(r, acc):
            s = slab(r)
            return jnp.minimum(acc, jnp.where(s >= lo, s, jnp.inf))
        acc = lax.fori_loop(0, slabs, body, jnp.full((RED_ROWS, Q_BLOCK), jnp.inf, F32))
        thr = jnp.min(acc, axis=0, keepdims=True)
        return thr, count(lambda s, r: s > thr)

    def slow_cond(carry):
        it, _, _, _, c_gt = carry
        return jnp.logical_and(it < SLOW_TRIPS, _any_lane(c_gt >= kf))

    def slow_body(carry):
        it, lo, hi, _, _ = carry
        for _ in range(4):
            mid = 0.5 * lo + 0.5 * hi
            ge = count(lambda s, r: s >= mid) >= kf
            lo, hi = jnp.where(ge, mid, lo), jnp.where(ge, hi, mid)
        thr, c_gt = kth(lo)
        return it + 1, lo, hi, thr, c_gt

    thr0, c_gt0 = kth(lo)
    _, _, _, thr, c_gt = lax.while_loop(slow_cond, slow_body, (jnp.int32(0), lo, hi, thr0, c_gt0))
    c_eq = count(lambda s, r: s == thr)
    need = (c_gt + c_eq) > kf

    def tie_step(_, carry):
        lo_i, hi_i = carry
        mid_i = jnp.floor((lo_i + hi_i) * 0.5)
        below = count(lambda s, r: jnp.logical_and(s == thr, (r * RED_ROWS + r_k).astype(F32) <= mid_i))
        ok = (c_gt + below) >= kf
        return jnp.where(ok, lo_i, mid_i), jnp.where(ok, mid_i, hi_i)

    last = jnp.asarray(slabs * RED_ROWS - 1, F32)
    lo_i = jnp.full((1, Q_BLOCK), -1.0, F32)
    hi_i = jnp.zeros((1, Q_BLOCK), F32) + last
    _, cut = lax.fori_loop(0, 13, tie_step, (lo_i, hi_i))
    cut = jnp.where(need, cut, last + 1.0).astype(jnp.int32)

    def write(r, carry):
        s = slab(r)
        krow = r * RED_ROWS + r_k
        sel = jnp.logical_or(s > thr, jnp.logical_and(s == thr, krow <= cut))
        on = jnp.where(r == slabs - 1, -later, 0.0)
        rhs_ref[pl.ds(pl.multiple_of(r * RED_ROWS, RED_ROWS), RED_ROWS), LANES:2 * LANES] = (
            jnp.where(sel, on, -MASK_OFF).astype(BF16))
        return carry

    lax.fori_loop(0, slabs, write, 0)


def _attend(nk, qt_ref, qcols, vt_ref, rhs_ref, s_ref, p_ref):
    r_k, c_q = _tile_iotas()
    feat = lax.broadcasted_iota(jnp.int32, (HALF, Q_BLOCK), 0)
    group = HEADS // 2
    width = group * Q_BLOCK

    def score_operand(hh):
        slope = 2.0 ** (-(hh + 1))
        alibi = jnp.where(feat == 0, CHUNK * slope, jnp.where(feat == 1, slope, 0.0)).astype(BF16)
        scaled_ident = jnp.where(r_k == c_q, slope, 0.0).astype(BF16)
        return jnp.concatenate([qt_ref[hh * HEAD_DIM:(hh + 1) * HEAD_DIM, qcols], alibi, scaled_ident], axis=0)

    def scores(g):
        lhs_t = jnp.concatenate([score_operand(g * group + i) for i in range(group)], axis=1)
        m_acc = jnp.full((RED_ROWS, width), -jnp.inf, F32)
        for rows in _row_blocks(nk, DOT_ROWS):
            blk = jnp.dot(rhs_ref[rows, :], lhs_t, preferred_element_type=F32)
            s_ref[g, rows, :] = blk
            for sub in range((rows.stop - rows.start) // RED_ROWS):
                m_acc = jnp.maximum(m_acc, blk[sub * RED_ROWS:(sub + 1) * RED_ROWS, :])
        return jnp.max(m_acc, axis=0, keepdims=True)

    def probabilities(g, m):
        step = KEY_BUCKET

        def body(r, carry):
            rows = pl.ds(pl.multiple_of(r * step, step), step)
            p_ref[g, rows, :] = jnp.exp((s_ref[g, rows, :] - m).astype(BF16))
            return carry

        lax.fori_loop(0, nk // step, body, 0)

    vt_one = jnp.concatenate([vt_ref[:, 0:nk], jnp.ones((2 * 8, nk), BF16)], axis=0)

    def values(g):
        o = jnp.dot(vt_one, p_ref[g, 0:nk, :], preferred_element_type=F32)
        return o[0:HEAD_DIM, :] * (1.0 / o[HEAD_DIM:HEAD_DIM + 1, :])

    m0 = scores(0)
    m1 = scores(1)
    probabilities(0, m0)
    out0 = values(0)
    probabilities(1, m1)
    outs = [out0, values(1)]
    tiles = []
    for t in range(HEADS // 2):
        o_g = outs[(2 * t) // group]
        a = (2 * t) % group
        tiles.append(jnp.concatenate([o_g[:, a * Q_BLOCK:(a + 1) * Q_BLOCK],
                                      o_g[:, (a + 1) * Q_BLOCK:(a + 2) * Q_BLOCK]], axis=0).T)
    return tiles


def _attn_kernel(qt_ref, qit_ref, wit_ref, kaug_ref, kk_ref, vt_ref, ma_ref, gate_ref, x_ref, gb_ref,
                 wo_ref, y_ref, idx_ref, rhs_ref, s_ref, p_ref, st_ref, *, seq, topk):
    jj = pl.program_id(1)
    kf = float(topk)
    qcols =[slice(u * Q_BLOCK, (u + 1) * Q_BLOCK) for u in range(PAIR)]

    @pl.when(jj == 0)
    def _():
        for u in range(PAIR):
            rhs_ref[u, :, 0:LANES] = kaug_ref[...]

    steps_per_bucket = KEY_BUCKET // (PAIR * Q_BLOCK)
    buckets = [((n + 1) * KEY_BUCKET, jnp.logical_and(jj >= n * steps_per_bucket, jj < (n + 1) * steps_per_bucket))
               for n in range(seq // KEY_BUCKET)]
    blocks = [PAIR * jj + u for u in range(PAIR)]

    for nk, here in buckets:
        @pl.when(here)
        def _(nk=nk):
            lane_q = lax.broadcasted_iota(jnp.int32, (1, Q_BLOCK), 1)
            states = []
            for u in range(PAIR):
                n_adm = blocks[u] * Q_BLOCK + CHUNK + CHUNK * (lane_q >= CHUNK).astype(jnp.int32)
                lo, hi = _index_scores(nk, n_adm, qit_ref, qcols[u], wit_ref[:, qcols[u]], kk_ref, s_ref,
                                       idx_ref.at[u])
                n_adm_f = n_adm.astype(F32)
                c_lo = jnp.where(n_adm_f <= kf, kf, n_adm_f)
                states.append((lo, hi, c_lo, jnp.zeros((1, Q_BLOCK), F32)))

            def cond(carry):
                it, flat = carry[0], carry[1:]
                still = jnp.logical_or(flat[2] != kf, flat[6] != kf)
                return jnp.logical_and(it < FAST_TRIPS, _any_lane(still))

            def body(carry):
                it, flat = carry[0], carry[1:]
                st = [flat[0:4], flat[4:8]]
                for _ in range(2):
                    st = [_probe(nk, kf, idx_ref.at[u], st[u]) for u in range(PAIR)]
                return (it + 1,) + tuple(st[0]) + tuple(st[1])

            final = lax.while_loop(cond, body, (jnp.int32(0),) + tuple(states[0]) + tuple(states[1]))[1:]
            for u in range(PAIR):
                lo, hi, c_lo, _ = final[4 * u:4 * u + 4]
                _write_selection(nk, blocks[u], lo, idx_ref.at[u], rhs_ref.at[u])
                st_ref[4 * u + 0:4 * u + 1, :] = lo
                st_ref[4 * u + 1:4 * u + 2, :] = hi
                st_ref[4 * u + 2:4 * u + 3, :] = c_lo

    for u in range(PAIR):
        lo = st_ref[4 * u + 0:4 * u + 1, :]
        hi = st_ref[4 * u + 1:4 * u + 2, :]
        c_lo = st_ref[4 * u + 2:4 * u + 3, :]

        @pl.when(_any_lane(c_lo != kf))
        def _(u=u, lo=lo, hi=hi):
            _exact_fallback(blocks[u] + 1, kf, lo, hi, idx_ref.at[u], rhs_ref.at[u])

    lane = lax.broadcasted_iota(jnp.int32, (PAIR * Q_BLOCK, LANES), 1)
    lo_half = lane < HALF
    for nk, here in buckets:
        @pl.when(here)
        def _(nk=nk):
            per_block = [_attend(nk, qt_ref, qcols[u], vt_ref, rhs_ref.at[u], s_ref, p_ref)
                         for u in range(PAIR)]
            mixed = [ma_ref[...]]
            for t in range(HEADS // 2):
                cols = slice(t * LANES, (t + 1) * LANES)
                o = jnp.concatenate([per_block[u][t] for u in range(PAIR)], axis=0)
                msq = _half_mean_sq(o * o, lo_half)
                ob = o * lax.rsqrt(msq + NORM_EPS) * gb_ref[:, cols] * gate_ref[:, cols].astype(F32)
                mixed.append(ob.astype(BF16))
            y_ref[...] = x_ref[...] + jnp.dot(jnp.concatenate(mixed, axis=1), wo_ref[...],
                                              preferred_element_type=F32)


def kernel(x, norm_gain, w_in, sgu_norm_gain, sgu_w, sgu_b, q_norm_gain, k_norm_gain,
           idx_k_norm_gain, branch_norm_gain, w_out):
    bsz, seq, d_model = x.shape
    assert d_model == D_MODEL and norm_gain.shape[0] == 1
    assert seq % PROJ_ROWS == 0 and seq % KEY_BUCKET == 0 and KEY_BUCKET % (PAIR * Q_BLOCK) == 0
    tokens = bsz * seq
    topk = min(TOPK_MAX, seq // 4)
    idx_w_scale = (IDX_HEADS ** -0.5) * (IDX_DIM ** -0.5)

    w = w_in[0]
    a3 = 3 * A_WIDTH
    w_q = w[:, a3:a3 + B_WIDTH]
    w_k = w[:, a3 + B_WIDTH:a3 + B_WIDTH + HEAD_DIM]
    w_v = w[:, a3 + B_WIDTH + HEAD_DIM:a3 + B_WIDTH + 2 * HEAD_DIM]
    o_g = a3 + B_WIDTH + 2 * HEAD_DIM
    w_g = w[:, o_g:o_g + B_WIDTH]
    o_i = o_g + B_WIDTH
    w_iq = w[:, o_i:o_i + IDX_HEADS * IDX_DIM]
    w_ik = w[:, o_i + IDX_HEADS * IDX_DIM:o_i + IDX_HEADS * IDX_DIM + IDX_DIM]
    w_iw = w[:, o_i + IDX_HEADS * IDX_DIM + IDX_DIM:]
    w_main = jnp.concatenate([w[:, :a3], w_g, w_k, w_ik], axis=1).astype(BF16)
    assert w_main.shape[1] == _PACKED_COLS
    w_t = jnp.concatenate([w_q.T, w_iq.T, w_v.T, w_iw.T,
                           jnp.zeros((_PACKED_ROWS - _ROW_W - IDX_HEADS, D_MODEL), F32)], axis=0).astype(BF16)
    x2 = x.reshape(tokens, D_MODEL)
    ng = norm_gain[0].reshape(1, D_MODEL)
    sgn = sgu_norm_gain[0].reshape(1, A_WIDTH)
    sw = sgu_w[0]
    sb = sgu_b[0].reshape(A_GROUPS, A_BLOCK, 1)
    qg = q_norm_gain[0].reshape(HEAD_DIM, 1)
    kg = jnp.concatenate([k_norm_gain[0], idx_k_norm_gain[0]]).reshape(1, LANES)
    ga = branch_norm_gain[0, :A_WIDTH].reshape(1, A_WIDTH)
    gb = branch_norm_gain[0, A_WIDTH:].reshape(1, B_WIDTH)
    wo = w_out[0].astype(BF16)

    tm = PROJ_ROWS
    full = lambda shape: pl.BlockSpec(shape, lambda i: (0,) * len(shape))
    rows = lambda width: pl.BlockSpec((tm, width), lambda i: (i, 0))
    colsT = lambda height: pl.BlockSpec((height, tm), lambda i: (0, i))
    outs = pl.pallas_call(
        functools.partial(_proj_kernel, tiles_per_seq=seq // tm, idx_w_scale=idx_w_scale),
        grid=(tokens // tm,),
        in_specs=[rows(D_MODEL), full((1, D_MODEL)), full((D_MODEL, _PACKED_COLS)),
                  full((_PACKED_ROWS, D_MODEL)), full((1, A_WIDTH)), full((A_GROUPS, A_BLOCK, A_BLOCK)),
                  full((A_GROUPS, A_BLOCK, 1)), full((HEAD_DIM, 1)), full((1, LANES)),
                  full((1, A_WIDTH))],
        out_specs=[rows(A_WIDTH), rows(B_WIDTH), rows(LANES), rows(LANES),
                   colsT(B_WIDTH), colsT(IDX_HEADS * IDX_DIM), colsT(HEAD_DIM), colsT(IDX_HEADS)],
        out_shape=[jax.ShapeDtypeStruct((tokens, A_WIDTH), BF16),
                   jax.ShapeDtypeStruct((tokens, B_WIDTH), BF16),
                   jax.ShapeDtypeStruct((tokens, LANES), BF16),
                   jax.ShapeDtypeStruct((tokens, LANES), BF16),
                   jax.ShapeDtypeStruct((B_WIDTH, tokens), BF16),
                   jax.ShapeDtypeStruct((IDX_HEADS * IDX_DIM, tokens), BF16),
                   jax.ShapeDtypeStruct((HEAD_DIM, tokens), BF16),
                   jax.ShapeDtypeStruct((IDX_HEADS, tokens), F32)],
        compiler_params=pltpu.CompilerParams(dimension_semantics=("arbitrary",),
                                             vmem_limit_bytes=VMEM_LIMIT),
        name="proj_sgu",
    )(x2, ng, w_main, w_t, sgn, sw, sb, qg, kg, ga)
    ma, gate, kaug, kk, qt, qit, vt, wit = outs

    qrows = PAIR * Q_BLOCK
    steps = seq // qrows
    qblk = lambda width: pl.BlockSpec((qrows, width), lambda b, j: (b * steps + j, 0))
    qblkT = lambda height: pl.BlockSpec((height, qrows), lambda b, j: (0, b * steps + j))
    const = lambda shape: pl.BlockSpec(shape, lambda b, j: (0,) * len(shape))
    y = pl.pallas_call(
        functools.partial(_attn_kernel, seq=seq, topk=topk),
        grid=(bsz, steps),
        in_specs=[qblkT(B_WIDTH), qblkT(IDX_HEADS * IDX_DIM), qblkT(IDX_HEADS),
                  pl.BlockSpec((seq, LANES), lambda b, j: (b, 0)),
                  pl.BlockSpec((seq, LANES), lambda b, j: (b, 0)),
                  pl.BlockSpec((HEAD_DIM, seq), lambda b, j: (0, b)),
                  qblk(A_WIDTH), qblk(B_WIDTH), qblk(D_MODEL),
                  const((1, B_WIDTH)), const((D_MODEL, D_MODEL))],
        out_specs=qblk(D_MODEL),
        out_shape=jax.ShapeDtypeStruct((tokens, D_MODEL), F32),
        scratch_shapes=[pltpu.VMEM((PAIR, seq, Q_BLOCK), F32),
                        pltpu.VMEM((PAIR, seq, 2 * LANES), BF16),
                        pltpu.VMEM((2, seq, (HEADS // 2) * Q_BLOCK), F32),
                        pltpu.VMEM((2, seq, (HEADS // 2) * Q_BLOCK), BF16),
                        pltpu.VMEM((8, Q_BLOCK), F32)],
        compiler_params=pltpu.CompilerParams(dimension_semantics=("arbitrary", "arbitrary"),
                                             vmem_limit_bytes=VMEM_LIMIT),
        name="dsa_attn_out",
    )(qt, qit, wit, kaug, kk, vt, ma, gate, x2, gb, wo)
    return y.reshape(bsz, seq, D_MODEL)
```

```python
import functools

import jax
import jax.numpy as jnp
from jax import lax
from jax.experimental import pallas as pl
from jax.experimental.pallas import tpu as pltpu

F32 = jnp.float32
BF16 = jnp.bfloat16

D_MODEL = 1024
CHUNK = 64
A_WIDTH = 512
A_GROUPS = 4
A_BLOCK = 128
HEADS = 8
HEAD_DIM = 64
B_WIDTH = HEADS * HEAD_DIM
IDX_HEADS = 8
IDX_DIM = 64
TOPK_MAX = 256
Q_BLOCK = 128
PAIR = 2
NORM_EPS = 1e-6
MASK_OFF = 1e32
LANES = 128
HALF = LANES // 2
PROJ_ROWS = 512
DOT_ROWS = 512
RED_ROWS = 128
KEY_BUCKET = 512
FAST_TRIPS = 24
INTERP_MARGIN = 0.02
SLOW_TRIPS = 70
VMEM_LIMIT = 48 * 1024 * 1024

_OFF_U, _OFF_V, _OFF_Z, _OFF_G, _OFF_K = 0, 512, 1024, 1536, 2048
_PACKED_COLS = 2176
_ROW_Q, _ROW_QI, _ROW_V, _ROW_W, _PACKED_ROWS = 0, 512, 1024, 1088, 1152

_NT = (((1,), (1,)), ((), ()))


def _gelu(x):
    c = 0.7978845608028654
    return 0.5 * x * (1.0 + jnp.tanh(c * (x + 0.044715 * (x * x * x))))


def _silu(x):
    return x / (1.0 + jnp.exp(-x))


def _row_blocks(total, size):
    return [slice(r, min(r + size, total)) for r in range(0, total, size)]


def _half_mean_sq(x2, lo_half):
    tot = jnp.sum(x2, axis=-1, keepdims=True)
    lo = jnp.sum(jnp.where(lo_half, x2, 0.0), axis=-1, keepdims=True)
    return jnp.where(lo_half, lo, tot - lo) * (1.0 / HALF)


def _proj_kernel(x_ref, ng_ref, w_ref, wt_ref, sgn_ref, sw_ref, sb_ref, qg_ref, kg_ref, ga_ref,
                 ma_ref, gate_ref, kaug_ref, kk_ref, qt_ref, qit_ref, vt_ref, wit_ref,
                 *, tiles_per_seq, idx_w_scale):
    tm = x_ref.shape[0]
    i = pl.program_id(0)
    x = x_ref[...]
    ms = jnp.mean(x * x, axis=-1, keepdims=True)
    h = (x * lax.rsqrt(ms + NORM_EPS) * ng_ref[...]).astype(BF16)

    lane = lax.broadcasted_iota(jnp.int32, (tm, LANES), 1)
    lo_half = lane < HALF

    def proj(off, width):
        return jnp.dot(h, w_ref[:, off:off + width], preferred_element_type=F32)

    gu = _gelu(proj(_OFF_U, A_WIDTH))
    gv = _gelu(proj(_OFF_V, A_WIDTH))
    pz = proj(_OFF_Z, A_WIDTH)
    r_i = lax.broadcasted_iota(jnp.int32, (A_BLOCK, A_BLOCK), 0)
    c_j = lax.broadcasted_iota(jnp.int32, (A_BLOCK, A_BLOCK), 1)
    causal = lax.shift_right_logical(c_j, 6) <= lax.shift_right_logical(r_i, 6)
    for g in range(A_GROUPS):
        cols = slice(g * LANES, (g + 1) * LANES)
        vg = gv[:, cols]
        mu = jnp.mean(vg, axis=-1, keepdims=True)
        d = vg - mu
        var = jnp.mean(d * d, axis=-1, keepdims=True)
        vn = (d * lax.rsqrt(var + NORM_EPS) * sgn_ref[:, cols]).astype(BF16)
        wg = jnp.where(causal, sw_ref[g], 0.0).astype(BF16)
        for blk in range(tm // A_BLOCK):
            rows = slice(blk * A_BLOCK, (blk + 1) * A_BLOCK)
            s = jnp.dot(wg, vn[rows, :], preferred_element_type=F32) + sb_ref[g]
            ya = gu[rows, cols] * s
            oa = ya * lax.rsqrt(jnp.mean(ya * ya, axis=-1, keepdims=True) + NORM_EPS) * ga_ref[:, cols]
            ma_ref[rows, cols] = (oa * _silu(pz[rows, cols])).astype(BF16)

    gate_ref[...] = _silu(proj(_OFF_G, B_WIDTH)).astype(BF16)

    pk = proj(_OFF_K, LANES)
    k_ms = jnp.sum(jnp.where(lo_half, pk * pk, 0.0), axis=-1, keepdims=True) * (1.0 / HALF)
    kn = pk * lax.rsqrt(k_ms + NORM_EPS)
    ik_mu = jnp.sum(jnp.where(lo_half, 0.0, pk), axis=-1, keepdims=True) * (1.0 / HALF)
    dk = pk - ik_mu
    ik_var = jnp.sum(jnp.where(lo_half, 0.0, dk * dk), axis=-1, keepdims=True) * (1.0 / HALF)
    kin = dk * lax.rsqrt(ik_var + NORM_EPS)
    tile = jnp.where(lo_half, kn, kin) * kg_ref[...]
    swapped = pltpu.roll(tile, HALF, axis=1)
    row = lax.broadcasted_iota(jnp.int32, (tm, LANES), 0)
    pos = (i % tiles_per_seq) * tm + row
    pos_hi = lax.shift_right_logical(pos, 6).astype(F32)
    pos_lo = (pos & (CHUNK - 1)).astype(F32)
    posfeat = jnp.where(lane == HALF, pos_hi, jnp.where(lane == HALF + 1, pos_lo, 0.0))
    kaug_ref[...] = jnp.where(lo_half, tile, posfeat).astype(BF16)
    kk_ref[...] = jnp.where(lo_half, swapped, tile).astype(BF16)

    pt = lax.dot_general(wt_ref[...], h, _NT, preferred_element_type=F32)
    for hh in range(HEADS):
        rows = slice(_ROW_Q + hh * HEAD_DIM, _ROW_Q + (hh + 1) * HEAD_DIM)
        xq = pt[rows, :]
        msq = jnp.mean(xq * xq, axis=0, keepdims=True)
        qt_ref[rows, :] = (xq * lax.rsqrt(msq + NORM_EPS) * qg_ref[...] * (HEAD_DIM ** -0.5)).astype(BF16)
    qit_ref[...] = pt[_ROW_QI:_ROW_QI + IDX_HEADS * IDX_DIM, :].astype(BF16)
    vt_ref[...] = pt[_ROW_V:_ROW_V + HEAD_DIM, :].astype(BF16)
    wit_ref[...] = pt[_ROW_W:_ROW_W + IDX_HEADS, :] * idx_w_scale


def _tile_iotas():
    r_k = lax.broadcasted_iota(jnp.int32, (Q_BLOCK, Q_BLOCK), 0)
    c_q = lax.broadcasted_iota(jnp.int32, (Q_BLOCK, Q_BLOCK), 1)
    return r_k, c_q


def _colsum(x):
    return jnp.sum(x, axis=0, keepdims=True)


def _any_lane(pred):
    return jnp.max(jnp.where(pred, 1.0, 0.0)) > 0.0


def _index_scores(nk, n_adm, qit_ref, qcols, wi, kk_ref, s_ref, idx_ref):
    r_k, c_q = _tile_iotas()
    top_rows = r_k < HALF
    per_head = []
    for t in range(IDX_HEADS // 2):
        qit = qit_ref[t * LANES:(t + 1) * LANES, qcols]
        zero = jnp.zeros_like(qit)
        per_head += [jnp.where(top_rows, qit, zero), jnp.where(top_rows, zero, qit)]
    half_heads = IDX_HEADS // 2
    for g in range(2):
        wg = jnp.concatenate(per_head[g * half_heads:(g + 1) * half_heads], axis=1)
        for rows in _row_blocks(nk, DOT_ROWS):
            s_ref[g, rows, :] = jnp.dot(kk_ref[rows, :], wg, preferred_element_type=F32)

    def weighted_relu(g, rows):
        acc = None
        for i in range(half_heads):
            hh = g * half_heads + i
            term = jnp.maximum(s_ref[g, rows, i * Q_BLOCK:(i + 1) * Q_BLOCK], 0.0) * wi[hh:hh + 1, :]
            acc = term if acc is None else acc + term
        return acc

    slabs = nk // RED_ROWS
    mn_acc = jnp.full((RED_ROWS, Q_BLOCK), jnp.inf, F32)
    mx_acc = jnp.full((RED_ROWS, Q_BLOCK), -jnp.inf, F32)
    tiny_acc = jnp.full((RED_ROWS, Q_BLOCK), jnp.inf, F32)
    for r in range(slabs):
        rows = slice(r * RED_ROWS, (r + 1) * RED_ROWS)
        acc = weighted_relu(0, rows) + weighted_relu(1, rows)
        if r >= slabs - KEY_BUCKET // RED_ROWS:
            adm = (r * RED_ROWS + r_k) < n_adm
            lo_fill = jnp.where(adm, acc, -jnp.inf)
            hi_fill = jnp.where(adm, acc, jnp.inf)
        else:
            lo_fill = hi_fill = acc
        mag = jnp.abs(hi_fill)
        idx_ref[rows, :] = lo_fill
        mn_acc = jnp.minimum(mn_acc, hi_fill)
        mx_acc = jnp.maximum(mx_acc, lo_fill)
        tiny_acc = jnp.minimum(tiny_acc, jnp.where(mag == 0.0, jnp.inf, mag))
    lo = jnp.min(mn_acc, axis=0, keepdims=True)
    hi = jnp.max(mx_acc, axis=0, keepdims=True)
    tiny = jnp.min(tiny_acc, axis=0, keepdims=True)

    unit = jnp.where(tiny < jnp.inf, tiny, 1.0)
    eps = unit * (0.5 / nk)
    rank0 = (1 + r_k).astype(F32)

    def spread(r, carry):
        rows = pl.ds(pl.multiple_of(r * RED_ROWS, RED_ROWS), RED_ROWS)
        s = idx_ref[rows, :]
        rank = rank0 + jnp.asarray(r * RED_ROWS, F32)
        idx_ref[rows, :] = jnp.where(s == 0.0, -(rank * eps), s)
        return carry

    lax.fori_loop(0, slabs, spread, 0)
    return jnp.minimum(lo, -0.5 * unit), hi


def _probe(nk, kf, idx_ref, state):
    lo, hi, c_lo, c_hi = state
    frac = (c_lo - kf) / jnp.maximum(c_lo - c_hi, 1.0)
    frac = jnp.minimum(jnp.maximum(frac, INTERP_MARGIN), 1.0 - INTERP_MARGIN)
    t = lo + (hi - lo) * frac
    acc = jnp.zeros((RED_ROWS, Q_BLOCK), F32)
    for r in range(nk // RED_ROWS):
        acc = acc + jnp.where(idx_ref[r * RED_ROWS:(r + 1) * RED_ROWS, :] >= t, 1.0, 0.0)
    c = _colsum(acc)
    ge = c >= kf
    return (jnp.where(ge, t, lo), jnp.where(ge, hi, t), jnp.where(ge, c, c_lo), jnp.where(ge, c_hi, c))


def _write_selection(nk, j_blk, lo, idx_ref, rhs_ref):
    r_k, c_q = _tile_iotas()
    later = 2.0 * jnp.maximum(r_k - c_q, 0).astype(F32)

    def write(r, carry):
        rows = pl.ds(pl.multiple_of(r * RED_ROWS, RED_ROWS), RED_ROWS)
        on = jnp.where(r == j_blk, -later, 0.0)
        rhs_ref[rows, LANES:2 * LANES] = jnp.where(idx_ref[rows, :] >= lo, on, -MASK_OFF).astype(BF16)
        return carry

    lax.fori_loop(0, nk // RED_ROWS, write, 0)


def _exact_fallback(slabs, kf, lo, hi, idx_ref, rhs_ref):
    r_k, c_q = _tile_iotas()
    later = 2.0 * jnp.maximum(r_k - c_q, 0).astype(F32)

    def slab(r):
        return idx_ref[pl.ds(pl.multiple_of(r * RED_ROWS, RED_ROWS), RED_ROWS), :]

    def count(pred):
        def body(r, acc):
            return acc + jnp.where(pred(slab(r), r), 1.0, 0.0)
        return _colsum(lax.fori_loop(0, slabs, body, jnp.zeros((RED_ROWS, Q_BLOCK), F32)))

    def kth(lo):
        def body(r, acc):
            s = slab(r)
            return jnp.minimum(acc, jnp.where(s >= lo, s, jnp.inf))
        acc = lax.fori_loop(0, slabs, body, jnp.full((RED_ROWS, Q_BLOCK), jnp.inf, F32))
        thr = jnp.min(acc, axis=0, keepdims=True)
        return thr, count(lambda s, r: s > thr)

    def slow_cond(carry):
        it, _, _, _, c_gt = carry
        return jnp.logical_and(it < SLOW_TRIPS, _any_lane(c_gt >= kf))

    def slow_body(carry):
        it, lo, hi, _, _ = carry
        for _ in range(4):
            mid = 0.5 * lo + 0.5 * hi
            ge = count(lambda s, r: s >= mid) >= kf
            lo, hi = jnp.where(ge, mid, lo), jnp.where(ge, hi, mid)
        thr, c_gt = kth(lo)
        return it + 1, lo, hi, thr, c_gt

    thr0, c_gt0 = kth(lo)
    _, _, _, thr, c_gt = lax.while_loop(slow_cond, slow_body, (jnp.int32(0), lo, hi, thr0, c_gt0))
    c_eq = count(lambda s, r: s == thr)
    need = (c_gt + c_eq) > kf

    def tie_step(_, carry):
        lo_i, hi_i = carry
        mid_i = jnp.floor((lo_i + hi_i) * 0.5)
        below = count(lambda s, r: jnp.logical_and(s == thr, (r * RED_ROWS + r_k).astype(F32) <= mid_i))
        ok = (c_gt + below) >= kf
        return jnp.where(ok, lo_i, mid_i), jnp.where(ok, mid_i, hi_i)

    last = jnp.asarray(slabs * RED_ROWS - 1, F32)
    lo_i = jnp.full((1, Q_BLOCK), -1.0, F32)
    hi_i = jnp.zeros((1, Q_BLOCK), F32) + last
    _, cut = lax.fori_loop(0, 13, tie_step, (lo_i, hi_i))
    cut = jnp.where(need, cut, last + 1.0).astype(jnp.int32)

    def write(r, carry):
        s = slab(r)
        krow = r * RED_ROWS + r_k
        sel = jnp.logical_or(s > thr, jnp.logical_and(s == thr, krow <= cut))
        on = jnp.where(r == slabs - 1, -later, 0.0)
        rhs_ref[pl.ds(pl.multiple_of(r * RED_ROWS, RED_ROWS), RED_ROWS), LANES:2 * LANES] = (
            jnp.where(sel, on, -MASK_OFF).astype(BF16))
        return carry

    lax.fori_loop(0, slabs, write, 0)


def _attend(nk, qt_ref, qcols, vt_ref, rhs_ref, s_ref, p_ref):
    r_k, c_q = _tile_iotas()
    feat = lax.broadcasted_iota(jnp.int32, (HALF, Q_BLOCK), 0)
    group = HEADS // 2
    width = group * Q_BLOCK
    units = [(u, g) for u in range(len(qcols)) for g in range(HEADS // group)]

    def score_operand(u, hh):
        slope = 2.0 ** (-(hh + 1))
        alibi = jnp.where(feat == 0, CHUNK * slope, jnp.where(feat == 1, slope, 0.0)).astype(BF16)
        scaled_ident = jnp.where(r_k == c_q, slope, 0.0).astype(BF16)
        return jnp.concatenate([qt_ref[hh * HEAD_DIM:(hh + 1) * HEAD_DIM, qcols[u]], alibi, scaled_ident], axis=0)

    def scores(n):
        u, g = units[n]
        lhs_t = jnp.concatenate([score_operand(u, g * group + i) for i in range(group)], axis=1)
        m_acc = jnp.full((RED_ROWS, width), -jnp.inf, F32)
        for rows in _row_blocks(nk, DOT_ROWS):
            blk = jnp.dot(rhs_ref[u, rows, :], lhs_t, preferred_element_type=F32)
            s_ref[n % 2, rows, :] = blk
            for sub in range((rows.stop - rows.start) // RED_ROWS):
                m_acc = jnp.maximum(m_acc, blk[sub * RED_ROWS:(sub + 1) * RED_ROWS, :])
        return jnp.max(m_acc, axis=0, keepdims=True)

    def probabilities(n, m):
        for rows in _row_blocks(nk, RED_ROWS):
            p_ref[n % 2, rows, :] = jnp.exp((s_ref[n % 2, rows, :] - m).astype(BF16))

    vt_one = jnp.concatenate([vt_ref[:, 0:nk], jnp.ones((2 * 8, nk), BF16)], axis=0)

    def values(n):
        o = jnp.dot(vt_one, p_ref[n % 2, 0:nk, :], preferred_element_type=F32)
        return o[0:HEAD_DIM, :] * (1.0 / o[HEAD_DIM:HEAD_DIM + 1, :])

    outs = []
    m = scores(0)
    for n in range(len(units)):
        probabilities(n, m)
        if n + 1 < len(units):
            m = scores(n + 1)
        outs.append(values(n))

    per_block = []
    for u in range(len(qcols)):
        tiles = []
        for t in range(HEADS // 2):
            o_g = outs[u * (HEADS // group) + (2 * t) // group]
            a = (2 * t) % group
            tiles.append(jnp.concatenate([o_g[:, a * Q_BLOCK:(a + 1) * Q_BLOCK],
                                          o_g[:, (a + 1) * Q_BLOCK:(a + 2) * Q_BLOCK]], axis=0).T)
        per_block.append(tiles)
    return per_block


def _attn_kernel(qt_ref, qit_ref, wit_ref, kaug_ref, kk_ref, vt_ref, ma_ref, gate_ref, x_ref, gb_ref,
                 wo_ref, y_ref, idx_ref, rhs_ref, s_ref, p_ref, st_ref, *, seq, topk):
    jj = pl.program_id(1)
    kf = float(topk)
    qcols =[slice(u * Q_BLOCK, (u + 1) * Q_BLOCK) for u in range(PAIR)]

    @pl.when(jj == 0)
    def _():
        for u in range(PAIR):
            rhs_ref[u, :, 0:LANES] = kaug_ref[...]

    steps_per_bucket = KEY_BUCKET // (PAIR * Q_BLOCK)
    buckets = [((n + 1) * KEY_BUCKET, jnp.logical_and(jj >= n * steps_per_bucket, jj < (n + 1) * steps_per_bucket))
               for n in range(seq // KEY_BUCKET)]
    blocks = [PAIR * jj + u for u in range(PAIR)]

    for nk, here in buckets:
        @pl.when(here)
        def _(nk=nk):
            lane_q = lax.broadcasted_iota(jnp.int32, (1, Q_BLOCK), 1)
            states = []
            for u in range(PAIR):
                n_adm = blocks[u] * Q_BLOCK + CHUNK + CHUNK * (lane_q >= CHUNK).astype(jnp.int32)
                lo, hi = _index_scores(nk, n_adm, qit_ref, qcols[u], wit_ref[:, qcols[u]], kk_ref, s_ref,
                                       idx_ref.at[u])
                n_adm_f = n_adm.astype(F32)
                c_lo = jnp.where(n_adm_f <= kf, kf, n_adm_f)
                states.append((lo, hi, c_lo, jnp.zeros((1, Q_BLOCK), F32)))

            def cond(carry):
                it, flat = carry[0], carry[1:]
                still = jnp.logical_or(flat[2] != kf, flat[6] != kf)
                return jnp.logical_and(it < FAST_TRIPS, _any_lane(still))

            def body(carry):
                it, flat = carry[0], carry[1:]
                st = [flat[0:4], flat[4:8]]
                for _ in range(2):
                    st = [_probe(nk, kf, idx_ref.at[u], st[u]) for u in range(PAIR)]
                return (it + 1,) + tuple(st[0]) + tuple(st[1])

            final = lax.while_loop(cond, body, (jnp.int32(0),) + tuple(states[0]) + tuple(states[1]))[1:]
            for u in range(PAIR):
                lo, hi, c_lo, _ = final[4 * u:4 * u + 4]
                _write_selection(nk, blocks[u], lo, idx_ref.at[u], rhs_ref.at[u])
                st_ref[4 * u + 0:4 * u + 1, :] = lo
                st_ref[4 * u + 1:4 * u + 2, :] = hi
                st_ref[4 * u + 2:4 * u + 3, :] = c_lo

    for u in range(PAIR):
        lo = st_ref[4 * u + 0:4 * u + 1, :]
        hi = st_ref[4 * u + 1:4 * u + 2, :]
        c_lo = st_ref[4 * u + 2:4 * u + 3, :]

        @pl.when(_any_lane(c_lo != kf))
        def _(u=u, lo=lo, hi=hi):
            _exact_fallback(blocks[u] + 1, kf, lo, hi, idx_ref.at[u], rhs_ref.at[u])

    lane = lax.broadcasted_iota(jnp.int32, (PAIR * Q_BLOCK, LANES), 1)
    lo_half = lane < HALF
    for nk, here in buckets:
        @pl.when(here)
        def _(nk=nk):
            per_block = _attend(nk, qt_ref, qcols, vt_ref, rhs_ref, s_ref, p_ref)
            mixed = [ma_ref[...]]
            for t in range(HEADS // 2):
                cols = slice(t * LANES, (t + 1) * LANES)
                o = jnp.concatenate([per_block[u][t] for u in range(PAIR)], axis=0)
                msq = _half_mean_sq(o * o, lo_half)
                ob = o * lax.rsqrt(msq + NORM_EPS) * gb_ref[:, cols] * gate_ref[:, cols].astype(F32)
                mixed.append(ob.astype(BF16))
            y_ref[...] = x_ref[...] + jnp.dot(jnp.concatenate(mixed, axis=1), wo_ref[...],
                                              preferred_element_type=F32)


def kernel(x, norm_gain, w_in, sgu_norm_gain, sgu_w, sgu_b, q_norm_gain, k_norm_gain,
           idx_k_norm_gain, branch_norm_gain, w_out):
    bsz, seq, d_model = x.shape
    assert d_model == D_MODEL and norm_gain.shape[0] == 1
    assert seq % PROJ_ROWS == 0 and seq % KEY_BUCKET == 0 and KEY_BUCKET % (PAIR * Q_BLOCK) == 0
    tokens = bsz * seq
    topk = min(TOPK_MAX, seq // 4)
    idx_w_scale = (IDX_HEADS ** -0.5) * (IDX_DIM ** -0.5)

    w = w_in[0]
    a3 = 3 * A_WIDTH
    w_q = w[:, a3:a3 + B_WIDTH]
    w_k = w[:, a3 + B_WIDTH:a3 + B_WIDTH + HEAD_DIM]
    w_v = w[:, a3 + B_WIDTH + HEAD_DIM:a3 + B_WIDTH + 2 * HEAD_DIM]
    o_g = a3 + B_WIDTH + 2 * HEAD_DIM
    w_g = w[:, o_g:o_g + B_WIDTH]
    o_i = o_g + B_WIDTH
    w_iq = w[:, o_i:o_i + IDX_HEADS * IDX_DIM]
    w_ik = w[:, o_i + IDX_HEADS * IDX_DIM:o_i + IDX_HEADS * IDX_DIM + IDX_DIM]
    w_iw = w[:, o_i + IDX_HEADS * IDX_DIM + IDX_DIM:]
    w_main = jnp.concatenate([w[:, :a3], w_g, w_k, w_ik], axis=1).astype(BF16)
    assert w_main.shape[1] == _PACKED_COLS
    w_t = jnp.concatenate([w_q.T, w_iq.T, w_v.T, w_iw.T,
                           jnp.zeros((_PACKED_ROWS - _ROW_W - IDX_HEADS, D_MODEL), F32)], axis=0).astype(BF16)
    x2 = x.reshape(tokens, D_MODEL)
    ng = norm_gain[0].reshape(1, D_MODEL)
    sgn = sgu_norm_gain[0].reshape(1, A_WIDTH)
    sw = sgu_w[0]
    sb = sgu_b[0].reshape(A_GROUPS, A_BLOCK, 1)
    qg = q_norm_gain[0].reshape(HEAD_DIM, 1)
    kg = jnp.concatenate([k_norm_gain[0], idx_k_norm_gain[0]]).reshape(1, LANES)
    ga = branch_norm_gain[0, :A_WIDTH].reshape(1, A_WIDTH)
    gb = branch_norm_gain[0, A_WIDTH:].reshape(1, B_WIDTH)
    wo = w_out[0].astype(BF16)

    tm = PROJ_ROWS
    full = lambda shape: pl.BlockSpec(shape, lambda i: (0,) * len(shape))
    rows = lambda width: pl.BlockSpec((tm, width), lambda i: (i, 0))
    colsT = lambda height: pl.BlockSpec((height, tm), lambda i: (0, i))
    outs = pl.pallas_call(
        functools.partial(_proj_kernel, tiles_per_seq=seq // tm, idx_w_scale=idx_w_scale),
        grid=(tokens // tm,),
        in_specs=[rows(D_MODEL), full((1, D_MODEL)), full((D_MODEL, _PACKED_COLS)),
                  full((_PACKED_ROWS, D_MODEL)), full((1, A_WIDTH)), full((A_GROUPS, A_BLOCK, A_BLOCK)),
                  full((A_GROUPS, A_BLOCK, 1)), full((HEAD_DIM, 1)), full((1, LANES)),
                  full((1, A_WIDTH))],
        out_specs=[rows(A_WIDTH), rows(B_WIDTH), rows(LANES), rows(LANES),
                   colsT(B_WIDTH), colsT(IDX_HEADS * IDX_DIM), colsT(HEAD_DIM), colsT(IDX_HEADS)],
        out_shape=[jax.ShapeDtypeStruct((tokens, A_WIDTH), BF16),
                   jax.ShapeDtypeStruct((tokens, B_WIDTH), BF16),
                   jax.ShapeDtypeStruct((tokens, LANES), BF16),
                   jax.ShapeDtypeStruct((tokens, LANES), BF16),
                   jax.ShapeDtypeStruct((B_WIDTH, tokens), BF16),
                   jax.ShapeDtypeStruct((IDX_HEADS * IDX_DIM, tokens), BF16),
                   jax.ShapeDtypeStruct((HEAD_DIM, tokens), BF16),
                   jax.ShapeDtypeStruct((IDX_HEADS, tokens), F32)],
        compiler_params=pltpu.CompilerParams(dimension_semantics=("arbitrary",),
                                             vmem_limit_bytes=VMEM_LIMIT),
        name="proj_sgu",
    )(x2, ng, w_main, w_t, sgn, sw, sb, qg, kg, ga)
    ma, gate, kaug, kk, qt, qit, vt, wit = outs

    qrows = PAIR * Q_BLOCK
    steps = seq // qrows
    qblk = lambda width: pl.BlockSpec((qrows, width), lambda b, j: (b * steps + j, 0))
    qblkT = lambda height: pl.BlockSpec((height, qrows), lambda b, j: (0, b * steps + j))
    const = lambda shape: pl.BlockSpec(shape, lambda b, j: (0,) * len(shape))
    y = pl.pallas_call(
        functools.partial(_attn_kernel, seq=seq, topk=topk),
        grid=(bsz, steps),
        in_specs=[qblkT(B_WIDTH), qblkT(IDX_HEADS * IDX_DIM), qblkT(IDX_HEADS),
                  pl.BlockSpec((seq, LANES), lambda b, j: (b, 0)),
                  pl.BlockSpec((seq, LANES), lambda b, j: (b, 0)),
                  pl.BlockSpec((HEAD_DIM, seq), lambda b, j: (0, b)),
                  qblk(A_WIDTH), qblk(B_WIDTH), qblk(D_MODEL),
                  const((1, B_WIDTH)), const((D_MODEL, D_MODEL))],
        out_specs=qblk(D_MODEL),
        out_shape=jax.ShapeDtypeStruct((tokens, D_MODEL), F32),
        scratch_shapes=[pltpu.VMEM((PAIR, seq, Q_BLOCK), F32),
                        pltpu.VMEM((PAIR, seq, 2 * LANES), BF16),
                        pltpu.VMEM((2, seq, (HEADS // 2) * Q_BLOCK), F32),
                        pltpu.VMEM((2, seq, (HEADS // 2) * Q_BLOCK), BF16),
                        pltpu.VMEM((8, Q_BLOCK), F32)],
        compiler_params=pltpu.CompilerParams(dimension_semantics=("arbitrary", "arbitrary"),
                                             vmem_limit_bytes=VMEM_LIMIT),
        name="dsa_attn_out",
    )(qt, qit, wit, kaug, kk, vt, ma, gate, x2, gb, wo)
    return y.reshape(bsz, seq, D_MODEL)
```

```python
import functools

import jax
import jax.numpy as jnp
from jax import lax
from jax.experimental import pallas as pl
from jax.experimental.pallas import tpu as pltpu

F32 = jnp.float32
BF16 = jnp.bfloat16

D_MODEL = 1024
CHUNK = 64
A_WIDTH = 512
A_GROUPS = 4
A_BLOCK = 128
HEADS = 8
HEAD_DIM = 64
B_WIDTH = HEADS * HEAD_DIM
IDX_HEADS = 8
IDX_DIM = 64
TOPK_MAX = 256
Q_BLOCK = 128
PAIR = 2
NORM_EPS = 1e-6
MASK_OFF = 1e32
LANES = 128
HALF = LANES // 2
PROJ_ROWS = 512
DOT_ROWS = 512
RED_ROWS = 128
KEY_BUCKET = 512
FAST_TRIPS = 24
INTERP_MARGIN = 0.02
SLOW_TRIPS = 70
CHUNKS_PER_TRIP = 2
VMEM_LIMIT = 56 * 1024 * 1024

_OFF_U, _OFF_V, _OFF_Z, _OFF_G, _OFF_K = 0, 512, 1024, 1536, 2048
_PACKED_COLS = 2176
_ROW_Q, _ROW_QI, _ROW_V, _ROW_W, _PACKED_ROWS = 0, 512, 1024, 1088, 1152

_NT = (((1,), (1,)), ((), ()))


def _gelu(x):
    c = 0.7978845608028654
    return 0.5 * x * (1.0 + jnp.tanh(c * (x + 0.044715 * (x * x * x))))


def _silu(x):
    return x / (1.0 + jnp.exp(-x))


def _row_blocks(total, size):
    return [slice(r, min(r + size, total)) for r in range(0, total, size)]


def _half_mean_sq(x2, lo_half):
    tot = jnp.sum(x2, axis=-1, keepdims=True)
    lo = jnp.sum(jnp.where(lo_half, x2, 0.0), axis=-1, keepdims=True)
    return jnp.where(lo_half, lo, tot - lo) * (1.0 / HALF)


def _proj_kernel(x_ref, ng_ref, w_ref, wt_ref, sgn_ref, sw_ref, sb_ref, qg_ref, kg_ref, ga_ref,
                 ma_ref, gate_ref, kaug_ref, kk_ref, qt_ref, qit_ref, vt_ref, wit_ref,
                 *, tiles_per_seq, idx_w_scale):
    tm = x_ref.shape[0]
    i = pl.program_id(0)
    x = x_ref[...]
    ms = jnp.mean(x * x, axis=-1, keepdims=True)
    h = (x * lax.rsqrt(ms + NORM_EPS) * ng_ref[...]).astype(BF16)

    lane = lax.broadcasted_iota(jnp.int32, (tm, LANES), 1)
    lo_half = lane < HALF

    def proj(off, width):
        return jnp.dot(h, w_ref[:, off:off + width], preferred_element_type=F32)

    gu = _gelu(proj(_OFF_U, A_WIDTH))
    gv = _gelu(proj(_OFF_V, A_WIDTH))
    pz = proj(_OFF_Z, A_WIDTH)
    r_i = lax.broadcasted_iota(jnp.int32, (A_BLOCK, A_BLOCK), 0)
    c_j = lax.broadcasted_iota(jnp.int32, (A_BLOCK, A_BLOCK), 1)
    causal = lax.shift_right_logical(c_j, 6) <= lax.shift_right_logical(r_i, 6)
    for g in range(A_GROUPS):
        cols = slice(g * LANES, (g + 1) * LANES)
        vg = gv[:, cols]
        mu = jnp.mean(vg, axis=-1, keepdims=True)
        d = vg - mu
        var = jnp.mean(d * d, axis=-1, keepdims=True)
        vn = (d * lax.rsqrt(var + NORM_EPS) * sgn_ref[:, cols]).astype(BF16)
        wg = jnp.where(causal, sw_ref[g], 0.0).astype(BF16)
        for blk in range(tm // A_BLOCK):
            rows = slice(blk * A_BLOCK, (blk + 1) * A_BLOCK)
            s = jnp.dot(wg, vn[rows, :], preferred_element_type=F32) + sb_ref[g]
            ya = gu[rows, cols] * s
            oa = ya * lax.rsqrt(jnp.mean(ya * ya, axis=-1, keepdims=True) + NORM_EPS) * ga_ref[:, cols]
            ma_ref[rows, cols] = (oa * _silu(pz[rows, cols])).astype(BF16)

    gate_ref[...] = _silu(proj(_OFF_G, B_WIDTH)).astype(BF16)

    pk = proj(_OFF_K, LANES)
    k_ms = jnp.sum(jnp.where(lo_half, pk * pk, 0.0), axis=-1, keepdims=True) * (1.0 / HALF)
    kn = pk * lax.rsqrt(k_ms + NORM_EPS)
    ik_mu = jnp.sum(jnp.where(lo_half, 0.0, pk), axis=-1, keepdims=True) * (1.0 / HALF)
    dk = pk - ik_mu
    ik_var = jnp.sum(jnp.where(lo_half, 0.0, dk * dk), axis=-1, keepdims=True) * (1.0 / HALF)
    kin = dk * lax.rsqrt(ik_var + NORM_EPS)
    tile = jnp.where(lo_half, kn, kin) * kg_ref[...]
    swapped = pltpu.roll(tile, HALF, axis=1)
    row = lax.broadcasted_iota(jnp.int32, (tm, LANES), 0)
    pos = (i % tiles_per_seq) * tm + row
    pos_hi = lax.shift_right_logical(pos, 6).astype(F32)
    pos_lo = (pos & (CHUNK - 1)).astype(F32)
    posfeat = jnp.where(lane == HALF, pos_hi, jnp.where(lane == HALF + 1, pos_lo, 0.0))
    kaug_ref[...] = jnp.where(lo_half, tile, posfeat).astype(BF16)
    kk_ref[...] = jnp.where(lo_half, swapped, tile).astype(BF16)

    pt = lax.dot_general(wt_ref[...], h, _NT, preferred_element_type=F32)
    for hh in range(HEADS):
        rows = slice(_ROW_Q + hh * HEAD_DIM, _ROW_Q + (hh + 1) * HEAD_DIM)
        xq = pt[rows, :]
        msq = jnp.mean(xq * xq, axis=0, keepdims=True)
        qt_ref[rows, :] = (xq * lax.rsqrt(msq + NORM_EPS) * qg_ref[...] * (HEAD_DIM ** -0.5)).astype(BF16)
    qit_ref[...] = pt[_ROW_QI:_ROW_QI + IDX_HEADS * IDX_DIM, :].astype(BF16)
    vt_ref[...] = pt[_ROW_V:_ROW_V + HEAD_DIM, :].astype(BF16)
    wit_ref[...] = pt[_ROW_W:_ROW_W + IDX_HEADS, :] * idx_w_scale


def _tile_iotas():
    r_k = lax.broadcasted_iota(jnp.int32, (Q_BLOCK, Q_BLOCK), 0)
    c_q = lax.broadcasted_iota(jnp.int32, (Q_BLOCK, Q_BLOCK), 1)
    return r_k, c_q


def _colsum(x):
    return jnp.sum(x, axis=0, keepdims=True)


def _any_lane(pred):
    return jnp.max(jnp.where(pred, 1.0, 0.0)) > 0.0


def _index_scores(nk, n_adm, qit_ref, qcols, wi, kk_ref, s_ref, idx_ref):
    r_k, c_q = _tile_iotas()
    top_rows = r_k < HALF
    per_head = []
    for t in range(IDX_HEADS // 2):
        qit = qit_ref[t * LANES:(t + 1) * LANES, qcols]
        zero = jnp.zeros_like(qit)
        per_head += [jnp.where(top_rows, qit, zero), jnp.where(top_rows, zero, qit)]
    half_heads = IDX_HEADS // 2
    for g in range(2):
        wg = jnp.concatenate(per_head[g * half_heads:(g + 1) * half_heads], axis=1)
        for rows in _row_blocks(nk, DOT_ROWS):
            s_ref[g, rows, :] = jnp.dot(kk_ref[rows, :], wg, preferred_element_type=F32)

    def weighted_relu(g, rows):
        acc = None
        for i in range(half_heads):
            hh = g * half_heads + i
            term = jnp.maximum(s_ref[g, rows, i * Q_BLOCK:(i + 1) * Q_BLOCK], 0.0) * wi[hh:hh + 1, :]
            acc = term if acc is None else acc + term
        return acc

    slabs = nk // RED_ROWS
    mn_acc = jnp.full((RED_ROWS, Q_BLOCK), jnp.inf, F32)
    mx_acc = jnp.full((RED_ROWS, Q_BLOCK), -jnp.inf, F32)
    tiny_acc = jnp.full((RED_ROWS, Q_BLOCK), jnp.inf, F32)
    for r in range(slabs):
        rows = slice(r * RED_ROWS, (r + 1) * RED_ROWS)
        acc = weighted_relu(0, rows) + weighted_relu(1, rows)
        if r >= slabs - KEY_BUCKET // RED_ROWS:
            adm = (r * RED_ROWS + r_k) < n_adm
            lo_fill = jnp.where(adm, acc, -jnp.inf)
            hi_fill = jnp.where(adm, acc, jnp.inf)
        else:
            lo_fill = hi_fill = acc
        mag = jnp.abs(hi_fill)
        idx_ref[rows, :] = lo_fill
        mn_acc = jnp.minimum(mn_acc, hi_fill)
        mx_acc = jnp.maximum(mx_acc, lo_fill)
        tiny_acc = jnp.minimum(tiny_acc, jnp.where(mag == 0.0, jnp.inf, mag))
    lo = jnp.min(mn_acc, axis=0, keepdims=True)
    hi = jnp.max(mx_acc, axis=0, keepdims=True)
    tiny = jnp.min(tiny_acc, axis=0, keepdims=True)

    unit = jnp.where(tiny < jnp.inf, tiny, 1.0)
    eps = unit * (0.5 / nk)
    rank0 = (1 + r_k).astype(F32)

    def spread(r, carry):
        rows = pl.ds(pl.multiple_of(r * RED_ROWS, RED_ROWS), RED_ROWS)
        s = idx_ref[rows, :]
        rank = rank0 + jnp.asarray(r * RED_ROWS, F32)
        idx_ref[rows, :] = jnp.where(s == 0.0, -(rank * eps), s)
        return carry

    lax.fori_loop(0, slabs, spread, 0)
    return jnp.minimum(lo, -0.5 * unit), hi


def _probe(nk, kf, idx_ref, state):
    lo, hi, c_lo, c_hi = state
    frac = (c_lo - kf) / jnp.maximum(c_lo - c_hi, 1.0)
    frac = jnp.minimum(jnp.maximum(frac, INTERP_MARGIN), 1.0 - INTERP_MARGIN)
    t = lo + (hi - lo) * frac
    acc = jnp.zeros((RED_ROWS, Q_BLOCK), F32)
    for r in range(nk // RED_ROWS):
        acc = acc + jnp.where(idx_ref[r * RED_ROWS:(r + 1) * RED_ROWS, :] >= t, 1.0, 0.0)
    c = _colsum(acc)
    ge = c >= kf
    return (jnp.where(ge, t, lo), jnp.where(ge, hi, t), jnp.where(ge, c, c_lo), jnp.where(ge, c_hi, c))


def _write_selection(nk, j_blk, lo, idx_ref, rhs_ref):
    r_k, c_q = _tile_iotas()
    later = 2.0 * jnp.maximum(r_k - c_q, 0).astype(F32)

    def write(r, carry):
        rows = pl.ds(pl.multiple_of(r * RED_ROWS, RED_ROWS), RED_ROWS)
        on = jnp.where(r == j_blk, -later, 0.0)
        rhs_ref[rows, LANES:2 * LANES] = jnp.where(idx_ref[rows, :] >= lo, on, -MASK_OFF).astype(BF16)
        return carry

    lax.fori_loop(0, nk // RED_ROWS, write, 0)


def _exact_fallback(slabs, kf, lo, hi, idx_ref, rhs_ref):
    r_k, c_q = _tile_iotas()
    later = 2.0 * jnp.maximum(r_k - c_q, 0).astype(F32)

    def slab(r):
        return idx_ref[pl.ds(pl.multiple_of(r * RED_ROWS, RED_ROWS), RED_ROWS), :]

    def count(pred):
        def body(r, acc):
            return acc + jnp.where(pred(slab(r), r), 1.0, 0.0)
        return _colsum(lax.fori_loop(0, slabs, body, jnp.zeros((RED_ROWS, Q_BLOCK), F32)))

    def kth(lo):
        def body(r, acc):
            s = slab(r)
            return jnp.minimum(acc, jnp.where(s >= lo, s, jnp.inf))
        acc = lax.fori_loop(0, slabs, body, jnp.full((RED_ROWS, Q_BLOCK), jnp.inf, F32))
        thr = jnp.min(acc, axis=0, keepdims=True)
        return thr, count(lambda s, r: s > thr)

    def slow_cond(carry):
        it, _, _, _, c_gt = carry
        return jnp.logical_and(it < SLOW_TRIPS, _any_lane(c_gt >= kf))

    def slow_body(carry):
        it, lo, hi, _, _ = carry
        for _ in range(4):
            mid = 0.5 * lo + 0.5 * hi
            ge = count(lambda s, r: s >= mid) >= kf
            lo, hi = jnp.where(ge, mid, lo), jnp.where(ge, hi, mid)
        thr, c_gt = kth(lo)
        return it + 1, lo, hi, thr, c_gt

    thr0, c_gt0 = kth(lo)
    _, _, _, thr, c_gt = lax.while_loop(slow_cond, slow_body, (jnp.int32(0), lo, hi, thr0, c_gt0))
    c_eq = count(lambda s, r: s == thr)
    need = (c_gt + c_eq) > kf

    def tie_step(_, carry):
        lo_i, hi_i = carry
        mid_i = jnp.floor((lo_i + hi_i) * 0.5)
        below = count(lambda s, r: jnp.logical_and(s == thr, (r * RED_ROWS + r_k).astype(F32) <= mid_i))
        ok = (c_gt + below) >= kf
        return jnp.where(ok, lo_i, mid_i), jnp.where(ok, mid_i, hi_i)

    last = jnp.asarray(slabs * RED_ROWS - 1, F32)
    lo_i = jnp.full((1, Q_BLOCK), -1.0, F32)
    hi_i = jnp.zeros((1, Q_BLOCK), F32) + last
    _, cut = lax.fori_loop(0, 13, tie_step, (lo_i, hi_i))
    cut = jnp.where(need, cut, last + 1.0).astype(jnp.int32)

    def write(r, carry):
        s = slab(r)
        krow = r * RED_ROWS + r_k
        sel = jnp.logical_or(s > thr, jnp.logical_and(s == thr, krow <= cut))
        on = jnp.where(r == slabs - 1, -later, 0.0)
        rhs_ref[pl.ds(pl.multiple_of(r * RED_ROWS, RED_ROWS), RED_ROWS), LANES:2 * LANES] = (
            jnp.where(sel, on, -MASK_OFF).astype(BF16))
        return carry

    lax.fori_loop(0, slabs, write, 0)


GROUP = HEADS // 2
UNITS = PAIR * (HEADS // GROUP)


def _score_operand(qt_ref, qcols, g):
    r_k, c_q = _tile_iotas()
    feat = lax.broadcasted_iota(jnp.int32, (HALF, Q_BLOCK), 0)
    heads = []
    for hh in range(g * GROUP, (g + 1) * GROUP):
        slope = 2.0 ** (-(hh + 1))
        alibi = jnp.where(feat == 0, CHUNK * slope, jnp.where(feat == 1, slope, 0.0)).astype(BF16)
        scaled_ident = jnp.where(r_k == c_q, slope, 0.0).astype(BF16)
        heads.append(jnp.concatenate([qt_ref[hh * HEAD_DIM:(hh + 1) * HEAD_DIM, qcols], alibi, scaled_ident],
                                     axis=0))
    return jnp.concatenate(heads, axis=1)


def _score_chunk(ch, prev, rhs_ref, lhs_ref, s_ref):
    n = ch & (UNITS - 1)
    rows = pl.ds(pl.multiple_of(lax.shift_right_logical(ch, 2) * DOT_ROWS, DOT_ROWS), DOT_ROWS)
    s_ref[n, rows, :] = jnp.dot(rhs_ref[prev, lax.shift_right_logical(n, 1), rows, :], lhs_ref[n],
                                preferred_element_type=F32)


def _attend(nk, vt_ref, s_ref, p_ref):
    width = GROUP * Q_BLOCK

    def probabilities(n):
        m_acc = jnp.full((RED_ROWS, width), -jnp.inf, F32)
        for rows in _row_blocks(nk, RED_ROWS):
            m_acc = jnp.maximum(m_acc, s_ref[n, rows, :])
        m = jnp.max(m_acc, axis=0, keepdims=True)
        for rows in _row_blocks(nk, RED_ROWS):
            p_ref[n % 2, rows, :] = jnp.exp((s_ref[n, rows, :] - m).astype(BF16))

    vt_one = jnp.concatenate([vt_ref[:, 0:nk], jnp.ones((2 * 8, nk), BF16)], axis=0)

    def values(n):
        o = jnp.dot(vt_one, p_ref[n % 2, 0:nk, :], preferred_element_type=F32)
        return o[0:HEAD_DIM, :] * (1.0 / o[HEAD_DIM:HEAD_DIM + 1, :])

    outs = []
    probabilities(0)
    for n in range(UNITS):
        outs.append(values(n))
        if n + 1 < UNITS:
            probabilities(n + 1)

    per_block = []
    for u in range(PAIR):
        tiles = []
        for t in range(HEADS // 2):
            o_g = outs[u * (HEADS // GROUP) + (2 * t) // GROUP]
            a = (2 * t) % GROUP
            tiles.append(jnp.concatenate([o_g[:, a * Q_BLOCK:(a + 1) * Q_BLOCK],
                                          o_g[:, (a + 1) * Q_BLOCK:(a + 2) * Q_BLOCK]], axis=0).T)
        per_block.append(tiles)
    return per_block


def _attn_kernel(qt_ref, qit_ref, wit_ref, kaug_ref, kk_ref, vt_ref, ma_ref, gate_ref, x_ref, gb_ref,
                 wo_ref, y_ref, idx_ref, rhs_ref, s_ref, p_ref, st_ref, lhs_ref, done_ref, *, seq, topk):
    b = pl.program_id(0)
    jj = pl.program_id(1)
    kf = float(topk)
    steps = seq // (PAIR * Q_BLOCK)
    qcols = [slice(u * Q_BLOCK, (u + 1) * Q_BLOCK) for u in range(PAIR)]
    cur = jj & 1
    prev = 1 - cur
    searching = jj < steps
    pending = jj >= 1

    @pl.when(jnp.logical_and(b == 0, jj == 0))
    def _():
        lhs_ref[...] = jnp.zeros(lhs_ref.shape, BF16)
        rhs_ref[:, :, :, LANES:2 * LANES] = jnp.zeros((2, PAIR, seq, LANES), BF16)

    @pl.when(jj == 0)
    def _():
        for slot in range(2):
            for u in range(PAIR):
                rhs_ref[slot, u, :, 0:LANES] = kaug_ref[...]
        y_ref[...] = jnp.zeros(y_ref.shape, F32)

    done_ref[0] = 0

    steps_per_bucket = KEY_BUCKET // (PAIR * Q_BLOCK)

    def buckets(step, active):
        return [((n + 1) * KEY_BUCKET,
                 jnp.logical_and(active, jnp.logical_and(step >= n * steps_per_bucket,
                                                         step < (n + 1) * steps_per_bucket)))
                for n in range(seq // KEY_BUCKET)]

    blocks = [PAIR * jj + u for u in range(PAIR)]
    total = UNITS * (lax.div(jnp.maximum(jj - 1, 0), steps_per_bucket) + 1)

    for nk, here in buckets(jj, searching):
        @pl.when(here)
        def _(nk=nk):
            lane_q = lax.broadcasted_iota(jnp.int32, (1, Q_BLOCK), 1)
            states = []
            for u in range(PAIR):
                n_adm = blocks[u] * Q_BLOCK + CHUNK + CHUNK * (lane_q >= CHUNK).astype(jnp.int32)
                lo, hi = _index_scores(nk, n_adm, qit_ref, qcols[u], wit_ref[:, qcols[u]], kk_ref, s_ref,
                                       idx_ref.at[u])
                n_adm_f = n_adm.astype(F32)
                c_lo = jnp.where(n_adm_f <= kf, kf, n_adm_f)
                states.append((lo, hi, c_lo, jnp.zeros((1, Q_BLOCK), F32)))

            def cond(carry):
                it, flat = carry[0], carry[1:]
                still = jnp.logical_or(flat[2] != kf, flat[6] != kf)
                return jnp.logical_and(it < FAST_TRIPS, _any_lane(still))

            def body(carry):
                it, flat = carry[0], carry[1:]
                for c in range(CHUNKS_PER_TRIP):
                    _score_chunk(jnp.minimum(it * CHUNKS_PER_TRIP + c, total - 1), prev, rhs_ref, lhs_ref, s_ref)
                st = [flat[0:4], flat[4:8]]
                for _ in range(2):
                    st = [_probe(nk, kf, idx_ref.at[u], st[u]) for u in range(PAIR)]
                return (it + 1,) + tuple(st[0]) + tuple(st[1])

            final = lax.while_loop(cond, body, (jnp.int32(0),) + tuple(states[0]) + tuple(states[1]))
            done_ref[0] = jnp.minimum(final[0] * CHUNKS_PER_TRIP, total)
            for u in range(PAIR):
                lo, hi, c_lo, _ = final[1 + 4 * u:5 + 4 * u]
                _write_selection(nk, blocks[u], lo, idx_ref.at[u], rhs_ref.at[cur, u])
                st_ref[4 * u + 0:4 * u + 1, :] = lo
                st_ref[4 * u + 1:4 * u + 2, :] = hi
                st_ref[4 * u + 2:4 * u + 3, :] = c_lo

    @pl.when(pending)
    def _():
        def chunk(ch, carry):
            _score_chunk(ch, prev, rhs_ref, lhs_ref, s_ref)
            return carry

        lax.fori_loop(done_ref[0], total, chunk, 0)

    for u in range(PAIR):
        lo = st_ref[4 * u + 0:4 * u + 1, :]
        hi = st_ref[4 * u + 1:4 * u + 2, :]
        c_lo = st_ref[4 * u + 2:4 * u + 3, :]

        @pl.when(jnp.logical_and(searching, _any_lane(c_lo != kf)))
        def _(u=u, lo=lo, hi=hi):
            _exact_fallback(blocks[u] + 1, kf, lo, hi, idx_ref.at[u], rhs_ref.at[cur, u])

    lane = lax.broadcasted_iota(jnp.int32, (PAIR * Q_BLOCK, LANES), 1)
    lo_half = lane < HALF
    for nk, here in buckets(jj - 1, pending):
        @pl.when(here)
        def _(nk=nk):
            per_block = _attend(nk, vt_ref, s_ref, p_ref)
            mixed = [ma_ref[...]]
            for t in range(HEADS // 2):
                cols = slice(t * LANES, (t + 1) * LANES)
                o = jnp.concatenate([per_block[u][t] for u in range(PAIR)], axis=0)
                msq = _half_mean_sq(o * o, lo_half)
                ob = o * lax.rsqrt(msq + NORM_EPS) * gb_ref[:, cols] * gate_ref[:, cols].astype(F32)
                mixed.append(ob.astype(BF16))
            y_ref[...] = x_ref[...] + jnp.dot(jnp.concatenate(mixed, axis=1), wo_ref[...],
                                              preferred_element_type=F32)

    @pl.when(searching)
    def _():
        for u in range(PAIR):
            for g in range(HEADS // GROUP):
                lhs_ref[u * (HEADS // GROUP) + g] = _score_operand(qt_ref, qcols[u], g)


def kernel(x, norm_gain, w_in, sgu_norm_gain, sgu_w, sgu_b, q_norm_gain, k_norm_gain,
           idx_k_norm_gain, branch_norm_gain, w_out):
    bsz, seq, d_model = x.shape
    assert d_model == D_MODEL and norm_gain.shape[0] == 1
    assert seq % PROJ_ROWS == 0 and seq % KEY_BUCKET == 0 and KEY_BUCKET % (PAIR * Q_BLOCK) == 0
    tokens = bsz * seq
    topk = min(TOPK_MAX, seq // 4)
    idx_w_scale = (IDX_HEADS ** -0.5) * (IDX_DIM ** -0.5)

    w = w_in[0]
    a3 = 3 * A_WIDTH
    w_q = w[:, a3:a3 + B_WIDTH]
    w_k = w[:, a3 + B_WIDTH:a3 + B_WIDTH + HEAD_DIM]
    w_v = w[:, a3 + B_WIDTH + HEAD_DIM:a3 + B_WIDTH + 2 * HEAD_DIM]
    o_g = a3 + B_WIDTH + 2 * HEAD_DIM
    w_g = w[:, o_g:o_g + B_WIDTH]
    o_i = o_g + B_WIDTH
    w_iq = w[:, o_i:o_i + IDX_HEADS * IDX_DIM]
    w_ik = w[:, o_i + IDX_HEADS * IDX_DIM:o_i + IDX_HEADS * IDX_DIM + IDX_DIM]
    w_iw = w[:, o_i + IDX_HEADS * IDX_DIM + IDX_DIM:]
    w_main = jnp.concatenate([w[:, :a3], w_g, w_k, w_ik], axis=1).astype(BF16)
    assert w_main.shape[1] == _PACKED_COLS
    w_t = jnp.concatenate([w_q.T, w_iq.T, w_v.T, w_iw.T,
                           jnp.zeros((_PACKED_ROWS - _ROW_W - IDX_HEADS, D_MODEL), F32)], axis=0).astype(BF16)
    x2 = x.reshape(tokens, D_MODEL)
    ng = norm_gain[0].reshape(1, D_MODEL)
    sgn = sgu_norm_gain[0].reshape(1, A_WIDTH)
    sw = sgu_w[0]
    sb = sgu_b[0].reshape(A_GROUPS, A_BLOCK, 1)
    qg = q_norm_gain[0].reshape(HEAD_DIM, 1)
    kg = jnp.concatenate([k_norm_gain[0], idx_k_norm_gain[0]]).reshape(1, LANES)
    ga = branch_norm_gain[0, :A_WIDTH].reshape(1, A_WIDTH)
    gb = branch_norm_gain[0, A_WIDTH:].reshape(1, B_WIDTH)
    wo = w_out[0].astype(BF16)

    tm = PROJ_ROWS
    full = lambda shape: pl.BlockSpec(shape, lambda i: (0,) * len(shape))
    rows = lambda width: pl.BlockSpec((tm, width), lambda i: (i, 0))
    colsT = lambda height: pl.BlockSpec((height, tm), lambda i: (0, i))
    outs = pl.pallas_call(
        functools.partial(_proj_kernel, tiles_per_seq=seq // tm, idx_w_scale=idx_w_scale),
        grid=(tokens // tm,),
        in_specs=[rows(D_MODEL), full((1, D_MODEL)), full((D_MODEL, _PACKED_COLS)),
                  full((_PACKED_ROWS, D_MODEL)), full((1, A_WIDTH)), full((A_GROUPS, A_BLOCK, A_BLOCK)),
                  full((A_GROUPS, A_BLOCK, 1)), full((HEAD_DIM, 1)), full((1, LANES)),
                  full((1, A_WIDTH))],
        out_specs=[rows(A_WIDTH), rows(B_WIDTH), rows(LANES), rows(LANES),
                   colsT(B_WIDTH), colsT(IDX_HEADS * IDX_DIM), colsT(HEAD_DIM), colsT(IDX_HEADS)],
        out_shape=[jax.ShapeDtypeStruct((tokens, A_WIDTH), BF16),
                   jax.ShapeDtypeStruct((tokens, B_WIDTH), BF16),
                   jax.ShapeDtypeStruct((tokens, LANES), BF16),
                   jax.ShapeDtypeStruct((tokens, LANES), BF16),
                   jax.ShapeDtypeStruct((B_WIDTH, tokens), BF16),
                   jax.ShapeDtypeStruct((IDX_HEADS * IDX_DIM, tokens), BF16),
                   jax.ShapeDtypeStruct((HEAD_DIM, tokens), BF16),
                   jax.ShapeDtypeStruct((IDX_HEADS, tokens), F32)],
        compiler_params=pltpu.CompilerParams(dimension_semantics=("arbitrary",),
                                             vmem_limit_bytes=VMEM_LIMIT),
        name="proj_sgu",
    )(x2, ng, w_main, w_t, sgn, sw, sb, qg, kg, ga)
    ma, gate, kaug, kk, qt, qit, vt, wit = outs

    qrows = PAIR * Q_BLOCK
    steps = seq // qrows
    qblk = lambda width: pl.BlockSpec((qrows, width), lambda b, j: (b * steps + jnp.maximum(j - 1, 0), 0))
    qblkT = lambda height: pl.BlockSpec((height, qrows), lambda b, j: (0, b * steps + jnp.minimum(j, steps - 1)))
    const = lambda shape: pl.BlockSpec(shape, lambda b, j: (0,) * len(shape))
    y = pl.pallas_call(
        functools.partial(_attn_kernel, seq=seq, topk=topk),
        grid=(bsz, steps + 1),
        in_specs=[qblkT(B_WIDTH), qblkT(IDX_HEADS * IDX_DIM), qblkT(IDX_HEADS),
                  pl.BlockSpec((seq, LANES), lambda b, j: (b, 0)),
                  pl.BlockSpec((seq, LANES), lambda b, j: (b, 0)),
                  pl.BlockSpec((HEAD_DIM, seq), lambda b, j: (0, b)),
                  qblk(A_WIDTH), qblk(B_WIDTH), qblk(D_MODEL),
                  const((1, B_WIDTH)), const((D_MODEL, D_MODEL))],
        out_specs=qblk(D_MODEL),
        out_shape=jax.ShapeDtypeStruct((tokens, D_MODEL), F32),
        scratch_shapes=[pltpu.VMEM((PAIR, seq, Q_BLOCK), F32),
                        pltpu.VMEM((2, PAIR, seq, 2 * LANES), BF16),
                        pltpu.VMEM((UNITS, seq, GROUP * Q_BLOCK), F32),
                        pltpu.VMEM((2, seq, GROUP * Q_BLOCK), BF16),
                        pltpu.VMEM((8, Q_BLOCK), F32),
                        pltpu.VMEM((UNITS, 2 * LANES, GROUP * Q_BLOCK), BF16),
                        pltpu.SMEM((1,), jnp.int32)],
        compiler_params=pltpu.CompilerParams(dimension_semantics=("arbitrary", "arbitrary"),
                                             vmem_limit_bytes=VMEM_LIMIT),
        name="dsa_attn_out",
    )(qt, qit, wit, kaug, kk, vt, ma, gate, x2, gb, wo)
    return y.reshape(bsz, seq, D_MODEL)
```

```python
import functools

import jax
import jax.numpy as jnp
from jax import lax
from jax.experimental import pallas as pl
from jax.experimental.pallas import tpu as pltpu

F32 = jnp.float32
BF16 = jnp.bfloat16

D_MODEL = 1024
CHUNK = 64
A_WIDTH = 512
A_GROUPS = 4
A_BLOCK = 128
HEADS = 8
HEAD_DIM = 64
B_WIDTH = HEADS * HEAD_DIM
IDX_HEADS = 8
IDX_DIM = 64
TOPK_MAX = 256
Q_BLOCK = 128
PAIR = 2
NORM_EPS = 1e-6
MASK_OFF = 1e32
LANES = 128
HALF = LANES // 2
PROJ_ROWS = 512
DOT_ROWS = 512
RED_ROWS = 128
KEY_BUCKET = 512
FAST_TRIPS = 24
INTERP_MARGIN = 0.02
SLOW_TRIPS = 70
VMEM_LIMIT = 48 * 1024 * 1024

_OFF_U, _OFF_V, _OFF_Z, _OFF_G, _OFF_K = 0, 512, 1024, 1536, 2048
_PACKED_COLS = 2304
_ROW_Q, _ROW_QI, _PACKED_ROWS = 0, 512, 1024

_NT = (((1,), (1,)), ((), ()))


def _gelu(x):
    c = 0.7978845608028654
    return 0.5 * x * (1.0 + jnp.tanh(c * (x + 0.044715 * (x * x * x))))


def _silu(x):
    return x / (1.0 + jnp.exp(-x))


def _row_blocks(total, size):
    return [slice(r, min(r + size, total)) for r in range(0, total, size)]


def _half_mean_sq(x2, lo_half):
    tot = jnp.sum(x2, axis=-1, keepdims=True)
    lo = jnp.sum(jnp.where(lo_half, x2, 0.0), axis=-1, keepdims=True)
    return jnp.where(lo_half, lo, tot - lo) * (1.0 / HALF)


def _proj_kernel(x_ref, ng_ref, w_ref, wt_ref, sgn_ref, sw_ref, sb_ref, qg_ref, kg_ref, ga_ref,
                 ma_ref, gate_ref, kaug_ref, kk_ref, qt_ref, qit_ref, vt_ref, wit_ref,
                 *, tiles_per_seq, idx_w_scale):
    tm = x_ref.shape[0]
    i = pl.program_id(0)
    x = x_ref[...]
    ms = jnp.mean(x * x, axis=-1, keepdims=True)
    h = (x * lax.rsqrt(ms + NORM_EPS) * ng_ref[...]).astype(BF16)

    lane = lax.broadcasted_iota(jnp.int32, (tm, LANES), 1)
    lo_half = lane < HALF

    def proj(off, width):
        return jnp.dot(h, w_ref[:, off:off + width], preferred_element_type=F32)

    gu = _gelu(proj(_OFF_U, A_WIDTH))
    gv = _gelu(proj(_OFF_V, A_WIDTH))
    pz = proj(_OFF_Z, A_WIDTH)
    r_i = lax.broadcasted_iota(jnp.int32, (A_BLOCK, A_BLOCK), 0)
    c_j = lax.broadcasted_iota(jnp.int32, (A_BLOCK, A_BLOCK), 1)
    causal = lax.shift_right_logical(c_j, 6) <= lax.shift_right_logical(r_i, 6)
    for g in range(A_GROUPS):
        cols = slice(g * LANES, (g + 1) * LANES)
        vg = gv[:, cols]
        mu = jnp.mean(vg, axis=-1, keepdims=True)
        d = vg - mu
        var = jnp.mean(d * d, axis=-1, keepdims=True)
        vn = (d * lax.rsqrt(var + NORM_EPS) * sgn_ref[:, cols]).astype(BF16)
        wg = jnp.where(causal, sw_ref[g], 0.0).astype(BF16)
        side = jnp.concatenate([vn[blk * A_BLOCK:(blk + 1) * A_BLOCK, :] for blk in range(tm // A_BLOCK)], axis=1)
        mixed = jnp.dot(wg, side, preferred_element_type=F32) + sb_ref[g]
        for blk in range(tm // A_BLOCK):
            rows = slice(blk * A_BLOCK, (blk + 1) * A_BLOCK)
            s = mixed[:, blk * LANES:(blk + 1) * LANES]
            ya = gu[rows, cols] * s
            oa = ya * lax.rsqrt(jnp.mean(ya * ya, axis=-1, keepdims=True) + NORM_EPS) * ga_ref[:, cols]
            ma_ref[rows, cols] = (oa * _silu(pz[rows, cols])).astype(BF16)

    gate_ref[...] = _silu(proj(_OFF_G, B_WIDTH)).astype(BF16)

    pkv = proj(_OFF_K, 2 * LANES)
    pk = pkv[:, 0:LANES]
    k_ms = jnp.sum(jnp.where(lo_half, pk * pk, 0.0), axis=-1, keepdims=True) * (1.0 / HALF)
    kn = pk * lax.rsqrt(k_ms + NORM_EPS)
    ik_mu = jnp.sum(jnp.where(lo_half, 0.0, pk), axis=-1, keepdims=True) * (1.0 / HALF)
    dk = pk - ik_mu
    ik_var = jnp.sum(jnp.where(lo_half, 0.0, dk * dk), axis=-1, keepdims=True) * (1.0 / HALF)
    kin = dk * lax.rsqrt(ik_var + NORM_EPS)
    tile = jnp.where(lo_half, kn, kin) * kg_ref[...]
    swapped = pltpu.roll(tile, HALF, axis=1)
    row = lax.broadcasted_iota(jnp.int32, (tm, LANES), 0)
    pos = (i % tiles_per_seq) * tm + row
    pos_hi = lax.shift_right_logical(pos, 6).astype(F32)
    pos_lo = (pos & (CHUNK - 1)).astype(F32)
    posfeat = jnp.where(lane == HALF, pos_hi, jnp.where(lane == HALF + 1, pos_lo, 0.0))
    kaug_ref[...] = jnp.where(lo_half, tile, posfeat).astype(BF16)
    kk_ref[...] = jnp.where(lo_half, swapped, tile).astype(BF16)

    pt = lax.dot_general(wt_ref[...], h, _NT, preferred_element_type=F32)
    for hh in range(HEADS):
        rows = slice(_ROW_Q + hh * HEAD_DIM, _ROW_Q + (hh + 1) * HEAD_DIM)
        xq = pt[rows, :]
        msq = jnp.mean(xq * xq, axis=0, keepdims=True)
        qt_ref[rows, :] = (xq * lax.rsqrt(msq + NORM_EPS) * qg_ref[...] * (HEAD_DIM ** -0.5)).astype(BF16)
    qit_ref[...] = pt[_ROW_QI:_ROW_QI + IDX_HEADS * IDX_DIM, :].astype(BF16)
    for blk in range(tm // LANES):
        cols = slice(blk * LANES, (blk + 1) * LANES)
        vw_t = pkv[cols, LANES:2 * LANES].T
        vt_ref[:, cols] = vw_t[0:HEAD_DIM, :].astype(BF16)
        wit_ref[:, cols] = vw_t[HEAD_DIM:HEAD_DIM + IDX_HEADS, :] * idx_w_scale


def _tile_iotas():
    r_k = lax.broadcasted_iota(jnp.int32, (Q_BLOCK, Q_BLOCK), 0)
    c_q = lax.broadcasted_iota(jnp.int32, (Q_BLOCK, Q_BLOCK), 1)
    return r_k, c_q


def _colsum(x):
    return jnp.sum(x, axis=0, keepdims=True)


def _any_lane(pred):
    return jnp.max(jnp.where(pred, 1.0, 0.0)) > 0.0


def _index_scores(nk, n_adm, qit_ref, qcols, wi, kk_ref, s_ref, idx_ref):
    r_k, c_q = _tile_iotas()
    top_rows = r_k < HALF
    per_head = []
    for t in range(IDX_HEADS // 2):
        qit = qit_ref[t * LANES:(t + 1) * LANES, qcols]
        zero = jnp.zeros_like(qit)
        per_head += [jnp.where(top_rows, qit, zero), jnp.where(top_rows, zero, qit)]
    half_heads = IDX_HEADS // 2
    for g in range(2):
        wg = jnp.concatenate(per_head[g * half_heads:(g + 1) * half_heads], axis=1)
        for rows in _row_blocks(nk, DOT_ROWS):
            s_ref[g, rows, :] = jnp.dot(kk_ref[rows, :], wg, preferred_element_type=F32)

    def weighted_relu(g, rows):
        acc = None
        for i in range(half_heads):
            hh = g * half_heads + i
            term = jnp.maximum(s_ref[g, rows, i * Q_BLOCK:(i + 1) * Q_BLOCK], 0.0) * wi[hh:hh + 1, :]
            acc = term if acc is None else acc + term
        return acc

    slabs = nk // RED_ROWS
    mn_acc = jnp.full((RED_ROWS, Q_BLOCK), jnp.inf, F32)
    mx_acc = jnp.full((RED_ROWS, Q_BLOCK), -jnp.inf, F32)
    tiny_acc = jnp.full((RED_ROWS, Q_BLOCK), jnp.inf, F32)
    for r in range(slabs):
        rows = slice(r * RED_ROWS, (r + 1) * RED_ROWS)
        acc = weighted_relu(0, rows) + weighted_relu(1, rows)
        if r >= slabs - KEY_BUCKET // RED_ROWS:
            adm = (r * RED_ROWS + r_k) < n_adm
            lo_fill = jnp.where(adm, acc, -jnp.inf)
            hi_fill = jnp.where(adm, acc, jnp.inf)
        else:
            lo_fill = hi_fill = acc
        mag = jnp.abs(hi_fill)
        idx_ref[rows, :] = lo_fill
        mn_acc = jnp.minimum(mn_acc, hi_fill)
        mx_acc = jnp.maximum(mx_acc, lo_fill)
        tiny_acc = jnp.minimum(tiny_acc, jnp.where(mag == 0.0, jnp.inf, mag))
    lo = jnp.min(mn_acc, axis=0, keepdims=True)
    hi = jnp.max(mx_acc, axis=0, keepdims=True)
    tiny = jnp.min(tiny_acc, axis=0, keepdims=True)

    unit = jnp.where(tiny < jnp.inf, tiny, 1.0)
    eps = unit * (0.5 / nk)
    rank0 = (1 + r_k).astype(F32)

    def spread(r, carry):
        rows = pl.ds(pl.multiple_of(r * RED_ROWS, RED_ROWS), RED_ROWS)
        s = idx_ref[rows, :]
        rank = rank0 + jnp.asarray(r * RED_ROWS, F32)
        idx_ref[rows, :] = jnp.where(s == 0.0, -(rank * eps), s)
        return carry

    lax.fori_loop(0, slabs, spread, 0)
    return jnp.minimum(lo, -0.5 * unit), hi


def _probe(nk, kf, idx_ref, state):
    lo, hi, c_lo, c_hi = state
    frac = (c_lo - kf) / jnp.maximum(c_lo - c_hi, 1.0)
    frac = jnp.minimum(jnp.maximum(frac, INTERP_MARGIN), 1.0 - INTERP_MARGIN)
    t = lo + (hi - lo) * frac
    acc = jnp.zeros((RED_ROWS, Q_BLOCK), F32)
    for r in range(nk // RED_ROWS):
        acc = acc + jnp.where(idx_ref[r * RED_ROWS:(r + 1) * RED_ROWS, :] >= t, 1.0, 0.0)
    c = _colsum(acc)
    ge = c >= kf
    return (jnp.where(ge, t, lo), jnp.where(ge, hi, t), jnp.where(ge, c, c_lo), jnp.where(ge, c_hi, c))


def _write_selection(nk, j_blk, lo, idx_ref, rhs_ref):
    r_k, c_q = _tile_iotas()
    later = 2.0 * jnp.maximum(r_k - c_q, 0).astype(F32)

    def write(r, carry):
        rows = pl.ds(pl.multiple_of(r * RED_ROWS, RED_ROWS), RED_ROWS)
        on = jnp.where(r == j_blk, -later, 0.0)
        rhs_ref[rows, LANES:2 * LANES] = jnp.where(idx_ref[rows, :] >= lo, on, -MASK_OFF).astype(BF16)
        return carry

    lax.fori_loop(0, nk // RED_ROWS, write, 0)


def _exact_fallback(slabs, kf, lo, hi, idx_ref, rhs_ref):
    r_k, c_q = _tile_iotas()
    later = 2.0 * jnp.maximum(r_k - c_q, 0).astype(F32)

    def slab(r):
        return idx_ref[pl.ds(pl.multiple_of(r * RED_ROWS, RED_ROWS), RED_ROWS), :]

    def count(pred):
        def body(r, acc):
            return acc + jnp.where(pred(slab(r), r), 1.0, 0.0)
        return _colsum(lax.fori_loop(0, slabs, body, jnp.zeros((RED_ROWS, Q_BLOCK), F32)))

    def kth(lo):
        def body(r, acc):
            s = slab(r)
            return jnp.minimum(acc, jnp.where(s >= lo, s, jnp.inf))
        acc = lax.fori_loop(0, slabs, body, jnp.full((RED_ROWS, Q_BLOCK), jnp.inf, F32))
        thr = jnp.min(acc, axis=0, keepdims=True)
        return thr, count(lambda s, r: s > thr)

    def slow_cond(carry):
        it, _, _, _, c_gt = carry
        return jnp.logical_and(it < SLOW_TRIPS, _any_lane(c_gt >= kf))

    def slow_body(carry):
        it, lo, hi, _, _ = carry
        for _ in range(4):
            mid = 0.5 * lo + 0.5 * hi
            ge = count(lambda s, r: s >= mid) >= kf
            lo, hi = jnp.where(ge, mid, lo), jnp.where(ge, hi, mid)
        thr, c_gt = kth(lo)
        return it + 1, lo, hi, thr, c_gt

    thr0, c_gt0 = kth(lo)
    _, _, _, thr, c_gt = lax.while_loop(slow_cond, slow_body, (jnp.int32(0), lo, hi, thr0, c_gt0))
    c_eq = count(lambda s, r: s == thr)
    need = (c_gt + c_eq) > kf

    def tie_step(_, carry):
        lo_i, hi_i = carry
        mid_i = jnp.floor((lo_i + hi_i) * 0.5)
        below = count(lambda s, r: jnp.logical_and(s == thr, (r * RED_ROWS + r_k).astype(F32) <= mid_i))
        ok = (c_gt + below) >= kf
        return jnp.where(ok, lo_i, mid_i), jnp.where(ok, mid_i, hi_i)

    last = jnp.asarray(slabs * RED_ROWS - 1, F32)
    lo_i = jnp.full((1, Q_BLOCK), -1.0, F32)
    hi_i = jnp.zeros((1, Q_BLOCK), F32) + last
    _, cut = lax.fori_loop(0, 13, tie_step, (lo_i, hi_i))
    cut = jnp.where(need, cut, last + 1.0).astype(jnp.int32)

    def write(r, carry):
        s = slab(r)
        krow = r * RED_ROWS + r_k
        sel = jnp.logical_or(s > thr, jnp.logical_and(s == thr, krow <= cut))
        on = jnp.where(r == slabs - 1, -later, 0.0)
        rhs_ref[pl.ds(pl.multiple_of(r * RED_ROWS, RED_ROWS), RED_ROWS), LANES:2 * LANES] = (
            jnp.where(sel, on, -MASK_OFF).astype(BF16))
        return carry

    lax.fori_loop(0, slabs, write, 0)


def _attend(nk, qt_ref, qcols, vt_ref, rhs_ref, s_ref, p_ref):
    r_k, c_q = _tile_iotas()
    feat = lax.broadcasted_iota(jnp.int32, (HALF, Q_BLOCK), 0)
    group = HEADS // 2
    width = group * Q_BLOCK
    units = [(u, g) for u in range(len(qcols)) for g in range(HEADS // group)]

    def score_operand(u, hh):
        slope = 2.0 ** (-(hh + 1))
        alibi = jnp.where(feat == 0, CHUNK * slope, jnp.where(feat == 1, slope, 0.0)).astype(BF16)
        scaled_ident = jnp.where(r_k == c_q, slope, 0.0).astype(BF16)
        return jnp.concatenate([qt_ref[hh * HEAD_DIM:(hh + 1) * HEAD_DIM, qcols[u]], alibi, scaled_ident], axis=0)

    def scores(n):
        u, g = units[n]
        lhs_t = jnp.concatenate([score_operand(u, g * group + i) for i in range(group)], axis=1)
        m_acc = jnp.full((RED_ROWS, width), -jnp.inf, F32)
        for rows in _row_blocks(nk, DOT_ROWS):
            blk = jnp.dot(rhs_ref[u, rows, :], lhs_t, preferred_element_type=F32)
            s_ref[n % 2, rows, :] = blk
            for sub in range((rows.stop - rows.start) // RED_ROWS):
                m_acc = jnp.maximum(m_acc, blk[sub * RED_ROWS:(sub + 1) * RED_ROWS, :])
        return jnp.max(m_acc, axis=0, keepdims=True)

    def probabilities(n, m):
        for rows in _row_blocks(nk, RED_ROWS):
            p_ref[n % 2, rows, :] = jnp.exp((s_ref[n % 2, rows, :] - m).astype(BF16))

    vt_one = jnp.concatenate([vt_ref[:, 0:nk], jnp.ones((2 * 8, nk), BF16)], axis=0)

    def values(n):
        o = jnp.dot(vt_one, p_ref[n % 2, 0:nk, :], preferred_element_type=F32)
        return o[0:HEAD_DIM, :] * (1.0 / o[HEAD_DIM:HEAD_DIM + 1, :])

    outs = []
    m = scores(0)
    for n in range(len(units)):
        probabilities(n, m)
        if n + 1 < len(units):
            m = scores(n + 1)
        outs.append(values(n))

    per_block = []
    for u in range(len(qcols)):
        tiles = []
        for t in range(HEADS // 2):
            o_g = outs[u * (HEADS // group) + (2 * t) // group]
            a = (2 * t) % group
            tiles.append(jnp.concatenate([o_g[:, a * Q_BLOCK:(a + 1) * Q_BLOCK],
                                          o_g[:, (a + 1) * Q_BLOCK:(a + 2) * Q_BLOCK]], axis=0).T)
        per_block.append(tiles)
    return per_block


def _attn_kernel(qt_ref, qit_ref, wit_ref, kaug_ref, kk_ref, vt_ref, ma_ref, gate_ref, x_ref, gb_ref,
                 wo_ref, y_ref, idx_ref, rhs_ref, s_ref, p_ref, st_ref, *, seq, topk):
    jj = pl.program_id(1)
    kf = float(topk)
    qcols = [slice(u * Q_BLOCK, (u + 1) * Q_BLOCK) for u in range(PAIR)]

    @pl.when(jj == 0)
    def _():
        for u in range(PAIR):
            rhs_ref[u, :, 0:LANES] = kaug_ref[...]

    steps_per_bucket = KEY_BUCKET // (PAIR * Q_BLOCK)
    buckets = [((n + 1) * KEY_BUCKET, jnp.logical_and(jj >= n * steps_per_bucket, jj < (n + 1) * steps_per_bucket))
               for n in range(seq // KEY_BUCKET)]
    blocks = [PAIR * jj + u for u in range(PAIR)]

    for nk, here in buckets:
        @pl.when(here)
        def _(nk=nk):
            lane_q = lax.broadcasted_iota(jnp.int32, (1, Q_BLOCK), 1)
            states = []
            for u in range(PAIR):
                n_adm = blocks[u] * Q_BLOCK + CHUNK + CHUNK * (lane_q >= CHUNK).astype(jnp.int32)
                lo, hi = _index_scores(nk, n_adm, qit_ref, qcols[u], wit_ref[:, qcols[u]], kk_ref, s_ref,
                                       idx_ref.at[u])
                n_adm_f = n_adm.astype(F32)
                c_lo = jnp.where(n_adm_f <= kf, kf, n_adm_f)
                states.append((lo, hi, c_lo, jnp.zeros((1, Q_BLOCK), F32)))

            def cond(carry):
                it, flat = carry[0], carry[1:]
                still = jnp.logical_or(flat[2] != kf, flat[6] != kf)
                return jnp.logical_and(it < FAST_TRIPS, _any_lane(still))

            def body(carry):
                it, flat = carry[0], carry[1:]
                st = [flat[0:4], flat[4:8]]
                for _ in range(2):
                    st = [_probe(nk, kf, idx_ref.at[u], st[u]) for u in range(PAIR)]
                return (it + 1,) + tuple(st[0]) + tuple(st[1])

            final = lax.while_loop(cond, body, (jnp.int32(0),) + tuple(states[0]) + tuple(states[1]))[1:]
            for u in range(PAIR):
                lo, hi, c_lo, _ = final[4 * u:4 * u + 4]
                _write_selection(nk, blocks[u], lo, idx_ref.at[u], rhs_ref.at[u])
                st_ref[4 * u + 0:4 * u + 1, :] = lo
                st_ref[4 * u + 1:4 * u + 2, :] = hi
                st_ref[4 * u + 2:4 * u + 3, :] = c_lo

    for u in range(PAIR):
        lo = st_ref[4 * u + 0:4 * u + 1, :]
        hi = st_ref[4 * u + 1:4 * u + 2, :]
        c_lo = st_ref[4 * u + 2:4 * u + 3, :]

        @pl.when(_any_lane(c_lo != kf))
        def _(u=u, lo=lo, hi=hi):
            _exact_fallback(blocks[u] + 1, kf, lo, hi, idx_ref.at[u], rhs_ref.at[u])

    lane = lax.broadcasted_iota(jnp.int32, (PAIR * Q_BLOCK, LANES), 1)
    lo_half = lane < HALF
    for nk, here in buckets:
        @pl.when(here)
        def _(nk=nk):
            per_block = _attend(nk, qt_ref, qcols, vt_ref, rhs_ref, s_ref, p_ref)
            mixed = [ma_ref[...]]
            for t in range(HEADS // 2):
                cols = slice(t * LANES, (t + 1) * LANES)
                o = jnp.concatenate([per_block[u][t] for u in range(PAIR)], axis=0)
                msq = _half_mean_sq(o * o, lo_half)
                ob = o * lax.rsqrt(msq + NORM_EPS) * gb_ref[:, cols] * gate_ref[:, cols].astype(F32)
                mixed.append(ob.astype(BF16))
            y_ref[...] = x_ref[...] + jnp.dot(jnp.concatenate(mixed, axis=1), wo_ref[...],
                                              preferred_element_type=F32)


def kernel(x, norm_gain, w_in, sgu_norm_gain, sgu_w, sgu_b, q_norm_gain, k_norm_gain,
           idx_k_norm_gain, branch_norm_gain, w_out):
    bsz, seq, d_model = x.shape
    assert d_model == D_MODEL and norm_gain.shape[0] == 1
    assert seq % PROJ_ROWS == 0 and seq % KEY_BUCKET == 0 and KEY_BUCKET % (PAIR * Q_BLOCK) == 0
    tokens = bsz * seq
    topk = min(TOPK_MAX, seq // 4)
    idx_w_scale = (IDX_HEADS ** -0.5) * (IDX_DIM ** -0.5)

    w = w_in[0]
    a3 = 3 * A_WIDTH
    w_q = w[:, a3:a3 + B_WIDTH]
    w_k = w[:, a3 + B_WIDTH:a3 + B_WIDTH + HEAD_DIM]
    w_v = w[:, a3 + B_WIDTH + HEAD_DIM:a3 + B_WIDTH + 2 * HEAD_DIM]
    o_g = a3 + B_WIDTH + 2 * HEAD_DIM
    w_g = w[:, o_g:o_g + B_WIDTH]
    o_i = o_g + B_WIDTH
    w_iq = w[:, o_i:o_i + IDX_HEADS * IDX_DIM]
    w_ik = w[:, o_i + IDX_HEADS * IDX_DIM:o_i + IDX_HEADS * IDX_DIM + IDX_DIM]
    w_iw = w[:, o_i + IDX_HEADS * IDX_DIM + IDX_DIM:]
    w_main = jnp.concatenate([w[:, :a3], w_g, w_k, w_ik, w_v, w_iw,
                              jnp.zeros((D_MODEL, LANES - HEAD_DIM - IDX_HEADS), F32)], axis=1).astype(BF16)
    assert w_main.shape[1] == _PACKED_COLS
    w_t = jnp.concatenate([w_q.T, w_iq.T], axis=0).astype(BF16)
    assert w_t.shape[0] == _PACKED_ROWS
    x2 = x.reshape(tokens, D_MODEL)
    ng = norm_gain[0].reshape(1, D_MODEL)
    sgn = sgu_norm_gain[0].reshape(1, A_WIDTH)
    sw = sgu_w[0]
    sb = sgu_b[0].reshape(A_GROUPS, A_BLOCK, 1)
    qg = q_norm_gain[0].reshape(HEAD_DIM, 1)
    kg = jnp.concatenate([k_norm_gain[0], idx_k_norm_gain[0]]).reshape(1, LANES)
    ga = branch_norm_gain[0, :A_WIDTH].reshape(1, A_WIDTH)
    gb = branch_norm_gain[0, A_WIDTH:].reshape(1, B_WIDTH)
    wo = w_out[0].astype(BF16)

    tm = PROJ_ROWS
    full = lambda shape: pl.BlockSpec(shape, lambda i: (0,) * len(shape))
    rows = lambda width: pl.BlockSpec((tm, width), lambda i: (i, 0))
    colsT = lambda height: pl.BlockSpec((height, tm), lambda i: (0, i))
    outs = pl.pallas_call(
        functools.partial(_proj_kernel, tiles_per_seq=seq // tm, idx_w_scale=idx_w_scale),
        grid=(tokens // tm,),
        in_specs=[rows(D_MODEL), full((1, D_MODEL)), full((D_MODEL, _PACKED_COLS)),
                  full((_PACKED_ROWS, D_MODEL)), full((1, A_WIDTH)), full((A_GROUPS, A_BLOCK, A_BLOCK)),
                  full((A_GROUPS, A_BLOCK, 1)), full((HEAD_DIM, 1)), full((1, LANES)),
                  full((1, A_WIDTH))],
        out_specs=[rows(A_WIDTH), rows(B_WIDTH), rows(LANES), rows(LANES),
                   colsT(B_WIDTH), colsT(IDX_HEADS * IDX_DIM), colsT(HEAD_DIM), colsT(IDX_HEADS)],
        out_shape=[jax.ShapeDtypeStruct((tokens, A_WIDTH), BF16),
                   jax.ShapeDtypeStruct((tokens, B_WIDTH), BF16),
                   jax.ShapeDtypeStruct((tokens, LANES), BF16),
                   jax.ShapeDtypeStruct((tokens, LANES), BF16),
                   jax.ShapeDtypeStruct((B_WIDTH, tokens), BF16),
                   jax.ShapeDtypeStruct((IDX_HEADS * IDX_DIM, tokens), BF16),
                   jax.ShapeDtypeStruct((HEAD_DIM, tokens), BF16),
                   jax.ShapeDtypeStruct((IDX_HEADS, tokens), F32)],
        compiler_params=pltpu.CompilerParams(dimension_semantics=("arbitrary",),
                                             vmem_limit_bytes=VMEM_LIMIT),
        name="proj_sgu",
    )(x2, ng, w_main, w_t, sgn, sw, sb, qg, kg, ga)
    ma, gate, kaug, kk, qt, qit, vt, wit = outs

    qrows = PAIR * Q_BLOCK
    steps = seq // qrows
    qblk = lambda width: pl.BlockSpec((qrows, width), lambda b, j: (b * steps + j, 0))
    qblkT = lambda height: pl.BlockSpec((height, qrows), lambda b, j: (0, b * steps + j))
    const = lambda shape: pl.BlockSpec(shape, lambda b, j: (0,) * len(shape))
    y = pl.pallas_call(
        functools.partial(_attn_kernel, seq=seq, topk=topk),
        grid=(bsz, steps),
        in_specs=[qblkT(B_WIDTH), qblkT(IDX_HEADS * IDX_DIM), qblkT(IDX_HEADS),
                  pl.BlockSpec((seq, LANES), lambda b, j: (b, 0)),
                  pl.BlockSpec((seq, LANES), lambda b, j: (b, 0)),
                  pl.BlockSpec((HEAD_DIM, seq), lambda b, j: (0, b)),
                  qblk(A_WIDTH), qblk(B_WIDTH), qblk(D_MODEL),
                  const((1, B_WIDTH)), const((D_MODEL, D_MODEL))],
        out_specs=qblk(D_MODEL),
        out_shape=jax.ShapeDtypeStruct((tokens, D_MODEL), F32),
        scratch_shapes=[pltpu.VMEM((PAIR, seq, Q_BLOCK), F32),
                        pltpu.VMEM((PAIR, seq, 2 * LANES), BF16),
                        pltpu.VMEM((2, seq, (HEADS // 2) * Q_BLOCK), F32),
                        pltpu.VMEM((2, seq, (HEADS // 2) * Q_BLOCK), BF16),
                        pltpu.VMEM((8, Q_BLOCK), F32)],
        compiler_params=pltpu.CompilerParams(dimension_semantics=("arbitrary", "arbitrary"),
                                             vmem_limit_bytes=VMEM_LIMIT),
        name="dsa_attn_out",
    )(qt, qit, wit, kaug, kk, vt, ma, gate, x2, gb, wo)
    return y.reshape(bsz, seq, D_MODEL)
```

```python
import functools

import jax
import jax.numpy as jnp
from jax import lax
from jax.experimental import pallas as pl
from jax.experimental.pallas import tpu as pltpu

F32 = jnp.float32
BF16 = jnp.bfloat16

D_MODEL = 1024
CHUNK = 64
A_WIDTH = 512
A_GROUPS = 4
A_BLOCK = 128
HEADS = 8
HEAD_DIM = 64
B_WIDTH = HEADS * HEAD_DIM
IDX_HEADS = 8
IDX_DIM = 64
TOPK_MAX = 256
Q_BLOCK = 128
PAIR = 2
NORM_EPS = 1e-6
MASK_OFF = 1e32
LANES = 128
HALF = LANES // 2
PROJ_ROWS = 512
DOT_ROWS = 512
RED_ROWS = 128
KEY_BUCKET = 512
FAST_TRIPS = 24
INTERP_MARGIN = 0.02
SLOW_TRIPS = 70
CHUNKS_PER_TRIP = 2
VMEM_LIMIT = 48 * 1024 * 1024

_OFF_U, _OFF_V, _OFF_Z, _OFF_G, _OFF_K = 0, 512, 1024, 1536, 2048
_PACKED_COLS = 2304
_ROW_Q, _ROW_QI, _PACKED_ROWS = 0, 512, 1024

_NT = (((1,), (1,)), ((), ()))


def _gelu(x):
    c = 0.7978845608028654
    return 0.5 * x * (1.0 + jnp.tanh(c * (x + 0.044715 * (x * x * x))))


def _silu(x):
    return x / (1.0 + jnp.exp(-x))


def _row_blocks(total, size):
    return [slice(r, min(r + size, total)) for r in range(0, total, size)]


def _half_mean_sq(x2, lo_half):
    tot = jnp.sum(x2, axis=-1, keepdims=True)
    lo = jnp.sum(jnp.where(lo_half, x2, 0.0), axis=-1, keepdims=True)
    return jnp.where(lo_half, lo, tot - lo) * (1.0 / HALF)


def _proj_kernel(x_ref, ng_ref, w_ref, wt_ref, sgn_ref, sw_ref, sb_ref, qg_ref, kg_ref, ga_ref,
                 ma_ref, gate_ref, kaug_ref, kk_ref, qt_ref, qit_ref, vt_ref, wit_ref,
                 *, tiles_per_seq, idx_w_scale):
    tm = x_ref.shape[0]
    i = pl.program_id(0)
    x = x_ref[...]
    ms = jnp.mean(x * x, axis=-1, keepdims=True)
    h = (x * lax.rsqrt(ms + NORM_EPS) * ng_ref[...]).astype(BF16)

    lane = lax.broadcasted_iota(jnp.int32, (tm, LANES), 1)
    lo_half = lane < HALF

    def proj(off, width):
        return jnp.dot(h, w_ref[:, off:off + width], preferred_element_type=F32)

    gu = _gelu(proj(_OFF_U, A_WIDTH))
    gv = _gelu(proj(_OFF_V, A_WIDTH))
    pz = proj(_OFF_Z, A_WIDTH)
    r_i = lax.broadcasted_iota(jnp.int32, (A_BLOCK, A_BLOCK), 0)
    c_j = lax.broadcasted_iota(jnp.int32, (A_BLOCK, A_BLOCK), 1)
    causal = lax.shift_right_logical(c_j, 6) <= lax.shift_right_logical(r_i, 6)
    for g in range(A_GROUPS):
        cols = slice(g * LANES, (g + 1) * LANES)
        vg = gv[:, cols]
        mu = jnp.mean(vg, axis=-1, keepdims=True)
        d = vg - mu
        var = jnp.mean(d * d, axis=-1, keepdims=True)
        vn = (d * lax.rsqrt(var + NORM_EPS) * sgn_ref[:, cols]).astype(BF16)
        wg = jnp.where(causal, sw_ref[g], 0.0).astype(BF16)
        side = jnp.concatenate([vn[blk * A_BLOCK:(blk + 1) * A_BLOCK, :] for blk in range(tm // A_BLOCK)], axis=1)
        mixed = jnp.dot(wg, side, preferred_element_type=F32) + sb_ref[g]
        for blk in range(tm // A_BLOCK):
            rows = slice(blk * A_BLOCK, (blk + 1) * A_BLOCK)
            s = mixed[:, blk * LANES:(blk + 1) * LANES]
            ya = gu[rows, cols] * s
            oa = ya * lax.rsqrt(jnp.mean(ya * ya, axis=-1, keepdims=True) + NORM_EPS) * ga_ref[:, cols]
            ma_ref[rows, cols] = (oa * _silu(pz[rows, cols])).astype(BF16)

    gate_ref[...] = _silu(proj(_OFF_G, B_WIDTH)).astype(BF16)

    pkv = proj(_OFF_K, 2 * LANES)
    pk = pkv[:, 0:LANES]
    k_ms = jnp.sum(jnp.where(lo_half, pk * pk, 0.0), axis=-1, keepdims=True) * (1.0 / HALF)
    kn = pk * lax.rsqrt(k_ms + NORM_EPS)
    ik_mu = jnp.sum(jnp.where(lo_half, 0.0, pk), axis=-1, keepdims=True) * (1.0 / HALF)
    dk = pk - ik_mu
    ik_var = jnp.sum(jnp.where(lo_half, 0.0, dk * dk), axis=-1, keepdims=True) * (1.0 / HALF)
    kin = dk * lax.rsqrt(ik_var + NORM_EPS)
    tile = jnp.where(lo_half, kn, kin) * kg_ref[...]
    swapped = pltpu.roll(tile, HALF, axis=1)
    row = lax.broadcasted_iota(jnp.int32, (tm, LANES), 0)
    pos = (i % tiles_per_seq) * tm + row
    pos_hi = lax.shift_right_logical(pos, 6).astype(F32)
    pos_lo = (pos & (CHUNK - 1)).astype(F32)
    posfeat = jnp.where(lane == HALF, pos_hi, jnp.where(lane == HALF + 1, pos_lo, 0.0))
    kaug_ref[...] = jnp.where(lo_half, tile, posfeat).astype(BF16)
    kk_ref[...] = jnp.where(lo_half, swapped, tile).astype(BF16)

    pt = lax.dot_general(wt_ref[...], h, _NT, preferred_element_type=F32)
    for hh in range(HEADS):
        rows = slice(_ROW_Q + hh * HEAD_DIM, _ROW_Q + (hh + 1) * HEAD_DIM)
        xq = pt[rows, :]
        msq = jnp.mean(xq * xq, axis=0, keepdims=True)
        qt_ref[rows, :] = (xq * lax.rsqrt(msq + NORM_EPS) * qg_ref[...] * (HEAD_DIM ** -0.5)).astype(BF16)
    qit_ref[...] = pt[_ROW_QI:_ROW_QI + IDX_HEADS * IDX_DIM, :].astype(BF16)
    for blk in range(tm // LANES):
        cols = slice(blk * LANES, (blk + 1) * LANES)
        vw_t = pkv[cols, LANES:2 * LANES].T
        vt_ref[:, cols] = vw_t[0:HEAD_DIM, :].astype(BF16)
        wit_ref[:, cols] = vw_t[HEAD_DIM:HEAD_DIM + IDX_HEADS, :] * idx_w_scale


def _tile_iotas():
    r_k = lax.broadcasted_iota(jnp.int32, (Q_BLOCK, Q_BLOCK), 0)
    c_q = lax.broadcasted_iota(jnp.int32, (Q_BLOCK, Q_BLOCK), 1)
    return r_k, c_q


def _colsum(x):
    return jnp.sum(x, axis=0, keepdims=True)


def _any_lane(pred):
    return jnp.max(jnp.where(pred, 1.0, 0.0)) > 0.0


def _index_scores(nk, n_adm, qit_ref, qcols, wi, kk_ref, s_ref, idx_ref):
    r_k, c_q = _tile_iotas()
    top_rows = r_k < HALF
    per_head = []
    for t in range(IDX_HEADS // 2):
        qit = qit_ref[t * LANES:(t + 1) * LANES, qcols]
        zero = jnp.zeros_like(qit)
        per_head += [jnp.where(top_rows, qit, zero), jnp.where(top_rows, zero, qit)]
    half_heads = IDX_HEADS // 2
    for g in range(2):
        wg = jnp.concatenate(per_head[g * half_heads:(g + 1) * half_heads], axis=1)
        for rows in _row_blocks(nk, DOT_ROWS):
            s_ref[g, rows, :] = jnp.dot(kk_ref[rows, :], wg, preferred_element_type=F32)

    def weighted_relu(g, rows):
        acc = None
        for i in range(half_heads):
            hh = g * half_heads + i
            term = jnp.maximum(s_ref[g, rows, i * Q_BLOCK:(i + 1) * Q_BLOCK], 0.0) * wi[hh:hh + 1, :]
            acc = term if acc is None else acc + term
        return acc

    slabs = nk // RED_ROWS
    mn_acc = jnp.full((RED_ROWS, Q_BLOCK), jnp.inf, F32)
    mx_acc = jnp.full((RED_ROWS, Q_BLOCK), -jnp.inf, F32)
    tiny_acc = jnp.full((RED_ROWS, Q_BLOCK), jnp.inf, F32)
    for r in range(slabs):
        rows = slice(r * RED_ROWS, (r + 1) * RED_ROWS)
        acc = weighted_relu(0, rows) + weighted_relu(1, rows)
        if r >= slabs - KEY_BUCKET // RED_ROWS:
            adm = (r * RED_ROWS + r_k) < n_adm
            lo_fill = jnp.where(adm, acc, -jnp.inf)
            hi_fill = jnp.where(adm, acc, jnp.inf)
        else:
            lo_fill = hi_fill = acc
        mag = jnp.abs(hi_fill)
        idx_ref[rows, :] = lo_fill
        mn_acc = jnp.minimum(mn_acc, hi_fill)
        mx_acc = jnp.maximum(mx_acc, lo_fill)
        tiny_acc = jnp.minimum(tiny_acc, jnp.where(mag == 0.0, jnp.inf, mag))
    lo = jnp.min(mn_acc, axis=0, keepdims=True)
    hi = jnp.max(mx_acc, axis=0, keepdims=True)
    tiny = jnp.min(tiny_acc, axis=0, keepdims=True)

    unit = jnp.where(tiny < jnp.inf, tiny, 1.0)
    eps = unit * (0.5 / nk)
    rank0 = (1 + r_k).astype(F32)

    def spread(r, carry):
        rows = pl.ds(pl.multiple_of(r * RED_ROWS, RED_ROWS), RED_ROWS)
        s = idx_ref[rows, :]
        rank = rank0 + jnp.asarray(r * RED_ROWS, F32)
        idx_ref[rows, :] = jnp.where(s == 0.0, -(rank * eps), s)
        return carry

    lax.fori_loop(0, slabs, spread, 0)
    return jnp.minimum(lo, -0.5 * unit), hi


def _probe(nk, kf, idx_ref, state):
    lo, hi, c_lo, c_hi = state
    frac = (c_lo - kf) / jnp.maximum(c_lo - c_hi, 1.0)
    frac = jnp.minimum(jnp.maximum(frac, INTERP_MARGIN), 1.0 - INTERP_MARGIN)
    t = lo + (hi - lo) * frac
    acc = jnp.zeros((RED_ROWS, Q_BLOCK), F32)
    for r in range(nk // RED_ROWS):
        acc = acc + jnp.where(idx_ref[r * RED_ROWS:(r + 1) * RED_ROWS, :] >= t, 1.0, 0.0)
    c = _colsum(acc)
    ge = c >= kf
    return (jnp.where(ge, t, lo), jnp.where(ge, hi, t), jnp.where(ge, c, c_lo), jnp.where(ge, c_hi, c))


def _write_selection(nk, j_blk, lo, idx_ref, rhs_ref):
    r_k, c_q = _tile_iotas()
    later = 2.0 * jnp.maximum(r_k - c_q, 0).astype(F32)

    def write(r, carry):
        rows = pl.ds(pl.multiple_of(r * RED_ROWS, RED_ROWS), RED_ROWS)
        on = jnp.where(r == j_blk, -later, 0.0)
        rhs_ref[rows, LANES:2 * LANES] = jnp.where(idx_ref[rows, :] >= lo, on, -MASK_OFF).astype(BF16)
        return carry

    lax.fori_loop(0, nk // RED_ROWS, write, 0)


def _exact_fallback(slabs, kf, lo, hi, idx_ref, rhs_ref):
    r_k, c_q = _tile_iotas()
    later = 2.0 * jnp.maximum(r_k - c_q, 0).astype(F32)

    def slab(r):
        return idx_ref[pl.ds(pl.multiple_of(r * RED_ROWS, RED_ROWS), RED_ROWS), :]

    def count(pred):
        def body(r, acc):
            return acc + jnp.where(pred(slab(r), r), 1.0, 0.0)
        return _colsum(lax.fori_loop(0, slabs, body, jnp.zeros((RED_ROWS, Q_BLOCK), F32)))

    def kth(lo):
        def body(r, acc):
            s = slab(r)
            return jnp.minimum(acc, jnp.where(s >= lo, s, jnp.inf))
        acc = lax.fori_loop(0, slabs, body, jnp.full((RED_ROWS, Q_BLOCK), jnp.inf, F32))
        thr = jnp.min(acc, axis=0, keepdims=True)
        return thr, count(lambda s, r: s > thr)

    def slow_cond(carry):
        it, _, _, _, c_gt = carry
        return jnp.logical_and(it < SLOW_TRIPS, _any_lane(c_gt >= kf))

    def slow_body(carry):
        it, lo, hi, _, _ = carry
        for _ in range(4):
            mid = 0.5 * lo + 0.5 * hi
            ge = count(lambda s, r: s >= mid) >= kf
            lo, hi = jnp.where(ge, mid, lo), jnp.where(ge, hi, mid)
        thr, c_gt = kth(lo)
        return it + 1, lo, hi, thr, c_gt

    thr0, c_gt0 = kth(lo)
    _, _, _, thr, c_gt = lax.while_loop(slow_cond, slow_body, (jnp.int32(0), lo, hi, thr0, c_gt0))
    c_eq = count(lambda s, r: s == thr)
    need = (c_gt + c_eq) > kf

    def tie_step(_, carry):
        lo_i, hi_i = carry
        mid_i = jnp.floor((lo_i + hi_i) * 0.5)
        below = count(lambda s, r: jnp.logical_and(s == thr, (r * RED_ROWS + r_k).astype(F32) <= mid_i))
        ok = (c_gt + below) >= kf
        return jnp.where(ok, lo_i, mid_i), jnp.where(ok, mid_i, hi_i)

    last = jnp.asarray(slabs * RED_ROWS - 1, F32)
    lo_i = jnp.full((1, Q_BLOCK), -1.0, F32)
    hi_i = jnp.zeros((1, Q_BLOCK), F32) + last
    _, cut = lax.fori_loop(0, 13, tie_step, (lo_i, hi_i))
    cut = jnp.where(need, cut, last + 1.0).astype(jnp.int32)

    def write(r, carry):
        s = slab(r)
        krow = r * RED_ROWS + r_k
        sel = jnp.logical_or(s > thr, jnp.logical_and(s == thr, krow <= cut))
        on = jnp.where(r == slabs - 1, -later, 0.0)
        rhs_ref[pl.ds(pl.multiple_of(r * RED_ROWS, RED_ROWS), RED_ROWS), LANES:2 * LANES] = (
            jnp.where(sel, on, -MASK_OFF).astype(BF16))
        return carry

    lax.fori_loop(0, slabs, write, 0)


GROUP = HEADS // 2
UNITS = PAIR * (HEADS // GROUP)
VALUE_ROWS = HEAD_DIM + 16


def _probabilities(nk, qt_ref, qcols, rhs_ref, s_ref, p_ref):
    r_k, c_q = _tile_iotas()
    feat = lax.broadcasted_iota(jnp.int32, (HALF, Q_BLOCK), 0)
    group = GROUP
    width = group * Q_BLOCK
    units = [(u, g) for u in range(len(qcols)) for g in range(HEADS // group)]

    def score_operand(u, hh):
        slope = 2.0 ** (-(hh + 1))
        alibi = jnp.where(feat == 0, CHUNK * slope, jnp.where(feat == 1, slope, 0.0)).astype(BF16)
        scaled_ident = jnp.where(r_k == c_q, slope, 0.0).astype(BF16)
        return jnp.concatenate([qt_ref[hh * HEAD_DIM:(hh + 1) * HEAD_DIM, qcols[u]], alibi, scaled_ident], axis=0)

    def scores(n):
        u, g = units[n]
        lhs_t = jnp.concatenate([score_operand(u, g * group + i) for i in range(group)], axis=1)
        m_acc = jnp.full((RED_ROWS, width), -jnp.inf, F32)
        for rows in _row_blocks(nk, DOT_ROWS):
            blk = jnp.dot(rhs_ref[u, rows, :], lhs_t, preferred_element_type=F32)
            s_ref[n % 2, rows, :] = blk
            for sub in range((rows.stop - rows.start) // RED_ROWS):
                m_acc = jnp.maximum(m_acc, blk[sub * RED_ROWS:(sub + 1) * RED_ROWS, :])
        return jnp.max(m_acc, axis=0, keepdims=True)

    def probabilities(n, m):
        for rows in _row_blocks(nk, RED_ROWS):
            p_ref[n, rows, :] = jnp.exp((s_ref[n % 2, rows, :] - m).astype(BF16))

    m = scores(0)
    for n in range(len(units)):
        m_next = scores(n + 1) if n + 1 < len(units) else None
        probabilities(n, m)
        m = m_next


def _value_chunk(ch, vtb_ref, p_ref, part_ref):
    if isinstance(ch, int):
        n, blk = ch % UNITS, ch // UNITS
        rows = slice(blk * DOT_ROWS, (blk + 1) * DOT_ROWS)
    else:
        n = ch & (UNITS - 1)
        blk = lax.shift_right_logical(ch, UNITS.bit_length() - 1)
        rows = pl.ds(pl.multiple_of(blk * DOT_ROWS, DOT_ROWS), DOT_ROWS)
    part_ref[n, blk] = jnp.dot(vtb_ref[blk], p_ref[n, rows, :], preferred_element_type=F32)


def _head_tiles(part_ref):
    per_block = []
    for u in range(PAIR):
        outs = []
        for g in range(HEADS // GROUP):
            n = u * (HEADS // GROUP) + g
            o = part_ref[n, 0]
            for blk in range(1, part_ref.shape[1]):
                o = o + part_ref[n, blk]
            outs.append(o[0:HEAD_DIM, :] * (1.0 / o[HEAD_DIM:HEAD_DIM + 1, :]))
        tiles = []
        for t in range(HEADS // 2):
            o_g = outs[(2 * t) // GROUP]
            a = (2 * t) % GROUP
            tiles.append(jnp.concatenate([o_g[:, a * Q_BLOCK:(a + 1) * Q_BLOCK],
                                          o_g[:, (a + 1) * Q_BLOCK:(a + 2) * Q_BLOCK]], axis=0).T)
        per_block.append(tiles)
    return per_block


def _attn_kernel(qt_ref, qit_ref, wit_ref, kaug_ref, kk_ref, vt_ref, ma_ref, gate_ref, x_ref, gb_ref,
                 wo_ref, y_ref, idx_ref, rhs_ref, s_ref, p_ref, st_ref, vtb_ref, part_ref, done_ref,
                 *, seq, topk):
    b = pl.program_id(0)
    jj = pl.program_id(1)
    kf = float(topk)
    steps = seq // (PAIR * Q_BLOCK)
    key_blocks = seq // DOT_ROWS
    qcols = [slice(u * Q_BLOCK, (u + 1) * Q_BLOCK) for u in range(PAIR)]
    searching = jj < steps
    pending = jj >= 1

    @pl.when(jnp.logical_and(b == 0, jj == 0))
    def _():
        p_ref[...] = jnp.zeros(p_ref.shape, BF16)

    @pl.when(jj == 0)
    def _():
        for u in range(PAIR):
            rhs_ref[u, :, 0:LANES] = kaug_ref[...]
        ones = jnp.ones((VALUE_ROWS - HEAD_DIM, DOT_ROWS), BF16)
        for blk in range(key_blocks):
            vtb_ref[blk] = jnp.concatenate([vt_ref[:, blk * DOT_ROWS:(blk + 1) * DOT_ROWS], ones], axis=0)
        y_ref[...] = jnp.zeros(y_ref.shape, F32)

    part_ref[...] = jnp.zeros(part_ref.shape, F32)
    done_ref[0] = 0

    steps_per_bucket = KEY_BUCKET // (PAIR * Q_BLOCK)

    def buckets(step, active):
        return [((n + 1) * KEY_BUCKET,
                 jnp.logical_and(active, jnp.logical_and(step >= n * steps_per_bucket,
                                                         step < (n + 1) * steps_per_bucket)))
                for n in range(seq // KEY_BUCKET)]

    blocks = [PAIR * jj + u for u in range(PAIR)]
    total = UNITS * (lax.div(jnp.maximum(jj - 1, 0), steps_per_bucket) + 1) * (KEY_BUCKET // DOT_ROWS)

    for nk, here in buckets(jj, searching):
        @pl.when(here)
        def _(nk=nk):
            lane_q = lax.broadcasted_iota(jnp.int32, (1, Q_BLOCK), 1)
            states = []
            for u in range(PAIR):
                n_adm = blocks[u] * Q_BLOCK + CHUNK + CHUNK * (lane_q >= CHUNK).astype(jnp.int32)
                lo, hi = _index_scores(nk, n_adm, qit_ref, qcols[u], wit_ref[:, qcols[u]], kk_ref, s_ref,
                                       idx_ref.at[u])
                n_adm_f = n_adm.astype(F32)
                c_lo = jnp.where(n_adm_f <= kf, kf, n_adm_f)
                states.append((lo, hi, c_lo, jnp.zeros((1, Q_BLOCK), F32)))

            def cond(carry):
                it, flat = carry[0], carry[1:]
                still = jnp.logical_or(flat[2] != kf, flat[6] != kf)
                return jnp.logical_and(it < FAST_TRIPS, _any_lane(still))

            def body(carry):
                it, flat = carry[0], carry[1:]
                st = [flat[0:4], flat[4:8]]
                for _ in range(2):
                    st = [_probe(nk, kf, idx_ref.at[u], st[u]) for u in range(PAIR)]
                for c in range(CHUNKS_PER_TRIP):
                    _value_chunk(jnp.minimum(it * CHUNKS_PER_TRIP + c, total - 1), vtb_ref, p_ref, part_ref)
                return (it + 1,) + tuple(st[0]) + tuple(st[1])

            final = lax.while_loop(cond, body, (jnp.int32(0),) + tuple(states[0]) + tuple(states[1]))
            done_ref[0] = jnp.minimum(final[0] * CHUNKS_PER_TRIP, total)
            for u in range(PAIR):
                lo, hi, c_lo, _ = final[1 + 4 * u:5 + 4 * u]
                _write_selection(nk, blocks[u], lo, idx_ref.at[u], rhs_ref.at[u])
                st_ref[4 * u + 0:4 * u + 1, :] = lo
                st_ref[4 * u + 1:4 * u + 2, :] = hi
                st_ref[4 * u + 2:4 * u + 3, :] = c_lo

    @pl.when(jnp.logical_and(pending, searching))
    def _():
        def chunk(ch, carry):
            _value_chunk(ch, vtb_ref, p_ref, part_ref)
            return carry

        lax.fori_loop(done_ref[0], total, chunk, 0)

    @pl.when(jj == steps)
    def _():
        for ch in range(UNITS * key_blocks):
            _value_chunk(ch, vtb_ref, p_ref, part_ref)

    lane = lax.broadcasted_iota(jnp.int32, (PAIR * Q_BLOCK, LANES), 1)
    lo_half = lane < HALF

    @pl.when(pending)
    def _():
        per_block = _head_tiles(part_ref)
        mixed = [ma_ref[...]]
        for t in range(HEADS // 2):
            cols = slice(t * LANES, (t + 1) * LANES)
            o = jnp.concatenate([per_block[u][t] for u in range(PAIR)], axis=0)
            msq = _half_mean_sq(o * o, lo_half)
            ob = o * lax.rsqrt(msq + NORM_EPS) * gb_ref[:, cols] * gate_ref[:, cols].astype(F32)
            mixed.append(ob.astype(BF16))
        y_ref[...] = x_ref[...] + jnp.dot(jnp.concatenate(mixed, axis=1), wo_ref[...],
                                          preferred_element_type=F32)

    for u in range(PAIR):
        lo = st_ref[4 * u + 0:4 * u + 1, :]
        hi = st_ref[4 * u + 1:4 * u + 2, :]
        c_lo = st_ref[4 * u + 2:4 * u + 3, :]

        @pl.when(jnp.logical_and(searching, _any_lane(c_lo != kf)))
        def _(u=u, lo=lo, hi=hi):
            _exact_fallback(blocks[u] + 1, kf, lo, hi, idx_ref.at[u], rhs_ref.at[u])

    for nk, here in buckets(jj, searching):
        @pl.when(here)
        def _(nk=nk):
            _probabilities(nk, qt_ref, qcols, rhs_ref, s_ref, p_ref)


def kernel(x, norm_gain, w_in, sgu_norm_gain, sgu_w, sgu_b, q_norm_gain, k_norm_gain,
           idx_k_norm_gain, branch_norm_gain, w_out):
    bsz, seq, d_model = x.shape
    assert d_model == D_MODEL and norm_gain.shape[0] == 1
    assert seq % PROJ_ROWS == 0 and seq % KEY_BUCKET == 0 and KEY_BUCKET % (PAIR * Q_BLOCK) == 0
    tokens = bsz * seq
    topk = min(TOPK_MAX, seq // 4)
    idx_w_scale = (IDX_HEADS ** -0.5) * (IDX_DIM ** -0.5)

    w = w_in[0]
    a3 = 3 * A_WIDTH
    w_q = w[:, a3:a3 + B_WIDTH]
    w_k = w[:, a3 + B_WIDTH:a3 + B_WIDTH + HEAD_DIM]
    w_v = w[:, a3 + B_WIDTH + HEAD_DIM:a3 + B_WIDTH + 2 * HEAD_DIM]
    o_g = a3 + B_WIDTH + 2 * HEAD_DIM
    w_g = w[:, o_g:o_g + B_WIDTH]
    o_i = o_g + B_WIDTH
    w_iq = w[:, o_i:o_i + IDX_HEADS * IDX_DIM]
    w_ik = w[:, o_i + IDX_HEADS * IDX_DIM:o_i + IDX_HEADS * IDX_DIM + IDX_DIM]
    w_iw = w[:, o_i + IDX_HEADS * IDX_DIM + IDX_DIM:]
    w_main = jnp.concatenate([w[:, :a3], w_g, w_k, w_ik, w_v, w_iw,
                              jnp.zeros((D_MODEL, LANES - HEAD_DIM - IDX_HEADS), F32)], axis=1).astype(BF16)
    assert w_main.shape[1] == _PACKED_COLS
    w_t = jnp.concatenate([w_q.T, w_iq.T], axis=0).astype(BF16)
    assert w_t.shape[0] == _PACKED_ROWS
    x2 = x.reshape(tokens, D_MODEL)
    ng = norm_gain[0].reshape(1, D_MODEL)
    sgn = sgu_norm_gain[0].reshape(1, A_WIDTH)
    sw = sgu_w[0]
    sb = sgu_b[0].reshape(A_GROUPS, A_BLOCK, 1)
    qg = q_norm_gain[0].reshape(HEAD_DIM, 1)
    kg = jnp.concatenate([k_norm_gain[0], idx_k_norm_gain[0]]).reshape(1, LANES)
    ga = branch_norm_gain[0, :A_WIDTH].reshape(1, A_WIDTH)
    gb = branch_norm_gain[0, A_WIDTH:].reshape(1, B_WIDTH)
    wo = w_out[0].astype(BF16)

    tm = PROJ_ROWS
    full = lambda shape: pl.BlockSpec(shape, lambda i: (0,) * len(shape))
    rows = lambda width: pl.BlockSpec((tm, width), lambda i: (i, 0))
    colsT = lambda height: pl.BlockSpec((height, tm), lambda i: (0, i))
    outs = pl.pallas_call(
        functools.partial(_proj_kernel, tiles_per_seq=seq // tm, idx_w_scale=idx_w_scale),
        grid=(tokens // tm,),
        in_specs=[rows(D_MODEL), full((1, D_MODEL)), full((D_MODEL, _PACKED_COLS)),
                  full((_PACKED_ROWS, D_MODEL)), full((1, A_WIDTH)), full((A_GROUPS, A_BLOCK, A_BLOCK)),
                  full((A_GROUPS, A_BLOCK, 1)), full((HEAD_DIM, 1)), full((1, LANES)),
                  full((1, A_WIDTH))],
        out_specs=[rows(A_WIDTH), rows(B_WIDTH), rows(LANES), rows(LANES),
                   colsT(B_WIDTH), colsT(IDX_HEADS * IDX_DIM), colsT(HEAD_DIM), colsT(IDX_HEADS)],
        out_shape=[jax.ShapeDtypeStruct((tokens, A_WIDTH), BF16),
                   jax.ShapeDtypeStruct((tokens, B_WIDTH), BF16),
                   jax.ShapeDtypeStruct((tokens, LANES), BF16),
                   jax.ShapeDtypeStruct((tokens, LANES), BF16),
                   jax.ShapeDtypeStruct((B_WIDTH, tokens), BF16),
                   jax.ShapeDtypeStruct((IDX_HEADS * IDX_DIM, tokens), BF16),
                   jax.ShapeDtypeStruct((HEAD_DIM, tokens), BF16),
                   jax.ShapeDtypeStruct((IDX_HEADS, tokens), F32)],
        compiler_params=pltpu.CompilerParams(dimension_semantics=("arbitrary",),
                                             vmem_limit_bytes=VMEM_LIMIT),
        name="proj_sgu",
    )(x2, ng, w_main, w_t, sgn, sw, sb, qg, kg, ga)
    ma, gate, kaug, kk, qt, qit, vt, wit = outs

    qrows = PAIR * Q_BLOCK
    steps = seq // qrows
    qblk = lambda width: pl.BlockSpec((qrows, width), lambda b, j: (b * steps + jnp.maximum(j - 1, 0), 0))
    qblkT = lambda height: pl.BlockSpec((height, qrows), lambda b, j: (0, b * steps + jnp.minimum(j, steps - 1)))
    const = lambda shape: pl.BlockSpec(shape, lambda b, j: (0,) * len(shape))
    y = pl.pallas_call(
        functools.partial(_attn_kernel, seq=seq, topk=topk),
        grid=(bsz, steps + 1),
        in_specs=[qblkT(B_WIDTH), qblkT(IDX_HEADS * IDX_DIM), qblkT(IDX_HEADS),
                  pl.BlockSpec((seq, LANES), lambda b, j: (b, 0)),
                  pl.BlockSpec((seq, LANES), lambda b, j: (b, 0)),
                  pl.BlockSpec((HEAD_DIM, seq), lambda b, j: (0, b)),
                  qblk(A_WIDTH), qblk(B_WIDTH), qblk(D_MODEL),
                  const((1, B_WIDTH)), const((D_MODEL, D_MODEL))],
        out_specs=qblk(D_MODEL),
        out_shape=jax.ShapeDtypeStruct((tokens, D_MODEL), F32),
        scratch_shapes=[pltpu.VMEM((PAIR, seq, Q_BLOCK), F32),
                        pltpu.VMEM((PAIR, seq, 2 * LANES), BF16),
                        pltpu.VMEM((2, seq, GROUP * Q_BLOCK), F32),
                        pltpu.VMEM((UNITS, seq, GROUP * Q_BLOCK), BF16),
                        pltpu.VMEM((8, Q_BLOCK), F32),
                        pltpu.VMEM((seq // DOT_ROWS, VALUE_ROWS, DOT_ROWS), BF16),
                        pltpu.VMEM((UNITS, seq // DOT_ROWS, VALUE_ROWS, GROUP * Q_BLOCK), F32),
                        pltpu.SMEM((1,), jnp.int32)],
        compiler_params=pltpu.CompilerParams(dimension_semantics=("arbitrary", "arbitrary"),
                                             vmem_limit_bytes=VMEM_LIMIT),
        name="dsa_attn_out",
    )(qt, qit, wit, kaug, kk, vt, ma, gate, x2, gb, wo)
    return y.reshape(bsz, seq, D_MODEL)
```

```python
import functools

import jax
import jax.numpy as jnp
from jax import lax
from jax.experimental import pallas as pl
from jax.experimental.pallas import tpu as pltpu

F32 = jnp.float32
BF16 = jnp.bfloat16

D_MODEL = 1024
CHUNK = 64
A_WIDTH = 512
A_GROUPS = 4
A_BLOCK = 128
HEADS = 8
HEAD_DIM = 64
B_WIDTH = HEADS * HEAD_DIM
IDX_HEADS = 8
IDX_DIM = 64
TOPK_MAX = 256
Q_BLOCK = 128
PAIR = 2
NORM_EPS = 1e-6
MASK_OFF = 1e32
LANES = 128
HALF = LANES // 2
PROJ_ROWS = 512
DOT_ROWS = 512
RED_ROWS = 128
KEY_BUCKET = 512
FAST_TRIPS = 24
INTERP_MARGIN = 0.02
SLOW_TRIPS = 70
CHUNKS_PER_TRIP = 2
VMEM_LIMIT = 48 * 1024 * 1024

_OFF_U, _OFF_V, _OFF_Z, _OFF_G, _OFF_K = 0, 512, 1024, 1536, 2048
_PACKED_COLS = 2304
_ROW_Q, _ROW_QI, _PACKED_ROWS = 0, 512, 1024

_NT = (((1,), (1,)), ((), ()))


def _gelu(x):
    c = 0.7978845608028654
    return 0.5 * x * (1.0 + jnp.tanh(c * (x + 0.044715 * (x * x * x))))


def _silu(x):
    return x / (1.0 + jnp.exp(-x))


def _row_blocks(total, size):
    return [slice(r, min(r + size, total)) for r in range(0, total, size)]


def _half_mean_sq(x2, lo_half):
    tot = jnp.sum(x2, axis=-1, keepdims=True)
    lo = jnp.sum(jnp.where(lo_half, x2, 0.0), axis=-1, keepdims=True)
    return jnp.where(lo_half, lo, tot - lo) * (1.0 / HALF)


def _proj_kernel(x_ref, ng_ref, w_ref, wt_ref, sgn_ref, sw_ref, sb_ref, qg_ref, kg_ref, ga_ref,
                 ma_ref, gate_ref, kaug_ref, kk_ref, qt_ref, qit_ref, vt_ref, wit_ref,
                 *, tiles_per_seq, idx_w_scale):
    tm = x_ref.shape[0]
    i = pl.program_id(0)
    x = x_ref[...]
    ms = jnp.mean(x * x, axis=-1, keepdims=True)
    h = (x * lax.rsqrt(ms + NORM_EPS) * ng_ref[...]).astype(BF16)

    lane = lax.broadcasted_iota(jnp.int32, (tm, LANES), 1)
    lo_half = lane < HALF

    def proj(off, width):
        return jnp.dot(h, w_ref[:, off:off + width], preferred_element_type=F32)

    gu = _gelu(proj(_OFF_U, A_WIDTH))
    gv = _gelu(proj(_OFF_V, A_WIDTH))
    pz = proj(_OFF_Z, A_WIDTH)
    r_i = lax.broadcasted_iota(jnp.int32, (A_BLOCK, A_BLOCK), 0)
    c_j = lax.broadcasted_iota(jnp.int32, (A_BLOCK, A_BLOCK), 1)
    causal = lax.shift_right_logical(c_j, 6) <= lax.shift_right_logical(r_i, 6)
    for g in range(A_GROUPS):
        cols = slice(g * LANES, (g + 1) * LANES)
        vg = gv[:, cols]
        mu = jnp.mean(vg, axis=-1, keepdims=True)
        d = vg - mu
        var = jnp.mean(d * d, axis=-1, keepdims=True)
        vn = (d * lax.rsqrt(var + NORM_EPS) * sgn_ref[:, cols]).astype(BF16)
        wg = jnp.where(causal, sw_ref[g], 0.0).astype(BF16)
        side = jnp.concatenate([vn[blk * A_BLOCK:(blk + 1) * A_BLOCK, :] for blk in range(tm // A_BLOCK)], axis=1)
        mixed = jnp.dot(wg, side, preferred_element_type=F32) + sb_ref[g]
        for blk in range(tm // A_BLOCK):
            rows = slice(blk * A_BLOCK, (blk + 1) * A_BLOCK)
            s = mixed[:, blk * LANES:(blk + 1) * LANES]
            ya = gu[rows, cols] * s
            oa = ya * lax.rsqrt(jnp.mean(ya * ya, axis=-1, keepdims=True) + NORM_EPS) * ga_ref[:, cols]
            ma_ref[rows, cols] = (oa * _silu(pz[rows, cols])).astype(BF16)

    gate_ref[...] = _silu(proj(_OFF_G, B_WIDTH)).astype(BF16)

    pkv = proj(_OFF_K, 2 * LANES)
    pk = pkv[:, 0:LANES]
    k_ms = jnp.sum(jnp.where(lo_half, pk * pk, 0.0), axis=-1, keepdims=True) * (1.0 / HALF)
    kn = pk * lax.rsqrt(k_ms + NORM_EPS)
    ik_mu = jnp.sum(jnp.where(lo_half, 0.0, pk), axis=-1, keepdims=True) * (1.0 / HALF)
    dk = pk - ik_mu
    ik_var = jnp.sum(jnp.where(lo_half, 0.0, dk * dk), axis=-1, keepdims=True) * (1.0 / HALF)
    kin = dk * lax.rsqrt(ik_var + NORM_EPS)
    tile = jnp.where(lo_half, kn, kin) * kg_ref[...]
    swapped = pltpu.roll(tile, HALF, axis=1)
    row = lax.broadcasted_iota(jnp.int32, (tm, LANES), 0)
    pos = (i % tiles_per_seq) * tm + row
    pos_hi = lax.shift_right_logical(pos, 6).astype(F32)
    pos_lo = (pos & (CHUNK - 1)).astype(F32)
    posfeat = jnp.where(lane == HALF, pos_hi, jnp.where(lane == HALF + 1, pos_lo, 0.0))
    kaug_ref[...] = jnp.where(lo_half, tile, posfeat).astype(BF16)
    kk_ref[...] = jnp.where(lo_half, swapped, tile).astype(BF16)

    pt = lax.dot_general(wt_ref[...], h, _NT, preferred_element_type=F32)
    for hh in range(HEADS):
        rows = slice(_ROW_Q + hh * HEAD_DIM, _ROW_Q + (hh + 1) * HEAD_DIM)
        xq = pt[rows, :]
        msq = jnp.mean(xq * xq, axis=0, keepdims=True)
        qt_ref[rows, :] = (xq * lax.rsqrt(msq + NORM_EPS) * qg_ref[...] * (HEAD_DIM ** -0.5)).astype(BF16)
    qit_ref[...] = pt[_ROW_QI:_ROW_QI + IDX_HEADS * IDX_DIM, :].astype(BF16)
    for blk in range(tm // LANES):
        cols = slice(blk * LANES, (blk + 1) * LANES)
        vw_t = pkv[cols, LANES:2 * LANES].T
        vt_ref[:, cols] = vw_t[0:HEAD_DIM, :].astype(BF16)
        wit_ref[:, cols] = vw_t[HEAD_DIM:HEAD_DIM + IDX_HEADS, :] * idx_w_scale


def _tile_iotas():
    r_k = lax.broadcasted_iota(jnp.int32, (Q_BLOCK, Q_BLOCK), 0)
    c_q = lax.broadcasted_iota(jnp.int32, (Q_BLOCK, Q_BLOCK), 1)
    return r_k, c_q


def _colsum(x):
    return jnp.sum(x, axis=0, keepdims=True)


def _any_lane(pred):
    return jnp.max(jnp.where(pred, 1.0, 0.0)) > 0.0


def _index_scores(nk, n_adm, qit_ref, qcols, wi, kk_ref, s_ref, idx_ref):
    r_k, c_q = _tile_iotas()
    top_rows = r_k < HALF
    per_head = []
    for t in range(IDX_HEADS // 2):
        qit = qit_ref[t * LANES:(t + 1) * LANES, qcols]
        zero = jnp.zeros_like(qit)
        per_head += [jnp.where(top_rows, qit, zero), jnp.where(top_rows, zero, qit)]
    half_heads = IDX_HEADS // 2
    for g in range(2):
        wg = jnp.concatenate(per_head[g * half_heads:(g + 1) * half_heads], axis=1)
        for rows in _row_blocks(nk, DOT_ROWS):
            s_ref[g, rows, :] = jnp.dot(kk_ref[rows, :], wg, preferred_element_type=F32)

    def weighted_relu(g, rows):
        acc = None
        for i in range(half_heads):
            hh = g * half_heads + i
            term = jnp.maximum(s_ref[g, rows, i * Q_BLOCK:(i + 1) * Q_BLOCK], 0.0) * wi[hh:hh + 1, :]
            acc = term if acc is None else acc + term
        return acc

    slabs = nk // RED_ROWS
    mn_acc = jnp.full((RED_ROWS, Q_BLOCK), jnp.inf, F32)
    mx_acc = jnp.full((RED_ROWS, Q_BLOCK), -jnp.inf, F32)
    tiny_acc = jnp.full((RED_ROWS, Q_BLOCK), jnp.inf, F32)
    for r in range(slabs):
        rows = slice(r * RED_ROWS, (r + 1) * RED_ROWS)
        acc = weighted_relu(0, rows) + weighted_relu(1, rows)
        if r >= slabs - KEY_BUCKET // RED_ROWS:
            adm = (r * RED_ROWS + r_k) < n_adm
            lo_fill = jnp.where(adm, acc, -jnp.inf)
            hi_fill = jnp.where(adm, acc, jnp.inf)
        else:
            lo_fill = hi_fill = acc
        mag = jnp.abs(hi_fill)
        idx_ref[rows, :] = lo_fill
        mn_acc = jnp.minimum(mn_acc, hi_fill)
        mx_acc = jnp.maximum(mx_acc, lo_fill)
        tiny_acc = jnp.minimum(tiny_acc, jnp.where(mag == 0.0, jnp.inf, mag))
    lo = jnp.min(mn_acc, axis=0, keepdims=True)
    hi = jnp.max(mx_acc, axis=0, keepdims=True)
    tiny = jnp.min(tiny_acc, axis=0, keepdims=True)

    unit = jnp.where(tiny < jnp.inf, tiny, 1.0)
    eps = unit * (0.5 / nk)
    rank0 = (1 + r_k).astype(F32)

    def spread(r, carry):
        rows = pl.ds(pl.multiple_of(r * RED_ROWS, RED_ROWS), RED_ROWS)
        s = idx_ref[rows, :]
        rank = rank0 + jnp.asarray(r * RED_ROWS, F32)
        idx_ref[rows, :] = jnp.where(s == 0.0, -(rank * eps), s)
        return carry

    lax.fori_loop(0, slabs, spread, 0)
    return jnp.minimum(lo, -0.5 * unit), hi


def _probe(nk, kf, idx_ref, state):
    lo, hi, c_lo, c_hi = state
    frac = (c_lo - kf) / jnp.maximum(c_lo - c_hi, 1.0)
    frac = jnp.minimum(jnp.maximum(frac, INTERP_MARGIN), 1.0 - INTERP_MARGIN)
    t = lo + (hi - lo) * frac
    acc = jnp.zeros((RED_ROWS, Q_BLOCK), F32)
    for r in range(nk // RED_ROWS):
        acc = acc + jnp.where(idx_ref[r * RED_ROWS:(r + 1) * RED_ROWS, :] >= t, 1.0, 0.0)
    c = _colsum(acc)
    ge = c >= kf
    return (jnp.where(ge, t, lo), jnp.where(ge, hi, t), jnp.where(ge, c, c_lo), jnp.where(ge, c_hi, c))


def _write_selection(nk, j_blk, lo, idx_ref, rhs_ref):
    r_k, c_q = _tile_iotas()
    later = 2.0 * jnp.maximum(r_k - c_q, 0).astype(F32)

    def write(r, carry):
        rows = pl.ds(pl.multiple_of(r * RED_ROWS, RED_ROWS), RED_ROWS)
        on = jnp.where(r == j_blk, -later, 0.0)
        rhs_ref[rows, LANES:2 * LANES] = jnp.where(idx_ref[rows, :] >= lo, on, -MASK_OFF).astype(BF16)
        return carry

    lax.fori_loop(0, nk // RED_ROWS, write, 0)


def _exact_fallback(slabs, kf, lo, hi, idx_ref, rhs_ref):
    r_k, c_q = _tile_iotas()
    later = 2.0 * jnp.maximum(r_k - c_q, 0).astype(F32)

    def slab(r):
        return idx_ref[pl.ds(pl.multiple_of(r * RED_ROWS, RED_ROWS), RED_ROWS), :]

    def count(pred):
        def body(r, acc):
            return acc + jnp.where(pred(slab(r), r), 1.0, 0.0)
        return _colsum(lax.fori_loop(0, slabs, body, jnp.zeros((RED_ROWS, Q_BLOCK), F32)))

    def kth(lo):
        def body(r, acc):
            s = slab(r)
            return jnp.minimum(acc, jnp.where(s >= lo, s, jnp.inf))
        acc = lax.fori_loop(0, slabs, body, jnp.full((RED_ROWS, Q_BLOCK), jnp.inf, F32))
        thr = jnp.min(acc, axis=0, keepdims=True)
        return thr, count(lambda s, r: s > thr)

    def slow_cond(carry):
        it, _, _, _, c_gt = carry
        return jnp.logical_and(it < SLOW_TRIPS, _any_lane(c_gt >= kf))

    def slow_body(carry):
        it, lo, hi, _, _ = carry
        for _ in range(4):
            mid = 0.5 * lo + 0.5 * hi
            ge = count(lambda s, r: s >= mid) >= kf
            lo, hi = jnp.where(ge, mid, lo), jnp.where(ge, hi, mid)
        thr, c_gt = kth(lo)
        return it + 1, lo, hi, thr, c_gt

    thr0, c_gt0 = kth(lo)
    _, _, _, thr, c_gt = lax.while_loop(slow_cond, slow_body, (jnp.int32(0), lo, hi, thr0, c_gt0))
    c_eq = count(lambda s, r: s == thr)
    need = (c_gt + c_eq) > kf

    def tie_step(_, carry):
        lo_i, hi_i = carry
        mid_i = jnp.floor((lo_i + hi_i) * 0.5)
        below = count(lambda s, r: jnp.logical_and(s == thr, (r * RED_ROWS + r_k).astype(F32) <= mid_i))
        ok = (c_gt + below) >= kf
        return jnp.where(ok, lo_i, mid_i), jnp.where(ok, mid_i, hi_i)

    last = jnp.asarray(slabs * RED_ROWS - 1, F32)
    lo_i = jnp.full((1, Q_BLOCK), -1.0, F32)
    hi_i = jnp.zeros((1, Q_BLOCK), F32) + last
    _, cut = lax.fori_loop(0, 13, tie_step, (lo_i, hi_i))
    cut = jnp.where(need, cut, last + 1.0).astype(jnp.int32)

    def write(r, carry):
        s = slab(r)
        krow = r * RED_ROWS + r_k
        sel = jnp.logical_or(s > thr, jnp.logical_and(s == thr, krow <= cut))
        on = jnp.where(r == slabs - 1, -later, 0.0)
        rhs_ref[pl.ds(pl.multiple_of(r * RED_ROWS, RED_ROWS), RED_ROWS), LANES:2 * LANES] = (
            jnp.where(sel, on, -MASK_OFF).astype(BF16))
        return carry

    lax.fori_loop(0, slabs, write, 0)


GROUP = HEADS // 2
UNITS = PAIR * (HEADS // GROUP)
VALUE_ROWS = HEAD_DIM + 16


def _probabilities(nk, qt_ref, qcols, rhs_ref, s_ref, p_ref, between):
    r_k, c_q = _tile_iotas()
    feat = lax.broadcasted_iota(jnp.int32, (HALF, Q_BLOCK), 0)
    group = GROUP
    width = group * Q_BLOCK
    units = [(u, g) for u in range(len(qcols)) for g in range(HEADS // group)]

    def score_operand(u, hh):
        slope = 2.0 ** (-(hh + 1))
        alibi = jnp.where(feat == 0, CHUNK * slope, jnp.where(feat == 1, slope, 0.0)).astype(BF16)
        scaled_ident = jnp.where(r_k == c_q, slope, 0.0).astype(BF16)
        return jnp.concatenate([qt_ref[hh * HEAD_DIM:(hh + 1) * HEAD_DIM, qcols[u]], alibi, scaled_ident], axis=0)

    def scores(n):
        u, g = units[n]
        lhs_t = jnp.concatenate([score_operand(u, g * group + i) for i in range(group)], axis=1)
        m_acc = jnp.full((RED_ROWS, width), -jnp.inf, F32)
        for rows in _row_blocks(nk, DOT_ROWS):
            blk = jnp.dot(rhs_ref[u, rows, :], lhs_t, preferred_element_type=F32)
            s_ref[n % 2, rows, :] = blk
            for sub in range((rows.stop - rows.start) // RED_ROWS):
                m_acc = jnp.maximum(m_acc, blk[sub * RED_ROWS:(sub + 1) * RED_ROWS, :])
        return jnp.max(m_acc, axis=0, keepdims=True)

    def probabilities(n, m):
        for rows in _row_blocks(nk, RED_ROWS):
            p_ref[n, rows, :] = jnp.exp((s_ref[n % 2, rows, :] - m).astype(BF16))

    m = scores(0)
    between()
    for n in range(len(units)):
        m_next = scores(n + 1) if n + 1 < len(units) else None
        probabilities(n, m)
        m = m_next


def _value_chunk(ch, vtb_ref, p_ref, part_ref):
    if isinstance(ch, int):
        n, blk = ch % UNITS, ch // UNITS
        rows = slice(blk * DOT_ROWS, (blk + 1) * DOT_ROWS)
    else:
        n = ch & (UNITS - 1)
        blk = lax.shift_right_logical(ch, UNITS.bit_length() - 1)
        rows = pl.ds(pl.multiple_of(blk * DOT_ROWS, DOT_ROWS), DOT_ROWS)
    part_ref[n, blk] = jnp.dot(vtb_ref[blk], p_ref[n, rows, :], preferred_element_type=F32)


def _head_tiles(part_ref):
    per_block = []
    for u in range(PAIR):
        outs = []
        for g in range(HEADS // GROUP):
            n = u * (HEADS // GROUP) + g
            o = part_ref[n, 0]
            for blk in range(1, part_ref.shape[1]):
                o = o + part_ref[n, blk]
            outs.append(o[0:HEAD_DIM, :] * (1.0 / o[HEAD_DIM:HEAD_DIM + 1, :]))
        tiles = []
        for t in range(HEADS // 2):
            o_g = outs[(2 * t) // GROUP]
            a = (2 * t) % GROUP
            tiles.append(jnp.concatenate([o_g[:, a * Q_BLOCK:(a + 1) * Q_BLOCK],
                                          o_g[:, (a + 1) * Q_BLOCK:(a + 2) * Q_BLOCK]], axis=0).T)
        per_block.append(tiles)
    return per_block


def _attn_kernel(qt_ref, qit_ref, wit_ref, kaug_ref, kk_ref, vt_ref, ma_ref, gate_ref, x_ref, gb_ref,
                 wo_ref, y_ref, idx_ref, rhs_ref, s_ref, p_ref, st_ref, vtb_ref, part_ref, done_ref,
                 *, seq, topk):
    b = pl.program_id(0)
    jj = pl.program_id(1)
    kf = float(topk)
    steps = seq // (PAIR * Q_BLOCK)
    key_blocks = seq // DOT_ROWS
    qcols = [slice(u * Q_BLOCK, (u + 1) * Q_BLOCK) for u in range(PAIR)]
    searching = jj < steps
    pending = jj >= 1

    @pl.when(jnp.logical_and(b == 0, jj == 0))
    def _():
        p_ref[...] = jnp.zeros(p_ref.shape, BF16)

    @pl.when(jj == 0)
    def _():
        for u in range(PAIR):
            rhs_ref[u, :, 0:LANES] = kaug_ref[...]
        ones = jnp.ones((VALUE_ROWS - HEAD_DIM, DOT_ROWS), BF16)
        for blk in range(key_blocks):
            vtb_ref[blk] = jnp.concatenate([vt_ref[:, blk * DOT_ROWS:(blk + 1) * DOT_ROWS], ones], axis=0)
        y_ref[...] = jnp.zeros(y_ref.shape, F32)

    part_ref[...] = jnp.zeros(part_ref.shape, F32)
    done_ref[0] = 0

    steps_per_bucket = KEY_BUCKET // (PAIR * Q_BLOCK)

    def buckets(step, active):
        return [((n + 1) * KEY_BUCKET,
                 jnp.logical_and(active, jnp.logical_and(step >= n * steps_per_bucket,
                                                         step < (n + 1) * steps_per_bucket)))
                for n in range(seq // KEY_BUCKET)]

    blocks = [PAIR * jj + u for u in range(PAIR)]
    total = UNITS * (lax.div(jnp.maximum(jj - 1, 0), steps_per_bucket) + 1) * (KEY_BUCKET // DOT_ROWS)

    for nk, here in buckets(jj, searching):
        @pl.when(here)
        def _(nk=nk):
            lane_q = lax.broadcasted_iota(jnp.int32, (1, Q_BLOCK), 1)
            states = []
            for u in range(PAIR):
                n_adm = blocks[u] * Q_BLOCK + CHUNK + CHUNK * (lane_q >= CHUNK).astype(jnp.int32)
                lo, hi = _index_scores(nk, n_adm, qit_ref, qcols[u], wit_ref[:, qcols[u]], kk_ref, s_ref,
                                       idx_ref.at[u])
                n_adm_f = n_adm.astype(F32)
                c_lo = jnp.where(n_adm_f <= kf, kf, n_adm_f)
                states.append((lo, hi, c_lo, jnp.zeros((1, Q_BLOCK), F32)))

            def cond(carry):
                it, flat = carry[0], carry[1:]
                still = jnp.logical_or(flat[2] != kf, flat[6] != kf)
                return jnp.logical_and(it < FAST_TRIPS, _any_lane(still))

            def body(carry):
                it, flat = carry[0], carry[1:]
                st = [flat[0:4], flat[4:8]]
                for _ in range(2):
                    st = [_probe(nk, kf, idx_ref.at[u], st[u]) for u in range(PAIR)]
                for c in range(CHUNKS_PER_TRIP):
                    _value_chunk(jnp.minimum(it * CHUNKS_PER_TRIP + c, total - 1), vtb_ref, p_ref, part_ref)
                return (it + 1,) + tuple(st[0]) + tuple(st[1])

            final = lax.while_loop(cond, body, (jnp.int32(0),) + tuple(states[0]) + tuple(states[1]))
            done_ref[0] = jnp.minimum(final[0] * CHUNKS_PER_TRIP, total)
            for u in range(PAIR):
                lo, hi, c_lo, _ = final[1 + 4 * u:5 + 4 * u]
                _write_selection(nk, blocks[u], lo, idx_ref.at[u], rhs_ref.at[u])
                st_ref[4 * u + 0:4 * u + 1, :] = lo
                st_ref[4 * u + 1:4 * u + 2, :] = hi
                st_ref[4 * u + 2:4 * u + 3, :] = c_lo

    @pl.when(jnp.logical_and(pending, searching))
    def _():
        def chunk(ch, carry):
            _value_chunk(ch, vtb_ref, p_ref, part_ref)
            return carry

        lax.fori_loop(done_ref[0], total, chunk, 0)

    @pl.when(jj == steps)
    def _():
        for ch in range(UNITS * key_blocks):
            _value_chunk(ch, vtb_ref, p_ref, part_ref)

    lane = lax.broadcasted_iota(jnp.int32, (PAIR * Q_BLOCK, LANES), 1)
    lo_half = lane < HALF

    def finish_pending():
        per_block = _head_tiles(part_ref)
        mixed = [ma_ref[...]]
        for t in range(HEADS // 2):
            cols = slice(t * LANES, (t + 1) * LANES)
            o = jnp.concatenate([per_block[u][t] for u in range(PAIR)], axis=0)
            msq = _half_mean_sq(o * o, lo_half)
            ob = o * lax.rsqrt(msq + NORM_EPS) * gb_ref[:, cols] * gate_ref[:, cols].astype(F32)
            mixed.append(ob.astype(BF16))
        y_ref[...] = x_ref[...] + jnp.dot(jnp.concatenate(mixed, axis=1), wo_ref[...],
                                          preferred_element_type=F32)

    for u in range(PAIR):
        lo = st_ref[4 * u + 0:4 * u + 1, :]
        hi = st_ref[4 * u + 1:4 * u + 2, :]
        c_lo = st_ref[4 * u + 2:4 * u + 3, :]

        @pl.when(jnp.logical_and(searching, _any_lane(c_lo != kf)))
        def _(u=u, lo=lo, hi=hi):
            _exact_fallback(blocks[u] + 1, kf, lo, hi, idx_ref.at[u], rhs_ref.at[u])

    for nk, here in buckets(jj, searching):
        @pl.when(here)
        def _(nk=nk):
            _probabilities(nk, qt_ref, qcols, rhs_ref, s_ref, p_ref, finish_pending)

    @pl.when(jj == steps)
    def _():
        finish_pending()


def kernel(x, norm_gain, w_in, sgu_norm_gain, sgu_w, sgu_b, q_norm_gain, k_norm_gain,
           idx_k_norm_gain, branch_norm_gain, w_out):
    bsz, seq, d_model = x.shape
    assert d_model == D_MODEL and norm_gain.shape[0] == 1
    assert seq % PROJ_ROWS == 0 and seq % KEY_BUCKET == 0 and KEY_BUCKET % (PAIR * Q_BLOCK) == 0
    tokens = bsz * seq
    topk = min(TOPK_MAX, seq // 4)
    idx_w_scale = (IDX_HEADS ** -0.5) * (IDX_DIM ** -0.5)

    w = w_in[0]
    a3 = 3 * A_WIDTH
    w_q = w[:, a3:a3 + B_WIDTH]
    w_k = w[:, a3 + B_WIDTH:a3 + B_WIDTH + HEAD_DIM]
    w_v = w[:, a3 + B_WIDTH + HEAD_DIM:a3 + B_WIDTH + 2 * HEAD_DIM]
    o_g = a3 + B_WIDTH + 2 * HEAD_DIM
    w_g = w[:, o_g:o_g + B_WIDTH]
    o_i = o_g + B_WIDTH
    w_iq = w[:, o_i:o_i + IDX_HEADS * IDX_DIM]
    w_ik = w[:, o_i + IDX_HEADS * IDX_DIM:o_i + IDX_HEADS * IDX_DIM + IDX_DIM]
    w_iw = w[:, o_i + IDX_HEADS * IDX_DIM + IDX_DIM:]
    w_main = jnp.concatenate([w[:, :a3], w_g, w_k, w_ik, w_v, w_iw,
                              jnp.zeros((D_MODEL, LANES - HEAD_DIM - IDX_HEADS), F32)], axis=1).astype(BF16)
    assert w_main.shape[1] == _PACKED_COLS
    w_t = jnp.concatenate([w_q.T, w_iq.T], axis=0).astype(BF16)
    assert w_t.shape[0] == _PACKED_ROWS
    x2 = x.reshape(tokens, D_MODEL)
    ng = norm_gain[0].reshape(1, D_MODEL)
    sgn = sgu_norm_gain[0].reshape(1, A_WIDTH)
    sw = sgu_w[0]
    sb = sgu_b[0].reshape(A_GROUPS, A_BLOCK, 1)
    qg = q_norm_gain[0].reshape(HEAD_DIM, 1)
    kg = jnp.concatenate([k_norm_gain[0], idx_k_norm_gain[0]]).reshape(1, LANES)
    ga = branch_norm_gain[0, :A_WIDTH].reshape(1, A_WIDTH)
    gb = branch_norm_gain[0, A_WIDTH:].reshape(1, B_WIDTH)
    wo = w_out[0].astype(BF16)

    tm = PROJ_ROWS
    full = lambda shape: pl.BlockSpec(shape, lambda i: (0,) * len(shape))
    rows = lambda width: pl.BlockSpec((tm, width), lambda i: (i, 0))
    colsT = lambda height: pl.BlockSpec((height, tm), lambda i: (0, i))
    outs = pl.pallas_call(
        functools.partial(_proj_kernel, tiles_per_seq=seq // tm, idx_w_scale=idx_w_scale),
        grid=(tokens // tm,),
        in_specs=[rows(D_MODEL), full((1, D_MODEL)), full((D_MODEL, _PACKED_COLS)),
                  full((_PACKED_ROWS, D_MODEL)), full((1, A_WIDTH)), full((A_GROUPS, A_BLOCK, A_BLOCK)),
                  full((A_GROUPS, A_BLOCK, 1)), full((HEAD_DIM, 1)), full((1, LANES)),
                  full((1, A_WIDTH))],
        out_specs=[rows(A_WIDTH), rows(B_WIDTH), rows(LANES), rows(LANES),
                   colsT(B_WIDTH), colsT(IDX_HEADS * IDX_DIM), colsT(HEAD_DIM), colsT(IDX_HEADS)],
        out_shape=[jax.ShapeDtypeStruct((tokens, A_WIDTH), BF16),
                   jax.ShapeDtypeStruct((tokens, B_WIDTH), BF16),
                   jax.ShapeDtypeStruct((tokens, LANES), BF16),
                   jax.ShapeDtypeStruct((tokens, LANES), BF16),
                   jax.ShapeDtypeStruct((B_WIDTH, tokens), BF16),
                   jax.ShapeDtypeStruct((IDX_HEADS * IDX_DIM, tokens), BF16),
                   jax.ShapeDtypeStruct((HEAD_DIM, tokens), BF16),
                   jax.ShapeDtypeStruct((IDX_HEADS, tokens), F32)],
        compiler_params=pltpu.CompilerParams(dimension_semantics=("arbitrary",),
                                             vmem_limit_bytes=VMEM_LIMIT),
        name="proj_sgu",
    )(x2, ng, w_main, w_t, sgn, sw, sb, qg, kg, ga)
    ma, gate, kaug, kk, qt, qit, vt, wit = outs

    qrows = PAIR * Q_BLOCK
    steps = seq // qrows
    qblk = lambda width: pl.BlockSpec((qrows, width), lambda b, j: (b * steps + jnp.maximum(j - 1, 0), 0))
    qblkT = lambda height: pl.BlockSpec((height, qrows), lambda b, j: (0, b * steps + jnp.minimum(j, steps - 1)))
    const = lambda shape: pl.BlockSpec(shape, lambda b, j: (0,) * len(shape))
    y = pl.pallas_call(
        functools.partial(_attn_kernel, seq=seq, topk=topk),
        grid=(bsz, steps + 1),
        in_specs=[qblkT(B_WIDTH), qblkT(IDX_HEADS * IDX_DIM), qblkT(IDX_HEADS),
                  pl.BlockSpec((seq, LANES), lambda b, j: (b, 0)),
                  pl.BlockSpec((seq, LANES), lambda b, j: (b, 0)),
                  pl.BlockSpec((HEAD_DIM, seq), lambda b, j: (0, b)),
                  qblk(A_WIDTH), qblk(B_WIDTH), qblk(D_MODEL),
                  const((1, B_WIDTH)), const((D_MODEL, D_MODEL))],
        out_specs=qblk(D_MODEL),
        out_shape=jax.ShapeDtypeStruct((tokens, D_MODEL), F32),
        scratch_shapes=[pltpu.VMEM((PAIR, seq, Q_BLOCK), F32),
                        pltpu.VMEM((PAIR, seq, 2 * LANES), BF16),
                        pltpu.VMEM((2, seq, GROUP * Q_BLOCK), F32),
                        pltpu.VMEM((UNITS, seq, GROUP * Q_BLOCK), BF16),
                        pltpu.VMEM((8, Q_BLOCK), F32),
                        pltpu.VMEM((seq // DOT_ROWS, VALUE_ROWS, DOT_ROWS), BF16),
                        pltpu.VMEM((UNITS, seq // DOT_ROWS, VALUE_ROWS, GROUP * Q_BLOCK), F32),
                        pltpu.SMEM((1,), jnp.int32)],
        compiler_params=pltpu.CompilerParams(dimension_semantics=("arbitrary", "arbitrary"),
                                             vmem_limit_bytes=VMEM_LIMIT),
        name="dsa_attn_out",
    )(qt, qit, wit, kaug, kk, vt, ma, gate, x2, gb, wo)
    return y.reshape(bsz, seq, D_MODEL)
```

```python
import functools

import jax
import jax.numpy as jnp
from jax import lax
from jax.experimental import pallas as pl
from jax.experimental.pallas import tpu as pltpu

F32 = jnp.float32
BF16 = jnp.bfloat16

D_MODEL = 1024
CHUNK = 64
A_WIDTH = 512
A_GROUPS = 4
A_BLOCK = 128
HEADS = 8
HEAD_DIM = 64
B_WIDTH = HEADS * HEAD_DIM
IDX_HEADS = 8
IDX_DIM = 64
TOPK_MAX = 256
Q_BLOCK = 128
PAIR = 2
NORM_EPS = 1e-6
MASK_OFF = 1e32
LANES = 128
HALF = LANES // 2
PROJ_ROWS = 512
DOT_ROWS = 512
RED_ROWS = 128
KEY_BUCKET = 512
FAST_TRIPS = 24
INTERP_MARGIN = 0.02
SLOW_TRIPS = 70
CHUNKS_PER_TRIP = 2
COARSE_TRIPS = 3
VMEM_LIMIT = 48 * 1024 * 1024

_OFF_U, _OFF_V, _OFF_Z, _OFF_G, _OFF_K = 0, 512, 1024, 1536, 2048
_PACKED_COLS = 2304
_ROW_Q, _ROW_QI, _PACKED_ROWS = 0, 512, 1024

_NT = (((1,), (1,)), ((), ()))


def _gelu(x):
    c = 0.7978845608028654
    return 0.5 * x * (1.0 + jnp.tanh(c * (x + 0.044715 * (x * x * x))))


def _silu(x):
    return x / (1.0 + jnp.exp(-x))


def _row_blocks(total, size):
    return [slice(r, min(r + size, total)) for r in range(0, total, size)]


def _half_mean_sq(x2, lo_half):
    tot = jnp.sum(x2, axis=-1, keepdims=True)
    lo = jnp.sum(jnp.where(lo_half, x2, 0.0), axis=-1, keepdims=True)
    return jnp.where(lo_half, lo, tot - lo) * (1.0 / HALF)


def _proj_kernel(x_ref, ng_ref, w_ref, wt_ref, sgn_ref, sw_ref, sb_ref, qg_ref, kg_ref, ga_ref,
                 ma_ref, gate_ref, kaug_ref, kk_ref, qt_ref, qit_ref, vt_ref, wit_ref,
                 *, tiles_per_seq, idx_w_scale):
    tm = x_ref.shape[0]
    i = pl.program_id(0)
    x = x_ref[...]
    ms = jnp.mean(x * x, axis=-1, keepdims=True)
    h = (x * lax.rsqrt(ms + NORM_EPS) * ng_ref[...]).astype(BF16)

    lane = lax.broadcasted_iota(jnp.int32, (tm, LANES), 1)
    lo_half = lane < HALF

    def proj(off, width):
        return jnp.dot(h, w_ref[:, off:off + width], preferred_element_type=F32)

    gu = _gelu(proj(_OFF_U, A_WIDTH))
    gv = _gelu(proj(_OFF_V, A_WIDTH))
    pz = proj(_OFF_Z, A_WIDTH)
    r_i = lax.broadcasted_iota(jnp.int32, (A_BLOCK, A_BLOCK), 0)
    c_j = lax.broadcasted_iota(jnp.int32, (A_BLOCK, A_BLOCK), 1)
    causal = lax.shift_right_logical(c_j, 6) <= lax.shift_right_logical(r_i, 6)
    for g in range(A_GROUPS):
        cols = slice(g * LANES, (g + 1) * LANES)
        vg = gv[:, cols]
        mu = jnp.mean(vg, axis=-1, keepdims=True)
        d = vg - mu
        var = jnp.mean(d * d, axis=-1, keepdims=True)
        vn = (d * lax.rsqrt(var + NORM_EPS) * sgn_ref[:, cols]).astype(BF16)
        wg = jnp.where(causal, sw_ref[g], 0.0).astype(BF16)
        side = jnp.concatenate([vn[blk * A_BLOCK:(blk + 1) * A_BLOCK, :] for blk in range(tm // A_BLOCK)], axis=1)
        mixed = jnp.dot(wg, side, preferred_element_type=F32) + sb_ref[g]
        for blk in range(tm // A_BLOCK):
            rows = slice(blk * A_BLOCK, (blk + 1) * A_BLOCK)
            s = mixed[:, blk * LANES:(blk + 1) * LANES]
            ya = gu[rows, cols] * s
            oa = ya * lax.rsqrt(jnp.mean(ya * ya, axis=-1, keepdims=True) + NORM_EPS) * ga_ref[:, cols]
            ma_ref[rows, cols] = (oa * _silu(pz[rows, cols])).astype(BF16)

    gate_ref[...] = _silu(proj(_OFF_G, B_WIDTH)).astype(BF16)

    pkv = proj(_OFF_K, 2 * LANES)
    pk = pkv[:, 0:LANES]
    k_ms = jnp.sum(jnp.where(lo_half, pk * pk, 0.0), axis=-1, keepdims=True) * (1.0 / HALF)
    kn = pk * lax.rsqrt(k_ms + NORM_EPS)
    ik_mu = jnp.sum(jnp.where(lo_half, 0.0, pk), axis=-1, keepdims=True) * (1.0 / HALF)
    dk = pk - ik_mu
    ik_var = jnp.sum(jnp.where(lo_half, 0.0, dk * dk), axis=-1, keepdims=True) * (1.0 / HALF)
    kin = dk * lax.rsqrt(ik_var + NORM_EPS)
    tile = jnp.where(lo_half, kn, kin) * kg_ref[...]
    swapped = pltpu.roll(tile, HALF, axis=1)
    row = lax.broadcasted_iota(jnp.int32, (tm, LANES), 0)
    pos = (i % tiles_per_seq) * tm + row
    pos_hi = lax.shift_right_logical(pos, 6).astype(F32)
    pos_lo = (pos & (CHUNK - 1)).astype(F32)
    posfeat = jnp.where(lane == HALF, pos_hi, jnp.where(lane == HALF + 1, pos_lo, 0.0))
    kaug_ref[...] = jnp.where(lo_half, tile, posfeat).astype(BF16)
    kk_ref[...] = jnp.where(lo_half, swapped, tile).astype(BF16)

    pt = lax.dot_general(wt_ref[...], h, _NT, preferred_element_type=F32)
    for hh in range(HEADS):
        rows = slice(_ROW_Q + hh * HEAD_DIM, _ROW_Q + (hh + 1) * HEAD_DIM)
        xq = pt[rows, :]
        msq = jnp.mean(xq * xq, axis=0, keepdims=True)
        qt_ref[rows, :] = (xq * lax.rsqrt(msq + NORM_EPS) * qg_ref[...] * (HEAD_DIM ** -0.5)).astype(BF16)
    qit_ref[...] = pt[_ROW_QI:_ROW_QI + IDX_HEADS * IDX_DIM, :].astype(BF16)
    for blk in range(tm // LANES):
        cols = slice(blk * LANES, (blk + 1) * LANES)
        vw_t = pkv[cols, LANES:2 * LANES].T
        vt_ref[:, cols] = vw_t[0:HEAD_DIM, :].astype(BF16)
        wit_ref[:, cols] = vw_t[HEAD_DIM:HEAD_DIM + IDX_HEADS, :] * idx_w_scale


def _tile_iotas():
    r_k = lax.broadcasted_iota(jnp.int32, (Q_BLOCK, Q_BLOCK), 0)
    c_q = lax.broadcasted_iota(jnp.int32, (Q_BLOCK, Q_BLOCK), 1)
    return r_k, c_q


def _colsum(x):
    return jnp.sum(x, axis=0, keepdims=True)


def _any_lane(pred):
    return jnp.max(jnp.where(pred, 1.0, 0.0)) > 0.0


def _index_scores(nk, n_adm, qit_ref, qcols, wi, kk_ref, s_ref, idx_ref, idx16_ref):
    r_k, c_q = _tile_iotas()
    top_rows = r_k < HALF
    per_head = []
    for t in range(IDX_HEADS // 2):
        qit = qit_ref[t * LANES:(t + 1) * LANES, qcols]
        zero = jnp.zeros_like(qit)
        per_head += [jnp.where(top_rows, qit, zero), jnp.where(top_rows, zero, qit)]
    half_heads = IDX_HEADS // 2
    for g in range(2):
        wg = jnp.concatenate(per_head[g * half_heads:(g + 1) * half_heads], axis=1)
        for rows in _row_blocks(nk, DOT_ROWS):
            s_ref[g, rows, :] = jnp.dot(kk_ref[rows, :], wg, preferred_element_type=F32)

    def weighted_relu(g, rows):
        acc = None
        for i in range(half_heads):
            hh = g * half_heads + i
            term = jnp.maximum(s_ref[g, rows, i * Q_BLOCK:(i + 1) * Q_BLOCK], 0.0) * wi[hh:hh + 1, :]
            acc = term if acc is None else acc + term
        return acc

    slabs = nk // RED_ROWS
    mn_acc = jnp.full((RED_ROWS, Q_BLOCK), jnp.inf, F32)
    mx_acc = jnp.full((RED_ROWS, Q_BLOCK), -jnp.inf, F32)
    tiny_acc = jnp.full((RED_ROWS, Q_BLOCK), jnp.inf, F32)
    for r in range(slabs):
        rows = slice(r * RED_ROWS, (r + 1) * RED_ROWS)
        acc = weighted_relu(0, rows) + weighted_relu(1, rows)
        if r >= slabs - KEY_BUCKET // RED_ROWS:
            adm = (r * RED_ROWS + r_k) < n_adm
            lo_fill = jnp.where(adm, acc, -jnp.inf)
            hi_fill = jnp.where(adm, acc, jnp.inf)
        else:
            lo_fill = hi_fill = acc
        mag = jnp.abs(hi_fill)
        idx_ref[rows, :] = lo_fill
        mn_acc = jnp.minimum(mn_acc, hi_fill)
        mx_acc = jnp.maximum(mx_acc, lo_fill)
        tiny_acc = jnp.minimum(tiny_acc, jnp.where(mag == 0.0, jnp.inf, mag))
    lo = jnp.min(mn_acc, axis=0, keepdims=True)
    hi = jnp.max(mx_acc, axis=0, keepdims=True)
    tiny = jnp.min(tiny_acc, axis=0, keepdims=True)

    unit = jnp.where(tiny < jnp.inf, tiny, 1.0)
    eps = unit * (0.5 / nk)
    rank0 = (1 + r_k).astype(F32)

    def spread(r, carry):
        rows = pl.ds(pl.multiple_of(r * RED_ROWS, RED_ROWS), RED_ROWS)
        s = idx_ref[rows, :]
        rank = rank0 + jnp.asarray(r * RED_ROWS, F32)
        spread_s = jnp.where(s == 0.0, -(rank * eps), s)
        idx_ref[rows, :] = spread_s
        idx16_ref[rows, :] = spread_s.astype(BF16)
        return carry

    lax.fori_loop(0, slabs, spread, 0)
    return jnp.minimum(lo, -0.5 * unit), hi


def _probe(nk, kf, idx_ref, state):
    lo, hi, c_lo, c_hi = state
    frac = (c_lo - kf) / jnp.maximum(c_lo - c_hi, 1.0)
    frac = jnp.minimum(jnp.maximum(frac, INTERP_MARGIN), 1.0 - INTERP_MARGIN)
    t = lo + (hi - lo) * frac
    acc = jnp.zeros((RED_ROWS, Q_BLOCK), F32)
    for r in range(nk // RED_ROWS):
        acc = acc + jnp.where(idx_ref[r * RED_ROWS:(r + 1) * RED_ROWS, :] >= t, 1.0, 0.0)
    c = _colsum(acc)
    ge = c >= kf
    return (jnp.where(ge, t, lo), jnp.where(ge, hi, t), jnp.where(ge, c, c_lo), jnp.where(ge, c_hi, c))


def _probe16(nk, kf, idx16_ref, state):
    lo, hi, c_lo, c_hi = state
    frac = (c_lo - kf) / jnp.maximum(c_lo - c_hi, 1.0)
    frac = jnp.minimum(jnp.maximum(frac, INTERP_MARGIN), 1.0 - INTERP_MARGIN)
    t16 = (lo + (hi - lo) * frac).astype(BF16)
    t = t16.astype(F32)
    one = jnp.ones((RED_ROWS, Q_BLOCK), BF16)
    zero = jnp.zeros((RED_ROWS, Q_BLOCK), BF16)
    acc = zero
    for r in range(nk // RED_ROWS):
        acc = acc + jnp.where(idx16_ref[r * RED_ROWS:(r + 1) * RED_ROWS, :] >= t16, one, zero)
    c = _colsum(acc.astype(F32))
    ge = c >= kf
    below = jnp.maximum(lo, t - (jnp.abs(t) * 2.0 ** -7 + 1e-30))
    settled = c_lo == kf
    return (jnp.where(ge, below, lo), jnp.where(ge, hi, jnp.minimum(hi, t)),
            jnp.where(jnp.logical_and(ge, jnp.logical_not(settled)), c + 0.5, c_lo), jnp.where(ge, c_hi, c))


def _write_selection(nk, j_blk, lo, idx_ref, rhs_ref):
    r_k, c_q = _tile_iotas()
    later = 2.0 * jnp.maximum(r_k - c_q, 0).astype(F32)

    def write(r, carry):
        rows = pl.ds(pl.multiple_of(r * RED_ROWS, RED_ROWS), RED_ROWS)
        on = jnp.where(r == j_blk, -later, 0.0)
        rhs_ref[rows, LANES:2 * LANES] = jnp.where(idx_ref[rows, :] >= lo, on, -MASK_OFF).astype(BF16)
        return carry

    lax.fori_loop(0, nk // RED_ROWS, write, 0)


def _exact_fallback(slabs, kf, lo, hi, idx_ref, rhs_ref):
    r_k, c_q = _tile_iotas()
    later = 2.0 * jnp.maximum(r_k - c_q, 0).astype(F32)

    def slab(r):
        return idx_ref[pl.ds(pl.multiple_of(r * RED_ROWS, RED_ROWS), RED_ROWS), :]

    def count(pred):
        def body(r, acc):
            return acc + jnp.where(pred(slab(r), r), 1.0, 0.0)
        return _colsum(lax.fori_loop(0, slabs, body, jnp.zeros((RED_ROWS, Q_BLOCK), F32)))

    def kth(lo):
        def body(r, acc):
            s = slab(r)
            return jnp.minimum(acc, jnp.where(s >= lo, s, jnp.inf))
        acc = lax.fori_loop(0, slabs, body, jnp.full((RED_ROWS, Q_BLOCK), jnp.inf, F32))
        thr = jnp.min(acc, axis=0, keepdims=True)
        return thr, count(lambda s, r: s > thr)

    def slow_cond(carry):
        it, _, _, _, c_gt = carry
        return jnp.logical_and(it < SLOW_TRIPS, _any_lane(c_gt >= kf))

    def slow_body(carry):
        it, lo, hi, _, _ = carry
        for _ in range(4):
            mid = 0.5 * lo + 0.5 * hi
            ge = count(lambda s, r: s >= mid) >= kf
            lo, hi = jnp.where(ge, mid, lo), jnp.where(ge, hi, mid)
        thr, c_gt = kth(lo)
        return it + 1, lo, hi, thr, c_gt

    thr0, c_gt0 = kth(lo)
    _, _, _, thr, c_gt = lax.while_loop(slow_cond, slow_body, (jnp.int32(0), lo, hi, thr0, c_gt0))
    c_eq = count(lambda s, r: s == thr)
    need = (c_gt + c_eq) > kf

    def tie_step(_, carry):
        lo_i, hi_i = carry
        mid_i = jnp.floor((lo_i + hi_i) * 0.5)
        below = count(lambda s, r: jnp.logical_and(s == thr, (r * RED_ROWS + r_k).astype(F32) <= mid_i))
        ok = (c_gt + below) >= kf
        return jnp.where(ok, lo_i, mid_i), jnp.where(ok, mid_i, hi_i)

    last = jnp.asarray(slabs * RED_ROWS - 1, F32)
    lo_i = jnp.full((1, Q_BLOCK), -1.0, F32)
    hi_i = jnp.zeros((1, Q_BLOCK), F32) + last
    _, cut = lax.fori_loop(0, 13, tie_step, (lo_i, hi_i))
    cut = jnp.where(need, cut, last + 1.0).astype(jnp.int32)

    def write(r, carry):
        s = slab(r)
        krow = r * RED_ROWS + r_k
        sel = jnp.logical_or(s > thr, jnp.logical_and(s == thr, krow <= cut))
        on = jnp.where(r == slabs - 1, -later, 0.0)
        rhs_ref[pl.ds(pl.multiple_of(r * RED_ROWS, RED_ROWS), RED_ROWS), LANES:2 * LANES] = (
            jnp.where(sel, on, -MASK_OFF).astype(BF16))
        return carry

    lax.fori_loop(0, slabs, write, 0)


GROUP = HEADS // 2
UNITS = PAIR * (HEADS // GROUP)
VALUE_ROWS = HEAD_DIM + 16


def _probabilities(nk, qt_ref, qcols, rhs_ref, s_ref, p_ref, between):
    r_k, c_q = _tile_iotas()
    feat = lax.broadcasted_iota(jnp.int32, (HALF, Q_BLOCK), 0)
    group = GROUP
    width = group * Q_BLOCK
    units = [(u, g) for u in range(len(qcols)) for g in range(HEADS // group)]

    def score_operand(u, hh):
        slope = 2.0 ** (-(hh + 1))
        alibi = jnp.where(feat == 0, CHUNK * slope, jnp.where(feat == 1, slope, 0.0)).astype(BF16)
        scaled_ident = jnp.where(r_k == c_q, slope, 0.0).astype(BF16)
        return jnp.concatenate([qt_ref[hh * HEAD_DIM:(hh + 1) * HEAD_DIM, qcols[u]], alibi, scaled_ident], axis=0)

    def scores(n):
        u, g = units[n]
        lhs_t = jnp.concatenate([score_operand(u, g * group + i) for i in range(group)], axis=1)
        m_acc = jnp.full((RED_ROWS, width), -jnp.inf, F32)
        for rows in _row_blocks(nk, DOT_ROWS):
            blk = jnp.dot(rhs_ref[u, rows, :], lhs_t, preferred_element_type=F32)
            s_ref[n % 2, rows, :] = blk
            for sub in range((rows.stop - rows.start) // RED_ROWS):
                m_acc = jnp.maximum(m_acc, blk[sub * RED_ROWS:(sub + 1) * RED_ROWS, :])
        return jnp.max(m_acc, axis=0, keepdims=True)

    def probabilities(n, m):
        for rows in _row_blocks(nk, RED_ROWS):
            p_ref[n, rows, :] = jnp.exp((s_ref[n % 2, rows, :] - m).astype(BF16))

    m = scores(0)
    between()
    for n in range(len(units)):
        m_next = scores(n + 1) if n + 1 < len(units) else None
        probabilities(n, m)
        m = m_next


def _value_chunk(ch, vtb_ref, p_ref, part_ref):
    if isinstance(ch, int):
        n, blk = ch % UNITS, ch // UNITS
        rows = slice(blk * DOT_ROWS, (blk + 1) * DOT_ROWS)
    else:
        n = ch & (UNITS - 1)
        blk = lax.shift_right_logical(ch, UNITS.bit_length() - 1)
        rows = pl.ds(pl.multiple_of(blk * DOT_ROWS, DOT_ROWS), DOT_ROWS)
    part_ref[n, blk] = jnp.dot(vtb_ref[blk], p_ref[n, rows, :], preferred_element_type=F32)


def _head_tiles(part_ref):
    per_block = []
    for u in range(PAIR):
        outs = []
        for g in range(HEADS // GROUP):
            n = u * (HEADS // GROUP) + g
            o = part_ref[n, 0]
            for blk in range(1, part_ref.shape[1]):
                o = o + part_ref[n, blk]
            outs.append(o[0:HEAD_DIM, :] * (1.0 / o[HEAD_DIM:HEAD_DIM + 1, :]))
        tiles = []
        for t in range(HEADS // 2):
            o_g = outs[(2 * t) // GROUP]
            a = (2 * t) % GROUP
            tiles.append(jnp.concatenate([o_g[:, a * Q_BLOCK:(a + 1) * Q_BLOCK],
                                          o_g[:, (a + 1) * Q_BLOCK:(a + 2) * Q_BLOCK]], axis=0).T)
        per_block.append(tiles)
    return per_block


def _attn_kernel(qt_ref, qit_ref, wit_ref, kaug_ref, kk_ref, vt_ref, ma_ref, gate_ref, x_ref, gb_ref,
                 wo_ref, y_ref, idx_ref, idx16_ref, rhs_ref, s_ref, p_ref, st_ref, vtb_ref, part_ref, done_ref,
                 *, seq, topk):
    b = pl.program_id(0)
    jj = pl.program_id(1)
    kf = float(topk)
    steps = seq // (PAIR * Q_BLOCK)
    key_blocks = seq // DOT_ROWS
    qcols = [slice(u * Q_BLOCK, (u + 1) * Q_BLOCK) for u in range(PAIR)]
    searching = jj < steps
    pending = jj >= 1

    @pl.when(jnp.logical_and(b == 0, jj == 0))
    def _():
        p_ref[...] = jnp.zeros(p_ref.shape, BF16)

    @pl.when(jj == 0)
    def _():
        for u in range(PAIR):
            rhs_ref[u, :, 0:LANES] = kaug_ref[...]
        ones = jnp.ones((VALUE_ROWS - HEAD_DIM, DOT_ROWS), BF16)
        for blk in range(key_blocks):
            vtb_ref[blk] = jnp.concatenate([vt_ref[:, blk * DOT_ROWS:(blk + 1) * DOT_ROWS], ones], axis=0)
        y_ref[...] = jnp.zeros(y_ref.shape, F32)

    part_ref[...] = jnp.zeros(part_ref.shape, F32)
    done_ref[0] = 0

    steps_per_bucket = KEY_BUCKET // (PAIR * Q_BLOCK)

    def buckets(step, active):
        return [((n + 1) * KEY_BUCKET,
                 jnp.logical_and(active, jnp.logical_and(step >= n * steps_per_bucket,
                                                         step < (n + 1) * steps_per_bucket)))
                for n in range(seq // KEY_BUCKET)]

    blocks = [PAIR * jj + u for u in range(PAIR)]
    total = UNITS * (lax.div(jnp.maximum(jj - 1, 0), steps_per_bucket) + 1) * (KEY_BUCKET // DOT_ROWS)

    for nk, here in buckets(jj, searching):
        @pl.when(here)
        def _(nk=nk):
            lane_q = lax.broadcasted_iota(jnp.int32, (1, Q_BLOCK), 1)
            states = []
            for u in range(PAIR):
                n_adm = blocks[u] * Q_BLOCK + CHUNK + CHUNK * (lane_q >= CHUNK).astype(jnp.int32)
                lo, hi = _index_scores(nk, n_adm, qit_ref, qcols[u], wit_ref[:, qcols[u]], kk_ref, s_ref,
                                       idx_ref.at[u], idx16_ref.at[u])
                n_adm_f = n_adm.astype(F32)
                c_lo = jnp.where(n_adm_f <= kf, kf, n_adm_f)
                states.append((lo, hi, c_lo, jnp.zeros((1, Q_BLOCK), F32)))

            def cond(carry):
                it, flat = carry[0], carry[1:]
                still = jnp.logical_or(flat[2] != kf, flat[6] != kf)
                return jnp.logical_and(it < FAST_TRIPS, _any_lane(still))

            def body(carry):
                it, flat = carry[0], carry[1:]
                st = [flat[0:4], flat[4:8]]
                for _ in range(2):
                    st = [_probe(nk, kf, idx_ref.at[u], st[u]) for u in range(PAIR)]
                for c in range(CHUNKS_PER_TRIP):
                    _value_chunk(jnp.minimum(it * CHUNKS_PER_TRIP + c, total - 1), vtb_ref, p_ref, part_ref)
                return (it + 1,) + tuple(st[0]) + tuple(st[1])

            def coarse(it, flat):
                st = [flat[0:4], flat[4:8]]
                for _ in range(2):
                    st = [_probe16(nk, kf, idx16_ref.at[u], st[u]) for u in range(PAIR)]
                for c in range(CHUNKS_PER_TRIP):
                    _value_chunk(jnp.minimum(it * CHUNKS_PER_TRIP + c, total - 1), vtb_ref, p_ref, part_ref)
                return tuple(st[0]) + tuple(st[1])

            start = lax.fori_loop(0, COARSE_TRIPS, coarse, tuple(states[0]) + tuple(states[1]))
            final = lax.while_loop(cond, body, (jnp.int32(COARSE_TRIPS),) + start)
            done_ref[0] = jnp.minimum(final[0] * CHUNKS_PER_TRIP, total)
            for u in range(PAIR):
                lo, hi, c_lo, _ = final[1 + 4 * u:5 + 4 * u]
                _write_selection(nk, blocks[u], lo, idx_ref.at[u], rhs_ref.at[u])
                st_ref[4 * u + 0:4 * u + 1, :] = lo
                st_ref[4 * u + 1:4 * u + 2, :] = hi
                st_ref[4 * u + 2:4 * u + 3, :] = c_lo

    @pl.when(jnp.logical_and(pending, searching))
    def _():
        def chunk(ch, carry):
            _value_chunk(ch, vtb_ref, p_ref, part_ref)
            return carry

        lax.fori_loop(done_ref[0], total, chunk, 0)

    @pl.when(jj == steps)
    def _():
        for ch in range(UNITS * key_blocks):
            _value_chunk(ch, vtb_ref, p_ref, part_ref)

    lane = lax.broadcasted_iota(jnp.int32, (PAIR * Q_BLOCK, LANES), 1)
    lo_half = lane < HALF

    def finish_pending():
        per_block = _head_tiles(part_ref)
        mixed = [ma_ref[...]]
        for t in range(HEADS // 2):
            cols = slice(t * LANES, (t + 1) * LANES)
            o = jnp.concatenate([per_block[u][t] for u in range(PAIR)], axis=0)
            msq = _half_mean_sq(o * o, lo_half)
            ob = o * lax.rsqrt(msq + NORM_EPS) * gb_ref[:, cols] * gate_ref[:, cols].astype(F32)
            mixed.append(ob.astype(BF16))
        y_ref[...] = x_ref[...] + jnp.dot(jnp.concatenate(mixed, axis=1), wo_ref[...],
                                          preferred_element_type=F32)

    for u in range(PAIR):
        lo = st_ref[4 * u + 0:4 * u + 1, :]
        hi = st_ref[4 * u + 1:4 * u + 2, :]
        c_lo = st_ref[4 * u + 2:4 * u + 3, :]

        @pl.when(jnp.logical_and(searching, _any_lane(c_lo != kf)))
        def _(u=u, lo=lo, hi=hi):
            _exact_fallback(blocks[u] + 1, kf, lo, hi, idx_ref.at[u], rhs_ref.at[u])

    for nk, here in buckets(jj, searching):
        @pl.when(here)
        def _(nk=nk):
            _probabilities(nk, qt_ref, qcols, rhs_ref, s_ref, p_ref, finish_pending)

    @pl.when(jj == steps)
    def _():
        finish_pending()


def kernel(x, norm_gain, w_in, sgu_norm_gain, sgu_w, sgu_b, q_norm_gain, k_norm_gain,
           idx_k_norm_gain, branch_norm_gain, w_out):
    bsz, seq, d_model = x.shape
    assert d_model == D_MODEL and norm_gain.shape[0] == 1
    assert seq % PROJ_ROWS == 0 and seq % KEY_BUCKET == 0 and KEY_BUCKET % (PAIR * Q_BLOCK) == 0
    tokens = bsz * seq
    topk = min(TOPK_MAX, seq // 4)
    idx_w_scale = (IDX_HEADS ** -0.5) * (IDX_DIM ** -0.5)

    w = w_in[0]
    a3 = 3 * A_WIDTH
    w_q = w[:, a3:a3 + B_WIDTH]
    w_k = w[:, a3 + B_WIDTH:a3 + B_WIDTH + HEAD_DIM]
    w_v = w[:, a3 + B_WIDTH + HEAD_DIM:a3 + B_WIDTH + 2 * HEAD_DIM]
    o_g = a3 + B_WIDTH + 2 * HEAD_DIM
    w_g = w[:, o_g:o_g + B_WIDTH]
    o_i = o_g + B_WIDTH
    w_iq = w[:, o_i:o_i + IDX_HEADS * IDX_DIM]
    w_ik = w[:, o_i + IDX_HEADS * IDX_DIM:o_i + IDX_HEADS * IDX_DIM + IDX_DIM]
    w_iw = w[:, o_i + IDX_HEADS * IDX_DIM + IDX_DIM:]
    w_main = jnp.concatenate([w[:, :a3], w_g, w_k, w_ik, w_v, w_iw,
                              jnp.zeros((D_MODEL, LANES - HEAD_DIM - IDX_HEADS), F32)], axis=1).astype(BF16)
    assert w_main.shape[1] == _PACKED_COLS
    w_t = jnp.concatenate([w_q.T, w_iq.T], axis=0).astype(BF16)
    assert w_t.shape[0] == _PACKED_ROWS
    x2 = x.reshape(tokens, D_MODEL)
    ng = norm_gain[0].reshape(1, D_MODEL)
    sgn = sgu_norm_gain[0].reshape(1, A_WIDTH)
    sw = sgu_w[0]
    sb = sgu_b[0].reshape(A_GROUPS, A_BLOCK, 1)
    qg = q_norm_gain[0].reshape(HEAD_DIM, 1)
    kg = jnp.concatenate([k_norm_gain[0], idx_k_norm_gain[0]]).reshape(1, LANES)
    ga = branch_norm_gain[0, :A_WIDTH].reshape(1, A_WIDTH)
    gb = branch_norm_gain[0, A_WIDTH:].reshape(1, B_WIDTH)
    wo = w_out[0].astype(BF16)

    tm = PROJ_ROWS
    full = lambda shape: pl.BlockSpec(shape, lambda i: (0,) * len(shape))
    rows = lambda width: pl.BlockSpec((tm, width), lambda i: (i, 0))
    colsT = lambda height: pl.BlockSpec((height, tm), lambda i: (0, i))
    outs = pl.pallas_call(
        functools.partial(_proj_kernel, tiles_per_seq=seq // tm, idx_w_scale=idx_w_scale),
        grid=(tokens // tm,),
        in_specs=[rows(D_MODEL), full((1, D_MODEL)), full((D_MODEL, _PACKED_COLS)),
                  full((_PACKED_ROWS, D_MODEL)), full((1, A_WIDTH)), full((A_GROUPS, A_BLOCK, A_BLOCK)),
                  full((A_GROUPS, A_BLOCK, 1)), full((HEAD_DIM, 1)), full((1, LANES)),
                  full((1, A_WIDTH))],
        out_specs=[rows(A_WIDTH), rows(B_WIDTH), rows(LANES), rows(LANES),
                   colsT(B_WIDTH), colsT(IDX_HEADS * IDX_DIM), colsT(HEAD_DIM), colsT(IDX_HEADS)],
        out_shape=[jax.ShapeDtypeStruct((tokens, A_WIDTH), BF16),
                   jax.ShapeDtypeStruct((tokens, B_WIDTH), BF16),
                   jax.ShapeDtypeStruct((tokens, LANES), BF16),
                   jax.ShapeDtypeStruct((tokens, LANES), BF16),
                   jax.ShapeDtypeStruct((B_WIDTH, tokens), BF16),
                   jax.ShapeDtypeStruct((IDX_HEADS * IDX_DIM, tokens), BF16),
                   jax.ShapeDtypeStruct((HEAD_DIM, tokens), BF16),
                   jax.ShapeDtypeStruct((IDX_HEADS, tokens), F32)],
        compiler_params=pltpu.CompilerParams(dimension_semantics=("arbitrary",),
                                             vmem_limit_bytes=VMEM_LIMIT),
        name="proj_sgu",
    )(x2, ng, w_main, w_t, sgn, sw, sb, qg, kg, ga)
    ma, gate, kaug, kk, qt, qit, vt, wit = outs

    qrows = PAIR * Q_BLOCK
    steps = seq // qrows
    qblk = lambda width: pl.BlockSpec((qrows, width), lambda b, j: (b * steps + jnp.maximum(j - 1, 0), 0))
    qblkT = lambda height: pl.BlockSpec((height, qrows), lambda b, j: (0, b * steps + jnp.minimum(j, steps - 1)))
    const = lambda shape: pl.BlockSpec(shape, lambda b, j: (0,) * len(shape))
    y = pl.pallas_call(
        functools.partial(_attn_kernel, seq=seq, topk=topk),
        grid=(bsz, steps + 1),
        in_specs=[qblkT(B_WIDTH), qblkT(IDX_HEADS * IDX_DIM), qblkT(IDX_HEADS),
                  pl.BlockSpec((seq, LANES), lambda b, j: (b, 0)),
                  pl.BlockSpec((seq, LANES), lambda b, j: (b, 0)),
                  pl.BlockSpec((HEAD_DIM, seq), lambda b, j: (0, b)),
                  qblk(A_WIDTH), qblk(B_WIDTH), qblk(D_MODEL),
                  const((1, B_WIDTH)), const((D_MODEL, D_MODEL))],
        out_specs=qblk(D_MODEL),
        out_shape=jax.ShapeDtypeStruct((tokens, D_MODEL), F32),
        scratch_shapes=[pltpu.VMEM((PAIR, seq, Q_BLOCK), F32),
                        pltpu.VMEM((PAIR, seq, Q_BLOCK), BF16),
                        pltpu.VMEM((PAIR, seq, 2 * LANES), BF16),
                        pltpu.VMEM((2, seq, GROUP * Q_BLOCK), F32),
                        pltpu.VMEM((UNITS, seq, GROUP * Q_BLOCK), BF16),
                        pltpu.VMEM((8, Q_BLOCK), F32),
                        pltpu.VMEM((seq // DOT_ROWS, VALUE_ROWS, DOT_ROWS), BF16),
                        pltpu.VMEM((UNITS, seq // DOT_ROWS, VALUE_ROWS, GROUP * Q_BLOCK), F32),
                        pltpu.SMEM((1,), jnp.int32)],
        compiler_params=pltpu.CompilerParams(dimension_semantics=("arbitrary", "arbitrary"),
                                             vmem_limit_bytes=VMEM_LIMIT),
        name="dsa_attn_out",
    )(qt, qit, wit, kaug, kk, vt, ma, gate, x2, gb, wo)
    return y.reshape(bsz, seq, D_MODEL)
```

```python
import functools

import jax
import jax.numpy as jnp
from jax import lax
from jax.experimental import pallas as pl
from jax.experimental.pallas import tpu as pltpu

F32 = jnp.float32
BF16 = jnp.bfloat16

D_MODEL = 1024
CHUNK = 64
A_WIDTH = 512
A_GROUPS = 4
A_BLOCK = 128
HEADS = 8
HEAD_DIM = 64
B_WIDTH = HEADS * HEAD_DIM
IDX_HEADS = 8
IDX_DIM = 64
TOPK_MAX = 256
Q_BLOCK = 128
PAIR = 2
NORM_EPS = 1e-6
MASK_OFF = 1e32
LANES = 128
HALF = LANES // 2
PROJ_ROWS = 512
DOT_ROWS = 512
RED_ROWS = 128
KEY_BUCKET = 512
FAST_TRIPS = 24
INTERP_MARGIN = 0.02
SLOW_TRIPS = 70
CHUNKS_PER_TRIP = 2
VMEM_LIMIT = 48 * 1024 * 1024

_OFF_U, _OFF_V, _OFF_Z, _OFF_G, _OFF_K = 0, 512, 1024, 1536, 2048
_PACKED_COLS = 2304
_ROW_Q, _ROW_QI, _PACKED_ROWS = 0, 512, 1024

_NT = (((1,), (1,)), ((), ()))


def _gelu(x):
    c = 0.7978845608028654
    return 0.5 * x * (1.0 + jnp.tanh(c * (x + 0.044715 * (x * x * x))))


def _silu(x):
    return x / (1.0 + jnp.exp(-x))


def _row_blocks(total, size):
    return [slice(r, min(r + size, total)) for r in range(0, total, size)]


def _half_mean_sq(x2, lo_half):
    tot = jnp.sum(x2, axis=-1, keepdims=True)
    lo = jnp.sum(jnp.where(lo_half, x2, 0.0), axis=-1, keepdims=True)
    return jnp.where(lo_half, lo, tot - lo) * (1.0 / HALF)


def _proj_kernel(x_ref, ng_ref, w_ref, wt_ref, sgn_ref, sw_ref, sb_ref, qg_ref, kg_ref, ga_ref,
                 ma_ref, gate_ref, kaug_ref, kk_ref, qt_ref, qit_ref, vt_ref, wit_ref,
                 *, tiles_per_seq, idx_w_scale):
    tm = x_ref.shape[0]
    i = pl.program_id(0)
    x = x_ref[...]
    ms = jnp.mean(x * x, axis=-1, keepdims=True)
    h = (x * lax.rsqrt(ms + NORM_EPS) * ng_ref[...]).astype(BF16)

    lane = lax.broadcasted_iota(jnp.int32, (tm, LANES), 1)
    lo_half = lane < HALF

    def proj(off, width):
        return jnp.dot(h, w_ref[:, off:off + width], preferred_element_type=F32)

    gu = _gelu(proj(_OFF_U, A_WIDTH))
    gv = _gelu(proj(_OFF_V, A_WIDTH))
    pz = proj(_OFF_Z, A_WIDTH)
    r_i = lax.broadcasted_iota(jnp.int32, (A_BLOCK, A_BLOCK), 0)
    c_j = lax.broadcasted_iota(jnp.int32, (A_BLOCK, A_BLOCK), 1)
    causal = lax.shift_right_logical(c_j, 6) <= lax.shift_right_logical(r_i, 6)
    for g in range(A_GROUPS):
        cols = slice(g * LANES, (g + 1) * LANES)
        vg = gv[:, cols]
        mu = jnp.mean(vg, axis=-1, keepdims=True)
        d = vg - mu
        var = jnp.mean(d * d, axis=-1, keepdims=True)
        vn = (d * lax.rsqrt(var + NORM_EPS) * sgn_ref[:, cols]).astype(BF16)
        wg = jnp.where(causal, sw_ref[g], 0.0).astype(BF16)
        side = jnp.concatenate([vn[blk * A_BLOCK:(blk + 1) * A_BLOCK, :] for blk in range(tm // A_BLOCK)], axis=1)
        mixed = jnp.dot(wg, side, preferred_element_type=F32) + sb_ref[g]
        for blk in range(tm // A_BLOCK):
            rows = slice(blk * A_BLOCK, (blk + 1) * A_BLOCK)
            s = mixed[:, blk * LANES:(blk + 1) * LANES]
            ya = gu[rows, cols] * s
            oa = ya * lax.rsqrt(jnp.mean(ya * ya, axis=-1, keepdims=True) + NORM_EPS) * ga_ref[:, cols]
            ma_ref[rows, cols] = (oa * _silu(pz[rows, cols])).astype(BF16)

    gate_ref[...] = _silu(proj(_OFF_G, B_WIDTH)).astype(BF16)

    pkv = proj(_OFF_K, 2 * LANES)
    pk = pkv[:, 0:LANES]
    k_ms = jnp.sum(jnp.where(lo_half, pk * pk, 0.0), axis=-1, keepdims=True) * (1.0 / HALF)
    kn = pk * lax.rsqrt(k_ms + NORM_EPS)
    ik_mu = jnp.sum(jnp.where(lo_half, 0.0, pk), axis=-1, keepdims=True) * (1.0 / HALF)
    dk = pk - ik_mu
    ik_var = jnp.sum(jnp.where(lo_half, 0.0, dk * dk), axis=-1, keepdims=True) * (1.0 / HALF)
    kin = dk * lax.rsqrt(ik_var + NORM_EPS)
    tile = jnp.where(lo_half, kn, kin) * kg_ref[...]
    swapped = pltpu.roll(tile, HALF, axis=1)
    row = lax.broadcasted_iota(jnp.int32, (tm, LANES), 0)
    pos = (i % tiles_per_seq) * tm + row
    pos_hi = lax.shift_right_logical(pos, 6).astype(F32)
    pos_lo = (pos & (CHUNK - 1)).astype(F32)
    posfeat = jnp.where(lane == HALF, pos_hi, jnp.where(lane == HALF + 1, pos_lo, 0.0))
    kaug_ref[...] = jnp.where(lo_half, tile, posfeat).astype(BF16)
    kk_ref[...] = jnp.where(lo_half, swapped, tile).astype(BF16)

    pt = lax.dot_general(wt_ref[...], h, _NT, preferred_element_type=F32)
    for hh in range(HEADS):
        rows = slice(_ROW_Q + hh * HEAD_DIM, _ROW_Q + (hh + 1) * HEAD_DIM)
        xq = pt[rows, :]
        msq = jnp.mean(xq * xq, axis=0, keepdims=True)
        qt_ref[rows, :] = (xq * lax.rsqrt(msq + NORM_EPS) * qg_ref[...] * (HEAD_DIM ** -0.5)).astype(BF16)
    qit_ref[...] = pt[_ROW_QI:_ROW_QI + IDX_HEADS * IDX_DIM, :].astype(BF16)
    for blk in range(tm // LANES):
        cols = slice(blk * LANES, (blk + 1) * LANES)
        vw_t = pkv[cols, LANES:2 * LANES].T
        vt_ref[:, cols] = vw_t[0:HEAD_DIM, :].astype(BF16)
        wit_ref[:, cols] = vw_t[HEAD_DIM:HEAD_DIM + IDX_HEADS, :] * idx_w_scale


def _tile_iotas():
    r_k = lax.broadcasted_iota(jnp.int32, (Q_BLOCK, Q_BLOCK), 0)
    c_q = lax.broadcasted_iota(jnp.int32, (Q_BLOCK, Q_BLOCK), 1)
    return r_k, c_q


def _colsum(x):
    return jnp.sum(x, axis=0, keepdims=True)


def _any_lane(pred):
    return jnp.max(jnp.where(pred, 1.0, 0.0)) > 0.0


def _index_scores(nk, n_adm, qit_ref, qcols, wi, kk_ref, s_ref, idx_ref):
    r_k, c_q = _tile_iotas()
    top_rows = r_k < HALF
    per_head = []
    for t in range(IDX_HEADS // 2):
        qit = qit_ref[t * LANES:(t + 1) * LANES, qcols]
        zero = jnp.zeros_like(qit)
        per_head += [jnp.where(top_rows, qit, zero), jnp.where(top_rows, zero, qit)]
    half_heads = IDX_HEADS // 2
    for g in range(2):
        wg = jnp.concatenate(per_head[g * half_heads:(g + 1) * half_heads], axis=1)
        for rows in _row_blocks(nk, DOT_ROWS):
            s_ref[g, rows, :] = jnp.dot(kk_ref[rows, :], wg, preferred_element_type=F32)

    def weighted_relu(g, rows):
        acc = None
        for i in range(half_heads):
            hh = g * half_heads + i
            term = jnp.maximum(s_ref[g, rows, i * Q_BLOCK:(i + 1) * Q_BLOCK], 0.0) * wi[hh:hh + 1, :]
            acc = term if acc is None else acc + term
        return acc

    slabs = nk // RED_ROWS
    mn_acc = jnp.full((RED_ROWS, Q_BLOCK), jnp.inf, F32)
    mx_acc = jnp.full((RED_ROWS, Q_BLOCK), -jnp.inf, F32)
    tiny_acc = jnp.full((RED_ROWS, Q_BLOCK), jnp.inf, F32)
    for r in range(slabs):
        rows = slice(r * RED_ROWS, (r + 1) * RED_ROWS)
        acc = weighted_relu(0, rows) + weighted_relu(1, rows)
        if r >= slabs - KEY_BUCKET // RED_ROWS:
            adm = (r * RED_ROWS + r_k) < n_adm
            lo_fill = jnp.where(adm, acc, -jnp.inf)
            hi_fill = jnp.where(adm, acc, jnp.inf)
        else:
            lo_fill = hi_fill = acc
        mag = jnp.abs(hi_fill)
        idx_ref[rows, :] = lo_fill
        mn_acc = jnp.minimum(mn_acc, hi_fill)
        mx_acc = jnp.maximum(mx_acc, lo_fill)
        tiny_acc = jnp.minimum(tiny_acc, jnp.where(mag == 0.0, jnp.inf, mag))
    lo = jnp.min(mn_acc, axis=0, keepdims=True)
    hi = jnp.max(mx_acc, axis=0, keepdims=True)
    tiny = jnp.min(tiny_acc, axis=0, keepdims=True)

    unit = jnp.where(tiny < jnp.inf, tiny, 1.0)
    eps = unit * (0.5 / nk)
    rank0 = (1 + r_k).astype(F32)

    def spread(r, carry):
        rows = pl.ds(pl.multiple_of(r * RED_ROWS, RED_ROWS), RED_ROWS)
        s = idx_ref[rows, :]
        rank = rank0 + jnp.asarray(r * RED_ROWS, F32)
        idx_ref[rows, :] = jnp.where(s == 0.0, -(rank * eps), s)
        return carry

    lax.fori_loop(0, slabs, spread, 0)
    return jnp.minimum(lo, -0.5 * unit), hi


def _probe(nk, kf, idx_ref, state):
    lo, hi, c_lo, c_hi = state
    frac = (c_lo - kf) / jnp.maximum(c_lo - c_hi, 1.0)
    frac = jnp.minimum(jnp.maximum(frac, INTERP_MARGIN), 1.0 - INTERP_MARGIN)
    t = lo + (hi - lo) * frac
    acc = jnp.zeros((RED_ROWS, Q_BLOCK), F32)
    for r in range(nk // RED_ROWS):
        acc = acc + jnp.where(idx_ref[r * RED_ROWS:(r + 1) * RED_ROWS, :] >= t, 1.0, 0.0)
    c = _colsum(acc)
    ge = c >= kf
    return (jnp.where(ge, t, lo), jnp.where(ge, hi, t), jnp.where(ge, c, c_lo), jnp.where(ge, c_hi, c))


def _write_selection(nk, j_blk, lo, idx_ref, rhs_ref):
    r_k, c_q = _tile_iotas()
    later = 2.0 * jnp.maximum(r_k - c_q, 0).astype(F32)

    def write(r, carry):
        rows = pl.ds(pl.multiple_of(r * RED_ROWS, RED_ROWS), RED_ROWS)
        on = jnp.where(r == j_blk, -later, 0.0)
        rhs_ref[rows, LANES:2 * LANES] = jnp.where(idx_ref[rows, :] >= lo, on, -MASK_OFF).astype(BF16)
        return carry

    lax.fori_loop(0, nk // RED_ROWS, write, 0)


def _exact_fallback(slabs, kf, lo, hi, idx_ref, rhs_ref):
    r_k, c_q = _tile_iotas()
    later = 2.0 * jnp.maximum(r_k - c_q, 0).astype(F32)

    def slab(r):
        return idx_ref[pl.ds(pl.multiple_of(r * RED_ROWS, RED_ROWS), RED_ROWS), :]

    def count(pred):
        def body(r, acc):
            return acc + jnp.where(pred(slab(r), r), 1.0, 0.0)
        return _colsum(lax.fori_loop(0, slabs, body, jnp.zeros((RED_ROWS, Q_BLOCK), F32)))

    def kth(lo):
        def body(r, acc):
            s = slab(r)
            return jnp.minimum(acc, jnp.where(s >= lo, s, jnp.inf))
        acc = lax.fori_loop(0, slabs, body, jnp.full((RED_ROWS, Q_BLOCK), jnp.inf, F32))
        thr = jnp.min(acc, axis=0, keepdims=True)
        return thr, count(lambda s, r: s > thr)

    def slow_cond(carry):
        it, _, _, _, c_gt = carry
        return jnp.logical_and(it < SLOW_TRIPS, _any_lane(c_gt >= kf))

    def slow_body(carry):
        it, lo, hi, _, _ = carry
        for _ in range(4):
            mid = 0.5 * lo + 0.5 * hi
            ge = count(lambda s, r: s >= mid) >= kf
            lo, hi = jnp.where(ge, mid, lo), jnp.where(ge, hi, mid)
        thr, c_gt = kth(lo)
        return it + 1, lo, hi, thr, c_gt

    thr0, c_gt0 = kth(lo)
    _, _, _, thr, c_gt = lax.while_loop(slow_cond, slow_body, (jnp.int32(0), lo, hi, thr0, c_gt0))
    c_eq = count(lambda s, r: s == thr)
    need = (c_gt + c_eq) > kf

    def tie_step(_, carry):
        lo_i, hi_i = carry
        mid_i = jnp.floor((lo_i + hi_i) * 0.5)
        below = count(lambda s, r: jnp.logical_and(s == thr, (r * RED_ROWS + r_k).astype(F32) <= mid_i))
        ok = (c_gt + below) >= kf
        return jnp.where(ok, lo_i, mid_i), jnp.where(ok, mid_i, hi_i)

    last = jnp.asarray(slabs * RED_ROWS - 1, F32)
    lo_i = jnp.full((1, Q_BLOCK), -1.0, F32)
    hi_i = jnp.zeros((1, Q_BLOCK), F32) + last
    _, cut = lax.fori_loop(0, 13, tie_step, (lo_i, hi_i))
    cut = jnp.where(need, cut, last + 1.0).astype(jnp.int32)

    def write(r, carry):
        s = slab(r)
        krow = r * RED_ROWS + r_k
        sel = jnp.logical_or(s > thr, jnp.logical_and(s == thr, krow <= cut))
        on = jnp.where(r == slabs - 1, -later, 0.0)
        rhs_ref[pl.ds(pl.multiple_of(r * RED_ROWS, RED_ROWS), RED_ROWS), LANES:2 * LANES] = (
            jnp.where(sel, on, -MASK_OFF).astype(BF16))
        return carry

    lax.fori_loop(0, slabs, write, 0)


GROUP = HEADS // 2
UNITS = PAIR * (HEADS // GROUP)
VALUE_ROWS = HEAD_DIM + 16


def _probabilities(nk, qt_ref, qcols, rhs_ref, s_ref, p_ref, between):
    r_k, c_q = _tile_iotas()
    feat = lax.broadcasted_iota(jnp.int32, (HALF, Q_BLOCK), 0)
    group = GROUP
    width = group * Q_BLOCK
    units = [(u, g) for u in range(len(qcols)) for g in range(HEADS // group)]

    def score_operand(u, hh):
        slope = 2.0 ** (-(hh + 1))
        alibi = jnp.where(feat == 0, CHUNK * slope, jnp.where(feat == 1, slope, 0.0)).astype(BF16)
        scaled_ident = jnp.where(r_k == c_q, slope, 0.0).astype(BF16)
        return jnp.concatenate([qt_ref[hh * HEAD_DIM:(hh + 1) * HEAD_DIM, qcols[u]], alibi, scaled_ident], axis=0)

    def scores(n):
        u, g = units[n]
        lhs_t = jnp.concatenate([score_operand(u, g * group + i) for i in range(group)], axis=1)
        m_acc = jnp.full((RED_ROWS, width), -jnp.inf, F32)
        for rows in _row_blocks(nk, DOT_ROWS):
            blk = jnp.dot(rhs_ref[u, rows, :], lhs_t, preferred_element_type=F32)
            s_ref[n % 2, rows, :] = blk
            for sub in range((rows.stop - rows.start) // RED_ROWS):
                m_acc = jnp.maximum(m_acc, blk[sub * RED_ROWS:(sub + 1) * RED_ROWS, :])
        return jnp.max(m_acc, axis=0, keepdims=True)

    def probabilities(n, m):
        for rows in _row_blocks(nk, RED_ROWS):
            p_ref[n, rows, :] = jnp.exp((s_ref[n % 2, rows, :] - m).astype(BF16))

    m = scores(0)
    between()
    for n in range(len(units)):
        m_next = scores(n + 1) if n + 1 < len(units) else None
        probabilities(n, m)
        m = m_next


def _value_chunk(ch, vtb_ref, p_ref, part_ref):
    if isinstance(ch, int):
        n, blk = ch % UNITS, ch // UNITS
        rows = slice(blk * DOT_ROWS, (blk + 1) * DOT_ROWS)
    else:
        n = ch & (UNITS - 1)
        blk = lax.shift_right_logical(ch, UNITS.bit_length() - 1)
        rows = pl.ds(pl.multiple_of(blk * DOT_ROWS, DOT_ROWS), DOT_ROWS)
    part_ref[n, blk] = jnp.dot(vtb_ref[blk], p_ref[n, rows, :], preferred_element_type=F32)


def _head_tiles(part_ref):
    per_block = []
    for u in range(PAIR):
        outs = []
        for g in range(HEADS // GROUP):
            n = u * (HEADS // GROUP) + g
            o = part_ref[n, 0]
            for blk in range(1, part_ref.shape[1]):
                o = o + part_ref[n, blk]
            outs.append(o[0:HEAD_DIM, :] * (1.0 / o[HEAD_DIM:HEAD_DIM + 1, :]))
        tiles = []
        for t in range(HEADS // 2):
            o_g = outs[(2 * t) // GROUP]
            a = (2 * t) % GROUP
            tiles.append(jnp.concatenate([o_g[:, a * Q_BLOCK:(a + 1) * Q_BLOCK],
                                          o_g[:, (a + 1) * Q_BLOCK:(a + 2) * Q_BLOCK]], axis=0).T)
        per_block.append(tiles)
    return per_block


def _attn_kernel(qt_ref, qit_ref, wit_ref, kaug_ref, kk_ref, vt_ref, ma_ref, gate_ref, x_ref, gb_ref,
                 wo_ref, y_ref, idx_ref, rhs_ref, s_ref, p_ref, st_ref, vtb_ref, part_ref, done_ref,
                 *, seq, n_pairs, topk):
    g = pl.program_id(0)
    kf = float(topk)
    steps = seq // (PAIR * Q_BLOCK)
    key_blocks = seq // DOT_ROWS
    qcols = [slice(u * Q_BLOCK, (u + 1) * Q_BLOCK) for u in range(PAIR)]
    searching = g < n_pairs
    pending = g >= 1
    jj = lax.rem(g, steps)
    g_prev = jnp.maximum(g - 1, 0)
    slot = lax.div(g, steps) & 1
    slot_prev = lax.div(g_prev, steps) & 1

    @pl.when(g == 0)
    def _():
        p_ref[...] = jnp.zeros(p_ref.shape, BF16)
        vtb_ref[...] = jnp.zeros(vtb_ref.shape, BF16)
        y_ref[...] = jnp.zeros(y_ref.shape, F32)

    @pl.when(jnp.logical_and(searching, jj == 0))
    def _():
        for u in range(PAIR):
            rhs_ref[u, :, 0:LANES] = kaug_ref[...]
        ones = jnp.ones((VALUE_ROWS - HEAD_DIM, DOT_ROWS), BF16)
        for blk in range(key_blocks):
            vtb_ref[slot, blk] = jnp.concatenate([vt_ref[:, blk * DOT_ROWS:(blk + 1) * DOT_ROWS], ones], axis=0)

    part_ref[...] = jnp.zeros(part_ref.shape, F32)
    done_ref[0] = 0

    steps_per_bucket = KEY_BUCKET // (PAIR * Q_BLOCK)

    def buckets(step, active):
        return [((n + 1) * KEY_BUCKET,
                 jnp.logical_and(active, jnp.logical_and(step >= n * steps_per_bucket,
                                                         step < (n + 1) * steps_per_bucket)))
                for n in range(seq // KEY_BUCKET)]

    blocks = [PAIR * jj + u for u in range(PAIR)]
    total = UNITS * (lax.div(lax.rem(g_prev, steps), steps_per_bucket) + 1) * (KEY_BUCKET // DOT_ROWS)
    vtb_prev = vtb_ref.at[slot_prev]

    for nk, here in buckets(jj, searching):
        @pl.when(here)
        def _(nk=nk):
            lane_q = lax.broadcasted_iota(jnp.int32, (1, Q_BLOCK), 1)
            states = []
            for u in range(PAIR):
                n_adm = blocks[u] * Q_BLOCK + CHUNK + CHUNK * (lane_q >= CHUNK).astype(jnp.int32)
                lo, hi = _index_scores(nk, n_adm, qit_ref, qcols[u], wit_ref[:, qcols[u]], kk_ref, s_ref,
                                       idx_ref.at[u])
                n_adm_f = n_adm.astype(F32)
                c_lo = jnp.where(n_adm_f <= kf, kf, n_adm_f)
                states.append((lo, hi, c_lo, jnp.zeros((1, Q_BLOCK), F32)))

            def cond(carry):
                it, flat = carry[0], carry[1:]
                still = jnp.logical_or(flat[2] != kf, flat[6] != kf)
                return jnp.logical_and(it < FAST_TRIPS, _any_lane(still))

            def body(carry):
                it, flat = carry[0], carry[1:]
                st = [flat[0:4], flat[4:8]]
                for _ in range(2):
                    st = [_probe(nk, kf, idx_ref.at[u], st[u]) for u in range(PAIR)]
                for c in range(CHUNKS_PER_TRIP):
                    _value_chunk(jnp.minimum(it * CHUNKS_PER_TRIP + c, total - 1), vtb_prev, p_ref, part_ref)
                return (it + 1,) + tuple(st[0]) + tuple(st[1])

            final = lax.while_loop(cond, body, (jnp.int32(0),) + tuple(states[0]) + tuple(states[1]))
            done_ref[0] = jnp.minimum(final[0] * CHUNKS_PER_TRIP, total)
            for u in range(PAIR):
                lo, hi, c_lo, _ = final[1 + 4 * u:5 + 4 * u]
                _write_selection(nk, blocks[u], lo, idx_ref.at[u], rhs_ref.at[u])
                st_ref[4 * u + 0:4 * u + 1, :] = lo
                st_ref[4 * u + 1:4 * u + 2, :] = hi
                st_ref[4 * u + 2:4 * u + 3, :] = c_lo

    @pl.when(jnp.logical_and(pending, searching))
    def _():
        def chunk(ch, carry):
            _value_chunk(ch, vtb_prev, p_ref, part_ref)
            return carry

        lax.fori_loop(done_ref[0], total, chunk, 0)

    @pl.when(g == n_pairs)
    def _():
        for ch in range(UNITS * key_blocks):
            _value_chunk(ch, vtb_prev, p_ref, part_ref)

    lane = lax.broadcasted_iota(jnp.int32, (PAIR * Q_BLOCK, LANES), 1)
    lo_half = lane < HALF

    def finish_pending():
        per_block = _head_tiles(part_ref)
        mixed = [ma_ref[...]]
        for t in range(HEADS // 2):
            cols = slice(t * LANES, (t + 1) * LANES)
            o = jnp.concatenate([per_block[u][t] for u in range(PAIR)], axis=0)
            msq = _half_mean_sq(o * o, lo_half)
            ob = o * lax.rsqrt(msq + NORM_EPS) * gb_ref[:, cols] * gate_ref[:, cols].astype(F32)
            mixed.append(ob.astype(BF16))
        y_ref[...] = x_ref[...] + jnp.dot(jnp.concatenate(mixed, axis=1), wo_ref[...],
                                          preferred_element_type=F32)

    for u in range(PAIR):
        lo = st_ref[4 * u + 0:4 * u + 1, :]
        hi = st_ref[4 * u + 1:4 * u + 2, :]
        c_lo = st_ref[4 * u + 2:4 * u + 3, :]

        @pl.when(jnp.logical_and(searching, _any_lane(c_lo != kf)))
        def _(u=u, lo=lo, hi=hi):
            _exact_fallback(blocks[u] + 1, kf, lo, hi, idx_ref.at[u], rhs_ref.at[u])

    for nk, here in buckets(jj, searching):
        @pl.when(here)
        def _(nk=nk):
            _probabilities(nk, qt_ref, qcols, rhs_ref, s_ref, p_ref, finish_pending)

    @pl.when(g == n_pairs)
    def _():
        finish_pending()


def kernel(x, norm_gain, w_in, sgu_norm_gain, sgu_w, sgu_b, q_norm_gain, k_norm_gain,
           idx_k_norm_gain, branch_norm_gain, w_out):
    bsz, seq, d_model = x.shape
    assert d_model == D_MODEL and norm_gain.shape[0] == 1
    assert seq % PROJ_ROWS == 0 and seq % KEY_BUCKET == 0 and KEY_BUCKET % (PAIR * Q_BLOCK) == 0
    tokens = bsz * seq
    topk = min(TOPK_MAX, seq // 4)
    idx_w_scale = (IDX_HEADS ** -0.5) * (IDX_DIM ** -0.5)

    w = w_in[0]
    a3 = 3 * A_WIDTH
    w_q = w[:, a3:a3 + B_WIDTH]
    w_k = w[:, a3 + B_WIDTH:a3 + B_WIDTH + HEAD_DIM]
    w_v = w[:, a3 + B_WIDTH + HEAD_DIM:a3 + B_WIDTH + 2 * HEAD_DIM]
    o_g = a3 + B_WIDTH + 2 * HEAD_DIM
    w_g = w[:, o_g:o_g + B_WIDTH]
    o_i = o_g + B_WIDTH
    w_iq = w[:, o_i:o_i + IDX_HEADS * IDX_DIM]
    w_ik = w[:, o_i + IDX_HEADS * IDX_DIM:o_i + IDX_HEADS * IDX_DIM + IDX_DIM]
    w_iw = w[:, o_i + IDX_HEADS * IDX_DIM + IDX_DIM:]
    w_main = jnp.concatenate([w[:, :a3], w_g, w_k, w_ik, w_v, w_iw,
                              jnp.zeros((D_MODEL, LANES - HEAD_DIM - IDX_HEADS), F32)], axis=1).astype(BF16)
    assert w_main.shape[1] == _PACKED_COLS
    w_t = jnp.concatenate([w_q.T, w_iq.T], axis=0).astype(BF16)
    assert w_t.shape[0] == _PACKED_ROWS
    x2 = x.reshape(tokens, D_MODEL)
    ng = norm_gain[0].reshape(1, D_MODEL)
    sgn = sgu_norm_gain[0].reshape(1, A_WIDTH)
    sw = sgu_w[0]
    sb = sgu_b[0].reshape(A_GROUPS, A_BLOCK, 1)
    qg = q_norm_gain[0].reshape(HEAD_DIM, 1)
    kg = jnp.concatenate([k_norm_gain[0], idx_k_norm_gain[0]]).reshape(1, LANES)
    ga = branch_norm_gain[0, :A_WIDTH].reshape(1, A_WIDTH)
    gb = branch_norm_gain[0, A_WIDTH:].reshape(1, B_WIDTH)
    wo = w_out[0].astype(BF16)

    tm = PROJ_ROWS
    full = lambda shape: pl.BlockSpec(shape, lambda i: (0,) * len(shape))
    rows = lambda width: pl.BlockSpec((tm, width), lambda i: (i, 0))
    colsT = lambda height: pl.BlockSpec((height, tm), lambda i: (0, i))
    outs = pl.pallas_call(
        functools.partial(_proj_kernel, tiles_per_seq=seq // tm, idx_w_scale=idx_w_scale),
        grid=(tokens // tm,),
        in_specs=[rows(D_MODEL), full((1, D_MODEL)), full((D_MODEL, _PACKED_COLS)),
                  full((_PACKED_ROWS, D_MODEL)), full((1, A_WIDTH)), full((A_GROUPS, A_BLOCK, A_BLOCK)),
                  full((A_GROUPS, A_BLOCK, 1)), full((HEAD_DIM, 1)), full((1, LANES)),
                  full((1, A_WIDTH))],
        out_specs=[rows(A_WIDTH), rows(B_WIDTH), rows(LANES), rows(LANES),
                   colsT(B_WIDTH), colsT(IDX_HEADS * IDX_DIM), colsT(HEAD_DIM), colsT(IDX_HEADS)],
        out_shape=[jax.ShapeDtypeStruct((tokens, A_WIDTH), BF16),
                   jax.ShapeDtypeStruct((tokens, B_WIDTH), BF16),
                   jax.ShapeDtypeStruct((tokens, LANES), BF16),
                   jax.ShapeDtypeStruct((tokens, LANES), BF16),
                   jax.ShapeDtypeStruct((B_WIDTH, tokens), BF16),
                   jax.ShapeDtypeStruct((IDX_HEADS * IDX_DIM, tokens), BF16),
                   jax.ShapeDtypeStruct((HEAD_DIM, tokens), BF16),
                   jax.ShapeDtypeStruct((IDX_HEADS, tokens), F32)],
        compiler_params=pltpu.CompilerParams(dimension_semantics=("arbitrary",),
                                             vmem_limit_bytes=VMEM_LIMIT),
        name="proj_sgu",
    )(x2, ng, w_main, w_t, sgn, sw, sb, qg, kg, ga)
    ma, gate, kaug, kk, qt, qit, vt, wit = outs

    qrows = PAIR * Q_BLOCK
    steps = seq // qrows
    n_pairs = bsz * steps
    cur = lambda g: jnp.minimum(g, n_pairs - 1)
    qblk = lambda width: pl.BlockSpec((qrows, width), lambda g: (jnp.maximum(g - 1, 0), 0))
    qblkT = lambda height: pl.BlockSpec((height, qrows), lambda g: (0, cur(g)))
    const = lambda shape: pl.BlockSpec(shape, lambda g: (0,) * len(shape))
    y = pl.pallas_call(
        functools.partial(_attn_kernel, seq=seq, n_pairs=n_pairs, topk=topk),
        grid=(n_pairs + 1,),
        in_specs=[qblkT(B_WIDTH), qblkT(IDX_HEADS * IDX_DIM), qblkT(IDX_HEADS),
                  pl.BlockSpec((seq, LANES), lambda g: (cur(g) // steps, 0)),
                  pl.BlockSpec((seq, LANES), lambda g: (cur(g) // steps, 0)),
                  pl.BlockSpec((HEAD_DIM, seq), lambda g: (0, cur(g) // steps)),
                  qblk(A_WIDTH), qblk(B_WIDTH), qblk(D_MODEL),
                  const((1, B_WIDTH)), const((D_MODEL, D_MODEL))],
        out_specs=qblk(D_MODEL),
        out_shape=jax.ShapeDtypeStruct((tokens, D_MODEL), F32),
        scratch_shapes=[pltpu.VMEM((PAIR, seq, Q_BLOCK), F32),
                        pltpu.VMEM((PAIR, seq, 2 * LANES), BF16),
                        pltpu.VMEM((2, seq, GROUP * Q_BLOCK), F32),
                        pltpu.VMEM((UNITS, seq, GROUP * Q_BLOCK), BF16),
                        pltpu.VMEM((8, Q_BLOCK), F32),
                        pltpu.VMEM((2, seq // DOT_ROWS, VALUE_ROWS, DOT_ROWS), BF16),
                        pltpu.VMEM((UNITS, seq // DOT_ROWS, VALUE_ROWS, GROUP * Q_BLOCK), F32),
                        pltpu.SMEM((1,), jnp.int32)],
        compiler_params=pltpu.CompilerParams(dimension_semantics=("arbitrary",),
                                             vmem_limit_bytes=VMEM_LIMIT),
        name="dsa_attn_out",
    )(qt, qit, wit, kaug, kk, vt, ma, gate, x2, gb, wo)
    return y.reshape(bsz, seq, D_MODEL)
```

```python
import functools

import jax
import jax.numpy as jnp
from jax import lax
from jax.experimental import pallas as pl
from jax.experimental.pallas import tpu as pltpu

F32 = jnp.float32
BF16 = jnp.bfloat16

D_MODEL = 1024
CHUNK = 64
A_WIDTH = 512
A_GROUPS = 4
A_BLOCK = 128
HEADS = 8
HEAD_DIM = 64
B_WIDTH = HEADS * HEAD_DIM
IDX_HEADS = 8
IDX_DIM = 64
TOPK_MAX = 256
Q_BLOCK = 128
PAIR = 2
NORM_EPS = 1e-6
MASK_OFF = 1e32
LANES = 128
HALF = LANES // 2
PROJ_ROWS = 512
DOT_ROWS = 512
RED_ROWS = 128
KEY_BUCKET = 512
FAST_TRIPS = 24
INTERP_MARGIN = 0.02
SLOW_TRIPS = 70
CHUNKS_PER_TRIP = 2
VMEM_LIMIT = 48 * 1024 * 1024

_OFF_U, _OFF_V, _OFF_Z, _OFF_G, _OFF_K = 0, 512, 1024, 1536, 2048
_PACKED_COLS = 2304
_ROW_Q, _ROW_QI, _PACKED_ROWS = 0, 512, 1024

_NT = (((1,), (1,)), ((), ()))


def _gelu(x):
    c = 0.7978845608028654
    return 0.5 * x * (1.0 + jnp.tanh(c * (x + 0.044715 * (x * x * x))))


def _silu(x):
    return x / (1.0 + jnp.exp(-x))


def _row_blocks(total, size):
    return [slice(r, min(r + size, total)) for r in range(0, total, size)]


def _half_mean_sq(x2, lo_half):
    tot = jnp.sum(x2, axis=-1, keepdims=True)
    lo = jnp.sum(jnp.where(lo_half, x2, 0.0), axis=-1, keepdims=True)
    return jnp.where(lo_half, lo, tot - lo) * (1.0 / HALF)


def _proj_kernel(x_ref, ng_ref, w_ref, wt_ref, sgn_ref, sw_ref, sb_ref, qg_ref, kg_ref, ga_ref,
                 ma_ref, gate_ref, kaug_ref, kk_ref, qt_ref, qit_ref, vt_ref, wit_ref,
                 *, tiles_per_seq, idx_w_scale):
    tm = x_ref.shape[0]
    i = pl.program_id(0)
    x = x_ref[...]
    ms = jnp.mean(x * x, axis=-1, keepdims=True)
    h = (x * lax.rsqrt(ms + NORM_EPS) * ng_ref[...]).astype(BF16)

    lane = lax.broadcasted_iota(jnp.int32, (tm, LANES), 1)
    lo_half = lane < HALF

    def proj(off, width):
        return jnp.dot(h, w_ref[:, off:off + width], preferred_element_type=F32)

    gu = _gelu(proj(_OFF_U, A_WIDTH))
    gv = _gelu(proj(_OFF_V, A_WIDTH))
    pz = proj(_OFF_Z, A_WIDTH)
    r_i = lax.broadcasted_iota(jnp.int32, (A_BLOCK, A_BLOCK), 0)
    c_j = lax.broadcasted_iota(jnp.int32, (A_BLOCK, A_BLOCK), 1)
    causal = lax.shift_right_logical(c_j, 6) <= lax.shift_right_logical(r_i, 6)
    for g in range(A_GROUPS):
        cols = slice(g * LANES, (g + 1) * LANES)
        vg = gv[:, cols]
        mu = jnp.mean(vg, axis=-1, keepdims=True)
        d = vg - mu
        var = jnp.mean(d * d, axis=-1, keepdims=True)
        vn = (d * lax.rsqrt(var + NORM_EPS) * sgn_ref[:, cols]).astype(BF16)
        wg = jnp.where(causal, sw_ref[g], 0.0).astype(BF16)
        side = jnp.concatenate([vn[blk * A_BLOCK:(blk + 1) * A_BLOCK, :] for blk in range(tm // A_BLOCK)], axis=1)
        mixed = jnp.dot(wg, side, preferred_element_type=F32) + sb_ref[g]
        for blk in range(tm // A_BLOCK):
            rows = slice(blk * A_BLOCK, (blk + 1) * A_BLOCK)
            s = mixed[:, blk * LANES:(blk + 1) * LANES]
            ya = gu[rows, cols] * s
            oa = ya * lax.rsqrt(jnp.mean(ya * ya, axis=-1, keepdims=True) + NORM_EPS) * ga_ref[:, cols]
            ma_ref[rows, cols] = (oa * _silu(pz[rows, cols])).astype(BF16)

    gate_ref[...] = _silu(proj(_OFF_G, B_WIDTH)).astype(BF16)

    pkv = proj(_OFF_K, 2 * LANES)
    pk = pkv[:, 0:LANES]
    k_ms = jnp.sum(jnp.where(lo_half, pk * pk, 0.0), axis=-1, keepdims=True) * (1.0 / HALF)
    kn = pk * lax.rsqrt(k_ms + NORM_EPS)
    ik_mu = jnp.sum(jnp.where(lo_half, 0.0, pk), axis=-1, keepdims=True) * (1.0 / HALF)
    dk = pk - ik_mu
    ik_var = jnp.sum(jnp.where(lo_half, 0.0, dk * dk), axis=-1, keepdims=True) * (1.0 / HALF)
    kin = dk * lax.rsqrt(ik_var + NORM_EPS)
    tile = jnp.where(lo_half, kn, kin) * kg_ref[...]
    swapped = pltpu.roll(tile, HALF, axis=1)
    row = lax.broadcasted_iota(jnp.int32, (tm, LANES), 0)
    pos = (i % tiles_per_seq) * tm + row
    pos_hi = lax.shift_right_logical(pos, 6).astype(F32)
    pos_lo = (pos & (CHUNK - 1)).astype(F32)
    posfeat = jnp.where(lane == HALF, pos_hi, jnp.where(lane == HALF + 1, pos_lo, 0.0))
    kaug_ref[...] = jnp.where(lo_half, tile, posfeat).astype(BF16)
    kk_ref[...] = jnp.where(lo_half, swapped, tile).astype(BF16)

    pt = lax.dot_general(wt_ref[...], h, _NT, preferred_element_type=F32)
    for hh in range(HEADS):
        rows = slice(_ROW_Q + hh * HEAD_DIM, _ROW_Q + (hh + 1) * HEAD_DIM)
        xq = pt[rows, :]
        msq = jnp.mean(xq * xq, axis=0, keepdims=True)
        qt_ref[rows, :] = (xq * lax.rsqrt(msq + NORM_EPS) * qg_ref[...] * (HEAD_DIM ** -0.5)).astype(BF16)
    qit_ref[...] = pt[_ROW_QI:_ROW_QI + IDX_HEADS * IDX_DIM, :].astype(BF16)
    for blk in range(tm // LANES):
        cols = slice(blk * LANES, (blk + 1) * LANES)
        vw_t = pkv[cols, LANES:2 * LANES].T
        vt_ref[:, cols] = vw_t[0:HEAD_DIM, :].astype(BF16)
        wit_ref[:, cols] = vw_t[HEAD_DIM:HEAD_DIM + IDX_HEADS, :] * idx_w_scale


def _tile_iotas():
    r_k = lax.broadcasted_iota(jnp.int32, (Q_BLOCK, Q_BLOCK), 0)
    c_q = lax.broadcasted_iota(jnp.int32, (Q_BLOCK, Q_BLOCK), 1)
    return r_k, c_q


def _colsum(x):
    return jnp.sum(x, axis=0, keepdims=True)


def _any_lane(pred):
    return jnp.max(jnp.where(pred, 1.0, 0.0)) > 0.0


def _index_scores(nk, n_adm, qit_ref, qcols, wi, kk_ref, s_ref, idx_ref):
    r_k, c_q = _tile_iotas()
    top_rows = r_k < HALF
    per_head = []
    for t in range(IDX_HEADS // 2):
        qit = qit_ref[t * LANES:(t + 1) * LANES, qcols]
        zero = jnp.zeros_like(qit)
        per_head += [jnp.where(top_rows, qit, zero), jnp.where(top_rows, zero, qit)]
    half_heads = IDX_HEADS // 2
    for g in range(2):
        wg = jnp.concatenate(per_head[g * half_heads:(g + 1) * half_heads], axis=1)
        for rows in _row_blocks(nk, DOT_ROWS):
            s_ref[g, rows, :] = jnp.dot(kk_ref[rows, :], wg, preferred_element_type=F32)

    def weighted_relu(g, rows):
        acc = None
        for i in range(half_heads):
            hh = g * half_heads + i
            term = jnp.maximum(s_ref[g, rows, i * Q_BLOCK:(i + 1) * Q_BLOCK], 0.0) * wi[hh:hh + 1, :]
            acc = term if acc is None else acc + term
        return acc

    slabs = nk // RED_ROWS
    mn_acc = jnp.full((RED_ROWS, Q_BLOCK), jnp.inf, F32)
    mx_acc = jnp.full((RED_ROWS, Q_BLOCK), -jnp.inf, F32)
    tiny_acc = jnp.full((RED_ROWS, Q_BLOCK), jnp.inf, F32)
    for r in range(slabs):
        rows = slice(r * RED_ROWS, (r + 1) * RED_ROWS)
        acc = weighted_relu(0, rows) + weighted_relu(1, rows)
        if r >= slabs - KEY_BUCKET // RED_ROWS:
            adm = (r * RED_ROWS + r_k) < n_adm
            lo_fill = jnp.where(adm, acc, -jnp.inf)
            hi_fill = jnp.where(adm, acc, jnp.inf)
        else:
            lo_fill = hi_fill = acc
        mag = jnp.abs(hi_fill)
        idx_ref[rows, :] = lo_fill
        mn_acc = jnp.minimum(mn_acc, hi_fill)
        mx_acc = jnp.maximum(mx_acc, lo_fill)
        tiny_acc = jnp.minimum(tiny_acc, jnp.where(mag == 0.0, jnp.inf, mag))
    lo = jnp.min(mn_acc, axis=0, keepdims=True)
    hi = jnp.max(mx_acc, axis=0, keepdims=True)
    tiny = jnp.min(tiny_acc, axis=0, keepdims=True)

    unit = jnp.where(tiny < jnp.inf, tiny, 1.0)
    eps = unit * (0.5 / nk)
    rank0 = (1 + r_k).astype(F32)

    def spread(r, carry):
        rows = pl.ds(pl.multiple_of(r * RED_ROWS, RED_ROWS), RED_ROWS)
        s = idx_ref[rows, :]
        rank = rank0 + jnp.asarray(r * RED_ROWS, F32)
        idx_ref[rows, :] = jnp.where(s == 0.0, -(rank * eps), s)
        return carry

    lax.fori_loop(0, slabs, spread, 0)
    return jnp.minimum(lo, -0.5 * unit), hi


def _probe(nk, kf, idx_ref, state):
    lo, hi, c_lo, c_hi = state
    frac = (c_lo - kf) / jnp.maximum(c_lo - c_hi, 1.0)
    frac = jnp.minimum(jnp.maximum(frac, INTERP_MARGIN), 1.0 - INTERP_MARGIN)
    t = lo + (hi - lo) * frac
    acc = jnp.zeros((RED_ROWS, Q_BLOCK), F32)
    for r in range(nk // RED_ROWS):
        acc = acc + jnp.where(idx_ref[r * RED_ROWS:(r + 1) * RED_ROWS, :] >= t, 1.0, 0.0)
    c = _colsum(acc)
    ge = c >= kf
    return (jnp.where(ge, t, lo), jnp.where(ge, hi, t), jnp.where(ge, c, c_lo), jnp.where(ge, c_hi, c))


def _write_selection(nk, j_blk, lo, idx_ref, rhs_ref):
    r_k, c_q = _tile_iotas()
    later = 2.0 * jnp.maximum(r_k - c_q, 0).astype(F32)

    def write(r, carry):
        rows = pl.ds(pl.multiple_of(r * RED_ROWS, RED_ROWS), RED_ROWS)
        on = jnp.where(r == j_blk, -later, 0.0)
        rhs_ref[rows, LANES:2 * LANES] = jnp.where(idx_ref[rows, :] >= lo, on, -MASK_OFF).astype(BF16)
        return carry

    lax.fori_loop(0, nk // RED_ROWS, write, 0)


def _exact_fallback(slabs, kf, lo, hi, idx_ref, rhs_ref):
    r_k, c_q = _tile_iotas()
    later = 2.0 * jnp.maximum(r_k - c_q, 0).astype(F32)

    def slab(r):
        return idx_ref[pl.ds(pl.multiple_of(r * RED_ROWS, RED_ROWS), RED_ROWS), :]

    def count(pred):
        def body(r, acc):
            return acc + jnp.where(pred(slab(r), r), 1.0, 0.0)
        return _colsum(lax.fori_loop(0, slabs, body, jnp.zeros((RED_ROWS, Q_BLOCK), F32)))

    def kth(lo):
        def body(r, acc):
            s = slab(r)
            return jnp.minimum(acc, jnp.where(s >= lo, s, jnp.inf))
        acc = lax.fori_loop(0, slabs, body, jnp.full((RED_ROWS, Q_BLOCK), jnp.inf, F32))
        thr = jnp.min(acc, axis=0, keepdims=True)
        return thr, count(lambda s, r: s > thr)

    def slow_cond(carry):
        it, _, _, _, c_gt = carry
        return jnp.logical_and(it < SLOW_TRIPS, _any_lane(c_gt >= kf))

    def slow_body(carry):
        it, lo, hi, _, _ = carry
        for _ in range(4):
            mid = 0.5 * lo + 0.5 * hi
            ge = count(lambda s, r: s >= mid) >= kf
            lo, hi = jnp.where(ge, mid, lo), jnp.where(ge, hi, mid)
        thr, c_gt = kth(lo)
        return it + 1, lo, hi, thr, c_gt

    thr0, c_gt0 = kth(lo)
    _, _, _, thr, c_gt = lax.while_loop(slow_cond, slow_body, (jnp.int32(0), lo, hi, thr0, c_gt0))
    c_eq = count(lambda s, r: s == thr)
    need = (c_gt + c_eq) > kf

    def tie_step(_, carry):
        lo_i, hi_i = carry
        mid_i = jnp.floor((lo_i + hi_i) * 0.5)
        below = count(lambda s, r: jnp.logical_and(s == thr, (r * RED_ROWS + r_k).astype(F32) <= mid_i))
        ok = (c_gt + below) >= kf
        return jnp.where(ok, lo_i, mid_i), jnp.where(ok, mid_i, hi_i)

    last = jnp.asarray(slabs * RED_ROWS - 1, F32)
    lo_i = jnp.full((1, Q_BLOCK), -1.0, F32)
    hi_i = jnp.zeros((1, Q_BLOCK), F32) + last
    _, cut = lax.fori_loop(0, 13, tie_step, (lo_i, hi_i))
    cut = jnp.where(need, cut, last + 1.0).astype(jnp.int32)

    def write(r, carry):
        s = slab(r)
        krow = r * RED_ROWS + r_k
        sel = jnp.logical_or(s > thr, jnp.logical_and(s == thr, krow <= cut))
        on = jnp.where(r == slabs - 1, -later, 0.0)
        rhs_ref[pl.ds(pl.multiple_of(r * RED_ROWS, RED_ROWS), RED_ROWS), LANES:2 * LANES] = (
            jnp.where(sel, on, -MASK_OFF).astype(BF16))
        return carry

    lax.fori_loop(0, slabs, write, 0)


GROUP = HEADS // 2
UNITS = PAIR * (HEADS // GROUP)
VALUE_ROWS = HEAD_DIM + 16


def _probabilities(nk, qt_ref, qcols, rhs_ref, s_ref, p_ref, between):
    r_k, c_q = _tile_iotas()
    feat = lax.broadcasted_iota(jnp.int32, (HALF, Q_BLOCK), 0)
    group = GROUP
    width = group * Q_BLOCK
    units = [(u, g) for u in range(len(qcols)) for g in range(HEADS // group)]

    def score_operand(u, hh):
        slope = 2.0 ** (-(hh + 1))
        alibi = jnp.where(feat == 0, CHUNK * slope, jnp.where(feat == 1, slope, 0.0)).astype(BF16)
        scaled_ident = jnp.where(r_k == c_q, slope, 0.0).astype(BF16)
        return jnp.concatenate([qt_ref[hh * HEAD_DIM:(hh + 1) * HEAD_DIM, qcols[u]], alibi, scaled_ident], axis=0)

    def scores(n):
        u, g = units[n]
        lhs_t = jnp.concatenate([score_operand(u, g * group + i) for i in range(group)], axis=1)
        m_acc = jnp.full((RED_ROWS, width), -jnp.inf, F32)
        for rows in _row_blocks(nk, DOT_ROWS):
            blk = jnp.dot(rhs_ref[u, rows, :], lhs_t, preferred_element_type=F32)
            s_ref[n % 2, rows, :] = blk
            for sub in range((rows.stop - rows.start) // RED_ROWS):
                m_acc = jnp.maximum(m_acc, blk[sub * RED_ROWS:(sub + 1) * RED_ROWS, :])
        return jnp.max(m_acc, axis=0, keepdims=True)

    def probabilities(n, m):
        for rows in _row_blocks(nk, RED_ROWS):
            p_ref[n, rows, :] = jnp.exp((s_ref[n % 2, rows, :] - m).astype(BF16))

    m = scores(0)
    between()
    for n in range(len(units)):
        m_next = scores(n + 1) if n + 1 < len(units) else None
        probabilities(n, m)
        m = m_next


def _value_chunk(ch, vtb_ref, p_ref, part_ref):
    if isinstance(ch, int):
        n, blk = ch % UNITS, ch // UNITS
        rows = slice(blk * DOT_ROWS, (blk + 1) * DOT_ROWS)
    else:
        n = ch & (UNITS - 1)
        blk = lax.shift_right_logical(ch, UNITS.bit_length() - 1)
        rows = pl.ds(pl.multiple_of(blk * DOT_ROWS, DOT_ROWS), DOT_ROWS)
    part_ref[n, blk] = jnp.dot(vtb_ref[blk], p_ref[n, rows, :], preferred_element_type=F32)


def _head_tiles(part_ref):
    per_block = []
    for u in range(PAIR):
        outs = []
        for g in range(HEADS // GROUP):
            n = u * (HEADS // GROUP) + g
            o = part_ref[n, 0]
            for blk in range(1, part_ref.shape[1]):
                o = o + part_ref[n, blk]
            outs.append(o[0:HEAD_DIM, :] * (1.0 / o[HEAD_DIM:HEAD_DIM + 1, :]))
        tiles = []
        for t in range(HEADS // 2):
            o_g = outs[(2 * t) // GROUP]
            a = (2 * t) % GROUP
            tiles.append(jnp.concatenate([o_g[:, a * Q_BLOCK:(a + 1) * Q_BLOCK],
                                          o_g[:, (a + 1) * Q_BLOCK:(a + 2) * Q_BLOCK]], axis=0).T)
        per_block.append(tiles)
    return per_block


def _attn_kernel(qt_ref, qit_ref, wit_ref, kaug_ref, kk_ref, vt_ref, ma_ref, gate_ref, x_ref, gb_ref,
                 wo_ref, y_ref, idx_ref, rhs_ref, s_ref, p_ref, st_ref, vtb_ref, part_ref, done_ref,
                 *, seq, n_pairs, topk):
    g = pl.program_id(0)
    kf = float(topk)
    steps = seq // (PAIR * Q_BLOCK)
    key_blocks = seq // DOT_ROWS
    qcols = [slice(u * Q_BLOCK, (u + 1) * Q_BLOCK) for u in range(PAIR)]
    searching = g < n_pairs
    pending = g >= 1
    jj = lax.rem(g, steps)
    g_prev = jnp.maximum(g - 1, 0)
    slot = lax.div(g, steps) & 1
    slot_prev = lax.div(g_prev, steps) & 1

    @pl.when(g == 0)
    def _():
        p_ref[...] = jnp.zeros(p_ref.shape, BF16)
        vtb_ref[...] = jnp.zeros(vtb_ref.shape, BF16)
        y_ref[...] = jnp.zeros(y_ref.shape, F32)

    @pl.when(jnp.logical_and(searching, jj == 0))
    def _():
        for u in range(PAIR):
            rhs_ref[u, :, 0:LANES] = kaug_ref[...]
        ones = jnp.ones((VALUE_ROWS - HEAD_DIM, DOT_ROWS), BF16)
        for blk in range(key_blocks):
            vtb_ref[slot, blk] = jnp.concatenate([vt_ref[:, blk * DOT_ROWS:(blk + 1) * DOT_ROWS], ones], axis=0)

    part_ref[...] = jnp.zeros(part_ref.shape, F32)
    done_ref[0] = 0

    steps_per_bucket = KEY_BUCKET // (PAIR * Q_BLOCK)

    def buckets(step, active):
        return [((n + 1) * KEY_BUCKET,
                 jnp.logical_and(active, jnp.logical_and(step >= n * steps_per_bucket,
                                                         step < (n + 1) * steps_per_bucket)))
                for n in range(seq // KEY_BUCKET)]

    blocks = [PAIR * jj + u for u in range(PAIR)]
    total = UNITS * (lax.div(lax.rem(g_prev, steps), steps_per_bucket) + 1) * (KEY_BUCKET // DOT_ROWS)
    vtb_prev = vtb_ref.at[slot_prev]

    for step in range(steps):
        @pl.when(jnp.logical_and(searching, jj == step))
        def _(nk=(step + 1) * PAIR * Q_BLOCK):
            lane_q = lax.broadcasted_iota(jnp.int32, (1, Q_BLOCK), 1)
            states = []
            for u in range(PAIR):
                n_adm = blocks[u] * Q_BLOCK + CHUNK + CHUNK * (lane_q >= CHUNK).astype(jnp.int32)
                lo, hi = _index_scores(nk, n_adm, qit_ref, qcols[u], wit_ref[:, qcols[u]], kk_ref, s_ref,
                                       idx_ref.at[u])
                n_adm_f = n_adm.astype(F32)
                c_lo = jnp.where(n_adm_f <= kf, kf, n_adm_f)
                states.append((lo, hi, c_lo, jnp.zeros((1, Q_BLOCK), F32)))

            def cond(carry):
                it, flat = carry[0], carry[1:]
                still = jnp.logical_or(flat[2] != kf, flat[6] != kf)
                return jnp.logical_and(it < FAST_TRIPS, _any_lane(still))

            def body(carry):
                it, flat = carry[0], carry[1:]
                st = [flat[0:4], flat[4:8]]
                for _ in range(2):
                    st = [_probe(nk, kf, idx_ref.at[u], st[u]) for u in range(PAIR)]
                for c in range(CHUNKS_PER_TRIP):
                    _value_chunk(jnp.minimum(it * CHUNKS_PER_TRIP + c, total - 1), vtb_prev, p_ref, part_ref)
                return (it + 1,) + tuple(st[0]) + tuple(st[1])

            final = lax.while_loop(cond, body, (jnp.int32(0),) + tuple(states[0]) + tuple(states[1]))
            done_ref[0] = jnp.minimum(final[0] * CHUNKS_PER_TRIP, total)
            for u in range(PAIR):
                lo, hi, c_lo, _ = final[1 + 4 * u:5 + 4 * u]
                _write_selection(nk, blocks[u], lo, idx_ref.at[u], rhs_ref.at[u])
                if nk % KEY_BUCKET:
                    rhs_ref[u, nk:nk + KEY_BUCKET - nk % KEY_BUCKET, LANES:2 * LANES] = jnp.full(
                        (KEY_BUCKET - nk % KEY_BUCKET, LANES), -MASK_OFF, BF16)
                st_ref[4 * u + 0:4 * u + 1, :] = lo
                st_ref[4 * u + 1:4 * u + 2, :] = hi
                st_ref[4 * u + 2:4 * u + 3, :] = c_lo

    @pl.when(jnp.logical_and(pending, searching))
    def _():
        def chunk(ch, carry):
            _value_chunk(ch, vtb_prev, p_ref, part_ref)
            return carry

        lax.fori_loop(done_ref[0], total, chunk, 0)

    @pl.when(g == n_pairs)
    def _():
        for ch in range(UNITS * key_blocks):
            _value_chunk(ch, vtb_prev, p_ref, part_ref)

    lane = lax.broadcasted_iota(jnp.int32, (PAIR * Q_BLOCK, LANES), 1)
    lo_half = lane < HALF

    def finish_pending():
        per_block = _head_tiles(part_ref)
        mixed = [ma_ref[...]]
        for t in range(HEADS // 2):
            cols = slice(t * LANES, (t + 1) * LANES)
            o = jnp.concatenate([per_block[u][t] for u in range(PAIR)], axis=0)
            msq = _half_mean_sq(o * o, lo_half)
            ob = o * lax.rsqrt(msq + NORM_EPS) * gb_ref[:, cols] * gate_ref[:, cols].astype(F32)
            mixed.append(ob.astype(BF16))
        y_ref[...] = x_ref[...] + jnp.dot(jnp.concatenate(mixed, axis=1), wo_ref[...],
                                          preferred_element_type=F32)

    for u in range(PAIR):
        lo = st_ref[4 * u + 0:4 * u + 1, :]
        hi = st_ref[4 * u + 1:4 * u + 2, :]
        c_lo = st_ref[4 * u + 2:4 * u + 3, :]

        @pl.when(jnp.logical_and(searching, _any_lane(c_lo != kf)))
        def _(u=u, lo=lo, hi=hi):
            _exact_fallback(blocks[u] + 1, kf, lo, hi, idx_ref.at[u], rhs_ref.at[u])

    for nk, here in buckets(jj, searching):
        @pl.when(here)
        def _(nk=nk):
            _probabilities(nk, qt_ref, qcols, rhs_ref, s_ref, p_ref, finish_pending)

    @pl.when(g == n_pairs)
    def _():
        finish_pending()


def kernel(x, norm_gain, w_in, sgu_norm_gain, sgu_w, sgu_b, q_norm_gain, k_norm_gain,
           idx_k_norm_gain, branch_norm_gain, w_out):
    bsz, seq, d_model = x.shape
    assert d_model == D_MODEL and norm_gain.shape[0] == 1
    assert seq % PROJ_ROWS == 0 and seq % KEY_BUCKET == 0 and KEY_BUCKET % (PAIR * Q_BLOCK) == 0
    tokens = bsz * seq
    topk = min(TOPK_MAX, seq // 4)
    idx_w_scale = (IDX_HEADS ** -0.5) * (IDX_DIM ** -0.5)

    w = w_in[0]
    a3 = 3 * A_WIDTH
    w_q = w[:, a3:a3 + B_WIDTH]
    w_k = w[:, a3 + B_WIDTH:a3 + B_WIDTH + HEAD_DIM]
    w_v = w[:, a3 + B_WIDTH + HEAD_DIM:a3 + B_WIDTH + 2 * HEAD_DIM]
    o_g = a3 + B_WIDTH + 2 * HEAD_DIM
    w_g = w[:, o_g:o_g + B_WIDTH]
    o_i = o_g + B_WIDTH
    w_iq = w[:, o_i:o_i + IDX_HEADS * IDX_DIM]
    w_ik = w[:, o_i + IDX_HEADS * IDX_DIM:o_i + IDX_HEADS * IDX_DIM + IDX_DIM]
    w_iw = w[:, o_i + IDX_HEADS * IDX_DIM + IDX_DIM:]
    w_main = jnp.concatenate([w[:, :a3], w_g, w_k, w_ik, w_v, w_iw,
                              jnp.zeros((D_MODEL, LANES - HEAD_DIM - IDX_HEADS), F32)], axis=1).astype(BF16)
    assert w_main.shape[1] == _PACKED_COLS
    w_t = jnp.concatenate([w_q.T, w_iq.T], axis=0).astype(BF16)
    assert w_t.shape[0] == _PACKED_ROWS
    x2 = x.reshape(tokens, D_MODEL)
    ng = norm_gain[0].reshape(1, D_MODEL)
    sgn = sgu_norm_gain[0].reshape(1, A_WIDTH)
    sw = sgu_w[0]
    sb = sgu_b[0].reshape(A_GROUPS, A_BLOCK, 1)
    qg = q_norm_gain[0].reshape(HEAD_DIM, 1)
    kg = jnp.concatenate([k_norm_gain[0], idx_k_norm_gain[0]]).reshape(1, LANES)
    ga = branch_norm_gain[0, :A_WIDTH].reshape(1, A_WIDTH)
    gb = branch_norm_gain[0, A_WIDTH:].reshape(1, B_WIDTH)
    wo = w_out[0].astype(BF16)

    tm = PROJ_ROWS
    full = lambda shape: pl.BlockSpec(shape, lambda i: (0,) * len(shape))
    rows = lambda width: pl.BlockSpec((tm, width), lambda i: (i, 0))
    colsT = lambda height: pl.BlockSpec((height, tm), lambda i: (0, i))
    outs = pl.pallas_call(
        functools.partial(_proj_kernel, tiles_per_seq=seq // tm, idx_w_scale=idx_w_scale),
        grid=(tokens // tm,),
        in_specs=[rows(D_MODEL), full((1, D_MODEL)), full((D_MODEL, _PACKED_COLS)),
                  full((_PACKED_ROWS, D_MODEL)), full((1, A_WIDTH)), full((A_GROUPS, A_BLOCK, A_BLOCK)),
                  full((A_GROUPS, A_BLOCK, 1)), full((HEAD_DIM, 1)), full((1, LANES)),
                  full((1, A_WIDTH))],
        out_specs=[rows(A_WIDTH), rows(B_WIDTH), rows(LANES), rows(LANES),
                   colsT(B_WIDTH), colsT(IDX_HEADS * IDX_DIM), colsT(HEAD_DIM), colsT(IDX_HEADS)],
        out_shape=[jax.ShapeDtypeStruct((tokens, A_WIDTH), BF16),
                   jax.ShapeDtypeStruct((tokens, B_WIDTH), BF16),
                   jax.ShapeDtypeStruct((tokens, LANES), BF16),
                   jax.ShapeDtypeStruct((tokens, LANES), BF16),
                   jax.ShapeDtypeStruct((B_WIDTH, tokens), BF16),
                   jax.ShapeDtypeStruct((IDX_HEADS * IDX_DIM, tokens), BF16),
                   jax.ShapeDtypeStruct((HEAD_DIM, tokens), BF16),
                   jax.ShapeDtypeStruct((IDX_HEADS, tokens), F32)],
        compiler_params=pltpu.CompilerParams(dimension_semantics=("arbitrary",),
                                             vmem_limit_bytes=VMEM_LIMIT),
        name="proj_sgu",
    )(x2, ng, w_main, w_t, sgn, sw, sb, qg, kg, ga)
    ma, gate, kaug, kk, qt, qit, vt, wit = outs

    qrows = PAIR * Q_BLOCK
    steps = seq // qrows
    n_pairs = bsz * steps
    cur = lambda g: jnp.minimum(g, n_pairs - 1)
    qblk = lambda width: pl.BlockSpec((qrows, width), lambda g: (jnp.maximum(g - 1, 0), 0))
    qblkT = lambda height: pl.BlockSpec((height, qrows), lambda g: (0, cur(g)))
    const = lambda shape: pl.BlockSpec(shape, lambda g: (0,) * len(shape))
    y = pl.pallas_call(
        functools.partial(_attn_kernel, seq=seq, n_pairs=n_pairs, topk=topk),
        grid=(n_pairs + 1,),
        in_specs=[qblkT(B_WIDTH), qblkT(IDX_HEADS * IDX_DIM), qblkT(IDX_HEADS),
                  pl.BlockSpec((seq, LANES), lambda g: (cur(g) // steps, 0)),
                  pl.BlockSpec((seq, LANES), lambda g: (cur(g) // steps, 0)),
                  pl.BlockSpec((HEAD_DIM, seq), lambda g: (0, cur(g) // steps)),
                  qblk(A_WIDTH), qblk(B_WIDTH), qblk(D_MODEL),
                  const((1, B_WIDTH)), const((D_MODEL, D_MODEL))],
        out_specs=qblk(D_MODEL),
        out_shape=jax.ShapeDtypeStruct((tokens, D_MODEL), F32),
        scratch_shapes=[pltpu.VMEM((PAIR, seq, Q_BLOCK), F32),
                        pltpu.VMEM((PAIR, seq, 2 * LANES), BF16),
                        pltpu.VMEM((2, seq, GROUP * Q_BLOCK), F32),
                        pltpu.VMEM((UNITS, seq, GROUP * Q_BLOCK), BF16),
                        pltpu.VMEM((8, Q_BLOCK), F32),
                        pltpu.VMEM((2, seq // DOT_ROWS, VALUE_ROWS, DOT_ROWS), BF16),
                        pltpu.VMEM((UNITS, seq // DOT_ROWS, VALUE_ROWS, GROUP * Q_BLOCK), F32),
                        pltpu.SMEM((1,), jnp.int32)],
        compiler_params=pltpu.CompilerParams(dimension_semantics=("arbitrary",),
                                             vmem_limit_bytes=VMEM_LIMIT),
        name="dsa_attn_out",
    )(qt, qit, wit, kaug, kk, vt, ma, gate, x2, gb, wo)
    return y.reshape(bsz, seq, D_MODEL)
```

```python
import functools

import jax
import jax.numpy as jnp
from jax import lax
from jax.experimental import pallas as pl
from jax.experimental.pallas import tpu as pltpu

F32 = jnp.float32
BF16 = jnp.bfloat16

D_MODEL = 1024
CHUNK = 64
A_WIDTH = 512
A_GROUPS = 4
A_BLOCK = 128
HEADS = 8
HEAD_DIM = 64
B_WIDTH = HEADS * HEAD_DIM
IDX_HEADS = 8
IDX_DIM = 64
TOPK_MAX = 256
Q_BLOCK = 128
PAIR = 2
NORM_EPS = 1e-6
MASK_OFF = 1e32
LANES = 128
HALF = LANES // 2
PROJ_ROWS = 1024
DOT_ROWS = 512
RED_ROWS = 128
KEY_BUCKET = 512
FAST_TRIPS = 24
INTERP_MARGIN = 0.02
SLOW_TRIPS = 70
CHUNKS_PER_TRIP = 2
VMEM_LIMIT = 48 * 1024 * 1024

_OFF_U, _OFF_V, _OFF_Z, _OFF_G, _OFF_K = 0, 512, 1024, 1536, 2048
_PACKED_COLS = 2304
_ROW_Q, _ROW_QI, _PACKED_ROWS = 0, 512, 1024

_NT = (((1,), (1,)), ((), ()))


def _gelu(x):
    c = 0.7978845608028654
    return 0.5 * x * (1.0 + jnp.tanh(c * (x + 0.044715 * (x * x * x))))


def _silu(x):
    return x / (1.0 + jnp.exp(-x))


def _row_blocks(total, size):
    return [slice(r, min(r + size, total)) for r in range(0, total, size)]


def _half_mean_sq(x2, lo_half):
    tot = jnp.sum(x2, axis=-1, keepdims=True)
    lo = jnp.sum(jnp.where(lo_half, x2, 0.0), axis=-1, keepdims=True)
    return jnp.where(lo_half, lo, tot - lo) * (1.0 / HALF)


def _proj_kernel(x_ref, ng_ref, w_ref, wt_ref, sgn_ref, sw_ref, sb_ref, qg_ref, kg_ref, ga_ref,
                 ma_ref, gate_ref, kaug_ref, kk_ref, qt_ref, qit_ref, vt_ref, wit_ref,
                 *, tiles_per_seq, idx_w_scale):
    tm = x_ref.shape[0]
    i = pl.program_id(0)
    x = x_ref[...]
    ms = jnp.mean(x * x, axis=-1, keepdims=True)
    h = (x * lax.rsqrt(ms + NORM_EPS) * ng_ref[...]).astype(BF16)

    lane = lax.broadcasted_iota(jnp.int32, (tm, LANES), 1)
    lo_half = lane < HALF

    def proj(off, width):
        return jnp.dot(h, w_ref[:, off:off + width], preferred_element_type=F32)

    gu = _gelu(proj(_OFF_U, A_WIDTH))
    gv = _gelu(proj(_OFF_V, A_WIDTH))
    pz = proj(_OFF_Z, A_WIDTH)
    r_i = lax.broadcasted_iota(jnp.int32, (A_BLOCK, A_BLOCK), 0)
    c_j = lax.broadcasted_iota(jnp.int32, (A_BLOCK, A_BLOCK), 1)
    causal = lax.shift_right_logical(c_j, 6) <= lax.shift_right_logical(r_i, 6)
    for g in range(A_GROUPS):
        cols = slice(g * LANES, (g + 1) * LANES)
        vg = gv[:, cols]
        mu = jnp.mean(vg, axis=-1, keepdims=True)
        d = vg - mu
        var = jnp.mean(d * d, axis=-1, keepdims=True)
        vn = (d * lax.rsqrt(var + NORM_EPS) * sgn_ref[:, cols]).astype(BF16)
        wg = jnp.where(causal, sw_ref[g], 0.0).astype(BF16)
        side = jnp.concatenate([vn[blk * A_BLOCK:(blk + 1) * A_BLOCK, :] for blk in range(tm // A_BLOCK)], axis=1)
        mixed = jnp.dot(wg, side, preferred_element_type=F32) + sb_ref[g]
        for blk in range(tm // A_BLOCK):
            rows = slice(blk * A_BLOCK, (blk + 1) * A_BLOCK)
            s = mixed[:, blk * LANES:(blk + 1) * LANES]
            ya = gu[rows, cols] * s
            oa = ya * lax.rsqrt(jnp.mean(ya * ya, axis=-1, keepdims=True) + NORM_EPS) * ga_ref[:, cols]
            ma_ref[rows, cols] = (oa * _silu(pz[rows, cols])).astype(BF16)

    gate_ref[...] = _silu(proj(_OFF_G, B_WIDTH)).astype(BF16)

    pkv = proj(_OFF_K, 2 * LANES)
    pk = pkv[:, 0:LANES]
    k_ms = jnp.sum(jnp.where(lo_half, pk * pk, 0.0), axis=-1, keepdims=True) * (1.0 / HALF)
    kn = pk * lax.rsqrt(k_ms + NORM_EPS)
    ik_mu = jnp.sum(jnp.where(lo_half, 0.0, pk), axis=-1, keepdims=True) * (1.0 / HALF)
    dk = pk - ik_mu
    ik_var = jnp.sum(jnp.where(lo_half, 0.0, dk * dk), axis=-1, keepdims=True) * (1.0 / HALF)
    kin = dk * lax.rsqrt(ik_var + NORM_EPS)
    tile = jnp.where(lo_half, kn, kin) * kg_ref[...]
    swapped = pltpu.roll(tile, HALF, axis=1)
    row = lax.broadcasted_iota(jnp.int32, (tm, LANES), 0)
    pos = (i % tiles_per_seq) * tm + row
    pos_hi = lax.shift_right_logical(pos, 6).astype(F32)
    pos_lo = (pos & (CHUNK - 1)).astype(F32)
    posfeat = jnp.where(lane == HALF, pos_hi, jnp.where(lane == HALF + 1, pos_lo, 0.0))
    kaug_ref[...] = jnp.where(lo_half, tile, posfeat).astype(BF16)
    kk_ref[...] = jnp.where(lo_half, swapped, tile).astype(BF16)

    pt = lax.dot_general(wt_ref[...], h, _NT, preferred_element_type=F32)
    for hh in range(HEADS):
        rows = slice(_ROW_Q + hh * HEAD_DIM, _ROW_Q + (hh + 1) * HEAD_DIM)
        xq = pt[rows, :]
        msq = jnp.mean(xq * xq, axis=0, keepdims=True)
        qt_ref[rows, :] = (xq * lax.rsqrt(msq + NORM_EPS) * qg_ref[...] * (HEAD_DIM ** -0.5)).astype(BF16)
    qit_ref[...] = pt[_ROW_QI:_ROW_QI + IDX_HEADS * IDX_DIM, :].astype(BF16)
    for blk in range(tm // LANES):
        cols = slice(blk * LANES, (blk + 1) * LANES)
        vw_t = pkv[cols, LANES:2 * LANES].T
        vt_ref[:, cols] = vw_t[0:HEAD_DIM, :].astype(BF16)
        wit_ref[:, cols] = vw_t[HEAD_DIM:HEAD_DIM + IDX_HEADS, :] * idx_w_scale


def _tile_iotas():
    r_k = lax.broadcasted_iota(jnp.int32, (Q_BLOCK, Q_BLOCK), 0)
    c_q = lax.broadcasted_iota(jnp.int32, (Q_BLOCK, Q_BLOCK), 1)
    return r_k, c_q


def _colsum(x):
    return jnp.sum(x, axis=0, keepdims=True)


def _any_lane(pred):
    return jnp.max(jnp.where(pred, 1.0, 0.0)) > 0.0


def _index_scores(nk, n_adm, qit_ref, qcols, wi, kk_ref, s_ref, idx_ref):
    r_k, c_q = _tile_iotas()
    top_rows = r_k < HALF
    per_head = []
    for t in range(IDX_HEADS // 2):
        qit = qit_ref[t * LANES:(t + 1) * LANES, qcols]
        zero = jnp.zeros_like(qit)
        per_head += [jnp.where(top_rows, qit, zero), jnp.where(top_rows, zero, qit)]
    half_heads = IDX_HEADS // 2
    for g in range(2):
        wg = jnp.concatenate(per_head[g * half_heads:(g + 1) * half_heads], axis=1)
        for rows in _row_blocks(nk, DOT_ROWS):
            s_ref[g, rows, :] = jnp.dot(kk_ref[rows, :], wg, preferred_element_type=F32)

    def weighted_relu(g, rows):
        acc = None
        for i in range(half_heads):
            hh = g * half_heads + i
            term = jnp.maximum(s_ref[g, rows, i * Q_BLOCK:(i + 1) * Q_BLOCK], 0.0) * wi[hh:hh + 1, :]
            acc = term if acc is None else acc + term
        return acc

    slabs = nk // RED_ROWS
    mn_acc = jnp.full((RED_ROWS, Q_BLOCK), jnp.inf, F32)
    mx_acc = jnp.full((RED_ROWS, Q_BLOCK), -jnp.inf, F32)
    tiny_acc = jnp.full((RED_ROWS, Q_BLOCK), jnp.inf, F32)
    for r in range(slabs):
        rows = slice(r * RED_ROWS, (r + 1) * RED_ROWS)
        acc = weighted_relu(0, rows) + weighted_relu(1, rows)
        if r >= slabs - KEY_BUCKET // RED_ROWS:
            adm = (r * RED_ROWS + r_k) < n_adm
            lo_fill = jnp.where(adm, acc, -jnp.inf)
            hi_fill = jnp.where(adm, acc, jnp.inf)
        else:
            lo_fill = hi_fill = acc
        mag = jnp.abs(hi_fill)
        idx_ref[rows, :] = lo_fill
        mn_acc = jnp.minimum(mn_acc, hi_fill)
        mx_acc = jnp.maximum(mx_acc, lo_fill)
        tiny_acc = jnp.minimum(tiny_acc, jnp.where(mag == 0.0, jnp.inf, mag))
    lo = jnp.min(mn_acc, axis=0, keepdims=True)
    hi = jnp.max(mx_acc, axis=0, keepdims=True)
    tiny = jnp.min(tiny_acc, axis=0, keepdims=True)

    unit = jnp.where(tiny < jnp.inf, tiny, 1.0)
    eps = unit * (0.5 / nk)
    rank0 = (1 + r_k).astype(F32)

    def spread(r, carry):
        rows = pl.ds(pl.multiple_of(r * RED_ROWS, RED_ROWS), RED_ROWS)
        s = idx_ref[rows, :]
        rank = rank0 + jnp.asarray(r * RED_ROWS, F32)
        idx_ref[rows, :] = jnp.where(s == 0.0, -(rank * eps), s)
        return carry

    lax.fori_loop(0, slabs, spread, 0)
    return jnp.minimum(lo, -0.5 * unit), hi


def _probe(nk, kf, idx_ref, state):
    lo, hi, c_lo, c_hi = state
    frac = (c_lo - kf) / jnp.maximum(c_lo - c_hi, 1.0)
    frac = jnp.minimum(jnp.maximum(frac, INTERP_MARGIN), 1.0 - INTERP_MARGIN)
    t = lo + (hi - lo) * frac
    acc = jnp.zeros((RED_ROWS, Q_BLOCK), F32)
    for r in range(nk // RED_ROWS):
        acc = acc + jnp.where(idx_ref[r * RED_ROWS:(r + 1) * RED_ROWS, :] >= t, 1.0, 0.0)
    c = _colsum(acc)
    ge = c >= kf
    return (jnp.where(ge, t, lo), jnp.where(ge, hi, t), jnp.where(ge, c, c_lo), jnp.where(ge, c_hi, c))


def _write_selection(nk, j_blk, lo, idx_ref, rhs_ref):
    r_k, c_q = _tile_iotas()
    later = 2.0 * jnp.maximum(r_k - c_q, 0).astype(F32)

    def write(r, carry):
        rows = pl.ds(pl.multiple_of(r * RED_ROWS, RED_ROWS), RED_ROWS)
        on = jnp.where(r == j_blk, -later, 0.0)
        rhs_ref[rows, LANES:2 * LANES] = jnp.where(idx_ref[rows, :] >= lo, on, -MASK_OFF).astype(BF16)
        return carry

    lax.fori_loop(0, nk // RED_ROWS, write, 0)


def _exact_fallback(slabs, kf, lo, hi, idx_ref, rhs_ref):
    r_k, c_q = _tile_iotas()
    later = 2.0 * jnp.maximum(r_k - c_q, 0).astype(F32)

    def slab(r):
        return idx_ref[pl.ds(pl.multiple_of(r * RED_ROWS, RED_ROWS), RED_ROWS), :]

    def count(pred):
        def body(r, acc):
            return acc + jnp.where(pred(slab(r), r), 1.0, 0.0)
        return _colsum(lax.fori_loop(0, slabs, body, jnp.zeros((RED_ROWS, Q_BLOCK), F32)))

    def kth(lo):
        def body(r, acc):
            s = slab(r)
            return jnp.minimum(acc, jnp.where(s >= lo, s, jnp.inf))
        acc = lax.fori_loop(0, slabs, body, jnp.full((RED_ROWS, Q_BLOCK), jnp.inf, F32))
        thr = jnp.min(acc, axis=0, keepdims=True)
        return thr, count(lambda s, r: s > thr)

    def slow_cond(carry):
        it, _, _, _, c_gt = carry
        return jnp.logical_and(it < SLOW_TRIPS, _any_lane(c_gt >= kf))

    def slow_body(carry):
        it, lo, hi, _, _ = carry
        for _ in range(4):
            mid = 0.5 * lo + 0.5 * hi
            ge = count(lambda s, r: s >= mid) >= kf
            lo, hi = jnp.where(ge, mid, lo), jnp.where(ge, hi, mid)
        thr, c_gt = kth(lo)
        return it + 1, lo, hi, thr, c_gt

    thr0, c_gt0 = kth(lo)
    _, _, _, thr, c_gt = lax.while_loop(slow_cond, slow_body, (jnp.int32(0), lo, hi, thr0, c_gt0))
    c_eq = count(lambda s, r: s == thr)
    need = (c_gt + c_eq) > kf

    def tie_step(_, carry):
        lo_i, hi_i = carry
        mid_i = jnp.floor((lo_i + hi_i) * 0.5)
        below = count(lambda s, r: jnp.logical_and(s == thr, (r * RED_ROWS + r_k).astype(F32) <= mid_i))
        ok = (c_gt + below) >= kf
        return jnp.where(ok, lo_i, mid_i), jnp.where(ok, mid_i, hi_i)

    last = jnp.asarray(slabs * RED_ROWS - 1, F32)
    lo_i = jnp.full((1, Q_BLOCK), -1.0, F32)
    hi_i = jnp.zeros((1, Q_BLOCK), F32) + last
    _, cut = lax.fori_loop(0, 13, tie_step, (lo_i, hi_i))
    cut = jnp.where(need, cut, last + 1.0).astype(jnp.int32)

    def write(r, carry):
        s = slab(r)
        krow = r * RED_ROWS + r_k
        sel = jnp.logical_or(s > thr, jnp.logical_and(s == thr, krow <= cut))
        on = jnp.where(r == slabs - 1, -later, 0.0)
        rhs_ref[pl.ds(pl.multiple_of(r * RED_ROWS, RED_ROWS), RED_ROWS), LANES:2 * LANES] = (
            jnp.where(sel, on, -MASK_OFF).astype(BF16))
        return carry

    lax.fori_loop(0, slabs, write, 0)


GROUP = HEADS // 2
UNITS = PAIR * (HEADS // GROUP)
VALUE_ROWS = HEAD_DIM + 16


def _probabilities(nk, qt_ref, qcols, rhs_ref, s_ref, p_ref, between):
    r_k, c_q = _tile_iotas()
    feat = lax.broadcasted_iota(jnp.int32, (HALF, Q_BLOCK), 0)
    group = GROUP
    width = group * Q_BLOCK
    units = [(u, g) for u in range(len(qcols)) for g in range(HEADS // group)]

    def score_operand(u, hh):
        slope = 2.0 ** (-(hh + 1))
        alibi = jnp.where(feat == 0, CHUNK * slope, jnp.where(feat == 1, slope, 0.0)).astype(BF16)
        scaled_ident = jnp.where(r_k == c_q, slope, 0.0).astype(BF16)
        return jnp.concatenate([qt_ref[hh * HEAD_DIM:(hh + 1) * HEAD_DIM, qcols[u]], alibi, scaled_ident], axis=0)

    def scores(n):
        u, g = units[n]
        lhs_t = jnp.concatenate([score_operand(u, g * group + i) for i in range(group)], axis=1)
        m_acc = jnp.full((RED_ROWS, width), -jnp.inf, F32)
        for rows in _row_blocks(nk, DOT_ROWS):
            blk = jnp.dot(rhs_ref[u, rows, :], lhs_t, preferred_element_type=F32)
            s_ref[n % 2, rows, :] = blk
            for sub in range((rows.stop - rows.start) // RED_ROWS):
                m_acc = jnp.maximum(m_acc, blk[sub * RED_ROWS:(sub + 1) * RED_ROWS, :])
        return jnp.max(m_acc, axis=0, keepdims=True)

    def probabilities(n, m):
        for rows in _row_blocks(nk, RED_ROWS):
            p_ref[n, rows, :] = jnp.exp((s_ref[n % 2, rows, :] - m).astype(BF16))

    m = scores(0)
    between()
    for n in range(len(units)):
        m_next = scores(n + 1) if n + 1 < len(units) else None
        probabilities(n, m)
        m = m_next


def _value_chunk(ch, vtb_ref, p_ref, part_ref):
    if isinstance(ch, int):
        n, blk = ch % UNITS, ch // UNITS
        rows = slice(blk * DOT_ROWS, (blk + 1) * DOT_ROWS)
    else:
        n = ch & (UNITS - 1)
        blk = lax.shift_right_logical(ch, UNITS.bit_length() - 1)
        rows = pl.ds(pl.multiple_of(blk * DOT_ROWS, DOT_ROWS), DOT_ROWS)
    part_ref[n, blk] = jnp.dot(vtb_ref[blk], p_ref[n, rows, :], preferred_element_type=F32)


def _head_tiles(part_ref):
    per_block = []
    for u in range(PAIR):
        outs = []
        for g in range(HEADS // GROUP):
            n = u * (HEADS // GROUP) + g
            o = part_ref[n, 0]
            for blk in range(1, part_ref.shape[1]):
                o = o + part_ref[n, blk]
            outs.append(o[0:HEAD_DIM, :] * (1.0 / o[HEAD_DIM:HEAD_DIM + 1, :]))
        tiles = []
        for t in range(HEADS // 2):
            o_g = outs[(2 * t) // GROUP]
            a = (2 * t) % GROUP
            tiles.append(jnp.concatenate([o_g[:, a * Q_BLOCK:(a + 1) * Q_BLOCK],
                                          o_g[:, (a + 1) * Q_BLOCK:(a + 2) * Q_BLOCK]], axis=0).T)
        per_block.append(tiles)
    return per_block


def _attn_kernel(qt_ref, qit_ref, wit_ref, kaug_ref, kk_ref, vt_ref, ma_ref, gate_ref, x_ref, gb_ref,
                 wo_ref, y_ref, idx_ref, rhs_ref, s_ref, p_ref, st_ref, vtb_ref, part_ref, done_ref,
                 *, seq, n_pairs, topk):
    g = pl.program_id(0)
    kf = float(topk)
    steps = seq // (PAIR * Q_BLOCK)
    key_blocks = seq // DOT_ROWS
    qcols = [slice(u * Q_BLOCK, (u + 1) * Q_BLOCK) for u in range(PAIR)]
    searching = g < n_pairs
    pending = g >= 1
    jj = lax.rem(g, steps)
    g_prev = jnp.maximum(g - 1, 0)
    slot = lax.div(g, steps) & 1
    slot_prev = lax.div(g_prev, steps) & 1

    @pl.when(g == 0)
    def _():
        p_ref[...] = jnp.zeros(p_ref.shape, BF16)
        vtb_ref[...] = jnp.zeros(vtb_ref.shape, BF16)
        y_ref[...] = jnp.zeros(y_ref.shape, F32)

    @pl.when(jnp.logical_and(searching, jj == 0))
    def _():
        for u in range(PAIR):
            rhs_ref[u, :, 0:LANES] = kaug_ref[...]
        ones = jnp.ones((VALUE_ROWS - HEAD_DIM, DOT_ROWS), BF16)
        for blk in range(key_blocks):
            vtb_ref[slot, blk] = jnp.concatenate([vt_ref[:, blk * DOT_ROWS:(blk + 1) * DOT_ROWS], ones], axis=0)

    part_ref[...] = jnp.zeros(part_ref.shape, F32)
    done_ref[0] = 0

    steps_per_bucket = KEY_BUCKET // (PAIR * Q_BLOCK)

    def buckets(step, active):
        return [((n + 1) * KEY_BUCKET,
                 jnp.logical_and(active, jnp.logical_and(step >= n * steps_per_bucket,
                                                         step < (n + 1) * steps_per_bucket)))
                for n in range(seq // KEY_BUCKET)]

    blocks = [PAIR * jj + u for u in range(PAIR)]
    total = UNITS * (lax.div(lax.rem(g_prev, steps), steps_per_bucket) + 1) * (KEY_BUCKET // DOT_ROWS)
    vtb_prev = vtb_ref.at[slot_prev]

    for nk, here in buckets(jj, searching):
        @pl.when(here)
        def _(nk=nk):
            lane_q = lax.broadcasted_iota(jnp.int32, (1, Q_BLOCK), 1)
            states = []
            for u in range(PAIR):
                n_adm = blocks[u] * Q_BLOCK + CHUNK + CHUNK * (lane_q >= CHUNK).astype(jnp.int32)
                lo, hi = _index_scores(nk, n_adm, qit_ref, qcols[u], wit_ref[:, qcols[u]], kk_ref, s_ref,
                                       idx_ref.at[u])
                n_adm_f = n_adm.astype(F32)
                c_lo = jnp.where(n_adm_f <= kf, kf, n_adm_f)
                states.append((lo, hi, c_lo, jnp.zeros((1, Q_BLOCK), F32)))

            def cond(carry):
                it, flat = carry[0], carry[1:]
                still = jnp.logical_or(flat[2] != kf, flat[6] != kf)
                return jnp.logical_and(it < FAST_TRIPS, _any_lane(still))

            def body(carry):
                it, flat = carry[0], carry[1:]
                st = [flat[0:4], flat[4:8]]
                for _ in range(2):
                    st = [_probe(nk, kf, idx_ref.at[u], st[u]) for u in range(PAIR)]
                for c in range(CHUNKS_PER_TRIP):
                    _value_chunk(jnp.minimum(it * CHUNKS_PER_TRIP + c, total - 1), vtb_prev, p_ref, part_ref)
                return (it + 1,) + tuple(st[0]) + tuple(st[1])

            final = lax.while_loop(cond, body, (jnp.int32(0),) + tuple(states[0]) + tuple(states[1]))
            done_ref[0] = jnp.minimum(final[0] * CHUNKS_PER_TRIP, total)
            for u in range(PAIR):
                lo, hi, c_lo, _ = final[1 + 4 * u:5 + 4 * u]
                _write_selection(nk, blocks[u], lo, idx_ref.at[u], rhs_ref.at[u])
                st_ref[4 * u + 0:4 * u + 1, :] = lo
                st_ref[4 * u + 1:4 * u + 2, :] = hi
                st_ref[4 * u + 2:4 * u + 3, :] = c_lo

    @pl.when(jnp.logical_and(pending, searching))
    def _():
        def chunk(ch, carry):
            _value_chunk(ch, vtb_prev, p_ref, part_ref)
            return carry

        lax.fori_loop(done_ref[0], total, chunk, 0)

    @pl.when(g == n_pairs)
    def _():
        for ch in range(UNITS * key_blocks):
            _value_chunk(ch, vtb_prev, p_ref, part_ref)

    lane = lax.broadcasted_iota(jnp.int32, (PAIR * Q_BLOCK, LANES), 1)
    lo_half = lane < HALF

    def finish_pending():
        per_block = _head_tiles(part_ref)
        mixed = [ma_ref[...]]
        for t in range(HEADS // 2):
            cols = slice(t * LANES, (t + 1) * LANES)
            o = jnp.concatenate([per_block[u][t] for u in range(PAIR)], axis=0)
            msq = _half_mean_sq(o * o, lo_half)
            ob = o * lax.rsqrt(msq + NORM_EPS) * gb_ref[:, cols] * gate_ref[:, cols].astype(F32)
            mixed.append(ob.astype(BF16))
        y_ref[...] = x_ref[...] + jnp.dot(jnp.concatenate(mixed, axis=1), wo_ref[...],
                                          preferred_element_type=F32)

    for u in range(PAIR):
        lo = st_ref[4 * u + 0:4 * u + 1, :]
        hi = st_ref[4 * u + 1:4 * u + 2, :]
        c_lo = st_ref[4 * u + 2:4 * u + 3, :]

        @pl.when(jnp.logical_and(searching, _any_lane(c_lo != kf)))
        def _(u=u, lo=lo, hi=hi):
            _exact_fallback(blocks[u] + 1, kf, lo, hi, idx_ref.at[u], rhs_ref.at[u])

    for nk, here in buckets(jj, searching):
        @pl.when(here)
        def _(nk=nk):
            _probabilities(nk, qt_ref, qcols, rhs_ref, s_ref, p_ref, finish_pending)

    @pl.when(g == n_pairs)
    def _():
        finish_pending()


def kernel(x, norm_gain, w_in, sgu_norm_gain, sgu_w, sgu_b, q_norm_gain, k_norm_gain,
           idx_k_norm_gain, branch_norm_gain, w_out):
    bsz, seq, d_model = x.shape
    assert d_model == D_MODEL and norm_gain.shape[0] == 1
    assert seq % PROJ_ROWS == 0 and seq % KEY_BUCKET == 0 and KEY_BUCKET % (PAIR * Q_BLOCK) == 0
    tokens = bsz * seq
    topk = min(TOPK_MAX, seq // 4)
    idx_w_scale = (IDX_HEADS ** -0.5) * (IDX_DIM ** -0.5)

    w = w_in[0]
    a3 = 3 * A_WIDTH
    w_q = w[:, a3:a3 + B_WIDTH]
    w_k = w[:, a3 + B_WIDTH:a3 + B_WIDTH + HEAD_DIM]
    w_v = w[:, a3 + B_WIDTH + HEAD_DIM:a3 + B_WIDTH + 2 * HEAD_DIM]
    o_g = a3 + B_WIDTH + 2 * HEAD_DIM
    w_g = w[:, o_g:o_g + B_WIDTH]
    o_i = o_g + B_WIDTH
    w_iq = w[:, o_i:o_i + IDX_HEADS * IDX_DIM]
    w_ik = w[:, o_i + IDX_HEADS * IDX_DIM:o_i + IDX_HEADS * IDX_DIM + IDX_DIM]
    w_iw = w[:, o_i + IDX_HEADS * IDX_DIM + IDX_DIM:]
    w_main = jnp.concatenate([w[:, :a3], w_g, w_k, w_ik, w_v, w_iw,
                              jnp.zeros((D_MODEL, LANES - HEAD_DIM - IDX_HEADS), F32)], axis=1).astype(BF16)
    assert w_main.shape[1] == _PACKED_COLS
    w_t = jnp.concatenate([w_q.T, w_iq.T], axis=0).astype(BF16)
    assert w_t.shape[0] == _PACKED_ROWS
    x2 = x.reshape(tokens, D_MODEL)
    ng = norm_gain[0].reshape(1, D_MODEL)
    sgn = sgu_norm_gain[0].reshape(1, A_WIDTH)
    sw = sgu_w[0]
    sb = sgu_b[0].reshape(A_GROUPS, A_BLOCK, 1)
    qg = q_norm_gain[0].reshape(HEAD_DIM, 1)
    kg = jnp.concatenate([k_norm_gain[0], idx_k_norm_gain[0]]).reshape(1, LANES)
    ga = branch_norm_gain[0, :A_WIDTH].reshape(1, A_WIDTH)
    gb = branch_norm_gain[0, A_WIDTH:].reshape(1, B_WIDTH)
    wo = w_out[0].astype(BF16)

    tm = PROJ_ROWS
    full = lambda shape: pl.BlockSpec(shape, lambda i: (0,) * len(shape))
    rows = lambda width: pl.BlockSpec((tm, width), lambda i: (i, 0))
    colsT = lambda height: pl.BlockSpec((height, tm), lambda i: (0, i))
    outs = pl.pallas_call(
        functools.partial(_proj_kernel, tiles_per_seq=seq // tm, idx_w_scale=idx_w_scale),
        grid=(tokens // tm,),
        in_specs=[rows(D_MODEL), full((1, D_MODEL)), full((D_MODEL, _PACKED_COLS)),
                  full((_PACKED_ROWS, D_MODEL)), full((1, A_WIDTH)), full((A_GROUPS, A_BLOCK, A_BLOCK)),
                  full((A_GROUPS, A_BLOCK, 1)), full((HEAD_DIM, 1)), full((1, LANES)),
                  full((1, A_WIDTH))],
        out_specs=[rows(A_WIDTH), rows(B_WIDTH), rows(LANES), rows(LANES),
                   colsT(B_WIDTH), colsT(IDX_HEADS * IDX_DIM), colsT(HEAD_DIM), colsT(IDX_HEADS)],
        out_shape=[jax.ShapeDtypeStruct((tokens, A_WIDTH), BF16),
                   jax.ShapeDtypeStruct((tokens, B_WIDTH), BF16),
                   jax.ShapeDtypeStruct((tokens, LANES), BF16),
                   jax.ShapeDtypeStruct((tokens, LANES), BF16),
                   jax.ShapeDtypeStruct((B_WIDTH, tokens), BF16),
                   jax.ShapeDtypeStruct((IDX_HEADS * IDX_DIM, tokens), BF16),
                   jax.ShapeDtypeStruct((HEAD_DIM, tokens), BF16),
                   jax.ShapeDtypeStruct((IDX_HEADS, tokens), F32)],
        compiler_params=pltpu.CompilerParams(dimension_semantics=("arbitrary",),
                                             vmem_limit_bytes=VMEM_LIMIT),
        name="proj_sgu",
    )(x2, ng, w_main, w_t, sgn, sw, sb, qg, kg, ga)
    ma, gate, kaug, kk, qt, qit, vt, wit = outs

    qrows = PAIR * Q_BLOCK
    steps = seq // qrows
    n_pairs = bsz * steps
    cur = lambda g: jnp.minimum(g, n_pairs - 1)
    qblk = lambda width: pl.BlockSpec((qrows, width), lambda g: (jnp.maximum(g - 1, 0), 0))
    qblkT = lambda height: pl.BlockSpec((height, qrows), lambda g: (0, cur(g)))
    const = lambda shape: pl.BlockSpec(shape, lambda g: (0,) * len(shape))
    y = pl.pallas_call(
        functools.partial(_attn_kernel, seq=seq, n_pairs=n_pairs, topk=topk),
        grid=(n_pairs + 1,),
        in_specs=[qblkT(B_WIDTH), qblkT(IDX_HEADS * IDX_DIM), qblkT(IDX_HEADS),
                  pl.BlockSpec((seq, LANES), lambda g: (cur(g) // steps, 0)),
                  pl.BlockSpec((seq, LANES), lambda g: (cur(g) // steps, 0)),
                  pl.BlockSpec((HEAD_DIM, seq), lambda g: (0, cur(g) // steps)),
                  qblk(A_WIDTH), qblk(B_WIDTH), qblk(D_MODEL),
                  const((1, B_WIDTH)), const((D_MODEL, D_MODEL))],
        out_specs=qblk(D_MODEL),
        out_shape=jax.ShapeDtypeStruct((tokens, D_MODEL), F32),
        scratch_shapes=[pltpu.VMEM((PAIR, seq, Q_BLOCK), F32),
                        pltpu.VMEM((PAIR, seq, 2 * LANES), BF16),
                        pltpu.VMEM((2, seq, GROUP * Q_BLOCK), F32),
                        pltpu.VMEM((UNITS, seq, GROUP * Q_BLOCK), BF16),
                        pltpu.VMEM((8, Q_BLOCK), F32),
                        pltpu.VMEM((2, seq // DOT_ROWS, VALUE_ROWS, DOT_ROWS), BF16),
                        pltpu.VMEM((UNITS, seq // DOT_ROWS, VALUE_ROWS, GROUP * Q_BLOCK), F32),
                        pltpu.SMEM((1,), jnp.int32)],
        compiler_params=pltpu.CompilerParams(dimension_semantics=("arbitrary",),
                                             vmem_limit_bytes=VMEM_LIMIT),
        name="dsa_attn_out",
    )(qt, qit, wit, kaug, kk, vt, ma, gate, x2, gb, wo)
    return y.reshape(bsz, seq, D_MODEL)
```

```python
import functools

import jax
import jax.numpy as jnp
from jax import lax
from jax.experimental import pallas as pl
from jax.experimental.pallas import tpu as pltpu

F32 = jnp.float32
BF16 = jnp.bfloat16

D_MODEL = 1024
CHUNK = 64
A_WIDTH = 512
A_GROUPS = 4
A_BLOCK = 128
HEADS = 8
HEAD_DIM = 64
B_WIDTH = HEADS * HEAD_DIM
IDX_HEADS = 8
IDX_DIM = 64
TOPK_MAX = 256
Q_BLOCK = 128
PAIR = 2
NORM_EPS = 1e-6
MASK_OFF = 1e32
LANES = 128
HALF = LANES // 2
PROJ_ROWS = 1024
DOT_ROWS = 512
RED_ROWS = 128
KEY_BUCKET = 512
FAST_TRIPS = 24
INTERP_MARGIN = 0.02
SLOW_TRIPS = 70
CHUNKS_PER_TRIP = 2
VMEM_LIMIT = 48 * 1024 * 1024

_OFF_U, _OFF_V, _OFF_Z, _OFF_G, _OFF_K = 0, 512, 1024, 1536, 2048
_PACKED_COLS = 2304
_ROW_Q, _ROW_QI, _PACKED_ROWS = 0, 512, 1024

_NT = (((1,), (1,)), ((), ()))


def _gelu(x):
    c = 0.7978845608028654
    return 0.5 * x * (1.0 + jnp.tanh(c * (x + 0.044715 * (x * x * x))))


def _silu(x):
    return x / (1.0 + jnp.exp(-x))


def _row_blocks(total, size):
    return [slice(r, min(r + size, total)) for r in range(0, total, size)]


def _half_mean_sq(x2, lo_half):
    tot = jnp.sum(x2, axis=-1, keepdims=True)
    lo = jnp.sum(jnp.where(lo_half, x2, 0.0), axis=-1, keepdims=True)
    return jnp.where(lo_half, lo, tot - lo) * (1.0 / HALF)


def _proj_kernel(x_ref, ng_ref, w_ref, wt_ref, sgn_ref, sw_ref, sb_ref, qg_ref, kg_ref, ga_ref,
                 ma_ref, gate_ref, kaug_ref, kk_ref, qt_ref, qit_ref, vt_ref, wit_ref,
                 *, tiles_per_seq, idx_w_scale):
    tm = x_ref.shape[0]
    i = pl.program_id(0)
    x = x_ref[...]
    ms = jnp.mean(x * x, axis=-1, keepdims=True)
    h = (x * lax.rsqrt(ms + NORM_EPS) * ng_ref[...]).astype(BF16)

    lane = lax.broadcasted_iota(jnp.int32, (tm, LANES), 1)
    lo_half = lane < HALF

    def proj(off, width):
        return jnp.dot(h, w_ref[:, off:off + width], preferred_element_type=F32)

    gu = _gelu(proj(_OFF_U, A_WIDTH))
    gv = _gelu(proj(_OFF_V, A_WIDTH))
    pz = proj(_OFF_Z, A_WIDTH)
    r_i = lax.broadcasted_iota(jnp.int32, (A_BLOCK, A_BLOCK), 0)
    c_j = lax.broadcasted_iota(jnp.int32, (A_BLOCK, A_BLOCK), 1)
    causal = lax.shift_right_logical(c_j, 6) <= lax.shift_right_logical(r_i, 6)
    for g in range(A_GROUPS):
        cols = slice(g * LANES, (g + 1) * LANES)
        vg = gv[:, cols]
        mu = jnp.mean(vg, axis=-1, keepdims=True)
        d = vg - mu
        var = jnp.mean(d * d, axis=-1, keepdims=True)
        vn = (d * lax.rsqrt(var + NORM_EPS) * sgn_ref[:, cols]).astype(BF16)
        wg = jnp.where(causal, sw_ref[g], 0.0).astype(BF16)
        side = jnp.concatenate([vn[blk * A_BLOCK:(blk + 1) * A_BLOCK, :] for blk in range(tm // A_BLOCK)], axis=1)
        mixed = jnp.dot(wg, side, preferred_element_type=F32) + sb_ref[g]
        for blk in range(tm // A_BLOCK):
            rows = slice(blk * A_BLOCK, (blk + 1) * A_BLOCK)
            s = mixed[:, blk * LANES:(blk + 1) * LANES]
            ya = gu[rows, cols] * s
            oa = ya * lax.rsqrt(jnp.mean(ya * ya, axis=-1, keepdims=True) + NORM_EPS) * ga_ref[:, cols]
            ma_ref[rows, cols] = (oa * _silu(pz[rows, cols])).astype(BF16)

    gate_ref[...] = _silu(proj(_OFF_G, B_WIDTH)).astype(BF16)

    pkv = proj(_OFF_K, 2 * LANES)
    pk = pkv[:, 0:LANES]
    k_ms = jnp.sum(jnp.where(lo_half, pk * pk, 0.0), axis=-1, keepdims=True) * (1.0 / HALF)
    kn = pk * lax.rsqrt(k_ms + NORM_EPS)
    ik_mu = jnp.sum(jnp.where(lo_half, 0.0, pk), axis=-1, keepdims=True) * (1.0 / HALF)
    dk = pk - ik_mu
    ik_var = jnp.sum(jnp.where(lo_half, 0.0, dk * dk), axis=-1, keepdims=True) * (1.0 / HALF)
    kin = dk * lax.rsqrt(ik_var + NORM_EPS)
    tile = jnp.where(lo_half, kn, kin) * kg_ref[...]
    swapped = pltpu.roll(tile, HALF, axis=1)
    row = lax.broadcasted_iota(jnp.int32, (tm, LANES), 0)
    pos = (i % tiles_per_seq) * tm + row
    pos_hi = lax.shift_right_logical(pos, 6).astype(F32)
    pos_lo = (pos & (CHUNK - 1)).astype(F32)
    posfeat = jnp.where(lane == HALF, pos_hi, jnp.where(lane == HALF + 1, pos_lo, 0.0))
    kaug_ref[...] = jnp.where(lo_half, tile, posfeat).astype(BF16)
    kk_ref[...] = jnp.where(lo_half, swapped, tile).astype(BF16)

    pt = lax.dot_general(wt_ref[...], h, _NT, preferred_element_type=F32)
    for hh in range(HEADS):
        rows = slice(_ROW_Q + hh * HEAD_DIM, _ROW_Q + (hh + 1) * HEAD_DIM)
        xq = pt[rows, :]
        msq = jnp.mean(xq * xq, axis=0, keepdims=True)
        qt_ref[rows, :] = (xq * lax.rsqrt(msq + NORM_EPS) * qg_ref[...] * (HEAD_DIM ** -0.5)).astype(BF16)
    qit_ref[...] = pt[_ROW_QI:_ROW_QI + IDX_HEADS * IDX_DIM, :].astype(BF16)
    for blk in range(tm // LANES):
        cols = slice(blk * LANES, (blk + 1) * LANES)
        vw_t = pkv[cols, LANES:2 * LANES].T
        vt_ref[:, cols] = vw_t[0:HEAD_DIM, :].astype(BF16)
        wit_ref[:, cols] = vw_t[HEAD_DIM:HEAD_DIM + IDX_HEADS, :] * idx_w_scale


def _tile_iotas():
    r_k = lax.broadcasted_iota(jnp.int32, (Q_BLOCK, Q_BLOCK), 0)
    c_q = lax.broadcasted_iota(jnp.int32, (Q_BLOCK, Q_BLOCK), 1)
    return r_k, c_q


def _colsum(x):
    return jnp.sum(x, axis=0, keepdims=True)


def _any_lane(pred):
    return jnp.max(jnp.where(pred, 1.0, 0.0)) > 0.0


def _index_scores(nk, n_adm, qit_ref, qcols, wi, kk_ref, s_ref, idx_ref):
    r_k, c_q = _tile_iotas()
    top_rows = r_k < HALF
    per_head = []
    for t in range(IDX_HEADS // 2):
        qit = qit_ref[t * LANES:(t + 1) * LANES, qcols]
        zero = jnp.zeros_like(qit)
        per_head += [jnp.where(top_rows, qit, zero), jnp.where(top_rows, zero, qit)]
    half_heads = IDX_HEADS // 2
    for g in range(2):
        wg = jnp.concatenate(per_head[g * half_heads:(g + 1) * half_heads], axis=1)
        for rows in _row_blocks(nk, DOT_ROWS):
            s_ref[g, rows, :] = jnp.dot(kk_ref[rows, :], wg, preferred_element_type=F32)

    def weighted_relu(g, rows):
        acc = None
        for i in range(half_heads):
            hh = g * half_heads + i
            term = jnp.maximum(s_ref[g, rows, i * Q_BLOCK:(i + 1) * Q_BLOCK], 0.0) * wi[hh:hh + 1, :]
            acc = term if acc is None else acc + term
        return acc

    slabs = nk // RED_ROWS
    mn_acc = jnp.full((RED_ROWS, Q_BLOCK), jnp.inf, F32)
    mx_acc = jnp.full((RED_ROWS, Q_BLOCK), -jnp.inf, F32)
    tiny_acc = jnp.full((RED_ROWS, Q_BLOCK), jnp.inf, F32)
    for r in range(slabs):
        rows = slice(r * RED_ROWS, (r + 1) * RED_ROWS)
        acc = weighted_relu(0, rows) + weighted_relu(1, rows)
        if r >= slabs - KEY_BUCKET // RED_ROWS:
            adm = (r * RED_ROWS + r_k) < n_adm
            lo_fill = jnp.where(adm, acc, -jnp.inf)
            hi_fill = jnp.where(adm, acc, jnp.inf)
        else:
            lo_fill = hi_fill = acc
        mag = jnp.abs(hi_fill)
        idx_ref[rows, :] = lo_fill
        mn_acc = jnp.minimum(mn_acc, hi_fill)
        mx_acc = jnp.maximum(mx_acc, lo_fill)
        tiny_acc = jnp.minimum(tiny_acc, jnp.where(mag == 0.0, jnp.inf, mag))
    lo = jnp.min(mn_acc, axis=0, keepdims=True)
    hi = jnp.max(mx_acc, axis=0, keepdims=True)
    tiny = jnp.min(tiny_acc, axis=0, keepdims=True)

    unit = jnp.where(tiny < jnp.inf, tiny, 1.0)
    eps = unit * (0.5 / nk)
    rank0 = (1 + r_k).astype(F32)

    def spread(r, carry):
        rows = pl.ds(pl.multiple_of(r * RED_ROWS, RED_ROWS), RED_ROWS)
        s = idx_ref[rows, :]
        rank = rank0 + jnp.asarray(r * RED_ROWS, F32)
        idx_ref[rows, :] = jnp.where(s == 0.0, -(rank * eps), s)
        return carry

    lax.fori_loop(0, slabs, spread, 0)
    return jnp.minimum(lo, -0.5 * unit), hi


def _probe(nk, kf, idx_ref, state):
    lo, hi, c_lo, c_hi = state
    frac = (c_lo - kf) / jnp.maximum(c_lo - c_hi, 1.0)
    frac = jnp.minimum(jnp.maximum(frac, INTERP_MARGIN), 1.0 - INTERP_MARGIN)
    t = lo + (hi - lo) * frac
    acc = jnp.zeros((RED_ROWS, Q_BLOCK), F32)
    for r in range(nk // RED_ROWS):
        acc = acc + jnp.where(idx_ref[r * RED_ROWS:(r + 1) * RED_ROWS, :] >= t, 1.0, 0.0)
    c = _colsum(acc)
    ge = c >= kf
    return (jnp.where(ge, t, lo), jnp.where(ge, hi, t), jnp.where(ge, c, c_lo), jnp.where(ge, c_hi, c))


def _write_selection(nk, j_blk, lo, idx_ref, rhs_ref):
    r_k, c_q = _tile_iotas()
    later = 2.0 * jnp.maximum(r_k - c_q, 0).astype(F32)

    def write(r, carry):
        rows = pl.ds(pl.multiple_of(r * RED_ROWS, RED_ROWS), RED_ROWS)
        on = jnp.where(r == j_blk, -later, 0.0)
        rhs_ref[rows, LANES:2 * LANES] = jnp.where(idx_ref[rows, :] >= lo, on, -MASK_OFF).astype(BF16)
        return carry

    lax.fori_loop(0, nk // RED_ROWS, write, 0)


def _exact_fallback(slabs, kf, lo, hi, idx_ref, rhs_ref):
    r_k, c_q = _tile_iotas()
    later = 2.0 * jnp.maximum(r_k - c_q, 0).astype(F32)

    def slab(r):
        return idx_ref[pl.ds(pl.multiple_of(r * RED_ROWS, RED_ROWS), RED_ROWS), :]

    def count(pred):
        def body(r, acc):
            return acc + jnp.where(pred(slab(r), r), 1.0, 0.0)
        return _colsum(lax.fori_loop(0, slabs, body, jnp.zeros((RED_ROWS, Q_BLOCK), F32)))

    def kth(lo):
        def body(r, acc):
            s = slab(r)
            return jnp.minimum(acc, jnp.where(s >= lo, s, jnp.inf))
        acc = lax.fori_loop(0, slabs, body, jnp.full((RED_ROWS, Q_BLOCK), jnp.inf, F32))
        thr = jnp.min(acc, axis=0, keepdims=True)
        return thr, count(lambda s, r: s > thr)

    def slow_cond(carry):
        it, _, _, _, c_gt = carry
        return jnp.logical_and(it < SLOW_TRIPS, _any_lane(c_gt >= kf))

    def slow_body(carry):
        it, lo, hi, _, _ = carry
        for _ in range(4):
            mid = 0.5 * lo + 0.5 * hi
            ge = count(lambda s, r: s >= mid) >= kf
            lo, hi = jnp.where(ge, mid, lo), jnp.where(ge, hi, mid)
        thr, c_gt = kth(lo)
        return it + 1, lo, hi, thr, c_gt

    thr0, c_gt0 = kth(lo)
    _, _, _, thr, c_gt = lax.while_loop(slow_cond, slow_body, (jnp.int32(0), lo, hi, thr0, c_gt0))
    c_eq = count(lambda s, r: s == thr)
    need = (c_gt + c_eq) > kf

    def tie_step(_, carry):
        lo_i, hi_i = carry
        mid_i = jnp.floor((lo_i + hi_i) * 0.5)
        below = count(lambda s, r: jnp.logical_and(s == thr, (r * RED_ROWS + r_k).astype(F32) <= mid_i))
        ok = (c_gt + below) >= kf
        return jnp.where(ok, lo_i, mid_i), jnp.where(ok, mid_i, hi_i)

    last = jnp.asarray(slabs * RED_ROWS - 1, F32)
    lo_i = jnp.full((1, Q_BLOCK), -1.0, F32)
    hi_i = jnp.zeros((1, Q_BLOCK), F32) + last
    _, cut = lax.fori_loop(0, 13, tie_step, (lo_i, hi_i))
    cut = jnp.where(need, cut, last + 1.0).astype(jnp.int32)

    def write(r, carry):
        s = slab(r)
        krow = r * RED_ROWS + r_k
        sel = jnp.logical_or(s > thr, jnp.logical_and(s == thr, krow <= cut))
        on = jnp.where(r == slabs - 1, -later, 0.0)
        rhs_ref[pl.ds(pl.multiple_of(r * RED_ROWS, RED_ROWS), RED_ROWS), LANES:2 * LANES] = (
            jnp.where(sel, on, -MASK_OFF).astype(BF16))
        return carry

    lax.fori_loop(0, slabs, write, 0)


GROUP = HEADS // 2
UNITS = PAIR * (HEADS // GROUP)
VALUE_ROWS = HEAD_DIM + 16


def _probabilities(nk, qt_ref, qcols, rhs_ref, s_ref, p_ref, between):
    r_k, c_q = _tile_iotas()
    feat = lax.broadcasted_iota(jnp.int32, (HALF, Q_BLOCK), 0)
    group = GROUP
    width = group * Q_BLOCK
    units = [(u, g) for u in range(len(qcols)) for g in range(HEADS // group)]

    def score_operand(u, hh):
        slope = 2.0 ** (-(hh + 1))
        alibi = jnp.where(feat == 0, CHUNK * slope, jnp.where(feat == 1, slope, 0.0)).astype(BF16)
        scaled_ident = jnp.where(r_k == c_q, slope, 0.0).astype(BF16)
        return jnp.concatenate([qt_ref[hh * HEAD_DIM:(hh + 1) * HEAD_DIM, qcols[u]], alibi, scaled_ident], axis=0)

    def scores(n):
        u, g = units[n]
        lhs_t = jnp.concatenate([score_operand(u, g * group + i) for i in range(group)], axis=1)
        m_acc = jnp.full((RED_ROWS, width), -jnp.inf, F32)
        for rows in _row_blocks(nk, DOT_ROWS):
            blk = jnp.dot(rhs_ref[u, rows, :], lhs_t, preferred_element_type=F32)
            s_ref[n % 2, rows, :] = blk
            for sub in range((rows.stop - rows.start) // RED_ROWS):
                m_acc = jnp.maximum(m_acc, blk[sub * RED_ROWS:(sub + 1) * RED_ROWS, :])
        return jnp.max(m_acc, axis=0, keepdims=True)

    def probabilities(n, m):
        for rows in _row_blocks(nk, RED_ROWS):
            p_ref[n, rows, :] = jnp.exp((s_ref[n % 2, rows, :] - m).astype(BF16))

    m = scores(0)
    between()
    for n in range(len(units)):
        m_next = scores(n + 1) if n + 1 < len(units) else None
        probabilities(n, m)
        m = m_next


def _value_chunk(ch, vtb_ref, p_ref, part_ref):
    if isinstance(ch, int):
        n, blk = ch % UNITS, ch // UNITS
        rows = slice(blk * DOT_ROWS, (blk + 1) * DOT_ROWS)
    else:
        n = ch & (UNITS - 1)
        blk = lax.shift_right_logical(ch, UNITS.bit_length() - 1)
        rows = pl.ds(pl.multiple_of(blk * DOT_ROWS, DOT_ROWS), DOT_ROWS)
    part_ref[n, blk] = jnp.dot(vtb_ref[blk], p_ref[n, rows, :], preferred_element_type=F32)


def _head_tiles(part_ref, n_blocks):
    per_block = []
    for u in range(PAIR):
        outs = []
        for g in range(HEADS // GROUP):
            n = u * (HEADS // GROUP) + g
            o = part_ref[n, 0]
            for blk in range(1, part_ref.shape[1]):
                o = o + jnp.where(blk < n_blocks, part_ref[n, blk], 0.0)
            outs.append(o[0:HEAD_DIM, :] * (1.0 / o[HEAD_DIM:HEAD_DIM + 1, :]))
        tiles = []
        for t in range(HEADS // 2):
            o_g = outs[(2 * t) // GROUP]
            a = (2 * t) % GROUP
            tiles.append(jnp.concatenate([o_g[:, a * Q_BLOCK:(a + 1) * Q_BLOCK],
                                          o_g[:, (a + 1) * Q_BLOCK:(a + 2) * Q_BLOCK]], axis=0).T)
        per_block.append(tiles)
    return per_block


def _attn_kernel(qt_ref, qit_ref, wit_ref, kaug_ref, kk_ref, vt_ref, ma_ref, gate_ref, x_ref, gb_ref,
                 wo_ref, y_ref, idx_ref, rhs_ref, s_ref, p_ref, st_ref, vtb_ref, part_ref, done_ref,
                 *, seq, n_pairs, topk):
    g = pl.program_id(0)
    kf = float(topk)
    steps = seq // (PAIR * Q_BLOCK)
    key_blocks = seq // DOT_ROWS
    qcols = [slice(u * Q_BLOCK, (u + 1) * Q_BLOCK) for u in range(PAIR)]
    searching = g < n_pairs
    pending = g >= 1
    jj = lax.rem(g, steps)
    g_prev = jnp.maximum(g - 1, 0)
    slot = lax.div(g, steps) & 1
    slot_prev = lax.div(g_prev, steps) & 1

    @pl.when(g == 0)
    def _():
        p_ref[...] = jnp.zeros(p_ref.shape, BF16)
        vtb_ref[...] = jnp.zeros(vtb_ref.shape, BF16)
        part_ref[...] = jnp.zeros(part_ref.shape, F32)
        y_ref[...] = jnp.zeros(y_ref.shape, F32)

    @pl.when(jnp.logical_and(searching, jj == 0))
    def _():
        for u in range(PAIR):
            rhs_ref[u, :, 0:LANES] = kaug_ref[...]
        ones = jnp.ones((VALUE_ROWS - HEAD_DIM, DOT_ROWS), BF16)
        for blk in range(key_blocks):
            vtb_ref[slot, blk] = jnp.concatenate([vt_ref[:, blk * DOT_ROWS:(blk + 1) * DOT_ROWS], ones], axis=0)

    done_ref[0] = 0

    steps_per_bucket = KEY_BUCKET // (PAIR * Q_BLOCK)

    def buckets(step, active):
        return [((n + 1) * KEY_BUCKET,
                 jnp.logical_and(active, jnp.logical_and(step >= n * steps_per_bucket,
                                                         step < (n + 1) * steps_per_bucket)))
                for n in range(seq // KEY_BUCKET)]

    blocks = [PAIR * jj + u for u in range(PAIR)]
    total = UNITS * (lax.div(lax.rem(g_prev, steps), steps_per_bucket) + 1) * (KEY_BUCKET // DOT_ROWS)
    vtb_prev = vtb_ref.at[slot_prev]

    for nk, here in buckets(jj, searching):
        @pl.when(here)
        def _(nk=nk):
            lane_q = lax.broadcasted_iota(jnp.int32, (1, Q_BLOCK), 1)
            states = []
            for u in range(PAIR):
                n_adm = blocks[u] * Q_BLOCK + CHUNK + CHUNK * (lane_q >= CHUNK).astype(jnp.int32)
                lo, hi = _index_scores(nk, n_adm, qit_ref, qcols[u], wit_ref[:, qcols[u]], kk_ref, s_ref,
                                       idx_ref.at[u])
                n_adm_f = n_adm.astype(F32)
                c_lo = jnp.where(n_adm_f <= kf, kf, n_adm_f)
                states.append((lo, hi, c_lo, jnp.zeros((1, Q_BLOCK), F32)))

            def cond(carry):
                it, flat = carry[0], carry[1:]
                still = jnp.logical_or(flat[2] != kf, flat[6] != kf)
                return jnp.logical_and(it < FAST_TRIPS, _any_lane(still))

            def body(carry):
                it, flat = carry[0], carry[1:]
                st = [flat[0:4], flat[4:8]]
                for _ in range(2):
                    st = [_probe(nk, kf, idx_ref.at[u], st[u]) for u in range(PAIR)]
                for c in range(CHUNKS_PER_TRIP):
                    _value_chunk(jnp.minimum(it * CHUNKS_PER_TRIP + c, total - 1), vtb_prev, p_ref, part_ref)
                return (it + 1,) + tuple(st[0]) + tuple(st[1])

            final = lax.while_loop(cond, body, (jnp.int32(0),) + tuple(states[0]) + tuple(states[1]))
            done_ref[0] = jnp.minimum(final[0] * CHUNKS_PER_TRIP, total)
            for u in range(PAIR):
                lo, hi, c_lo, _ = final[1 + 4 * u:5 + 4 * u]
                _write_selection(nk, blocks[u], lo, idx_ref.at[u], rhs_ref.at[u])
                st_ref[4 * u + 0:4 * u + 1, :] = lo
                st_ref[4 * u + 1:4 * u + 2, :] = hi
                st_ref[4 * u + 2:4 * u + 3, :] = c_lo

    @pl.when(jnp.logical_and(pending, searching))
    def _():
        def chunk(ch, carry):
            _value_chunk(ch, vtb_prev, p_ref, part_ref)
            return carry

        lax.fori_loop(done_ref[0], total, chunk, 0)

    @pl.when(g == n_pairs)
    def _():
        for ch in range(UNITS * key_blocks):
            _value_chunk(ch, vtb_prev, p_ref, part_ref)

    lane = lax.broadcasted_iota(jnp.int32, (PAIR * Q_BLOCK, LANES), 1)
    lo_half = lane < HALF

    def finish_pending():
        per_block = _head_tiles(part_ref, lax.div(total, UNITS))
        mixed = [ma_ref[...]]
        for t in range(HEADS // 2):
            cols = slice(t * LANES, (t + 1) * LANES)
            o = jnp.concatenate([per_block[u][t] for u in range(PAIR)], axis=0)
            msq = _half_mean_sq(o * o, lo_half)
            ob = o * lax.rsqrt(msq + NORM_EPS) * gb_ref[:, cols] * gate_ref[:, cols].astype(F32)
            mixed.append(ob.astype(BF16))
        y_ref[...] = x_ref[...] + jnp.dot(jnp.concatenate(mixed, axis=1), wo_ref[...],
                                          preferred_element_type=F32)

    for u in range(PAIR):
        lo = st_ref[4 * u + 0:4 * u + 1, :]
        hi = st_ref[4 * u + 1:4 * u + 2, :]
        c_lo = st_ref[4 * u + 2:4 * u + 3, :]

        @pl.when(jnp.logical_and(searching, _any_lane(c_lo != kf)))
        def _(u=u, lo=lo, hi=hi):
            _exact_fallback(blocks[u] + 1, kf, lo, hi, idx_ref.at[u], rhs_ref.at[u])

    for nk, here in buckets(jj, searching):
        @pl.when(here)
        def _(nk=nk):
            _probabilities(nk, qt_ref, qcols, rhs_ref, s_ref, p_ref, finish_pending)

    @pl.when(g == n_pairs)
    def _():
        finish_pending()


def kernel(x, norm_gain, w_in, sgu_norm_gain, sgu_w, sgu_b, q_norm_gain, k_norm_gain,
           idx_k_norm_gain, branch_norm_gain, w_out):
    bsz, seq, d_model = x.shape
    assert d_model == D_MODEL and norm_gain.shape[0] == 1
    assert seq % PROJ_ROWS == 0 and seq % KEY_BUCKET == 0 and KEY_BUCKET % (PAIR * Q_BLOCK) == 0
    tokens = bsz * seq
    topk = min(TOPK_MAX, seq // 4)
    idx_w_scale = (IDX_HEADS ** -0.5) * (IDX_DIM ** -0.5)

    w = w_in[0]
    a3 = 3 * A_WIDTH
    w_q = w[:, a3:a3 + B_WIDTH]
    w_k = w[:, a3 + B_WIDTH:a3 + B_WIDTH + HEAD_DIM]
    w_v = w[:, a3 + B_WIDTH + HEAD_DIM:a3 + B_WIDTH + 2 * HEAD_DIM]
    o_g = a3 + B_WIDTH + 2 * HEAD_DIM
    w_g = w[:, o_g:o_g + B_WIDTH]
    o_i = o_g + B_WIDTH
    w_iq = w[:, o_i:o_i + IDX_HEADS * IDX_DIM]
    w_ik = w[:, o_i + IDX_HEADS * IDX_DIM:o_i + IDX_HEADS * IDX_DIM + IDX_DIM]
    w_iw = w[:, o_i + IDX_HEADS * IDX_DIM + IDX_DIM:]
    w_main = jnp.concatenate([w[:, :a3], w_g, w_k, w_ik, w_v, w_iw,
                              jnp.zeros((D_MODEL, LANES - HEAD_DIM - IDX_HEADS), F32)], axis=1).astype(BF16)
    assert w_main.shape[1] == _PACKED_COLS
    w_t = jnp.concatenate([w_q.T, w_iq.T], axis=0).astype(BF16)
    assert w_t.shape[0] == _PACKED_ROWS
    x2 = x.reshape(tokens, D_MODEL)
    ng = norm_gain[0].reshape(1, D_MODEL)
    sgn = sgu_norm_gain[0].reshape(1, A_WIDTH)
    sw = sgu_w[0]
    sb = sgu_b[0].reshape(A_GROUPS, A_BLOCK, 1)
    qg = q_norm_gain[0].reshape(HEAD_DIM, 1)
    kg = jnp.concatenate([k_norm_gain[0], idx_k_norm_gain[0]]).reshape(1, LANES)
    ga = branch_norm_gain[0, :A_WIDTH].reshape(1, A_WIDTH)
    gb = branch_norm_gain[0, A_WIDTH:].reshape(1, B_WIDTH)
    wo = w_out[0].astype(BF16)

    tm = PROJ_ROWS
    full = lambda shape: pl.BlockSpec(shape, lambda i: (0,) * len(shape))
    rows = lambda width: pl.BlockSpec((tm, width), lambda i: (i, 0))
    colsT = lambda height: pl.BlockSpec((height, tm), lambda i: (0, i))
    outs = pl.pallas_call(
        functools.partial(_proj_kernel, tiles_per_seq=seq // tm, idx_w_scale=idx_w_scale),
        grid=(tokens // tm,),
        in_specs=[rows(D_MODEL), full((1, D_MODEL)), full((D_MODEL, _PACKED_COLS)),
                  full((_PACKED_ROWS, D_MODEL)), full((1, A_WIDTH)), full((A_GROUPS, A_BLOCK, A_BLOCK)),
                  full((A_GROUPS, A_BLOCK, 1)), full((HEAD_DIM, 1)), full((1, LANES)),
                  full((1, A_WIDTH))],
        out_specs=[rows(A_WIDTH), rows(B_WIDTH), rows(LANES), rows(LANES),
                   colsT(B_WIDTH), colsT(IDX_HEADS * IDX_DIM), colsT(HEAD_DIM), colsT(IDX_HEADS)],
        out_shape=[jax.ShapeDtypeStruct((tokens, A_WIDTH), BF16),
                   jax.ShapeDtypeStruct((tokens, B_WIDTH), BF16),
                   jax.ShapeDtypeStruct((tokens, LANES), BF16),
                   jax.ShapeDtypeStruct((tokens, LANES), BF16),
                   jax.ShapeDtypeStruct((B_WIDTH, tokens), BF16),
                   jax.ShapeDtypeStruct((IDX_HEADS * IDX_DIM, tokens), BF16),
                   jax.ShapeDtypeStruct((HEAD_DIM, tokens), BF16),
                   jax.ShapeDtypeStruct((IDX_HEADS, tokens), F32)],
        compiler_params=pltpu.CompilerParams(dimension_semantics=("arbitrary",),
                                             vmem_limit_bytes=VMEM_LIMIT),
        name="proj_sgu",
    )(x2, ng, w_main, w_t, sgn, sw, sb, qg, kg, ga)
    ma, gate, kaug, kk, qt, qit, vt, wit = outs

    qrows = PAIR * Q_BLOCK
    steps = seq // qrows
    n_pairs = bsz * steps
    cur = lambda g: jnp.minimum(g, n_pairs - 1)
    qblk = lambda width: pl.BlockSpec((qrows, width), lambda g: (jnp.maximum(g - 1, 0), 0))
    qblkT = lambda height: pl.BlockSpec((height, qrows), lambda g: (0, cur(g)))
    const = lambda shape: pl.BlockSpec(shape, lambda g: (0,) * len(shape))
    y = pl.pallas_call(
        functools.partial(_attn_kernel, seq=seq, n_pairs=n_pairs, topk=topk),
        grid=(n_pairs + 1,),
        in_specs=[qblkT(B_WIDTH), qblkT(IDX_HEADS * IDX_DIM), qblkT(IDX_HEADS),
                  pl.BlockSpec((seq, LANES), lambda g: (cur(g) // steps, 0)),
                  pl.BlockSpec((seq, LANES), lambda g: (cur(g) // steps, 0)),
                  pl.BlockSpec((HEAD_DIM, seq), lambda g: (0, cur(g) // steps)),
                  qblk(A_WIDTH), qblk(B_WIDTH), qblk(D_MODEL),
                  const((1, B_WIDTH)), const((D_MODEL, D_MODEL))],
        out_specs=qblk(D_MODEL),
        out_shape=jax.ShapeDtypeStruct((tokens, D_MODEL), F32),
        scratch_shapes=[pltpu.VMEM((PAIR, seq, Q_BLOCK), F32),
                        pltpu.VMEM((PAIR, seq, 2 * LANES), BF16),
                        pltpu.VMEM((2, seq, GROUP * Q_BLOCK), F32),
                        pltpu.VMEM((UNITS, seq, GROUP * Q_BLOCK), BF16),
                        pltpu.VMEM((8, Q_BLOCK), F32),
                        pltpu.VMEM((2, seq // DOT_ROWS, VALUE_ROWS, DOT_ROWS), BF16),
                        pltpu.VMEM((UNITS, seq // DOT_ROWS, VALUE_ROWS, GROUP * Q_BLOCK), F32),
                        pltpu.SMEM((1,), jnp.int32)],
        compiler_params=pltpu.CompilerParams(dimension_semantics=("arbitrary",),
                                             vmem_limit_bytes=VMEM_LIMIT),
        name="dsa_attn_out",
    )(qt, qit, wit, kaug, kk, vt, ma, gate, x2, gb, wo)
    return y.reshape(bsz, seq, D_MODEL)
```

```python
import functools

import jax
import jax.numpy as jnp
from jax import lax
from jax.experimental import pallas as pl
from jax.experimental.pallas import tpu as pltpu

F32 = jnp.float32
BF16 = jnp.bfloat16

D_MODEL = 1024
CHUNK = 64
A_WIDTH = 512
A_GROUPS = 4
A_BLOCK = 128
HEADS = 8
HEAD_DIM = 64
B_WIDTH = HEADS * HEAD_DIM
IDX_HEADS = 8
IDX_DIM = 64
TOPK_MAX = 256
Q_BLOCK = 128
PAIR = 2
NORM_EPS = 1e-6
MASK_OFF = 1e32
LANES = 128
HALF = LANES // 2
PROJ_ROWS = 1024
DOT_ROWS = 512
RED_ROWS = 128
KEY_BUCKET = 512
FAST_TRIPS = 24
INTERP_MARGIN = 0.02
SLOW_TRIPS = 70
CHUNKS_PER_TRIP = 2
VMEM_LIMIT = 48 * 1024 * 1024

_OFF_U, _OFF_V, _OFF_Z, _OFF_G, _OFF_K = 0, 512, 1024, 1536, 2048
_PACKED_COLS = 2304
_ROW_Q, _ROW_QI, _PACKED_ROWS = 0, 512, 1024

_NT = (((1,), (1,)), ((), ()))


def _gelu(x):
    c = 0.7978845608028654
    return 0.5 * x * (1.0 + jnp.tanh(c * (x + 0.044715 * (x * x * x))))


def _silu(x):
    return x / (1.0 + jnp.exp(-x))


def _row_blocks(total, size):
    return [slice(r, min(r + size, total)) for r in range(0, total, size)]


def _half_mean_sq(x2, lo_half):
    tot = jnp.sum(x2, axis=-1, keepdims=True)
    lo = jnp.sum(jnp.where(lo_half, x2, 0.0), axis=-1, keepdims=True)
    return jnp.where(lo_half, lo, tot - lo) * (1.0 / HALF)


def _proj_kernel(x_ref, ng_ref, w_ref, wt_ref, sgn_ref, sw_ref, sb_ref, qg_ref, kg_ref, ga_ref,
                 ma_ref, gate_ref, kaug_ref, kk_ref, qt_ref, qit_ref, vt_ref, wit_ref,
                 *, tiles_per_seq, idx_w_scale):
    tm = x_ref.shape[0]
    i = pl.program_id(0)
    x = x_ref[...]
    ms = jnp.mean(x * x, axis=-1, keepdims=True)
    h = (x * lax.rsqrt(ms + NORM_EPS) * ng_ref[...]).astype(BF16)

    lane = lax.broadcasted_iota(jnp.int32, (tm, LANES), 1)
    lo_half = lane < HALF

    def proj(off, width):
        return jnp.dot(h, w_ref[:, off:off + width], preferred_element_type=F32)

    gu = _gelu(proj(_OFF_U, A_WIDTH))
    gv = _gelu(proj(_OFF_V, A_WIDTH))
    pz = proj(_OFF_Z, A_WIDTH)
    r_i = lax.broadcasted_iota(jnp.int32, (A_BLOCK, A_BLOCK), 0)
    c_j = lax.broadcasted_iota(jnp.int32, (A_BLOCK, A_BLOCK), 1)
    causal = lax.shift_right_logical(c_j, 6) <= lax.shift_right_logical(r_i, 6)
    for g in range(A_GROUPS):
        cols = slice(g * LANES, (g + 1) * LANES)
        vg = gv[:, cols]
        mu = jnp.mean(vg, axis=-1, keepdims=True)
        d = vg - mu
        var = jnp.mean(d * d, axis=-1, keepdims=True)
        vn = (d * lax.rsqrt(var + NORM_EPS) * sgn_ref[:, cols]).astype(BF16)
        wg = jnp.where(causal, sw_ref[g], 0.0).astype(BF16)
        side = jnp.concatenate([vn[blk * A_BLOCK:(blk + 1) * A_BLOCK, :] for blk in range(tm // A_BLOCK)], axis=1)
        mixed = jnp.dot(wg, side, preferred_element_type=F32) + sb_ref[g]
        for blk in range(tm // A_BLOCK):
            rows = slice(blk * A_BLOCK, (blk + 1) * A_BLOCK)
            s = mixed[:, blk * LANES:(blk + 1) * LANES]
            ya = gu[rows, cols] * s
            oa = ya * lax.rsqrt(jnp.mean(ya * ya, axis=-1, keepdims=True) + NORM_EPS) * ga_ref[:, cols]
            ma_ref[rows, cols] = (oa * _silu(pz[rows, cols])).astype(BF16)

    gate_ref[...] = _silu(proj(_OFF_G, B_WIDTH)).astype(BF16)

    pkv = proj(_OFF_K, 2 * LANES)
    pk = pkv[:, 0:LANES]
    k_ms = jnp.sum(jnp.where(lo_half, pk * pk, 0.0), axis=-1, keepdims=True) * (1.0 / HALF)
    kn = pk * lax.rsqrt(k_ms + NORM_EPS)
    ik_mu = jnp.sum(jnp.where(lo_half, 0.0, pk), axis=-1, keepdims=True) * (1.0 / HALF)
    dk = pk - ik_mu
    ik_var = jnp.sum(jnp.where(lo_half, 0.0, dk * dk), axis=-1, keepdims=True) * (1.0 / HALF)
    kin = dk * lax.rsqrt(ik_var + NORM_EPS)
    tile = jnp.where(lo_half, kn, kin) * kg_ref[...]
    swapped = pltpu.roll(tile, HALF, axis=1)
    row = lax.broadcasted_iota(jnp.int32, (tm, LANES), 0)
    pos = (i % tiles_per_seq) * tm + row
    pos_hi = lax.shift_right_logical(pos, 6).astype(F32)
    pos_lo = (pos & (CHUNK - 1)).astype(F32)
    posfeat = jnp.where(lane == HALF, pos_hi, jnp.where(lane == HALF + 1, pos_lo, 0.0))
    kaug_ref[...] = jnp.where(lo_half, tile, posfeat).astype(BF16)
    kk_ref[...] = jnp.where(lo_half, swapped, tile).astype(BF16)

    pt = lax.dot_general(wt_ref[...], h, _NT, preferred_element_type=F32)
    for hh in range(HEADS):
        rows = slice(_ROW_Q + hh * HEAD_DIM, _ROW_Q + (hh + 1) * HEAD_DIM)
        xq = pt[rows, :]
        msq = jnp.mean(xq * xq, axis=0, keepdims=True)
        qt_ref[rows, :] = (xq * lax.rsqrt(msq + NORM_EPS) * qg_ref[...] * (HEAD_DIM ** -0.5)).astype(BF16)
    qit_ref[...] = pt[_ROW_QI:_ROW_QI + IDX_HEADS * IDX_DIM, :].astype(BF16)
    for blk in range(tm // LANES):
        cols = slice(blk * LANES, (blk + 1) * LANES)
        vw_t = pkv[cols, LANES:2 * LANES].T
        vt_ref[:, cols] = vw_t[0:HEAD_DIM, :].astype(BF16)
        wit_ref[:, cols] = vw_t[HEAD_DIM:HEAD_DIM + IDX_HEADS, :] * idx_w_scale


def _tile_iotas():
    r_k = lax.broadcasted_iota(jnp.int32, (Q_BLOCK, Q_BLOCK), 0)
    c_q = lax.broadcasted_iota(jnp.int32, (Q_BLOCK, Q_BLOCK), 1)
    return r_k, c_q


def _colsum(x):
    return jnp.sum(x, axis=0, keepdims=True)


def _any_lane(pred):
    return jnp.max(jnp.where(pred, 1.0, 0.0)) > 0.0


def _index_scores(nk, n_adm, qit_ref, qcols, wi, kk_ref, s_ref, idx_ref):
    r_k, c_q = _tile_iotas()
    top_rows = r_k < HALF
    per_head = []
    for t in range(IDX_HEADS // 2):
        qit = qit_ref[t * LANES:(t + 1) * LANES, qcols]
        zero = jnp.zeros_like(qit)
        per_head += [jnp.where(top_rows, qit, zero), jnp.where(top_rows, zero, qit)]
    half_heads = IDX_HEADS // 2
    for g in range(2):
        wg = jnp.concatenate(per_head[g * half_heads:(g + 1) * half_heads], axis=1)
        for rows in _row_blocks(nk, DOT_ROWS):
            s_ref[g, rows, :] = jnp.dot(kk_ref[rows, :], wg, preferred_element_type=F32)

    def weighted_relu(g, rows):
        acc = None
        for i in range(half_heads):
            hh = g * half_heads + i
            term = jnp.maximum(s_ref[g, rows, i * Q_BLOCK:(i + 1) * Q_BLOCK], 0.0) * wi[hh:hh + 1, :]
            acc = term if acc is None else acc + term
        return acc

    slabs = nk // RED_ROWS
    mn_acc = jnp.full((RED_ROWS, Q_BLOCK), jnp.inf, F32)
    mx_acc = jnp.full((RED_ROWS, Q_BLOCK), -jnp.inf, F32)
    tiny_acc = jnp.full((RED_ROWS, Q_BLOCK), jnp.inf, F32)
    for r in range(slabs):
        rows = slice(r * RED_ROWS, (r + 1) * RED_ROWS)
        acc = weighted_relu(0, rows) + weighted_relu(1, rows)
        if r >= slabs - KEY_BUCKET // RED_ROWS:
            adm = (r * RED_ROWS + r_k) < n_adm
            lo_fill = jnp.where(adm, acc, -jnp.inf)
            hi_fill = jnp.where(adm, acc, jnp.inf)
        else:
            lo_fill = hi_fill = acc
        mag = jnp.abs(hi_fill)
        idx_ref[rows, :] = lo_fill
        mn_acc = jnp.minimum(mn_acc, hi_fill)
        mx_acc = jnp.maximum(mx_acc, lo_fill)
        tiny_acc = jnp.minimum(tiny_acc, jnp.where(mag == 0.0, jnp.inf, mag))
    lo = jnp.min(mn_acc, axis=0, keepdims=True)
    hi = jnp.max(mx_acc, axis=0, keepdims=True)
    tiny = jnp.min(tiny_acc, axis=0, keepdims=True)

    unit = jnp.where(tiny < jnp.inf, tiny, 1.0)
    eps = unit * (0.5 / nk)
    rank0 = (1 + r_k).astype(F32)

    def spread(r, carry):
        rows = pl.ds(pl.multiple_of(r * RED_ROWS, RED_ROWS), RED_ROWS)
        s = idx_ref[rows, :]
        rank = rank0 + jnp.asarray(r * RED_ROWS, F32)
        idx_ref[rows, :] = jnp.where(s == 0.0, -(rank * eps), s)
        return carry

    lax.fori_loop(0, slabs, spread, 0, unroll=KEY_BUCKET // RED_ROWS)
    return jnp.minimum(lo, -0.5 * unit), hi


def _probe(nk, kf, idx_ref, state):
    lo, hi, c_lo, c_hi = state
    frac = (c_lo - kf) / jnp.maximum(c_lo - c_hi, 1.0)
    frac = jnp.minimum(jnp.maximum(frac, INTERP_MARGIN), 1.0 - INTERP_MARGIN)
    t = lo + (hi - lo) * frac
    acc = jnp.zeros((RED_ROWS, Q_BLOCK), F32)
    for r in range(nk // RED_ROWS):
        acc = acc + jnp.where(idx_ref[r * RED_ROWS:(r + 1) * RED_ROWS, :] >= t, 1.0, 0.0)
    c = _colsum(acc)
    ge = c >= kf
    return (jnp.where(ge, t, lo), jnp.where(ge, hi, t), jnp.where(ge, c, c_lo), jnp.where(ge, c_hi, c))


def _write_selection(nk, j_blk, lo, idx_ref, rhs_ref):
    r_k, c_q = _tile_iotas()
    later = 2.0 * jnp.maximum(r_k - c_q, 0).astype(F32)

    def write(r, carry):
        rows = pl.ds(pl.multiple_of(r * RED_ROWS, RED_ROWS), RED_ROWS)
        on = jnp.where(r == j_blk, -later, 0.0)
        rhs_ref[rows, LANES:2 * LANES] = jnp.where(idx_ref[rows, :] >= lo, on, -MASK_OFF).astype(BF16)
        return carry

    lax.fori_loop(0, nk // RED_ROWS, write, 0, unroll=KEY_BUCKET // RED_ROWS)


def _exact_fallback(slabs, kf, lo, hi, idx_ref, rhs_ref):
    r_k, c_q = _tile_iotas()
    later = 2.0 * jnp.maximum(r_k - c_q, 0).astype(F32)

    def slab(r):
        return idx_ref[pl.ds(pl.multiple_of(r * RED_ROWS, RED_ROWS), RED_ROWS), :]

    def count(pred):
        def body(r, acc):
            return acc + jnp.where(pred(slab(r), r), 1.0, 0.0)
        return _colsum(lax.fori_loop(0, slabs, body, jnp.zeros((RED_ROWS, Q_BLOCK), F32)))

    def kth(lo):
        def body(r, acc):
            s = slab(r)
            return jnp.minimum(acc, jnp.where(s >= lo, s, jnp.inf))
        acc = lax.fori_loop(0, slabs, body, jnp.full((RED_ROWS, Q_BLOCK), jnp.inf, F32))
        thr = jnp.min(acc, axis=0, keepdims=True)
        return thr, count(lambda s, r: s > thr)

    def slow_cond(carry):
        it, _, _, _, c_gt = carry
        return jnp.logical_and(it < SLOW_TRIPS, _any_lane(c_gt >= kf))

    def slow_body(carry):
        it, lo, hi, _, _ = carry
        for _ in range(4):
            mid = 0.5 * lo + 0.5 * hi
            ge = count(lambda s, r: s >= mid) >= kf
            lo, hi = jnp.where(ge, mid, lo), jnp.where(ge, hi, mid)
        thr, c_gt = kth(lo)
        return it + 1, lo, hi, thr, c_gt

    thr0, c_gt0 = kth(lo)
    _, _, _, thr, c_gt = lax.while_loop(slow_cond, slow_body, (jnp.int32(0), lo, hi, thr0, c_gt0))
    c_eq = count(lambda s, r: s == thr)
    need = (c_gt + c_eq) > kf

    def tie_step(_, carry):
        lo_i, hi_i = carry
        mid_i = jnp.floor((lo_i + hi_i) * 0.5)
        below = count(lambda s, r: jnp.logical_and(s == thr, (r * RED_ROWS + r_k).astype(F32) <= mid_i))
        ok = (c_gt + below) >= kf
        return jnp.where(ok, lo_i, mid_i), jnp.where(ok, mid_i, hi_i)

    last = jnp.asarray(slabs * RED_ROWS - 1, F32)
    lo_i = jnp.full((1, Q_BLOCK), -1.0, F32)
    hi_i = jnp.zeros((1, Q_BLOCK), F32) + last
    _, cut = lax.fori_loop(0, 13, tie_step, (lo_i, hi_i))
    cut = jnp.where(need, cut, last + 1.0).astype(jnp.int32)

    def write(r, carry):
        s = slab(r)
        krow = r * RED_ROWS + r_k
        sel = jnp.logical_or(s > thr, jnp.logical_and(s == thr, krow <= cut))
        on = jnp.where(r == slabs - 1, -later, 0.0)
        rhs_ref[pl.ds(pl.multiple_of(r * RED_ROWS, RED_ROWS), RED_ROWS), LANES:2 * LANES] = (
            jnp.where(sel, on, -MASK_OFF).astype(BF16))
        return carry

    lax.fori_loop(0, slabs, write, 0)


GROUP = HEADS // 2
UNITS = PAIR * (HEADS // GROUP)
VALUE_ROWS = HEAD_DIM + 16


def _probabilities(nk, qt_ref, qcols, rhs_ref, s_ref, p_ref, between):
    r_k, c_q = _tile_iotas()
    feat = lax.broadcasted_iota(jnp.int32, (HALF, Q_BLOCK), 0)
    group = GROUP
    width = group * Q_BLOCK
    units = [(u, g) for u in range(len(qcols)) for g in range(HEADS // group)]

    def score_operand(u, hh):
        slope = 2.0 ** (-(hh + 1))
        alibi = jnp.where(feat == 0, CHUNK * slope, jnp.where(feat == 1, slope, 0.0)).astype(BF16)
        scaled_ident = jnp.where(r_k == c_q, slope, 0.0).astype(BF16)
        return jnp.concatenate([qt_ref[hh * HEAD_DIM:(hh + 1) * HEAD_DIM, qcols[u]], alibi, scaled_ident], axis=0)

    def scores(n):
        u, g = units[n]
        lhs_t = jnp.concatenate([score_operand(u, g * group + i) for i in range(group)], axis=1)
        m_acc = jnp.full((RED_ROWS, width), -jnp.inf, F32)
        for rows in _row_blocks(nk, DOT_ROWS):
            blk = jnp.dot(rhs_ref[u, rows, :], lhs_t, preferred_element_type=F32)
            s_ref[n % 2, rows, :] = blk
            for sub in range((rows.stop - rows.start) // RED_ROWS):
                m_acc = jnp.maximum(m_acc, blk[sub * RED_ROWS:(sub + 1) * RED_ROWS, :])
        return jnp.max(m_acc, axis=0, keepdims=True)

    def probabilities(n, m):
        for rows in _row_blocks(nk, RED_ROWS):
            p_ref[n, rows, :] = jnp.exp((s_ref[n % 2, rows, :] - m).astype(BF16))

    m = scores(0)
    between()
    for n in range(len(units)):
        m_next = scores(n + 1) if n + 1 < len(units) else None
        probabilities(n, m)
        m = m_next


def _value_chunk(ch, vtb_ref, p_ref, part_ref):
    if isinstance(ch, int):
        n, blk = ch % UNITS, ch // UNITS
        rows = slice(blk * DOT_ROWS, (blk + 1) * DOT_ROWS)
    else:
        n = ch & (UNITS - 1)
        blk = lax.shift_right_logical(ch, UNITS.bit_length() - 1)
        rows = pl.ds(pl.multiple_of(blk * DOT_ROWS, DOT_ROWS), DOT_ROWS)
    part_ref[n, blk] = jnp.dot(vtb_ref[blk], p_ref[n, rows, :], preferred_element_type=F32)


def _head_tiles(part_ref, n_blocks):
    per_block = []
    for u in range(PAIR):
        outs = []
        for g in range(HEADS // GROUP):
            n = u * (HEADS // GROUP) + g
            o = part_ref[n, 0]
            for blk in range(1, part_ref.shape[1]):
                o = o + jnp.where(blk < n_blocks, part_ref[n, blk], 0.0)
            outs.append(o[0:HEAD_DIM, :] * (1.0 / o[HEAD_DIM:HEAD_DIM + 1, :]))
        tiles = []
        for t in range(HEADS // 2):
            o_g = outs[(2 * t) // GROUP]
            a = (2 * t) % GROUP
            tiles.append(jnp.concatenate([o_g[:, a * Q_BLOCK:(a + 1) * Q_BLOCK],
                                          o_g[:, (a + 1) * Q_BLOCK:(a + 2) * Q_BLOCK]], axis=0).T)
        per_block.append(tiles)
    return per_block


def _attn_kernel(qt_ref, qit_ref, wit_ref, kaug_ref, kk_ref, vt_ref, ma_ref, gate_ref, x_ref, gb_ref,
                 wo_ref, y_ref, idx_ref, rhs_ref, s_ref, p_ref, st_ref, vtb_ref, part_ref, done_ref,
                 *, seq, n_pairs, topk):
    g = pl.program_id(0)
    kf = float(topk)
    steps = seq // (PAIR * Q_BLOCK)
    key_blocks = seq // DOT_ROWS
    qcols = [slice(u * Q_BLOCK, (u + 1) * Q_BLOCK) for u in range(PAIR)]
    searching = g < n_pairs
    pending = g >= 1
    jj = lax.rem(g, steps)
    g_prev = jnp.maximum(g - 1, 0)
    slot = lax.div(g, steps) & 1
    slot_prev = lax.div(g_prev, steps) & 1

    @pl.when(g == 0)
    def _():
        p_ref[...] = jnp.zeros(p_ref.shape, BF16)
        vtb_ref[...] = jnp.zeros(vtb_ref.shape, BF16)
        part_ref[...] = jnp.zeros(part_ref.shape, F32)
        y_ref[...] = jnp.zeros(y_ref.shape, F32)

    @pl.when(jnp.logical_and(searching, jj == 0))
    def _():
        for u in range(PAIR):
            rhs_ref[u, :, 0:LANES] = kaug_ref[...]
        ones = jnp.ones((VALUE_ROWS - HEAD_DIM, DOT_ROWS), BF16)
        for blk in range(key_blocks):
            vtb_ref[slot, blk] = jnp.concatenate([vt_ref[:, blk * DOT_ROWS:(blk + 1) * DOT_ROWS], ones], axis=0)

    done_ref[0] = 0

    steps_per_bucket = KEY_BUCKET // (PAIR * Q_BLOCK)

    def buckets(step, active):
        return [((n + 1) * KEY_BUCKET,
                 jnp.logical_and(active, jnp.logical_and(step >= n * steps_per_bucket,
                                                         step < (n + 1) * steps_per_bucket)))
                for n in range(seq // KEY_BUCKET)]

    blocks = [PAIR * jj + u for u in range(PAIR)]
    total = UNITS * (lax.div(lax.rem(g_prev, steps), steps_per_bucket) + 1) * (KEY_BUCKET // DOT_ROWS)
    vtb_prev = vtb_ref.at[slot_prev]

    for nk, here in buckets(jj, searching):
        @pl.when(here)
        def _(nk=nk):
            lane_q = lax.broadcasted_iota(jnp.int32, (1, Q_BLOCK), 1)
            states = []
            for u in range(PAIR):
                n_adm = blocks[u] * Q_BLOCK + CHUNK + CHUNK * (lane_q >= CHUNK).astype(jnp.int32)
                lo, hi = _index_scores(nk, n_adm, qit_ref, qcols[u], wit_ref[:, qcols[u]], kk_ref, s_ref,
                                       idx_ref.at[u])
                n_adm_f = n_adm.astype(F32)
                c_lo = jnp.where(n_adm_f <= kf, kf, n_adm_f)
                states.append((lo, hi, c_lo, jnp.zeros((1, Q_BLOCK), F32)))

            def cond(carry):
                it, flat = carry[0], carry[1:]
                still = jnp.logical_or(flat[2] != kf, flat[6] != kf)
                return jnp.logical_and(it < FAST_TRIPS, _any_lane(still))

            def body(carry):
                it, flat = carry[0], carry[1:]
                st = [flat[0:4], flat[4:8]]
                for _ in range(2):
                    st = [_probe(nk, kf, idx_ref.at[u], st[u]) for u in range(PAIR)]
                for c in range(CHUNKS_PER_TRIP):
                    _value_chunk(jnp.minimum(it * CHUNKS_PER_TRIP + c, total - 1), vtb_prev, p_ref, part_ref)
                return (it + 1,) + tuple(st[0]) + tuple(st[1])

            final = lax.while_loop(cond, body, (jnp.int32(0),) + tuple(states[0]) + tuple(states[1]))
            done_ref[0] = jnp.minimum(final[0] * CHUNKS_PER_TRIP, total)
            for u in range(PAIR):
                lo, hi, c_lo, _ = final[1 + 4 * u:5 + 4 * u]
                _write_selection(nk, blocks[u], lo, idx_ref.at[u], rhs_ref.at[u])
                st_ref[4 * u + 0:4 * u + 1, :] = lo
                st_ref[4 * u + 1:4 * u + 2, :] = hi
                st_ref[4 * u + 2:4 * u + 3, :] = c_lo

    @pl.when(jnp.logical_and(pending, searching))
    def _():
        def chunk(ch, carry):
            _value_chunk(ch, vtb_prev, p_ref, part_ref)
            return carry

        lax.fori_loop(done_ref[0], total, chunk, 0)

    @pl.when(g == n_pairs)
    def _():
        for ch in range(UNITS * key_blocks):
            _value_chunk(ch, vtb_prev, p_ref, part_ref)

    lane = lax.broadcasted_iota(jnp.int32, (PAIR * Q_BLOCK, LANES), 1)
    lo_half = lane < HALF

    def finish_pending():
        per_block = _head_tiles(part_ref, lax.div(total, UNITS))
        mixed = [ma_ref[...]]
        for t in range(HEADS // 2):
            cols = slice(t * LANES, (t + 1) * LANES)
            o = jnp.concatenate([per_block[u][t] for u in range(PAIR)], axis=0)
            msq = _half_mean_sq(o * o, lo_half)
            ob = o * lax.rsqrt(msq + NORM_EPS) * gb_ref[:, cols] * gate_ref[:, cols].astype(F32)
            mixed.append(ob.astype(BF16))
        y_ref[...] = x_ref[...] + jnp.dot(jnp.concatenate(mixed, axis=1), wo_ref[...],
                                          preferred_element_type=F32)

    for u in range(PAIR):
        lo = st_ref[4 * u + 0:4 * u + 1, :]
        hi = st_ref[4 * u + 1:4 * u + 2, :]
        c_lo = st_ref[4 * u + 2:4 * u + 3, :]

        @pl.when(jnp.logical_and(searching, _any_lane(c_lo != kf)))
        def _(u=u, lo=lo, hi=hi):
            _exact_fallback(blocks[u] + 1, kf, lo, hi, idx_ref.at[u], rhs_ref.at[u])

    for nk, here in buckets(jj, searching):
        @pl.when(here)
        def _(nk=nk):
            _probabilities(nk, qt_ref, qcols, rhs_ref, s_ref, p_ref, finish_pending)

    @pl.when(g == n_pairs)
    def _():
        finish_pending()


def kernel(x, norm_gain, w_in, sgu_norm_gain, sgu_w, sgu_b, q_norm_gain, k_norm_gain,
           idx_k_norm_gain, branch_norm_gain, w_out):
    bsz, seq, d_model = x.shape
    assert d_model == D_MODEL and norm_gain.shape[0] == 1
    assert seq % PROJ_ROWS == 0 and seq % KEY_BUCKET == 0 and KEY_BUCKET % (PAIR * Q_BLOCK) == 0
    tokens = bsz * seq
    topk = min(TOPK_MAX, seq // 4)
    idx_w_scale = (IDX_HEADS ** -0.5) * (IDX_DIM ** -0.5)

    w = w_in[0]
    a3 = 3 * A_WIDTH
    w_q = w[:, a3:a3 + B_WIDTH]
    w_k = w[:, a3 + B_WIDTH:a3 + B_WIDTH + HEAD_DIM]
    w_v = w[:, a3 + B_WIDTH + HEAD_DIM:a3 + B_WIDTH + 2 * HEAD_DIM]
    o_g = a3 + B_WIDTH + 2 * HEAD_DIM
    w_g = w[:, o_g:o_g + B_WIDTH]
    o_i = o_g + B_WIDTH
    w_iq = w[:, o_i:o_i + IDX_HEADS * IDX_DIM]
    w_ik = w[:, o_i + IDX_HEADS * IDX_DIM:o_i + IDX_HEADS * IDX_DIM + IDX_DIM]
    w_iw = w[:, o_i + IDX_HEADS * IDX_DIM + IDX_DIM:]
    w_main = jnp.concatenate([w[:, :a3], w_g, w_k, w_ik, w_v, w_iw,
                              jnp.zeros((D_MODEL, LANES - HEAD_DIM - IDX_HEADS), F32)], axis=1).astype(BF16)
    assert w_main.shape[1] == _PACKED_COLS
    w_t = jnp.concatenate([w_q.T, w_iq.T], axis=0).astype(BF16)
    assert w_t.shape[0] == _PACKED_ROWS
    x2 = x.reshape(tokens, D_MODEL)
    ng = norm_gain[0].reshape(1, D_MODEL)
    sgn = sgu_norm_gain[0].reshape(1, A_WIDTH)
    sw = sgu_w[0]
    sb = sgu_b[0].reshape(A_GROUPS, A_BLOCK, 1)
    qg = q_norm_gain[0].reshape(HEAD_DIM, 1)
    kg = jnp.concatenate([k_norm_gain[0], idx_k_norm_gain[0]]).reshape(1, LANES)
    ga = branch_norm_gain[0, :A_WIDTH].reshape(1, A_WIDTH)
    gb = branch_norm_gain[0, A_WIDTH:].reshape(1, B_WIDTH)
    wo = w_out[0].astype(BF16)

    tm = PROJ_ROWS
    full = lambda shape: pl.BlockSpec(shape, lambda i: (0,) * len(shape))
    rows = lambda width: pl.BlockSpec((tm, width), lambda i: (i, 0))
    colsT = lambda height: pl.BlockSpec((height, tm), lambda i: (0, i))
    outs = pl.pallas_call(
        functools.partial(_proj_kernel, tiles_per_seq=seq // tm, idx_w_scale=idx_w_scale),
        grid=(tokens // tm,),
        in_specs=[rows(D_MODEL), full((1, D_MODEL)), full((D_MODEL, _PACKED_COLS)),
                  full((_PACKED_ROWS, D_MODEL)), full((1, A_WIDTH)), full((A_GROUPS, A_BLOCK, A_BLOCK)),
                  full((A_GROUPS, A_BLOCK, 1)), full((HEAD_DIM, 1)), full((1, LANES)),
                  full((1, A_WIDTH))],
        out_specs=[rows(A_WIDTH), rows(B_WIDTH), rows(LANES), rows(LANES),
                   colsT(B_WIDTH), colsT(IDX_HEADS * IDX_DIM), colsT(HEAD_DIM), colsT(IDX_HEADS)],
        out_shape=[jax.ShapeDtypeStruct((tokens, A_WIDTH), BF16),
                   jax.ShapeDtypeStruct((tokens, B_WIDTH), BF16),
                   jax.ShapeDtypeStruct((tokens, LANES), BF16),
                   jax.ShapeDtypeStruct((tokens, LANES), BF16),
                   jax.ShapeDtypeStruct((B_WIDTH, tokens), BF16),
                   jax.ShapeDtypeStruct((IDX_HEADS * IDX_DIM, tokens), BF16),
                   jax.ShapeDtypeStruct((HEAD_DIM, tokens), BF16),
                   jax.ShapeDtypeStruct((IDX_HEADS, tokens), F32)],
        compiler_params=pltpu.CompilerParams(dimension_semantics=("arbitrary",),
                                             vmem_limit_bytes=VMEM_LIMIT),
        name="proj_sgu",
    )(x2, ng, w_main, w_t, sgn, sw, sb, qg, kg, ga)
    ma, gate, kaug, kk, qt, qit, vt, wit = outs

    qrows = PAIR * Q_BLOCK
    steps = seq // qrows
    n_pairs = bsz * steps
    cur = lambda g: jnp.minimum(g, n_pairs - 1)
    qblk = lambda width: pl.BlockSpec((qrows, width), lambda g: (jnp.maximum(g - 1, 0), 0))
    qblkT = lambda height: pl.BlockSpec((height, qrows), lambda g: (0, cur(g)))
    const = lambda shape: pl.BlockSpec(shape, lambda g: (0,) * len(shape))
    y = pl.pallas_call(
        functools.partial(_attn_kernel, seq=seq, n_pairs=n_pairs, topk=topk),
        grid=(n_pairs + 1,),
        in_specs=[qblkT(B_WIDTH), qblkT(IDX_HEADS * IDX_DIM), qblkT(IDX_HEADS),
                  pl.BlockSpec((seq, LANES), lambda g: (cur(g) // steps, 0)),
                  pl.BlockSpec((seq, LANES), lambda g: (cur(g) // steps, 0)),
                  pl.BlockSpec((HEAD_DIM, seq), lambda g: (0, cur(g) // steps)),
                  qblk(A_WIDTH), qblk(B_WIDTH), qblk(D_MODEL),
                  const((1, B_WIDTH)), const((D_MODEL, D_MODEL))],
        out_specs=qblk(D_MODEL),
        out_shape=jax.ShapeDtypeStruct((tokens, D_MODEL), F32),
        scratch_shapes=[pltpu.VMEM((PAIR, seq, Q_BLOCK), F32),
                        pltpu.VMEM((PAIR, seq, 2 * LANES), BF16),
                        pltpu.VMEM((2, seq, GROUP * Q_BLOCK), F32),
                        pltpu.VMEM((UNITS, seq, GROUP * Q_BLOCK), BF16),
                        pltpu.VMEM((8, Q_BLOCK), F32),
                        pltpu.VMEM((2, seq // DOT_ROWS, VALUE_ROWS, DOT_ROWS), BF16),
                        pltpu.VMEM((UNITS, seq // DOT_ROWS, VALUE_ROWS, GROUP * Q_BLOCK), F32),
                        pltpu.SMEM((1,), jnp.int32)],
        compiler_params=pltpu.CompilerParams(dimension_semantics=("arbitrary",),
                                             vmem_limit_bytes=VMEM_LIMIT),
        name="dsa_attn_out",
    )(qt, qit, wit, kaug, kk, vt, ma, gate, x2, gb, wo)
    return y.reshape(bsz, seq, D_MODEL)
```

```python
import functools

import jax
import jax.numpy as jnp
from jax import lax
from jax.experimental import pallas as pl
from jax.experimental.pallas import tpu as pltpu

F32 = jnp.float32
BF16 = jnp.bfloat16

D_MODEL = 1024
CHUNK = 64
A_WIDTH = 512
A_GROUPS = 4
A_BLOCK = 128
HEADS = 8
HEAD_DIM = 64
B_WIDTH = HEADS * HEAD_DIM
IDX_HEADS = 8
IDX_DIM = 64
TOPK_MAX = 256
Q_BLOCK = 128
PAIR = 2
NORM_EPS = 1e-6
MASK_OFF = 1e32
LANES = 128
HALF = LANES // 2
PROJ_ROWS = 1024
DOT_ROWS = 512
RED_ROWS = 128
KEY_BUCKET = 512
PROBES_PER_TRIP = 3
FAST_TRIPS = 16
INTERP_MARGIN = 0.02
SLOW_TRIPS = 70
CHUNKS_PER_TRIP = 3
VMEM_LIMIT = 48 * 1024 * 1024

_OFF_U, _OFF_V, _OFF_Z, _OFF_G, _OFF_K = 0, 512, 1024, 1536, 2048
_PACKED_COLS = 2304
_ROW_Q, _ROW_QI, _PACKED_ROWS = 0, 512, 1024

_NT = (((1,), (1,)), ((), ()))


def _gelu(x):
    c = 0.7978845608028654
    return 0.5 * x * (1.0 + jnp.tanh(c * (x + 0.044715 * (x * x * x))))


def _silu(x):
    return x / (1.0 + jnp.exp(-x))


def _row_blocks(total, size):
    return [slice(r, min(r + size, total)) for r in range(0, total, size)]


def _half_mean_sq(x2, lo_half):
    tot = jnp.sum(x2, axis=-1, keepdims=True)
    lo = jnp.sum(jnp.where(lo_half, x2, 0.0), axis=-1, keepdims=True)
    return jnp.where(lo_half, lo, tot - lo) * (1.0 / HALF)


def _proj_kernel(x_ref, ng_ref, w_ref, wt_ref, sgn_ref, sw_ref, sb_ref, qg_ref, kg_ref, ga_ref,
                 ma_ref, gate_ref, kaug_ref, kk_ref, qt_ref, qit_ref, vt_ref, wit_ref,
                 *, tiles_per_seq, idx_w_scale):
    tm = x_ref.shape[0]
    i = pl.program_id(0)
    x = x_ref[...]
    ms = jnp.mean(x * x, axis=-1, keepdims=True)
    h = (x * lax.rsqrt(ms + NORM_EPS) * ng_ref[...]).astype(BF16)

    lane = lax.broadcasted_iota(jnp.int32, (tm, LANES), 1)
    lo_half = lane < HALF

    def proj(off, width):
        return jnp.dot(h, w_ref[:, off:off + width], preferred_element_type=F32)

    gu = _gelu(proj(_OFF_U, A_WIDTH))
    gv = _gelu(proj(_OFF_V, A_WIDTH))
    pz = proj(_OFF_Z, A_WIDTH)
    r_i = lax.broadcasted_iota(jnp.int32, (A_BLOCK, A_BLOCK), 0)
    c_j = lax.broadcasted_iota(jnp.int32, (A_BLOCK, A_BLOCK), 1)
    causal = lax.shift_right_logical(c_j, 6) <= lax.shift_right_logical(r_i, 6)
    for g in range(A_GROUPS):
        cols = slice(g * LANES, (g + 1) * LANES)
        vg = gv[:, cols]
        mu = jnp.mean(vg, axis=-1, keepdims=True)
        d = vg - mu
        var = jnp.mean(d * d, axis=-1, keepdims=True)
        vn = (d * lax.rsqrt(var + NORM_EPS) * sgn_ref[:, cols]).astype(BF16)
        wg = jnp.where(causal, sw_ref[g], 0.0).astype(BF16)
        side = jnp.concatenate([vn[blk * A_BLOCK:(blk + 1) * A_BLOCK, :] for blk in range(tm // A_BLOCK)], axis=1)
        mixed = jnp.dot(wg, side, preferred_element_type=F32) + sb_ref[g]
        for blk in range(tm // A_BLOCK):
            rows = slice(blk * A_BLOCK, (blk + 1) * A_BLOCK)
            s = mixed[:, blk * LANES:(blk + 1) * LANES]
            ya = gu[rows, cols] * s
            oa = ya * lax.rsqrt(jnp.mean(ya * ya, axis=-1, keepdims=True) + NORM_EPS) * ga_ref[:, cols]
            ma_ref[rows, cols] = (oa * _silu(pz[rows, cols])).astype(BF16)

    gate_ref[...] = _silu(proj(_OFF_G, B_WIDTH)).astype(BF16)

    pkv = proj(_OFF_K, 2 * LANES)
    pk = pkv[:, 0:LANES]
    k_ms = jnp.sum(jnp.where(lo_half, pk * pk, 0.0), axis=-1, keepdims=True) * (1.0 / HALF)
    kn = pk * lax.rsqrt(k_ms + NORM_EPS)
    ik_mu = jnp.sum(jnp.where(lo_half, 0.0, pk), axis=-1, keepdims=True) * (1.0 / HALF)
    dk = pk - ik_mu
    ik_var = jnp.sum(jnp.where(lo_half, 0.0, dk * dk), axis=-1, keepdims=True) * (1.0 / HALF)
    kin = dk * lax.rsqrt(ik_var + NORM_EPS)
    tile = jnp.where(lo_half, kn, kin) * kg_ref[...]
    swapped = pltpu.roll(tile, HALF, axis=1)
    row = lax.broadcasted_iota(jnp.int32, (tm, LANES), 0)
    pos = (i % tiles_per_seq) * tm + row
    pos_hi = lax.shift_right_logical(pos, 6).astype(F32)
    pos_lo = (pos & (CHUNK - 1)).astype(F32)
    posfeat = jnp.where(lane == HALF, pos_hi, jnp.where(lane == HALF + 1, pos_lo, 0.0))
    kaug_ref[...] = jnp.where(lo_half, tile, posfeat).astype(BF16)
    kk_ref[...] = jnp.where(lo_half, swapped, tile).astype(BF16)

    pt = lax.dot_general(wt_ref[...], h, _NT, preferred_element_type=F32)
    for hh in range(HEADS):
        rows = slice(_ROW_Q + hh * HEAD_DIM, _ROW_Q + (hh + 1) * HEAD_DIM)
        xq = pt[rows, :]
        msq = jnp.mean(xq * xq, axis=0, keepdims=True)
        qt_ref[rows, :] = (xq * lax.rsqrt(msq + NORM_EPS) * qg_ref[...] * (HEAD_DIM ** -0.5)).astype(BF16)
    qit_ref[...] = pt[_ROW_QI:_ROW_QI + IDX_HEADS * IDX_DIM, :].astype(BF16)
    for blk in range(tm // LANES):
        cols = slice(blk * LANES, (blk + 1) * LANES)
        vw_t = pkv[cols, LANES:2 * LANES].T
        vt_ref[:, cols] = vw_t[0:HEAD_DIM, :].astype(BF16)
        wit_ref[:, cols] = vw_t[HEAD_DIM:HEAD_DIM + IDX_HEADS, :] * idx_w_scale


def _tile_iotas():
    r_k = lax.broadcasted_iota(jnp.int32, (Q_BLOCK, Q_BLOCK), 0)
    c_q = lax.broadcasted_iota(jnp.int32, (Q_BLOCK, Q_BLOCK), 1)
    return r_k, c_q


def _colsum(x):
    return jnp.sum(x, axis=0, keepdims=True)


def _any_lane(pred):
    return jnp.max(jnp.where(pred, 1.0, 0.0)) > 0.0


def _index_scores(nk, n_adm, qit_ref, qcols, wi, kk_ref, s_ref, idx_ref):
    r_k, c_q = _tile_iotas()
    top_rows = r_k < HALF
    per_head = []
    for t in range(IDX_HEADS // 2):
        qit = qit_ref[t * LANES:(t + 1) * LANES, qcols]
        zero = jnp.zeros_like(qit)
        per_head += [jnp.where(top_rows, qit, zero), jnp.where(top_rows, zero, qit)]
    half_heads = IDX_HEADS // 2
    for g in range(2):
        wg = jnp.concatenate(per_head[g * half_heads:(g + 1) * half_heads], axis=1)
        for rows in _row_blocks(nk, DOT_ROWS):
            s_ref[g, rows, :] = jnp.dot(kk_ref[rows, :], wg, preferred_element_type=F32)

    def weighted_relu(g, rows):
        acc = None
        for i in range(half_heads):
            hh = g * half_heads + i
            term = jnp.maximum(s_ref[g, rows, i * Q_BLOCK:(i + 1) * Q_BLOCK], 0.0) * wi[hh:hh + 1, :]
            acc = term if acc is None else acc + term
        return acc

    slabs = nk // RED_ROWS
    mn_acc = jnp.full((RED_ROWS, Q_BLOCK), jnp.inf, F32)
    mx_acc = jnp.full((RED_ROWS, Q_BLOCK), -jnp.inf, F32)
    tiny_acc = jnp.full((RED_ROWS, Q_BLOCK), jnp.inf, F32)
    for r in range(slabs):
        rows = slice(r * RED_ROWS, (r + 1) * RED_ROWS)
        acc = weighted_relu(0, rows) + weighted_relu(1, rows)
        if r >= slabs - KEY_BUCKET // RED_ROWS:
            adm = (r * RED_ROWS + r_k) < n_adm
            lo_fill = jnp.where(adm, acc, -jnp.inf)
            hi_fill = jnp.where(adm, acc, jnp.inf)
        else:
            lo_fill = hi_fill = acc
        mag = jnp.abs(hi_fill)
        idx_ref[rows, :] = lo_fill
        mn_acc = jnp.minimum(mn_acc, hi_fill)
        mx_acc = jnp.maximum(mx_acc, lo_fill)
        tiny_acc = jnp.minimum(tiny_acc, jnp.where(mag == 0.0, jnp.inf, mag))
    lo = jnp.min(mn_acc, axis=0, keepdims=True)
    hi = jnp.max(mx_acc, axis=0, keepdims=True)
    tiny = jnp.min(tiny_acc, axis=0, keepdims=True)

    unit = jnp.where(tiny < jnp.inf, tiny, 1.0)
    eps = unit * (0.5 / nk)
    rank0 = (1 + r_k).astype(F32)

    def spread(r, carry):
        rows = pl.ds(pl.multiple_of(r * RED_ROWS, RED_ROWS), RED_ROWS)
        s = idx_ref[rows, :]
        rank = rank0 + jnp.asarray(r * RED_ROWS, F32)
        idx_ref[rows, :] = jnp.where(s == 0.0, -(rank * eps), s)
        return carry

    lax.fori_loop(0, slabs, spread, 0, unroll=KEY_BUCKET // RED_ROWS)
    return jnp.minimum(lo, -0.5 * unit), hi


def _probe(nk, kf, idx_ref, state):
    lo, hi, c_lo, c_hi = state
    frac = (c_lo - kf) / jnp.maximum(c_lo - c_hi, 1.0)
    frac = jnp.minimum(jnp.maximum(frac, INTERP_MARGIN), 1.0 - INTERP_MARGIN)
    t = lo + (hi - lo) * frac
    acc = jnp.zeros((RED_ROWS, Q_BLOCK), F32)
    for r in range(nk // RED_ROWS):
        acc = acc + jnp.where(idx_ref[r * RED_ROWS:(r + 1) * RED_ROWS, :] >= t, 1.0, 0.0)
    c = _colsum(acc)
    ge = c >= kf
    return (jnp.where(ge, t, lo), jnp.where(ge, hi, t), jnp.where(ge, c, c_lo), jnp.where(ge, c_hi, c))


def _write_selection(nk, j_blk, lo, idx_ref, rhs_ref):
    r_k, c_q = _tile_iotas()
    later = 2.0 * jnp.maximum(r_k - c_q, 0).astype(F32)

    def write(r, carry):
        rows = pl.ds(pl.multiple_of(r * RED_ROWS, RED_ROWS), RED_ROWS)
        on = jnp.where(r == j_blk, -later, 0.0)
        rhs_ref[rows, LANES:2 * LANES] = jnp.where(idx_ref[rows, :] >= lo, on, -MASK_OFF).astype(BF16)
        return carry

    lax.fori_loop(0, nk // RED_ROWS, write, 0, unroll=KEY_BUCKET // RED_ROWS)


def _exact_fallback(slabs, kf, lo, hi, idx_ref, rhs_ref):
    r_k, c_q = _tile_iotas()
    later = 2.0 * jnp.maximum(r_k - c_q, 0).astype(F32)

    def slab(r):
        return idx_ref[pl.ds(pl.multiple_of(r * RED_ROWS, RED_ROWS), RED_ROWS), :]

    def count(pred):
        def body(r, acc):
            return acc + jnp.where(pred(slab(r), r), 1.0, 0.0)
        return _colsum(lax.fori_loop(0, slabs, body, jnp.zeros((RED_ROWS, Q_BLOCK), F32)))

    def kth(lo):
        def body(r, acc):
            s = slab(r)
            return jnp.minimum(acc, jnp.where(s >= lo, s, jnp.inf))
        acc = lax.fori_loop(0, slabs, body, jnp.full((RED_ROWS, Q_BLOCK), jnp.inf, F32))
        thr = jnp.min(acc, axis=0, keepdims=True)
        return thr, count(lambda s, r: s > thr)

    def slow_cond(carry):
        it, _, _, _, c_gt = carry
        return jnp.logical_and(it < SLOW_TRIPS, _any_lane(c_gt >= kf))

    def slow_body(carry):
        it, lo, hi, _, _ = carry
        for _ in range(4):
            mid = 0.5 * lo + 0.5 * hi
            ge = count(lambda s, r: s >= mid) >= kf
            lo, hi = jnp.where(ge, mid, lo), jnp.where(ge, hi, mid)
        thr, c_gt = kth(lo)
        return it + 1, lo, hi, thr, c_gt

    thr0, c_gt0 = kth(lo)
    _, _, _, thr, c_gt = lax.while_loop(slow_cond, slow_body, (jnp.int32(0), lo, hi, thr0, c_gt0))
    c_eq = count(lambda s, r: s == thr)
    need = (c_gt + c_eq) > kf

    def tie_step(_, carry):
        lo_i, hi_i = carry
        mid_i = jnp.floor((lo_i + hi_i) * 0.5)
        below = count(lambda s, r: jnp.logical_and(s == thr, (r * RED_ROWS + r_k).astype(F32) <= mid_i))
        ok = (c_gt + below) >= kf
        return jnp.where(ok, lo_i, mid_i), jnp.where(ok, mid_i, hi_i)

    last = jnp.asarray(slabs * RED_ROWS - 1, F32)
    lo_i = jnp.full((1, Q_BLOCK), -1.0, F32)
    hi_i = jnp.zeros((1, Q_BLOCK), F32) + last
    _, cut = lax.fori_loop(0, 13, tie_step, (lo_i, hi_i))
    cut = jnp.where(need, cut, last + 1.0).astype(jnp.int32)

    def write(r, carry):
        s = slab(r)
        krow = r * RED_ROWS + r_k
        sel = jnp.logical_or(s > thr, jnp.logical_and(s == thr, krow <= cut))
        on = jnp.where(r == slabs - 1, -later, 0.0)
        rhs_ref[pl.ds(pl.multiple_of(r * RED_ROWS, RED_ROWS), RED_ROWS), LANES:2 * LANES] = (
            jnp.where(sel, on, -MASK_OFF).astype(BF16))
        return carry

    lax.fori_loop(0, slabs, write, 0)


GROUP = HEADS // 2
UNITS = PAIR * (HEADS // GROUP)
VALUE_ROWS = HEAD_DIM + 16


def _probabilities(nk, qt_ref, qcols, rhs_ref, s_ref, p_ref, between):
    r_k, c_q = _tile_iotas()
    feat = lax.broadcasted_iota(jnp.int32, (HALF, Q_BLOCK), 0)
    group = GROUP
    width = group * Q_BLOCK
    units = [(u, g) for u in range(len(qcols)) for g in range(HEADS // group)]

    def score_operand(u, hh):
        slope = 2.0 ** (-(hh + 1))
        alibi = jnp.where(feat == 0, CHUNK * slope, jnp.where(feat == 1, slope, 0.0)).astype(BF16)
        scaled_ident = jnp.where(r_k == c_q, slope, 0.0).astype(BF16)
        return jnp.concatenate([qt_ref[hh * HEAD_DIM:(hh + 1) * HEAD_DIM, qcols[u]], alibi, scaled_ident], axis=0)

    def scores(n):
        u, g = units[n]
        lhs_t = jnp.concatenate([score_operand(u, g * group + i) for i in range(group)], axis=1)
        m_acc = jnp.full((RED_ROWS, width), -jnp.inf, F32)
        for rows in _row_blocks(nk, DOT_ROWS):
            blk = jnp.dot(rhs_ref[u, rows, :], lhs_t, preferred_element_type=F32)
            s_ref[n % 2, rows, :] = blk
            for sub in range((rows.stop - rows.start) // RED_ROWS):
                m_acc = jnp.maximum(m_acc, blk[sub * RED_ROWS:(sub + 1) * RED_ROWS, :])
        return jnp.max(m_acc, axis=0, keepdims=True)

    def probabilities(n, m):
        for rows in _row_blocks(nk, RED_ROWS):
            p_ref[n, rows, :] = jnp.exp((s_ref[n % 2, rows, :] - m).astype(BF16))

    m = scores(0)
    between()
    for n in range(len(units)):
        m_next = scores(n + 1) if n + 1 < len(units) else None
        probabilities(n, m)
        m = m_next


def _value_chunk(ch, vtb_ref, p_ref, part_ref):
    if isinstance(ch, int):
        n, blk = ch % UNITS, ch // UNITS
        rows = slice(blk * DOT_ROWS, (blk + 1) * DOT_ROWS)
    else:
        n = ch & (UNITS - 1)
        blk = lax.shift_right_logical(ch, UNITS.bit_length() - 1)
        rows = pl.ds(pl.multiple_of(blk * DOT_ROWS, DOT_ROWS), DOT_ROWS)
    part_ref[n, blk] = jnp.dot(vtb_ref[blk], p_ref[n, rows, :], preferred_element_type=F32)


def _head_tiles(part_ref, n_blocks):
    per_block = []
    for u in range(PAIR):
        outs = []
        for g in range(HEADS // GROUP):
            n = u * (HEADS // GROUP) + g
            o = part_ref[n, 0]
            for blk in range(1, part_ref.shape[1]):
                o = o + jnp.where(blk < n_blocks, part_ref[n, blk], 0.0)
            outs.append(o[0:HEAD_DIM, :] * (1.0 / o[HEAD_DIM:HEAD_DIM + 1, :]))
        tiles = []
        for t in range(HEADS // 2):
            o_g = outs[(2 * t) // GROUP]
            a = (2 * t) % GROUP
            tiles.append(jnp.concatenate([o_g[:, a * Q_BLOCK:(a + 1) * Q_BLOCK],
                                          o_g[:, (a + 1) * Q_BLOCK:(a + 2) * Q_BLOCK]], axis=0).T)
        per_block.append(tiles)
    return per_block


def _attn_kernel(qt_ref, qit_ref, wit_ref, kaug_ref, kk_ref, vt_ref, ma_ref, gate_ref, x_ref, gb_ref,
                 wo_ref, y_ref, idx_ref, rhs_ref, s_ref, p_ref, st_ref, vtb_ref, part_ref, done_ref,
                 *, seq, n_pairs, topk):
    g = pl.program_id(0)
    kf = float(topk)
    steps = seq // (PAIR * Q_BLOCK)
    key_blocks = seq // DOT_ROWS
    qcols = [slice(u * Q_BLOCK, (u + 1) * Q_BLOCK) for u in range(PAIR)]
    searching = g < n_pairs
    pending = g >= 1
    jj = lax.rem(g, steps)
    g_prev = jnp.maximum(g - 1, 0)
    slot = lax.div(g, steps) & 1
    slot_prev = lax.div(g_prev, steps) & 1

    @pl.when(g == 0)
    def _():
        p_ref[...] = jnp.zeros(p_ref.shape, BF16)
        vtb_ref[...] = jnp.zeros(vtb_ref.shape, BF16)
        part_ref[...] = jnp.zeros(part_ref.shape, F32)
        y_ref[...] = jnp.zeros(y_ref.shape, F32)

    @pl.when(jnp.logical_and(searching, jj == 0))
    def _():
        for u in range(PAIR):
            rhs_ref[u, :, 0:LANES] = kaug_ref[...]
        ones = jnp.ones((VALUE_ROWS - HEAD_DIM, DOT_ROWS), BF16)
        for blk in range(key_blocks):
            vtb_ref[slot, blk] = jnp.concatenate([vt_ref[:, blk * DOT_ROWS:(blk + 1) * DOT_ROWS], ones], axis=0)

    done_ref[0] = 0

    steps_per_bucket = KEY_BUCKET // (PAIR * Q_BLOCK)

    def buckets(step, active):
        return [((n + 1) * KEY_BUCKET,
                 jnp.logical_and(active, jnp.logical_and(step >= n * steps_per_bucket,
                                                         step < (n + 1) * steps_per_bucket)))
                for n in range(seq // KEY_BUCKET)]

    blocks = [PAIR * jj + u for u in range(PAIR)]
    total = UNITS * (lax.div(lax.rem(g_prev, steps), steps_per_bucket) + 1) * (KEY_BUCKET // DOT_ROWS)
    vtb_prev = vtb_ref.at[slot_prev]

    for nk, here in buckets(jj, searching):
        @pl.when(here)
        def _(nk=nk):
            lane_q = lax.broadcasted_iota(jnp.int32, (1, Q_BLOCK), 1)
            states = []
            for u in range(PAIR):
                n_adm = blocks[u] * Q_BLOCK + CHUNK + CHUNK * (lane_q >= CHUNK).astype(jnp.int32)
                lo, hi = _index_scores(nk, n_adm, qit_ref, qcols[u], wit_ref[:, qcols[u]], kk_ref, s_ref,
                                       idx_ref.at[u])
                n_adm_f = n_adm.astype(F32)
                c_lo = jnp.where(n_adm_f <= kf, kf, n_adm_f)
                states.append((lo, hi, c_lo, jnp.zeros((1, Q_BLOCK), F32)))

            def cond(carry):
                it, flat = carry[0], carry[1:]
                still = jnp.logical_or(flat[2] != kf, flat[6] != kf)
                return jnp.logical_and(it < FAST_TRIPS, _any_lane(still))

            def body(carry):
                it, flat = carry[0], carry[1:]
                st = [flat[0:4], flat[4:8]]
                for _ in range(PROBES_PER_TRIP):
                    st = [_probe(nk, kf, idx_ref.at[u], st[u]) for u in range(PAIR)]
                for c in range(CHUNKS_PER_TRIP):
                    _value_chunk(jnp.minimum(it * CHUNKS_PER_TRIP + c, total - 1), vtb_prev, p_ref, part_ref)
                return (it + 1,) + tuple(st[0]) + tuple(st[1])

            final = lax.while_loop(cond, body, (jnp.int32(0),) + tuple(states[0]) + tuple(states[1]))
            done_ref[0] = jnp.minimum(final[0] * CHUNKS_PER_TRIP, total)
            for u in range(PAIR):
                lo, hi, c_lo, _ = final[1 + 4 * u:5 + 4 * u]
                _write_selection(nk, blocks[u], lo, idx_ref.at[u], rhs_ref.at[u])
                st_ref[4 * u + 0:4 * u + 1, :] = lo
                st_ref[4 * u + 1:4 * u + 2, :] = hi
                st_ref[4 * u + 2:4 * u + 3, :] = c_lo

    @pl.when(jnp.logical_and(pending, searching))
    def _():
        def chunk(ch, carry):
            _value_chunk(ch, vtb_prev, p_ref, part_ref)
            return carry

        lax.fori_loop(done_ref[0], total, chunk, 0)

    @pl.when(g == n_pairs)
    def _():
        for ch in range(UNITS * key_blocks):
            _value_chunk(ch, vtb_prev, p_ref, part_ref)

    lane = lax.broadcasted_iota(jnp.int32, (PAIR * Q_BLOCK, LANES), 1)
    lo_half = lane < HALF

    def finish_pending():
        per_block = _head_tiles(part_ref, lax.div(total, UNITS))
        mixed = [ma_ref[...]]
        for t in range(HEADS // 2):
            cols = slice(t * LANES, (t + 1) * LANES)
            o = jnp.concatenate([per_block[u][t] for u in range(PAIR)], axis=0)
            msq = _half_mean_sq(o * o, lo_half)
            ob = o * lax.rsqrt(msq + NORM_EPS) * gb_ref[:, cols] * gate_ref[:, cols].astype(F32)
            mixed.append(ob.astype(BF16))
        y_ref[...] = x_ref[...] + jnp.dot(jnp.concatenate(mixed, axis=1), wo_ref[...],
                                          preferred_element_type=F32)

    for u in range(PAIR):
        lo = st_ref[4 * u + 0:4 * u + 1, :]
        hi = st_ref[4 * u + 1:4 * u + 2, :]
        c_lo = st_ref[4 * u + 2:4 * u + 3, :]

        @pl.when(jnp.logical_and(searching, _any_lane(c_lo != kf)))
        def _(u=u, lo=lo, hi=hi):
            _exact_fallback(blocks[u] + 1, kf, lo, hi, idx_ref.at[u], rhs_ref.at[u])

    for nk, here in buckets(jj, searching):
        @pl.when(here)
        def _(nk=nk):
            _probabilities(nk, qt_ref, qcols, rhs_ref, s_ref, p_ref, finish_pending)

    @pl.when(g == n_pairs)
    def _():
        finish_pending()


def kernel(x, norm_gain, w_in, sgu_norm_gain, sgu_w, sgu_b, q_norm_gain, k_norm_gain,
           idx_k_norm_gain, branch_norm_gain, w_out):
    bsz, seq, d_model = x.shape
    assert d_model == D_MODEL and norm_gain.shape[0] == 1
    assert seq % PROJ_ROWS == 0 and seq % KEY_BUCKET == 0 and KEY_BUCKET % (PAIR * Q_BLOCK) == 0
    tokens = bsz * seq
    topk = min(TOPK_MAX, seq // 4)
    idx_w_scale = (IDX_HEADS ** -0.5) * (IDX_DIM ** -0.5)

    w = w_in[0]
    a3 = 3 * A_WIDTH
    w_q = w[:, a3:a3 + B_WIDTH]
    w_k = w[:, a3 + B_WIDTH:a3 + B_WIDTH + HEAD_DIM]
    w_v = w[:, a3 + B_WIDTH + HEAD_DIM:a3 + B_WIDTH + 2 * HEAD_DIM]
    o_g = a3 + B_WIDTH + 2 * HEAD_DIM
    w_g = w[:, o_g:o_g + B_WIDTH]
    o_i = o_g + B_WIDTH
    w_iq = w[:, o_i:o_i + IDX_HEADS * IDX_DIM]
    w_ik = w[:, o_i + IDX_HEADS * IDX_DIM:o_i + IDX_HEADS * IDX_DIM + IDX_DIM]
    w_iw = w[:, o_i + IDX_HEADS * IDX_DIM + IDX_DIM:]
    w_main = jnp.concatenate([w[:, :a3], w_g, w_k, w_ik, w_v, w_iw,
                              jnp.zeros((D_MODEL, LANES - HEAD_DIM - IDX_HEADS), F32)], axis=1).astype(BF16)
    assert w_main.shape[1] == _PACKED_COLS
    w_t = jnp.concatenate([w_q.T, w_iq.T], axis=0).astype(BF16)
    assert w_t.shape[0] == _PACKED_ROWS
    x2 = x.reshape(tokens, D_MODEL)
    ng = norm_gain[0].reshape(1, D_MODEL)
    sgn = sgu_norm_gain[0].reshape(1, A_WIDTH)
    sw = sgu_w[0]
    sb = sgu_b[0].reshape(A_GROUPS, A_BLOCK, 1)
    qg = q_norm_gain[0].reshape(HEAD_DIM, 1)
    kg = jnp.concatenate([k_norm_gain[0], idx_k_norm_gain[0]]).reshape(1, LANES)
    ga = branch_norm_gain[0, :A_WIDTH].reshape(1, A_WIDTH)
    gb = branch_norm_gain[0, A_WIDTH:].reshape(1, B_WIDTH)
    wo = w_out[0].astype(BF16)

    tm = PROJ_ROWS
    full = lambda shape: pl.BlockSpec(shape, lambda i: (0,) * len(shape))
    rows = lambda width: pl.BlockSpec((tm, width), lambda i: (i, 0))
    colsT = lambda height: pl.BlockSpec((height, tm), lambda i: (0, i))
    outs = pl.pallas_call(
        functools.partial(_proj_kernel, tiles_per_seq=seq // tm, idx_w_scale=idx_w_scale),
        grid=(tokens // tm,),
        in_specs=[rows(D_MODEL), full((1, D_MODEL)), full((D_MODEL, _PACKED_COLS)),
                  full((_PACKED_ROWS, D_MODEL)), full((1, A_WIDTH)), full((A_GROUPS, A_BLOCK, A_BLOCK)),
                  full((A_GROUPS, A_BLOCK, 1)), full((HEAD_DIM, 1)), full((1, LANES)),
                  full((1, A_WIDTH))],
        out_specs=[rows(A_WIDTH), rows(B_WIDTH), rows(LANES), rows(LANES),
                   colsT(B_WIDTH), colsT(IDX_HEADS * IDX_DIM), colsT(HEAD_DIM), colsT(IDX_HEADS)],
        out_shape=[jax.ShapeDtypeStruct((tokens, A_WIDTH), BF16),
                   jax.ShapeDtypeStruct((tokens, B_WIDTH), BF16),
                   jax.ShapeDtypeStruct((tokens, LANES), BF16),
                   jax.ShapeDtypeStruct((tokens, LANES), BF16),
                   jax.ShapeDtypeStruct((B_WIDTH, tokens), BF16),
                   jax.ShapeDtypeStruct((IDX_HEADS * IDX_DIM, tokens), BF16),
                   jax.ShapeDtypeStruct((HEAD_DIM, tokens), BF16),
                   jax.ShapeDtypeStruct((IDX_HEADS, tokens), F32)],
        compiler_params=pltpu.CompilerParams(dimension_semantics=("arbitrary",),
                                             vmem_limit_bytes=VMEM_LIMIT),
        name="proj_sgu",
    )(x2, ng, w_main, w_t, sgn, sw, sb, qg, kg, ga)
    ma, gate, kaug, kk, qt, qit, vt, wit = outs

    qrows = PAIR * Q_BLOCK
    steps = seq // qrows
    n_pairs = bsz * steps
    cur = lambda g: jnp.minimum(g, n_pairs - 1)
    qblk = lambda width: pl.BlockSpec((qrows, width), lambda g: (jnp.maximum(g - 1, 0), 0))
    qblkT = lambda height: pl.BlockSpec((height, qrows), lambda g: (0, cur(g)))
    const = lambda shape: pl.BlockSpec(shape, lambda g: (0,) * len(shape))
    y = pl.pallas_call(
        functools.partial(_attn_kernel, seq=seq, n_pairs=n_pairs, topk=topk),
        grid=(n_pairs + 1,),
        in_specs=[qblkT(B_WIDTH), qblkT(IDX_HEADS * IDX_DIM), qblkT(IDX_HEADS),
                  pl.BlockSpec((seq, LANES), lambda g: (cur(g) // steps, 0)),
                  pl.BlockSpec((seq, LANES), lambda g: (cur(g) // steps, 0)),
                  pl.BlockSpec((HEAD_DIM, seq), lambda g: (0, cur(g) // steps)),
                  qblk(A_WIDTH), qblk(B_WIDTH), qblk(D_MODEL),
                  const((1, B_WIDTH)), const((D_MODEL, D_MODEL))],
        out_specs=qblk(D_MODEL),
        out_shape=jax.ShapeDtypeStruct((tokens, D_MODEL), F32),
        scratch_shapes=[pltpu.VMEM((PAIR, seq, Q_BLOCK), F32),
                        pltpu.VMEM((PAIR, seq, 2 * LANES), BF16),
                        pltpu.VMEM((2, seq, GROUP * Q_BLOCK), F32),
                        pltpu.VMEM((UNITS, seq, GROUP * Q_BLOCK), BF16),
                        pltpu.VMEM((8, Q_BLOCK), F32),
                        pltpu.VMEM((2, seq // DOT_ROWS, VALUE_ROWS, DOT_ROWS), BF16),
                        pltpu.VMEM((UNITS, seq // DOT_ROWS, VALUE_ROWS, GROUP * Q_BLOCK), F32),
                        pltpu.SMEM((1,), jnp.int32)],
        compiler_params=pltpu.CompilerParams(dimension_semantics=("arbitrary",),
                                             vmem_limit_bytes=VMEM_LIMIT),
        name="dsa_attn_out",
    )(qt, qit, wit, kaug, kk, vt, ma, gate, x2, gb, wo)
    return y.reshape(bsz, seq, D_MODEL)
```

```python
import functools

import jax
import jax.numpy as jnp
from jax import lax
from jax.experimental import pallas as pl
from jax.experimental.pallas import tpu as pltpu

F32 = jnp.float32
BF16 = jnp.bfloat16

D_MODEL = 1024
CHUNK = 64
A_WIDTH = 512
A_GROUPS = 4
A_BLOCK = 128
HEADS = 8
HEAD_DIM = 64
B_WIDTH = HEADS * HEAD_DIM
IDX_HEADS = 8
IDX_DIM = 64
TOPK_MAX = 256
Q_BLOCK = 128
PAIR = 2
NORM_EPS = 1e-6
MASK_OFF = 1e32
LANES = 128
HALF = LANES // 2
PROJ_ROWS = 1024
DOT_ROWS = 512
RED_ROWS = 128
KEY_BUCKET = 512
PROBES_PER_TRIP = 3
FAST_TRIPS = 16
INTERP_MARGIN = 0.02
SLOW_TRIPS = 70
CHUNKS_PER_TRIP = 3
VMEM_LIMIT = 48 * 1024 * 1024

_OFF_U, _OFF_V, _OFF_Z, _OFF_G, _OFF_K = 0, 512, 1024, 1536, 2048
_PACKED_COLS = 2304
_ROW_Q, _ROW_QI, _PACKED_ROWS = 0, 512, 1024

_NT = (((1,), (1,)), ((), ()))


def _gelu(x):
    c = 0.7978845608028654
    return 0.5 * x * (1.0 + jnp.tanh(c * (x + 0.044715 * (x * x * x))))


def _silu(x):
    return x / (1.0 + jnp.exp(-x))


def _row_blocks(total, size):
    return [slice(r, min(r + size, total)) for r in range(0, total, size)]


def _half_mean_sq(x2, lo_half):
    tot = jnp.sum(x2, axis=-1, keepdims=True)
    lo = jnp.sum(jnp.where(lo_half, x2, 0.0), axis=-1, keepdims=True)
    return jnp.where(lo_half, lo, tot - lo) * (1.0 / HALF)


def _proj_kernel(x_ref, ng_ref, w_ref, wt_ref, sgn_ref, sw_ref, sb_ref, qg_ref, kg_ref, ga_ref,
                 ma_ref, gate_ref, kaug_ref, kk_ref, qt_ref, qit_ref, vt_ref, wit_ref,
                 *, tiles_per_seq, idx_w_scale):
    tm = x_ref.shape[0]
    i = pl.program_id(0)
    x = x_ref[...]
    ms = jnp.mean(x * x, axis=-1, keepdims=True)
    h = (x * lax.rsqrt(ms + NORM_EPS) * ng_ref[...]).astype(BF16)

    lane = lax.broadcasted_iota(jnp.int32, (tm, LANES), 1)
    lo_half = lane < HALF

    def proj(off, width):
        return jnp.dot(h, w_ref[:, off:off + width], preferred_element_type=F32)

    gu = _gelu(proj(_OFF_U, A_WIDTH))
    gv = _gelu(proj(_OFF_V, A_WIDTH))
    pz = proj(_OFF_Z, A_WIDTH)
    r_i = lax.broadcasted_iota(jnp.int32, (A_BLOCK, A_BLOCK), 0)
    c_j = lax.broadcasted_iota(jnp.int32, (A_BLOCK, A_BLOCK), 1)
    causal = lax.shift_right_logical(c_j, 6) <= lax.shift_right_logical(r_i, 6)
    for g in range(A_GROUPS):
        cols = slice(g * LANES, (g + 1) * LANES)
        vg = gv[:, cols]
        mu = jnp.mean(vg, axis=-1, keepdims=True)
        d = vg - mu
        var = jnp.mean(d * d, axis=-1, keepdims=True)
        vn = (d * lax.rsqrt(var + NORM_EPS) * sgn_ref[:, cols]).astype(BF16)
        wg = jnp.where(causal, sw_ref[g], 0.0).astype(BF16)
        side = jnp.concatenate([vn[blk * A_BLOCK:(blk + 1) * A_BLOCK, :] for blk in range(tm // A_BLOCK)], axis=1)
        mixed = jnp.dot(wg, side, preferred_element_type=F32) + sb_ref[g]
        for blk in range(tm // A_BLOCK):
            rows = slice(blk * A_BLOCK, (blk + 1) * A_BLOCK)
            s = mixed[:, blk * LANES:(blk + 1) * LANES]
            ya = gu[rows, cols] * s
            oa = ya * lax.rsqrt(jnp.mean(ya * ya, axis=-1, keepdims=True) + NORM_EPS) * ga_ref[:, cols]
            ma_ref[rows, cols] = (oa * _silu(pz[rows, cols])).astype(BF16)

    gate_ref[...] = _silu(proj(_OFF_G, B_WIDTH)).astype(BF16)

    pkv = proj(_OFF_K, 2 * LANES)
    pk = pkv[:, 0:LANES]
    k_ms = jnp.sum(jnp.where(lo_half, pk * pk, 0.0), axis=-1, keepdims=True) * (1.0 / HALF)
    kn = pk * lax.rsqrt(k_ms + NORM_EPS)
    ik_mu = jnp.sum(jnp.where(lo_half, 0.0, pk), axis=-1, keepdims=True) * (1.0 / HALF)
    dk = pk - ik_mu
    ik_var = jnp.sum(jnp.where(lo_half, 0.0, dk * dk), axis=-1, keepdims=True) * (1.0 / HALF)
    kin = dk * lax.rsqrt(ik_var + NORM_EPS)
    tile = jnp.where(lo_half, kn, kin) * kg_ref[...]
    swapped = pltpu.roll(tile, HALF, axis=1)
    row = lax.broadcasted_iota(jnp.int32, (tm, LANES), 0)
    pos = (i % tiles_per_seq) * tm + row
    pos_hi = lax.shift_right_logical(pos, 6).astype(F32)
    pos_lo = (pos & (CHUNK - 1)).astype(F32)
    posfeat = jnp.where(lane == HALF, pos_hi, jnp.where(lane == HALF + 1, pos_lo, 0.0))
    kaug_ref[...] = jnp.where(lo_half, tile, posfeat).astype(BF16)
    kk_ref[...] = jnp.where(lo_half, swapped, tile).astype(BF16)

    pt = lax.dot_general(wt_ref[...], h, _NT, preferred_element_type=F32)
    for hh in range(HEADS):
        rows = slice(_ROW_Q + hh * HEAD_DIM, _ROW_Q + (hh + 1) * HEAD_DIM)
        xq = pt[rows, :]
        msq = jnp.mean(xq * xq, axis=0, keepdims=True)
        qt_ref[rows, :] = (xq * lax.rsqrt(msq + NORM_EPS) * qg_ref[...] * (HEAD_DIM ** -0.5)).astype(BF16)
    qit_ref[...] = pt[_ROW_QI:_ROW_QI + IDX_HEADS * IDX_DIM, :].astype(BF16)
    for blk in range(tm // LANES):
        cols = slice(blk * LANES, (blk + 1) * LANES)
        vw_t = pkv[cols, LANES:2 * LANES].T
        vt_ref[:, cols] = vw_t[0:HEAD_DIM, :].astype(BF16)
        wit_ref[:, cols] = vw_t[HEAD_DIM:HEAD_DIM + IDX_HEADS, :] * idx_w_scale


def _tile_iotas():
    r_k = lax.broadcasted_iota(jnp.int32, (Q_BLOCK, Q_BLOCK), 0)
    c_q = lax.broadcasted_iota(jnp.int32, (Q_BLOCK, Q_BLOCK), 1)
    return r_k, c_q


def _colsum(x):
    return jnp.sum(x, axis=0, keepdims=True)


def _any_lane(pred):
    return jnp.max(jnp.where(pred, 1.0, 0.0)) > 0.0


def _index_scores(nk, n_adm, qit_ref, qcols, wi, kk_ref, s_ref, idx_ref):
    r_k, c_q = _tile_iotas()
    top_rows = r_k < HALF
    per_head = []
    for t in range(IDX_HEADS // 2):
        qit = qit_ref[t * LANES:(t + 1) * LANES, qcols]
        zero = jnp.zeros_like(qit)
        per_head += [jnp.where(top_rows, qit, zero), jnp.where(top_rows, zero, qit)]
    half_heads = IDX_HEADS // 2
    for g in range(2):
        wg = jnp.concatenate(per_head[g * half_heads:(g + 1) * half_heads], axis=1)
        for rows in _row_blocks(nk, DOT_ROWS):
            s_ref[g, rows, :] = jnp.dot(kk_ref[rows, :], wg, preferred_element_type=F32)

    def weighted_relu(g, rows):
        acc = None
        for i in range(half_heads):
            hh = g * half_heads + i
            term = jnp.maximum(s_ref[g, rows, i * Q_BLOCK:(i + 1) * Q_BLOCK], 0.0) * wi[hh:hh + 1, :]
            acc = term if acc is None else acc + term
        return acc

    slabs = nk // RED_ROWS
    mn_acc = jnp.full((RED_ROWS, Q_BLOCK), jnp.inf, F32)
    mx_acc = jnp.full((RED_ROWS, Q_BLOCK), -jnp.inf, F32)
    tiny_acc = jnp.full((RED_ROWS, Q_BLOCK), jnp.inf, F32)
    for r in range(slabs):
        rows = slice(r * RED_ROWS, (r + 1) * RED_ROWS)
        acc = weighted_relu(0, rows) + weighted_relu(1, rows)
        if r >= slabs - KEY_BUCKET // RED_ROWS:
            adm = (r * RED_ROWS + r_k) < n_adm
            lo_fill = jnp.where(adm, acc, -jnp.inf)
            hi_fill = jnp.where(adm, acc, jnp.inf)
        else:
            lo_fill = hi_fill = acc
        mag = jnp.abs(hi_fill)
        idx_ref[rows, :] = lo_fill
        mn_acc = jnp.minimum(mn_acc, hi_fill)
        mx_acc = jnp.maximum(mx_acc, lo_fill)
        tiny_acc = jnp.minimum(tiny_acc, jnp.where(mag == 0.0, jnp.inf, mag))
    lo = jnp.min(mn_acc, axis=0, keepdims=True)
    hi = jnp.max(mx_acc, axis=0, keepdims=True)
    tiny = jnp.min(tiny_acc, axis=0, keepdims=True)

    unit = jnp.where(tiny < jnp.inf, tiny, 1.0)
    eps = unit * (0.5 / nk)
    rank0 = (1 + r_k).astype(F32)

    def spread(r, carry):
        rows = pl.ds(pl.multiple_of(r * RED_ROWS, RED_ROWS), RED_ROWS)
        s = idx_ref[rows, :]
        rank = rank0 + jnp.asarray(r * RED_ROWS, F32)
        idx_ref[rows, :] = jnp.where(s == 0.0, -(rank * eps), s)
        return carry

    lax.fori_loop(0, slabs, spread, 0, unroll=KEY_BUCKET // RED_ROWS)
    return jnp.minimum(lo, -0.5 * unit), hi


def _probe(nk, kf, idx_ref, state):
    lo, hi, c_lo, c_hi = state
    frac = (c_lo - kf) / jnp.maximum(c_lo - c_hi, 1.0)
    frac = jnp.minimum(jnp.maximum(frac, INTERP_MARGIN), 1.0 - INTERP_MARGIN)
    t = lo + (hi - lo) * frac
    acc = jnp.zeros((RED_ROWS, Q_BLOCK), F32)
    for r in range(nk // RED_ROWS):
        acc = acc + jnp.where(idx_ref[r * RED_ROWS:(r + 1) * RED_ROWS, :] >= t, 1.0, 0.0)
    c = _colsum(acc)
    ge = c >= kf
    return (jnp.where(ge, t, lo), jnp.where(ge, hi, t), jnp.where(ge, c, c_lo), jnp.where(ge, c_hi, c))


def _write_selection(nk, j_blk, lo, idx_ref, rhs_ref):
    r_k, c_q = _tile_iotas()
    later = 2.0 * jnp.maximum(r_k - c_q, 0).astype(F32)

    def write(r, carry):
        rows = pl.ds(pl.multiple_of(r * RED_ROWS, RED_ROWS), RED_ROWS)
        on = jnp.where(r == j_blk, -later, 0.0)
        rhs_ref[rows, LANES:2 * LANES] = jnp.where(idx_ref[rows, :] >= lo, on, -MASK_OFF).astype(BF16)
        return carry

    lax.fori_loop(0, nk // RED_ROWS, write, 0, unroll=KEY_BUCKET // RED_ROWS)


def _exact_fallback(slabs, kf, lo, hi, idx_ref, rhs_ref):
    r_k, c_q = _tile_iotas()
    later = 2.0 * jnp.maximum(r_k - c_q, 0).astype(F32)

    def slab(r):
        return idx_ref[pl.ds(pl.multiple_of(r * RED_ROWS, RED_ROWS), RED_ROWS), :]

    def count(pred):
        def body(r, acc):
            return acc + jnp.where(pred(slab(r), r), 1.0, 0.0)
        return _colsum(lax.fori_loop(0, slabs, body, jnp.zeros((RED_ROWS, Q_BLOCK), F32)))

    def kth(lo):
        def body(r, acc):
            s = slab(r)
            return jnp.minimum(acc, jnp.where(s >= lo, s, jnp.inf))
        acc = lax.fori_loop(0, slabs, body, jnp.full((RED_ROWS, Q_BLOCK), jnp.inf, F32))
        thr = jnp.min(acc, axis=0, keepdims=True)
        return thr, count(lambda s, r: s > thr)

    def slow_cond(carry):
        it, _, _, _, c_gt = carry
        return jnp.logical_and(it < SLOW_TRIPS, _any_lane(c_gt >= kf))

    def slow_body(carry):
        it, lo, hi, _, _ = carry
        for _ in range(4):
            mid = 0.5 * lo + 0.5 * hi
            ge = count(lambda s, r: s >= mid) >= kf
            lo, hi = jnp.where(ge, mid, lo), jnp.where(ge, hi, mid)
        thr, c_gt = kth(lo)
        return it + 1, lo, hi, thr, c_gt

    thr0, c_gt0 = kth(lo)
    _, _, _, thr, c_gt = lax.while_loop(slow_cond, slow_body, (jnp.int32(0), lo, hi, thr0, c_gt0))
    c_eq = count(lambda s, r: s == thr)
    need = (c_gt + c_eq) > kf

    def tie_step(_, carry):
        lo_i, hi_i = carry
        mid_i = jnp.floor((lo_i + hi_i) * 0.5)
        below = count(lambda s, r: jnp.logical_and(s == thr, (r * RED_ROWS + r_k).astype(F32) <= mid_i))
        ok = (c_gt + below) >= kf
        return jnp.where(ok, lo_i, mid_i), jnp.where(ok, mid_i, hi_i)

    last = jnp.asarray(slabs * RED_ROWS - 1, F32)
    lo_i = jnp.full((1, Q_BLOCK), -1.0, F32)
    hi_i = jnp.zeros((1, Q_BLOCK), F32) + last
    _, cut = lax.fori_loop(0, 13, tie_step, (lo_i, hi_i))
    cut = jnp.where(need, cut, last + 1.0).astype(jnp.int32)

    def write(r, carry):
        s = slab(r)
        krow = r * RED_ROWS + r_k
        sel = jnp.logical_or(s > thr, jnp.logical_and(s == thr, krow <= cut))
        on = jnp.where(r == slabs - 1, -later, 0.0)
        rhs_ref[pl.ds(pl.multiple_of(r * RED_ROWS, RED_ROWS), RED_ROWS), LANES:2 * LANES] = (
            jnp.where(sel, on, -MASK_OFF).astype(BF16))
        return carry

    lax.fori_loop(0, slabs, write, 0)


GROUP = HEADS // 2
UNITS = PAIR * (HEADS // GROUP)
VALUE_ROWS = HEAD_DIM + 16


def _probabilities(nk, qt_ref, qcols, rhs_ref, s_ref, p_ref, between):
    r_k, c_q = _tile_iotas()
    feat = lax.broadcasted_iota(jnp.int32, (HALF, Q_BLOCK), 0)
    group = GROUP
    width = group * Q_BLOCK
    units = [(u, g) for u in range(len(qcols)) for g in range(HEADS // group)]

    def score_operand(u, hh):
        slope = 2.0 ** (-(hh + 1))
        alibi = jnp.where(feat == 0, CHUNK * slope, jnp.where(feat == 1, slope, 0.0)).astype(BF16)
        scaled_ident = jnp.where(r_k == c_q, slope, 0.0).astype(BF16)
        return jnp.concatenate([qt_ref[hh * HEAD_DIM:(hh + 1) * HEAD_DIM, qcols[u]], alibi, scaled_ident], axis=0)

    def scores(n):
        u, g = units[n]
        lhs_t = jnp.concatenate([score_operand(u, g * group + i) for i in range(group)], axis=1)
        m_acc = jnp.full((RED_ROWS, width), -jnp.inf, F32)
        for rows in _row_blocks(nk, DOT_ROWS):
            blk = jnp.dot(rhs_ref[u, rows, :], lhs_t, preferred_element_type=F32)
            s_ref[n % 2, rows, :] = blk
            for sub in range((rows.stop - rows.start) // RED_ROWS):
                m_acc = jnp.maximum(m_acc, blk[sub * RED_ROWS:(sub + 1) * RED_ROWS, :])
        return jnp.max(m_acc, axis=0, keepdims=True)

    def probabilities(n, m):
        for rows in _row_blocks(nk, RED_ROWS):
            p_ref[n, rows, :] = jnp.exp((s_ref[n % 2, rows, :] - m).astype(BF16))

    m = scores(0)
    between()
    for n in range(len(units)):
        m_next = scores(n + 1) if n + 1 < len(units) else None
        probabilities(n, m)
        m = m_next


def _value_chunk(ch, vtb_ref, p_ref, part_ref):
    if isinstance(ch, int):
        n, blk = ch % UNITS, ch // UNITS
        rows = slice(blk * DOT_ROWS, (blk + 1) * DOT_ROWS)
    else:
        n = ch & (UNITS - 1)
        blk = lax.shift_right_logical(ch, UNITS.bit_length() - 1)
        rows = pl.ds(pl.multiple_of(blk * DOT_ROWS, DOT_ROWS), DOT_ROWS)
    part_ref[n, blk] = jnp.dot(vtb_ref[blk], p_ref[n, rows, :], preferred_element_type=F32)


def _head_tiles(part_ref, n_blocks):
    per_block = []
    for u in range(PAIR):
        outs = []
        for g in range(HEADS // GROUP):
            n = u * (HEADS // GROUP) + g
            o = part_ref[n, 0]
            for blk in range(1, part_ref.shape[1]):
                o = o + jnp.where(blk < n_blocks, part_ref[n, blk], 0.0)
            outs.append(o[0:HEAD_DIM, :] * (1.0 / o[HEAD_DIM:HEAD_DIM + 1, :]))
        tiles = []
        for t in range(HEADS // 2):
            o_g = outs[(2 * t) // GROUP]
            a = (2 * t) % GROUP
            tiles.append(jnp.concatenate([o_g[:, a * Q_BLOCK:(a + 1) * Q_BLOCK],
                                          o_g[:, (a + 1) * Q_BLOCK:(a + 2) * Q_BLOCK]], axis=0).T)
        per_block.append(tiles)
    return per_block


def _attn_kernel(qt_ref, qit_ref, wit_ref, kaug_ref, kk_ref, vt_ref, ma_ref, gate_ref, x_ref, gb_ref,
                 wo_ref, y_ref, idx_ref, rhs_ref, s_ref, p_ref, st_ref, vtb_ref, part_ref, done_ref,
                 *, seq, n_pairs, topk):
    g = pl.program_id(0)
    kf = float(topk)
    steps = seq // (PAIR * Q_BLOCK)
    key_blocks = seq // DOT_ROWS
    qcols = [slice(u * Q_BLOCK, (u + 1) * Q_BLOCK) for u in range(PAIR)]
    searching = g < n_pairs
    pending = g >= 1
    jj = lax.rem(g, steps)
    g_prev = jnp.maximum(g - 1, 0)
    slot = lax.div(g, steps) & 1
    slot_prev = lax.div(g_prev, steps) & 1

    @pl.when(g == 0)
    def _():
        p_ref[...] = jnp.zeros(p_ref.shape, BF16)
        vtb_ref[...] = jnp.zeros(vtb_ref.shape, BF16)
        part_ref[...] = jnp.zeros(part_ref.shape, F32)
        y_ref[...] = jnp.zeros(y_ref.shape, F32)

    @pl.when(jnp.logical_and(searching, jj == 0))
    def _():
        for u in range(PAIR):
            rhs_ref[u, :, 0:LANES] = kaug_ref[...]
        ones = jnp.ones((VALUE_ROWS - HEAD_DIM, DOT_ROWS), BF16)
        for blk in range(key_blocks):
            vtb_ref[slot, blk] = jnp.concatenate([vt_ref[:, blk * DOT_ROWS:(blk + 1) * DOT_ROWS], ones], axis=0)

    done_ref[0] = 0
    done_ref[1] = 0

    steps_per_bucket = KEY_BUCKET // (PAIR * Q_BLOCK)

    def buckets(step, active):
        return [((n + 1) * KEY_BUCKET,
                 jnp.logical_and(active, jnp.logical_and(step >= n * steps_per_bucket,
                                                         step < (n + 1) * steps_per_bucket)))
                for n in range(seq // KEY_BUCKET)]

    blocks = [PAIR * jj + u for u in range(PAIR)]
    total = UNITS * (lax.div(lax.rem(g_prev, steps), steps_per_bucket) + 1) * (KEY_BUCKET // DOT_ROWS)
    vtb_prev = vtb_ref.at[slot_prev]

    for nk, here in buckets(jj, searching):
        @pl.when(here)
        def _(nk=nk):
            lane_q = lax.broadcasted_iota(jnp.int32, (1, Q_BLOCK), 1)
            states = []
            for u in range(PAIR):
                n_adm = blocks[u] * Q_BLOCK + CHUNK + CHUNK * (lane_q >= CHUNK).astype(jnp.int32)
                lo, hi = _index_scores(nk, n_adm, qit_ref, qcols[u], wit_ref[:, qcols[u]], kk_ref, s_ref,
                                       idx_ref.at[u])
                n_adm_f = n_adm.astype(F32)
                c_lo = jnp.where(n_adm_f <= kf, kf, n_adm_f)
                states.append((lo, hi, c_lo, jnp.zeros((1, Q_BLOCK), F32)))

            def cond(carry):
                it, flat = carry[0], carry[1:]
                still = jnp.logical_or(flat[2] != kf, flat[6] != kf)
                return jnp.logical_and(it < FAST_TRIPS, _any_lane(still))

            def body(carry):
                it, flat = carry[0], carry[1:]
                st = [flat[0:4], flat[4:8]]
                for _ in range(PROBES_PER_TRIP):
                    st = [_probe(nk, kf, idx_ref.at[u], st[u]) for u in range(PAIR)]
                for c in range(CHUNKS_PER_TRIP):
                    _value_chunk(jnp.minimum(it * CHUNKS_PER_TRIP + c, total - 1), vtb_prev, p_ref, part_ref)
                return (it + 1,) + tuple(st[0]) + tuple(st[1])

            final = lax.while_loop(cond, body, (jnp.int32(0),) + tuple(states[0]) + tuple(states[1]))
            done_ref[0] = jnp.minimum(final[0] * CHUNKS_PER_TRIP, total)
            done_ref[1] = final[0]
            for u in range(PAIR):
                lo, hi, c_lo, _ = final[1 + 4 * u:5 + 4 * u]
                _write_selection(nk, blocks[u], lo, idx_ref.at[u], rhs_ref.at[u])
                st_ref[4 * u + 0:4 * u + 1, :] = lo
                st_ref[4 * u + 1:4 * u + 2, :] = hi
                st_ref[4 * u + 2:4 * u + 3, :] = c_lo

    @pl.when(jnp.logical_and(pending, searching))
    def _():
        def chunk(ch, carry):
            _value_chunk(ch, vtb_prev, p_ref, part_ref)
            return carry

        lax.fori_loop(done_ref[0], total, chunk, 0)

    @pl.when(g == n_pairs)
    def _():
        for ch in range(UNITS * key_blocks):
            _value_chunk(ch, vtb_prev, p_ref, part_ref)

    lane = lax.broadcasted_iota(jnp.int32, (PAIR * Q_BLOCK, LANES), 1)
    lo_half = lane < HALF

    def finish_pending():
        per_block = _head_tiles(part_ref, lax.div(total, UNITS))
        mixed = [ma_ref[...]]
        for t in range(HEADS // 2):
            cols = slice(t * LANES, (t + 1) * LANES)
            o = jnp.concatenate([per_block[u][t] for u in range(PAIR)], axis=0)
            msq = _half_mean_sq(o * o, lo_half)
            ob = o * lax.rsqrt(msq + NORM_EPS) * gb_ref[:, cols] * gate_ref[:, cols].astype(F32)
            mixed.append(ob.astype(BF16))
        y_ref[...] = x_ref[...] + jnp.dot(jnp.concatenate(mixed, axis=1), wo_ref[...],
                                          preferred_element_type=F32)

    @pl.when(jnp.logical_and(searching, done_ref[1] >= FAST_TRIPS))
    def _():
        for u in range(PAIR):
            lo = st_ref[4 * u + 0:4 * u + 1, :]
            hi = st_ref[4 * u + 1:4 * u + 2, :]
            c_lo = st_ref[4 * u + 2:4 * u + 3, :]

            @pl.when(_any_lane(c_lo != kf))
            def _(u=u, lo=lo, hi=hi):
                _exact_fallback(blocks[u] + 1, kf, lo, hi, idx_ref.at[u], rhs_ref.at[u])

    for nk, here in buckets(jj, searching):
        @pl.when(here)
        def _(nk=nk):
            _probabilities(nk, qt_ref, qcols, rhs_ref, s_ref, p_ref, finish_pending)

    @pl.when(g == n_pairs)
    def _():
        finish_pending()


def kernel(x, norm_gain, w_in, sgu_norm_gain, sgu_w, sgu_b, q_norm_gain, k_norm_gain,
           idx_k_norm_gain, branch_norm_gain, w_out):
    bsz, seq, d_model = x.shape
    assert d_model == D_MODEL and norm_gain.shape[0] == 1
    assert seq % PROJ_ROWS == 0 and seq % KEY_BUCKET == 0 and KEY_BUCKET % (PAIR * Q_BLOCK) == 0
    tokens = bsz * seq
    topk = min(TOPK_MAX, seq // 4)
    idx_w_scale = (IDX_HEADS ** -0.5) * (IDX_DIM ** -0.5)

    w = w_in[0]
    a3 = 3 * A_WIDTH
    w_q = w[:, a3:a3 + B_WIDTH]
    w_k = w[:, a3 + B_WIDTH:a3 + B_WIDTH + HEAD_DIM]
    w_v = w[:, a3 + B_WIDTH + HEAD_DIM:a3 + B_WIDTH + 2 * HEAD_DIM]
    o_g = a3 + B_WIDTH + 2 * HEAD_DIM
    w_g = w[:, o_g:o_g + B_WIDTH]
    o_i = o_g + B_WIDTH
    w_iq = w[:, o_i:o_i + IDX_HEADS * IDX_DIM]
    w_ik = w[:, o_i + IDX_HEADS * IDX_DIM:o_i + IDX_HEADS * IDX_DIM + IDX_DIM]
    w_iw = w[:, o_i + IDX_HEADS * IDX_DIM + IDX_DIM:]
    w_main = jnp.concatenate([w[:, :a3], w_g, w_k, w_ik, w_v, w_iw,
                              jnp.zeros((D_MODEL, LANES - HEAD_DIM - IDX_HEADS), F32)], axis=1).astype(BF16)
    assert w_main.shape[1] == _PACKED_COLS
    w_t = jnp.concatenate([w_q.T, w_iq.T], axis=0).astype(BF16)
    assert w_t.shape[0] == _PACKED_ROWS
    x2 = x.reshape(tokens, D_MODEL)
    ng = norm_gain[0].reshape(1, D_MODEL)
    sgn = sgu_norm_gain[0].reshape(1, A_WIDTH)
    sw = sgu_w[0]
    sb = sgu_b[0].reshape(A_GROUPS, A_BLOCK, 1)
    qg = q_norm_gain[0].reshape(HEAD_DIM, 1)
    kg = jnp.concatenate([k_norm_gain[0], idx_k_norm_gain[0]]).reshape(1, LANES)
    ga = branch_norm_gain[0, :A_WIDTH].reshape(1, A_WIDTH)
    gb = branch_norm_gain[0, A_WIDTH:].reshape(1, B_WIDTH)
    wo = w_out[0].astype(BF16)

    tm = PROJ_ROWS
    full = lambda shape: pl.BlockSpec(shape, lambda i: (0,) * len(shape))
    rows = lambda width: pl.BlockSpec((tm, width), lambda i: (i, 0))
    colsT = lambda height: pl.BlockSpec((height, tm), lambda i: (0, i))
    outs = pl.pallas_call(
        functools.partial(_proj_kernel, tiles_per_seq=seq // tm, idx_w_scale=idx_w_scale),
        grid=(tokens // tm,),
        in_specs=[rows(D_MODEL), full((1, D_MODEL)), full((D_MODEL, _PACKED_COLS)),
                  full((_PACKED_ROWS, D_MODEL)), full((1, A_WIDTH)), full((A_GROUPS, A_BLOCK, A_BLOCK)),
                  full((A_GROUPS, A_BLOCK, 1)), full((HEAD_DIM, 1)), full((1, LANES)),
                  full((1, A_WIDTH))],
        out_specs=[rows(A_WIDTH), rows(B_WIDTH), rows(LANES), rows(LANES),
                   colsT(B_WIDTH), colsT(IDX_HEADS * IDX_DIM), colsT(HEAD_DIM), colsT(IDX_HEADS)],
        out_shape=[jax.ShapeDtypeStruct((tokens, A_WIDTH), BF16),
                   jax.ShapeDtypeStruct((tokens, B_WIDTH), BF16),
                   jax.ShapeDtypeStruct((tokens, LANES), BF16),
                   jax.ShapeDtypeStruct((tokens, LANES), BF16),
                   jax.ShapeDtypeStruct((B_WIDTH, tokens), BF16),
                   jax.ShapeDtypeStruct((IDX_HEADS * IDX_DIM, tokens), BF16),
                   jax.ShapeDtypeStruct((HEAD_DIM, tokens), BF16),
                   jax.ShapeDtypeStruct((IDX_HEADS, tokens), F32)],
        compiler_params=pltpu.CompilerParams(dimension_semantics=("arbitrary",),
                                             vmem_limit_bytes=VMEM_LIMIT),
        name="proj_sgu",
    )(x2, ng, w_main, w_t, sgn, sw, sb, qg, kg, ga)
    ma, gate, kaug, kk, qt, qit, vt, wit = outs

    qrows = PAIR * Q_BLOCK
    steps = seq // qrows
    n_pairs = bsz * steps
    cur = lambda g: jnp.minimum(g, n_pairs - 1)
    qblk = lambda width: pl.BlockSpec((qrows, width), lambda g: (jnp.maximum(g - 1, 0), 0))
    qblkT = lambda height: pl.BlockSpec((height, qrows), lambda g: (0, cur(g)))
    const = lambda shape: pl.BlockSpec(shape, lambda g: (0,) * len(shape))
    y = pl.pallas_call(
        functools.partial(_attn_kernel, seq=seq, n_pairs=n_pairs, topk=topk),
        grid=(n_pairs + 1,),
        in_specs=[qblkT(B_WIDTH), qblkT(IDX_HEADS * IDX_DIM), qblkT(IDX_HEADS),
                  pl.BlockSpec((seq, LANES), lambda g: (cur(g) // steps, 0)),
                  pl.BlockSpec((seq, LANES), lambda g: (cur(g) // steps, 0)),
                  pl.BlockSpec((HEAD_DIM, seq), lambda g: (0, cur(g) // steps)),
                  qblk(A_WIDTH), qblk(B_WIDTH), qblk(D_MODEL),
                  const((1, B_WIDTH)), const((D_MODEL, D_MODEL))],
        out_specs=qblk(D_MODEL),
        out_shape=jax.ShapeDtypeStruct((tokens, D_MODEL), F32),
        scratch_shapes=[pltpu.VMEM((PAIR, seq, Q_BLOCK), F32),
                        pltpu.VMEM((PAIR, seq, 2 * LANES), BF16),
                        pltpu.VMEM((2, seq, GROUP * Q_BLOCK), F32),
                        pltpu.VMEM((UNITS, seq, GROUP * Q_BLOCK), BF16),
                        pltpu.VMEM((8, Q_BLOCK), F32),
                        pltpu.VMEM((2, seq // DOT_ROWS, VALUE_ROWS, DOT_ROWS), BF16),
                        pltpu.VMEM((UNITS, seq // DOT_ROWS, VALUE_ROWS, GROUP * Q_BLOCK), F32),
                        pltpu.SMEM((2,), jnp.int32)],
        compiler_params=pltpu.CompilerParams(dimension_semantics=("arbitrary",),
                                             vmem_limit_bytes=VMEM_LIMIT),
        name="dsa_attn_out",
    )(qt, qit, wit, kaug, kk, vt, ma, gate, x2, gb, wo)
    return y.reshape(bsz, seq, D_MODEL)
```

```python
import functools

import jax
import jax.numpy as jnp
from jax import lax
from jax.experimental import pallas as pl
from jax.experimental.pallas import tpu as pltpu

F32 = jnp.float32
BF16 = jnp.bfloat16

D_MODEL = 1024
CHUNK = 64
A_WIDTH = 512
A_GROUPS = 4
A_BLOCK = 128
HEADS = 8
HEAD_DIM = 64
B_WIDTH = HEADS * HEAD_DIM
IDX_HEADS = 8
IDX_DIM = 64
TOPK_MAX = 256
Q_BLOCK = 128
PAIR = 2
NORM_EPS = 1e-6
MASK_OFF = 1e32
LANES = 128
HALF = LANES // 2
PROJ_ROWS = 1024
DOT_ROWS = 512
RED_ROWS = 128
KEY_BUCKET = 512
PROBES_PER_TRIP = 3
FAST_TRIPS = 16
UNTESTED_TRIPS = 4
INTERP_MARGIN = 0.02
SLOW_TRIPS = 70
CHUNKS_PER_TRIP = 3
VMEM_LIMIT = 48 * 1024 * 1024

_OFF_U, _OFF_V, _OFF_Z, _OFF_G, _OFF_K = 0, 512, 1024, 1536, 2048
_PACKED_COLS = 2304
_ROW_Q, _ROW_QI, _PACKED_ROWS = 0, 512, 1024

_NT = (((1,), (1,)), ((), ()))


def _gelu(x):
    c = 0.7978845608028654
    return 0.5 * x * (1.0 + jnp.tanh(c * (x + 0.044715 * (x * x * x))))


def _silu(x):
    return x / (1.0 + jnp.exp(-x))


def _row_blocks(total, size):
    return [slice(r, min(r + size, total)) for r in range(0, total, size)]


def _half_mean_sq(x2, lo_half):
    tot = jnp.sum(x2, axis=-1, keepdims=True)
    lo = jnp.sum(jnp.where(lo_half, x2, 0.0), axis=-1, keepdims=True)
    return jnp.where(lo_half, lo, tot - lo) * (1.0 / HALF)


def _proj_kernel(x_ref, ng_ref, w_ref, wt_ref, sgn_ref, sw_ref, sb_ref, qg_ref, kg_ref, ga_ref,
                 ma_ref, gate_ref, kaug_ref, kk_ref, qt_ref, qit_ref, vt_ref, wit_ref,
                 *, tiles_per_seq, idx_w_scale):
    tm = x_ref.shape[0]
    i = pl.program_id(0)
    x = x_ref[...]
    ms = jnp.mean(x * x, axis=-1, keepdims=True)
    h = (x * lax.rsqrt(ms + NORM_EPS) * ng_ref[...]).astype(BF16)

    lane = lax.broadcasted_iota(jnp.int32, (tm, LANES), 1)
    lo_half = lane < HALF

    def proj(off, width):
        return jnp.dot(h, w_ref[:, off:off + width], preferred_element_type=F32)

    gu = _gelu(proj(_OFF_U, A_WIDTH))
    gv = _gelu(proj(_OFF_V, A_WIDTH))
    pz = proj(_OFF_Z, A_WIDTH)
    r_i = lax.broadcasted_iota(jnp.int32, (A_BLOCK, A_BLOCK), 0)
    c_j = lax.broadcasted_iota(jnp.int32, (A_BLOCK, A_BLOCK), 1)
    causal = lax.shift_right_logical(c_j, 6) <= lax.shift_right_logical(r_i, 6)
    for g in range(A_GROUPS):
        cols = slice(g * LANES, (g + 1) * LANES)
        vg = gv[:, cols]
        mu = jnp.mean(vg, axis=-1, keepdims=True)
        d = vg - mu
        var = jnp.mean(d * d, axis=-1, keepdims=True)
        vn = (d * lax.rsqrt(var + NORM_EPS) * sgn_ref[:, cols]).astype(BF16)
        wg = jnp.where(causal, sw_ref[g], 0.0).astype(BF16)
        side = jnp.concatenate([vn[blk * A_BLOCK:(blk + 1) * A_BLOCK, :] for blk in range(tm // A_BLOCK)], axis=1)
        mixed = jnp.dot(wg, side, preferred_element_type=F32) + sb_ref[g]
        for blk in range(tm // A_BLOCK):
            rows = slice(blk * A_BLOCK, (blk + 1) * A_BLOCK)
            s = mixed[:, blk * LANES:(blk + 1) * LANES]
            ya = gu[rows, cols] * s
            oa = ya * lax.rsqrt(jnp.mean(ya * ya, axis=-1, keepdims=True) + NORM_EPS) * ga_ref[:, cols]
            ma_ref[rows, cols] = (oa * _silu(pz[rows, cols])).astype(BF16)

    gate_ref[...] = _silu(proj(_OFF_G, B_WIDTH)).astype(BF16)

    pkv = proj(_OFF_K, 2 * LANES)
    pk = pkv[:, 0:LANES]
    k_ms = jnp.sum(jnp.where(lo_half, pk * pk, 0.0), axis=-1, keepdims=True) * (1.0 / HALF)
    kn = pk * lax.rsqrt(k_ms + NORM_EPS)
    ik_mu = jnp.sum(jnp.where(lo_half, 0.0, pk), axis=-1, keepdims=True) * (1.0 / HALF)
    dk = pk - ik_mu
    ik_var = jnp.sum(jnp.where(lo_half, 0.0, dk * dk), axis=-1, keepdims=True) * (1.0 / HALF)
    kin = dk * lax.rsqrt(ik_var + NORM_EPS)
    tile = jnp.where(lo_half, kn, kin) * kg_ref[...]
    swapped = pltpu.roll(tile, HALF, axis=1)
    row = lax.broadcasted_iota(jnp.int32, (tm, LANES), 0)
    pos = (i % tiles_per_seq) * tm + row
    pos_hi = lax.shift_right_logical(pos, 6).astype(F32)
    pos_lo = (pos & (CHUNK - 1)).astype(F32)
    posfeat = jnp.where(lane == HALF, pos_hi, jnp.where(lane == HALF + 1, pos_lo, 0.0))
    kaug_ref[...] = jnp.where(lo_half, tile, posfeat).astype(BF16)
    kk_ref[...] = jnp.where(lo_half, swapped, tile).astype(BF16)

    pt = lax.dot_general(wt_ref[...], h, _NT, preferred_element_type=F32)
    for hh in range(HEADS):
        rows = slice(_ROW_Q + hh * HEAD_DIM, _ROW_Q + (hh + 1) * HEAD_DIM)
        xq = pt[rows, :]
        msq = jnp.mean(xq * xq, axis=0, keepdims=True)
        qt_ref[rows, :] = (xq * lax.rsqrt(msq + NORM_EPS) * qg_ref[...] * (HEAD_DIM ** -0.5)).astype(BF16)
    qit_ref[...] = pt[_ROW_QI:_ROW_QI + IDX_HEADS * IDX_DIM, :].astype(BF16)
    for blk in range(tm // LANES):
        cols = slice(blk * LANES, (blk + 1) * LANES)
        vw_t = pkv[cols, LANES:2 * LANES].T
        vt_ref[:, cols] = vw_t[0:HEAD_DIM, :].astype(BF16)
        wit_ref[:, cols] = vw_t[HEAD_DIM:HEAD_DIM + IDX_HEADS, :] * idx_w_scale


def _tile_iotas():
    r_k = lax.broadcasted_iota(jnp.int32, (Q_BLOCK, Q_BLOCK), 0)
    c_q = lax.broadcasted_iota(jnp.int32, (Q_BLOCK, Q_BLOCK), 1)
    return r_k, c_q


def _colsum(x):
    return jnp.sum(x, axis=0, keepdims=True)


def _any_lane(pred):
    return jnp.max(jnp.where(pred, 1.0, 0.0)) > 0.0


def _index_scores(nk, n_adm, qit_ref, qcols, wi, kk_ref, s_ref, idx_ref):
    r_k, c_q = _tile_iotas()
    top_rows = r_k < HALF
    per_head = []
    for t in range(IDX_HEADS // 2):
        qit = qit_ref[t * LANES:(t + 1) * LANES, qcols]
        zero = jnp.zeros_like(qit)
        per_head += [jnp.where(top_rows, qit, zero), jnp.where(top_rows, zero, qit)]
    half_heads = IDX_HEADS // 2
    for g in range(2):
        wg = jnp.concatenate(per_head[g * half_heads:(g + 1) * half_heads], axis=1)
        for rows in _row_blocks(nk, DOT_ROWS):
            s_ref[g, rows, :] = jnp.dot(kk_ref[rows, :], wg, preferred_element_type=F32)

    def weighted_relu(g, rows):
        acc = None
        for i in range(half_heads):
            hh = g * half_heads + i
            term = jnp.maximum(s_ref[g, rows, i * Q_BLOCK:(i + 1) * Q_BLOCK], 0.0) * wi[hh:hh + 1, :]
            acc = term if acc is None else acc + term
        return acc

    slabs = nk // RED_ROWS
    mn_acc = jnp.full((RED_ROWS, Q_BLOCK), jnp.inf, F32)
    mx_acc = jnp.full((RED_ROWS, Q_BLOCK), -jnp.inf, F32)
    tiny_acc = jnp.full((RED_ROWS, Q_BLOCK), jnp.inf, F32)
    for r in range(slabs):
        rows = slice(r * RED_ROWS, (r + 1) * RED_ROWS)
        acc = weighted_relu(0, rows) + weighted_relu(1, rows)
        if r >= slabs - KEY_BUCKET // RED_ROWS:
            adm = (r * RED_ROWS + r_k) < n_adm
            lo_fill = jnp.where(adm, acc, -jnp.inf)
            hi_fill = jnp.where(adm, acc, jnp.inf)
        else:
            lo_fill = hi_fill = acc
        mag = jnp.abs(hi_fill)
        idx_ref[rows, :] = lo_fill
        mn_acc = jnp.minimum(mn_acc, hi_fill)
        mx_acc = jnp.maximum(mx_acc, lo_fill)
        tiny_acc = jnp.minimum(tiny_acc, jnp.where(mag == 0.0, jnp.inf, mag))
    lo = jnp.min(mn_acc, axis=0, keepdims=True)
    hi = jnp.max(mx_acc, axis=0, keepdims=True)
    tiny = jnp.min(tiny_acc, axis=0, keepdims=True)

    unit = jnp.where(tiny < jnp.inf, tiny, 1.0)
    eps = unit * (0.5 / nk)
    rank0 = (1 + r_k).astype(F32)

    def spread(r, carry):
        rows = pl.ds(pl.multiple_of(r * RED_ROWS, RED_ROWS), RED_ROWS)
        s = idx_ref[rows, :]
        rank = rank0 + jnp.asarray(r * RED_ROWS, F32)
        idx_ref[rows, :] = jnp.where(s == 0.0, -(rank * eps), s)
        return carry

    lax.fori_loop(0, slabs, spread, 0, unroll=KEY_BUCKET // RED_ROWS)
    return jnp.minimum(lo, -0.5 * unit), hi


def _probe(nk, kf, idx_ref, state):
    lo, hi, c_lo, c_hi = state
    frac = (c_lo - kf) / jnp.maximum(c_lo - c_hi, 1.0)
    frac = jnp.minimum(jnp.maximum(frac, INTERP_MARGIN), 1.0 - INTERP_MARGIN)
    t = lo + (hi - lo) * frac
    acc = jnp.zeros((RED_ROWS, Q_BLOCK), F32)
    for r in range(nk // RED_ROWS):
        acc = acc + jnp.where(idx_ref[r * RED_ROWS:(r + 1) * RED_ROWS, :] >= t, 1.0, 0.0)
    c = _colsum(acc)
    ge = c >= kf
    return (jnp.where(ge, t, lo), jnp.where(ge, hi, t), jnp.where(ge, c, c_lo), jnp.where(ge, c_hi, c))


def _write_selection(nk, j_blk, lo, idx_ref, rhs_ref):
    r_k, c_q = _tile_iotas()
    later = 2.0 * jnp.maximum(r_k - c_q, 0).astype(F32)

    def write(r, carry):
        rows = pl.ds(pl.multiple_of(r * RED_ROWS, RED_ROWS), RED_ROWS)
        on = jnp.where(r == j_blk, -later, 0.0)
        rhs_ref[rows, LANES:2 * LANES] = jnp.where(idx_ref[rows, :] >= lo, on, -MASK_OFF).astype(BF16)
        return carry

    lax.fori_loop(0, nk // RED_ROWS, write, 0, unroll=KEY_BUCKET // RED_ROWS)


def _exact_fallback(slabs, kf, lo, hi, idx_ref, rhs_ref):
    r_k, c_q = _tile_iotas()
    later = 2.0 * jnp.maximum(r_k - c_q, 0).astype(F32)

    def slab(r):
        return idx_ref[pl.ds(pl.multiple_of(r * RED_ROWS, RED_ROWS), RED_ROWS), :]

    def count(pred):
        def body(r, acc):
            return acc + jnp.where(pred(slab(r), r), 1.0, 0.0)
        return _colsum(lax.fori_loop(0, slabs, body, jnp.zeros((RED_ROWS, Q_BLOCK), F32)))

    def kth(lo):
        def body(r, acc):
            s = slab(r)
            return jnp.minimum(acc, jnp.where(s >= lo, s, jnp.inf))
        acc = lax.fori_loop(0, slabs, body, jnp.full((RED_ROWS, Q_BLOCK), jnp.inf, F32))
        thr = jnp.min(acc, axis=0, keepdims=True)
        return thr, count(lambda s, r: s > thr)

    def slow_cond(carry):
        it, _, _, _, c_gt = carry
        return jnp.logical_and(it < SLOW_TRIPS, _any_lane(c_gt >= kf))

    def slow_body(carry):
        it, lo, hi, _, _ = carry
        for _ in range(4):
            mid = 0.5 * lo + 0.5 * hi
            ge = count(lambda s, r: s >= mid) >= kf
            lo, hi = jnp.where(ge, mid, lo), jnp.where(ge, hi, mid)
        thr, c_gt = kth(lo)
        return it + 1, lo, hi, thr, c_gt

    thr0, c_gt0 = kth(lo)
    _, _, _, thr, c_gt = lax.while_loop(slow_cond, slow_body, (jnp.int32(0), lo, hi, thr0, c_gt0))
    c_eq = count(lambda s, r: s == thr)
    need = (c_gt + c_eq) > kf

    def tie_step(_, carry):
        lo_i, hi_i = carry
        mid_i = jnp.floor((lo_i + hi_i) * 0.5)
        below = count(lambda s, r: jnp.logical_and(s == thr, (r * RED_ROWS + r_k).astype(F32) <= mid_i))
        ok = (c_gt + below) >= kf
        return jnp.where(ok, lo_i, mid_i), jnp.where(ok, mid_i, hi_i)

    last = jnp.asarray(slabs * RED_ROWS - 1, F32)
    lo_i = jnp.full((1, Q_BLOCK), -1.0, F32)
    hi_i = jnp.zeros((1, Q_BLOCK), F32) + last
    _, cut = lax.fori_loop(0, 13, tie_step, (lo_i, hi_i))
    cut = jnp.where(need, cut, last + 1.0).astype(jnp.int32)

    def write(r, carry):
        s = slab(r)
        krow = r * RED_ROWS + r_k
        sel = jnp.logical_or(s > thr, jnp.logical_and(s == thr, krow <= cut))
        on = jnp.where(r == slabs - 1, -later, 0.0)
        rhs_ref[pl.ds(pl.multiple_of(r * RED_ROWS, RED_ROWS), RED_ROWS), LANES:2 * LANES] = (
            jnp.where(sel, on, -MASK_OFF).astype(BF16))
        return carry

    lax.fori_loop(0, slabs, write, 0)


GROUP = HEADS // 2
UNITS = PAIR * (HEADS // GROUP)
VALUE_ROWS = HEAD_DIM + 16


def _probabilities(nk, qt_ref, qcols, rhs_ref, s_ref, p_ref, between):
    r_k, c_q = _tile_iotas()
    feat = lax.broadcasted_iota(jnp.int32, (HALF, Q_BLOCK), 0)
    group = GROUP
    width = group * Q_BLOCK
    units = [(u, g) for u in range(len(qcols)) for g in range(HEADS // group)]

    def score_operand(u, hh):
        slope = 2.0 ** (-(hh + 1))
        alibi = jnp.where(feat == 0, CHUNK * slope, jnp.where(feat == 1, slope, 0.0)).astype(BF16)
        scaled_ident = jnp.where(r_k == c_q, slope, 0.0).astype(BF16)
        return jnp.concatenate([qt_ref[hh * HEAD_DIM:(hh + 1) * HEAD_DIM, qcols[u]], alibi, scaled_ident], axis=0)

    def scores(n):
        u, g = units[n]
        lhs_t = jnp.concatenate([score_operand(u, g * group + i) for i in range(group)], axis=1)
        m_acc = jnp.full((RED_ROWS, width), -jnp.inf, F32)
        for rows in _row_blocks(nk, DOT_ROWS):
            blk = jnp.dot(rhs_ref[u, rows, :], lhs_t, preferred_element_type=F32)
            s_ref[n % 2, rows, :] = blk
            for sub in range((rows.stop - rows.start) // RED_ROWS):
                m_acc = jnp.maximum(m_acc, blk[sub * RED_ROWS:(sub + 1) * RED_ROWS, :])
        return jnp.max(m_acc, axis=0, keepdims=True)

    def probabilities(n, m):
        for rows in _row_blocks(nk, RED_ROWS):
            p_ref[n, rows, :] = jnp.exp((s_ref[n % 2, rows, :] - m).astype(BF16))

    m = scores(0)
    between()
    for n in range(len(units)):
        m_next = scores(n + 1) if n + 1 < len(units) else None
        probabilities(n, m)
        m = m_next


def _value_chunk(ch, vtb_ref, p_ref, part_ref):
    if isinstance(ch, int):
        n, blk = ch % UNITS, ch // UNITS
        rows = slice(blk * DOT_ROWS, (blk + 1) * DOT_ROWS)
    else:
        n = ch & (UNITS - 1)
        blk = lax.shift_right_logical(ch, UNITS.bit_length() - 1)
        rows = pl.ds(pl.multiple_of(blk * DOT_ROWS, DOT_ROWS), DOT_ROWS)
    part_ref[n, blk] = jnp.dot(vtb_ref[blk], p_ref[n, rows, :], preferred_element_type=F32)


def _head_tiles(part_ref, n_blocks):
    per_block = []
    for u in range(PAIR):
        outs = []
        for g in range(HEADS // GROUP):
            n = u * (HEADS // GROUP) + g
            o = part_ref[n, 0]
            for blk in range(1, part_ref.shape[1]):
                o = o + jnp.where(blk < n_blocks, part_ref[n, blk], 0.0)
            outs.append(o[0:HEAD_DIM, :] * (1.0 / o[HEAD_DIM:HEAD_DIM + 1, :]))
        tiles = []
        for t in range(HEADS // 2):
            o_g = outs[(2 * t) // GROUP]
            a = (2 * t) % GROUP
            tiles.append(jnp.concatenate([o_g[:, a * Q_BLOCK:(a + 1) * Q_BLOCK],
                                          o_g[:, (a + 1) * Q_BLOCK:(a + 2) * Q_BLOCK]], axis=0).T)
        per_block.append(tiles)
    return per_block


def _attn_kernel(qt_ref, qit_ref, wit_ref, kaug_ref, kk_ref, vt_ref, ma_ref, gate_ref, x_ref, gb_ref,
                 wo_ref, y_ref, idx_ref, rhs_ref, s_ref, p_ref, st_ref, vtb_ref, part_ref, done_ref,
                 *, seq, n_pairs, topk):
    g = pl.program_id(0)
    kf = float(topk)
    steps = seq // (PAIR * Q_BLOCK)
    key_blocks = seq // DOT_ROWS
    qcols = [slice(u * Q_BLOCK, (u + 1) * Q_BLOCK) for u in range(PAIR)]
    searching = g < n_pairs
    pending = g >= 1
    jj = lax.rem(g, steps)
    g_prev = jnp.maximum(g - 1, 0)
    slot = lax.div(g, steps) & 1
    slot_prev = lax.div(g_prev, steps) & 1

    @pl.when(g == 0)
    def _():
        p_ref[...] = jnp.zeros(p_ref.shape, BF16)
        vtb_ref[...] = jnp.zeros(vtb_ref.shape, BF16)
        part_ref[...] = jnp.zeros(part_ref.shape, F32)
        y_ref[...] = jnp.zeros(y_ref.shape, F32)

    @pl.when(jnp.logical_and(searching, jj == 0))
    def _():
        for u in range(PAIR):
            rhs_ref[u, :, 0:LANES] = kaug_ref[...]
        ones = jnp.ones((VALUE_ROWS - HEAD_DIM, DOT_ROWS), BF16)
        for blk in range(key_blocks):
            vtb_ref[slot, blk] = jnp.concatenate([vt_ref[:, blk * DOT_ROWS:(blk + 1) * DOT_ROWS], ones], axis=0)

    done_ref[0] = 0
    done_ref[1] = 0

    steps_per_bucket = KEY_BUCKET // (PAIR * Q_BLOCK)

    def buckets(step, active):
        return [((n + 1) * KEY_BUCKET,
                 jnp.logical_and(active, jnp.logical_and(step >= n * steps_per_bucket,
                                                         step < (n + 1) * steps_per_bucket)))
                for n in range(seq // KEY_BUCKET)]

    blocks = [PAIR * jj + u for u in range(PAIR)]
    total = UNITS * (lax.div(lax.rem(g_prev, steps), steps_per_bucket) + 1) * (KEY_BUCKET // DOT_ROWS)
    vtb_prev = vtb_ref.at[slot_prev]

    for nk, here in buckets(jj, searching):
        @pl.when(here)
        def _(nk=nk):
            lane_q = lax.broadcasted_iota(jnp.int32, (1, Q_BLOCK), 1)
            states = []
            for u in range(PAIR):
                n_adm = blocks[u] * Q_BLOCK + CHUNK + CHUNK * (lane_q >= CHUNK).astype(jnp.int32)
                lo, hi = _index_scores(nk, n_adm, qit_ref, qcols[u], wit_ref[:, qcols[u]], kk_ref, s_ref,
                                       idx_ref.at[u])
                n_adm_f = n_adm.astype(F32)
                c_lo = jnp.where(n_adm_f <= kf, kf, n_adm_f)
                states.append((lo, hi, c_lo, jnp.zeros((1, Q_BLOCK), F32)))

            def cond(carry):
                it, flat = carry[0], carry[1:]
                def tested():
                    still = jnp.logical_or(flat[2] != kf, flat[6] != kf)
                    return jnp.logical_and(it < FAST_TRIPS, _any_lane(still))

                return lax.cond(it < UNTESTED_TRIPS, lambda: jnp.bool_(True), tested)

            def body(carry):
                it, flat = carry[0], carry[1:]
                st = [flat[0:4], flat[4:8]]
                for _ in range(PROBES_PER_TRIP):
                    st = [_probe(nk, kf, idx_ref.at[u], st[u]) for u in range(PAIR)]
                for c in range(CHUNKS_PER_TRIP):
                    _value_chunk(jnp.minimum(it * CHUNKS_PER_TRIP + c, total - 1), vtb_prev, p_ref, part_ref)
                return (it + 1,) + tuple(st[0]) + tuple(st[1])

            final = lax.while_loop(cond, body, (jnp.int32(0),) + tuple(states[0]) + tuple(states[1]))
            done_ref[0] = jnp.minimum(final[0] * CHUNKS_PER_TRIP, total)
            done_ref[1] = final[0]
            for u in range(PAIR):
                lo, hi, c_lo, _ = final[1 + 4 * u:5 + 4 * u]
                _write_selection(nk, blocks[u], lo, idx_ref.at[u], rhs_ref.at[u])
                st_ref[4 * u + 0:4 * u + 1, :] = lo
                st_ref[4 * u + 1:4 * u + 2, :] = hi
                st_ref[4 * u + 2:4 * u + 3, :] = c_lo

    @pl.when(jnp.logical_and(pending, searching))
    def _():
        def chunk(ch, carry):
            _value_chunk(ch, vtb_prev, p_ref, part_ref)
            return carry

        lax.fori_loop(done_ref[0], total, chunk, 0)

    @pl.when(g == n_pairs)
    def _():
        for ch in range(UNITS * key_blocks):
            _value_chunk(ch, vtb_prev, p_ref, part_ref)

    lane = lax.broadcasted_iota(jnp.int32, (PAIR * Q_BLOCK, LANES), 1)
    lo_half = lane < HALF

    def finish_pending():
        per_block = _head_tiles(part_ref, lax.div(total, UNITS))
        mixed = [ma_ref[...]]
        for t in range(HEADS // 2):
            cols = slice(t * LANES, (t + 1) * LANES)
            o = jnp.concatenate([per_block[u][t] for u in range(PAIR)], axis=0)
            msq = _half_mean_sq(o * o, lo_half)
            ob = o * lax.rsqrt(msq + NORM_EPS) * gb_ref[:, cols] * gate_ref[:, cols].astype(F32)
            mixed.append(ob.astype(BF16))
        y_ref[...] = x_ref[...] + jnp.dot(jnp.concatenate(mixed, axis=1), wo_ref[...],
                                          preferred_element_type=F32)

    @pl.when(jnp.logical_and(searching, done_ref[1] >= FAST_TRIPS))
    def _():
        for u in range(PAIR):
            lo = st_ref[4 * u + 0:4 * u + 1, :]
            hi = st_ref[4 * u + 1:4 * u + 2, :]
            c_lo = st_ref[4 * u + 2:4 * u + 3, :]

            @pl.when(_any_lane(c_lo != kf))
            def _(u=u, lo=lo, hi=hi):
                _exact_fallback(blocks[u] + 1, kf, lo, hi, idx_ref.at[u], rhs_ref.at[u])

    for nk, here in buckets(jj, searching):
        @pl.when(here)
        def _(nk=nk):
            _probabilities(nk, qt_ref, qcols, rhs_ref, s_ref, p_ref, finish_pending)

    @pl.when(g == n_pairs)
    def _():
        finish_pending()


def kernel(x, norm_gain, w_in, sgu_norm_gain, sgu_w, sgu_b, q_norm_gain, k_norm_gain,
           idx_k_norm_gain, branch_norm_gain, w_out):
    bsz, seq, d_model = x.shape
    assert d_model == D_MODEL and norm_gain.shape[0] == 1
    assert seq % PROJ_ROWS == 0 and seq % KEY_BUCKET == 0 and KEY_BUCKET % (PAIR * Q_BLOCK) == 0
    tokens = bsz * seq
    topk = min(TOPK_MAX, seq // 4)
    idx_w_scale = (IDX_HEADS ** -0.5) * (IDX_DIM ** -0.5)

    w = w_in[0]
    a3 = 3 * A_WIDTH
    w_q = w[:, a3:a3 + B_WIDTH]
    w_k = w[:, a3 + B_WIDTH:a3 + B_WIDTH + HEAD_DIM]
    w_v = w[:, a3 + B_WIDTH + HEAD_DIM:a3 + B_WIDTH + 2 * HEAD_DIM]
    o_g = a3 + B_WIDTH + 2 * HEAD_DIM
    w_g = w[:, o_g:o_g + B_WIDTH]
    o_i = o_g + B_WIDTH
    w_iq = w[:, o_i:o_i + IDX_HEADS * IDX_DIM]
    w_ik = w[:, o_i + IDX_HEADS * IDX_DIM:o_i + IDX_HEADS * IDX_DIM + IDX_DIM]
    w_iw = w[:, o_i + IDX_HEADS * IDX_DIM + IDX_DIM:]
    w_main = jnp.concatenate([w[:, :a3], w_g, w_k, w_ik, w_v, w_iw,
                              jnp.zeros((D_MODEL, LANES - HEAD_DIM - IDX_HEADS), F32)], axis=1).astype(BF16)
    assert w_main.shape[1] == _PACKED_COLS
    w_t = jnp.concatenate([w_q.T, w_iq.T], axis=0).astype(BF16)
    assert w_t.shape[0] == _PACKED_ROWS
    x2 = x.reshape(tokens, D_MODEL)
    ng = norm_gain[0].reshape(1, D_MODEL)
    sgn = sgu_norm_gain[0].reshape(1, A_WIDTH)
    sw = sgu_w[0]
    sb = sgu_b[0].reshape(A_GROUPS, A_BLOCK, 1)
    qg = q_norm_gain[0].reshape(HEAD_DIM, 1)
    kg = jnp.concatenate([k_norm_gain[0], idx_k_norm_gain[0]]).reshape(1, LANES)
    ga = branch_norm_gain[0, :A_WIDTH].reshape(1, A_WIDTH)
    gb = branch_norm_gain[0, A_WIDTH:].reshape(1, B_WIDTH)
    wo = w_out[0].astype(BF16)

    tm = PROJ_ROWS
    full = lambda shape: pl.BlockSpec(shape, lambda i: (0,) * len(shape))
    rows = lambda width: pl.BlockSpec((tm, width), lambda i: (i, 0))
    colsT = lambda height: pl.BlockSpec((height, tm), lambda i: (0, i))
    outs = pl.pallas_call(
        functools.partial(_proj_kernel, tiles_per_seq=seq // tm, idx_w_scale=idx_w_scale),
        grid=(tokens // tm,),
        in_specs=[rows(D_MODEL), full((1, D_MODEL)), full((D_MODEL, _PACKED_COLS)),
                  full((_PACKED_ROWS, D_MODEL)), full((1, A_WIDTH)), full((A_GROUPS, A_BLOCK, A_BLOCK)),
                  full((A_GROUPS, A_BLOCK, 1)), full((HEAD_DIM, 1)), full((1, LANES)),
                  full((1, A_WIDTH))],
        out_specs=[rows(A_WIDTH), rows(B_WIDTH), rows(LANES), rows(LANES),
                   colsT(B_WIDTH), colsT(IDX_HEADS * IDX_DIM), colsT(HEAD_DIM), colsT(IDX_HEADS)],
        out_shape=[jax.ShapeDtypeStruct((tokens, A_WIDTH), BF16),
                   jax.ShapeDtypeStruct((tokens, B_WIDTH), BF16),
                   jax.ShapeDtypeStruct((tokens, LANES), BF16),
                   jax.ShapeDtypeStruct((tokens, LANES), BF16),
                   jax.ShapeDtypeStruct((B_WIDTH, tokens), BF16),
                   jax.ShapeDtypeStruct((IDX_HEADS * IDX_DIM, tokens), BF16),
                   jax.ShapeDtypeStruct((HEAD_DIM, tokens), BF16),
                   jax.ShapeDtypeStruct((IDX_HEADS, tokens), F32)],
        compiler_params=pltpu.CompilerParams(dimension_semantics=("arbitrary",),
                                             vmem_limit_bytes=VMEM_LIMIT),
        name="proj_sgu",
    )(x2, ng, w_main, w_t, sgn, sw, sb, qg, kg, ga)
    ma, gate, kaug, kk, qt, qit, vt, wit = outs

    qrows = PAIR * Q_BLOCK
    steps = seq // qrows
    n_pairs = bsz * steps
    cur = lambda g: jnp.minimum(g, n_pairs - 1)
    qblk = lambda width: pl.BlockSpec((qrows, width), lambda g: (jnp.maximum(g - 1, 0), 0))
    qblkT = lambda height: pl.BlockSpec((height, qrows), lambda g: (0, cur(g)))
    const = lambda shape: pl.BlockSpec(shape, lambda g: (0,) * len(shape))
    y = pl.pallas_call(
        functools.partial(_attn_kernel, seq=seq, n_pairs=n_pairs, topk=topk),
        grid=(n_pairs + 1,),
        in_specs=[qblkT(B_WIDTH), qblkT(IDX_HEADS * IDX_DIM), qblkT(IDX_HEADS),
                  pl.BlockSpec((seq, LANES), lambda g: (cur(g) // steps, 0)),
                  pl.BlockSpec((seq, LANES), lambda g: (cur(g) // steps, 0)),
                  pl.BlockSpec((HEAD_DIM, seq), lambda g: (0, cur(g) // steps)),
                  qblk(A_WIDTH), qblk(B_WIDTH), qblk(D_MODEL),
                  const((1, B_WIDTH)), const((D_MODEL, D_MODEL))],
        out_specs=qblk(D_MODEL),
        out_shape=jax.ShapeDtypeStruct((tokens, D_MODEL), F32),
        scratch_shapes=[pltpu.VMEM((PAIR, seq, Q_BLOCK), F32),
                        pltpu.VMEM((PAIR, seq, 2 * LANES), BF16),
                        pltpu.VMEM((2, seq, GROUP * Q_BLOCK), F32),
                        pltpu.VMEM((UNITS, seq, GROUP * Q_BLOCK), BF16),
                        pltpu.VMEM((8, Q_BLOCK), F32),
                        pltpu.VMEM((2, seq // DOT_ROWS, VALUE_ROWS, DOT_ROWS), BF16),
                        pltpu.VMEM((UNITS, seq // DOT_ROWS, VALUE_ROWS, GROUP * Q_BLOCK), F32),
                        pltpu.SMEM((2,), jnp.int32)],
        compiler_params=pltpu.CompilerParams(dimension_semantics=("arbitrary",),
                                             vmem_limit_bytes=VMEM_LIMIT),
        name="dsa_attn_out",
    )(qt, qit, wit, kaug, kk, vt, ma, gate, x2, gb, wo)
    return y.reshape(bsz, seq, D_MODEL)
```

```python
import functools

import jax
import jax.numpy as jnp
from jax import lax
from jax.experimental import pallas as pl
from jax.experimental.pallas import tpu as pltpu

F32 = jnp.float32
BF16 = jnp.bfloat16

D_MODEL = 1024
CHUNK = 64
A_WIDTH = 512
A_GROUPS = 4
A_BLOCK = 128
HEADS = 8
HEAD_DIM = 64
B_WIDTH = HEADS * HEAD_DIM
IDX_HEADS = 8
IDX_DIM = 64
TOPK_MAX = 256
Q_BLOCK = 128
PAIR = 2
NORM_EPS = 1e-6
MASK_OFF = 1e32
LANES = 128
HALF = LANES // 2
PROJ_ROWS = 1024
DOT_ROWS = 512
RED_ROWS = 128
KEY_BUCKET = 512
PROBES_PER_TRIP = 3
FAST_TRIPS = 16
UNTESTED_TRIPS = 5
INTERP_MARGIN = 0.02
SLOW_TRIPS = 70
CHUNKS_PER_TRIP = 3
VMEM_LIMIT = 48 * 1024 * 1024

_OFF_U, _OFF_V, _OFF_Z, _OFF_G, _OFF_K = 0, 512, 1024, 1536, 2048
_PACKED_COLS = 2304
_ROW_Q, _ROW_QI, _PACKED_ROWS = 0, 512, 1024

_NT = (((1,), (1,)), ((), ()))


def _gelu(x):
    c = 0.7978845608028654
    return 0.5 * x * (1.0 + jnp.tanh(c * (x + 0.044715 * (x * x * x))))


def _silu(x):
    return x / (1.0 + jnp.exp(-x))


def _row_blocks(total, size):
    return [slice(r, min(r + size, total)) for r in range(0, total, size)]


def _half_mean_sq(x2, lo_half):
    tot = jnp.sum(x2, axis=-1, keepdims=True)
    lo = jnp.sum(jnp.where(lo_half, x2, 0.0), axis=-1, keepdims=True)
    return jnp.where(lo_half, lo, tot - lo) * (1.0 / HALF)


def _proj_kernel(x_ref, ng_ref, w_ref, wt_ref, sgn_ref, sw_ref, sb_ref, qg_ref, kg_ref, ga_ref,
                 ma_ref, gate_ref, kaug_ref, kk_ref, qt_ref, qit_ref, vt_ref, wit_ref,
                 *, tiles_per_seq, idx_w_scale):
    tm = x_ref.shape[0]
    i = pl.program_id(0)
    x = x_ref[...]
    ms = jnp.mean(x * x, axis=-1, keepdims=True)
    h = (x * lax.rsqrt(ms + NORM_EPS) * ng_ref[...]).astype(BF16)

    lane = lax.broadcasted_iota(jnp.int32, (tm, LANES), 1)
    lo_half = lane < HALF

    def proj(off, width):
        return jnp.dot(h, w_ref[:, off:off + width], preferred_element_type=F32)

    gu = _gelu(proj(_OFF_U, A_WIDTH))
    gv = _gelu(proj(_OFF_V, A_WIDTH))
    pz = proj(_OFF_Z, A_WIDTH)
    r_i = lax.broadcasted_iota(jnp.int32, (A_BLOCK, A_BLOCK), 0)
    c_j = lax.broadcasted_iota(jnp.int32, (A_BLOCK, A_BLOCK), 1)
    causal = lax.shift_right_logical(c_j, 6) <= lax.shift_right_logical(r_i, 6)
    for g in range(A_GROUPS):
        cols = slice(g * LANES, (g + 1) * LANES)
        vg = gv[:, cols]
        mu = jnp.mean(vg, axis=-1, keepdims=True)
        d = vg - mu
        var = jnp.mean(d * d, axis=-1, keepdims=True)
        vn = (d * lax.rsqrt(var + NORM_EPS) * sgn_ref[:, cols]).astype(BF16)
        wg = jnp.where(causal, sw_ref[g], 0.0).astype(BF16)
        side = jnp.concatenate([vn[blk * A_BLOCK:(blk + 1) * A_BLOCK, :] for blk in range(tm // A_BLOCK)], axis=1)
        mixed = jnp.dot(wg, side, preferred_element_type=F32) + sb_ref[g]
        for blk in range(tm // A_BLOCK):
            rows = slice(blk * A_BLOCK, (blk + 1) * A_BLOCK)
            s = mixed[:, blk * LANES:(blk + 1) * LANES]
            ya = gu[rows, cols] * s
            oa = ya * lax.rsqrt(jnp.mean(ya * ya, axis=-1, keepdims=True) + NORM_EPS) * ga_ref[:, cols]
            ma_ref[rows, cols] = (oa * _silu(pz[rows, cols])).astype(BF16)

    gate_ref[...] = _silu(proj(_OFF_G, B_WIDTH)).astype(BF16)

    pkv = proj(_OFF_K, 2 * LANES)
    pk = pkv[:, 0:LANES]
    k_ms = jnp.sum(jnp.where(lo_half, pk * pk, 0.0), axis=-1, keepdims=True) * (1.0 / HALF)
    kn = pk * lax.rsqrt(k_ms + NORM_EPS)
    ik_mu = jnp.sum(jnp.where(lo_half, 0.0, pk), axis=-1, keepdims=True) * (1.0 / HALF)
    dk = pk - ik_mu
    ik_var = jnp.sum(jnp.where(lo_half, 0.0, dk * dk), axis=-1, keepdims=True) * (1.0 / HALF)
    kin = dk * lax.rsqrt(ik_var + NORM_EPS)
    tile = jnp.where(lo_half, kn, kin) * kg_ref[...]
    swapped = pltpu.roll(tile, HALF, axis=1)
    row = lax.broadcasted_iota(jnp.int32, (tm, LANES), 0)
    pos = (i % tiles_per_seq) * tm + row
    pos_hi = lax.shift_right_logical(pos, 6).astype(F32)
    pos_lo = (pos & (CHUNK - 1)).astype(F32)
    posfeat = jnp.where(lane == HALF, pos_hi, jnp.where(lane == HALF + 1, pos_lo, 0.0))
    kaug_ref[...] = jnp.where(lo_half, tile, posfeat).astype(BF16)
    kk_ref[...] = jnp.where(lo_half, swapped, tile).astype(BF16)

    pt = lax.dot_general(wt_ref[...], h, _NT, preferred_element_type=F32)
    for hh in range(HEADS):
        rows = slice(_ROW_Q + hh * HEAD_DIM, _ROW_Q + (hh + 1) * HEAD_DIM)
        xq = pt[rows, :]
        msq = jnp.mean(xq * xq, axis=0, keepdims=True)
        qt_ref[rows, :] = (xq * lax.rsqrt(msq + NORM_EPS) * qg_ref[...] * (HEAD_DIM ** -0.5)).astype(BF16)
    qit_ref[...] = pt[_ROW_QI:_ROW_QI + IDX_HEADS * IDX_DIM, :].astype(BF16)
    for blk in range(tm // LANES):
        cols = slice(blk * LANES, (blk + 1) * LANES)
        vw_t = pkv[cols, LANES:2 * LANES].T
        vt_ref[:, cols] = vw_t[0:HEAD_DIM, :].astype(BF16)
        wit_ref[:, cols] = vw_t[HEAD_DIM:HEAD_DIM + IDX_HEADS, :] * idx_w_scale


def _tile_iotas():
    r_k = lax.broadcasted_iota(jnp.int32, (Q_BLOCK, Q_BLOCK), 0)
    c_q = lax.broadcasted_iota(jnp.int32, (Q_BLOCK, Q_BLOCK), 1)
    return r_k, c_q


def _colsum(x):
    return jnp.sum(x, axis=0, keepdims=True)


def _any_lane(pred):
    return jnp.max(jnp.where(pred, 1.0, 0.0)) > 0.0


def _index_scores(nk, n_adm, qit_ref, qcols, wi, kk_ref, s_ref, idx_ref):
    r_k, c_q = _tile_iotas()
    top_rows = r_k < HALF
    per_head = []
    for t in range(IDX_HEADS // 2):
        qit = qit_ref[t * LANES:(t + 1) * LANES, qcols]
        zero = jnp.zeros_like(qit)
        per_head += [jnp.where(top_rows, qit, zero), jnp.where(top_rows, zero, qit)]
    half_heads = IDX_HEADS // 2
    for g in range(2):
        wg = jnp.concatenate(per_head[g * half_heads:(g + 1) * half_heads], axis=1)
        for rows in _row_blocks(nk, DOT_ROWS):
            s_ref[g, rows, :] = jnp.dot(kk_ref[rows, :], wg, preferred_element_type=F32)

    def weighted_relu(g, rows):
        acc = None
        for i in range(half_heads):
            hh = g * half_heads + i
            term = jnp.maximum(s_ref[g, rows, i * Q_BLOCK:(i + 1) * Q_BLOCK], 0.0) * wi[hh:hh + 1, :]
            acc = term if acc is None else acc + term
        return acc

    slabs = nk // RED_ROWS
    mn_acc = jnp.full((RED_ROWS, Q_BLOCK), jnp.inf, F32)
    mx_acc = jnp.full((RED_ROWS, Q_BLOCK), -jnp.inf, F32)
    tiny_acc = jnp.full((RED_ROWS, Q_BLOCK), jnp.inf, F32)
    for r in range(slabs):
        rows = slice(r * RED_ROWS, (r + 1) * RED_ROWS)
        acc = weighted_relu(0, rows) + weighted_relu(1, rows)
        if r >= slabs - KEY_BUCKET // RED_ROWS:
            adm = (r * RED_ROWS + r_k) < n_adm
            lo_fill = jnp.where(adm, acc, -jnp.inf)
            hi_fill = jnp.where(adm, acc, jnp.inf)
        else:
            lo_fill = hi_fill = acc
        mag = jnp.abs(hi_fill)
        idx_ref[rows, :] = lo_fill
        mn_acc = jnp.minimum(mn_acc, hi_fill)
        mx_acc = jnp.maximum(mx_acc, lo_fill)
        tiny_acc = jnp.minimum(tiny_acc, jnp.where(mag == 0.0, jnp.inf, mag))
    lo = jnp.min(mn_acc, axis=0, keepdims=True)
    hi = jnp.max(mx_acc, axis=0, keepdims=True)
    tiny = jnp.min(tiny_acc, axis=0, keepdims=True)

    unit = jnp.where(tiny < jnp.inf, tiny, 1.0)
    eps = unit * (0.5 / nk)
    rank0 = (1 + r_k).astype(F32)

    def spread(r, carry):
        rows = pl.ds(pl.multiple_of(r * RED_ROWS, RED_ROWS), RED_ROWS)
        s = idx_ref[rows, :]
        rank = rank0 + jnp.asarray(r * RED_ROWS, F32)
        idx_ref[rows, :] = jnp.where(s == 0.0, -(rank * eps), s)
        return carry

    lax.fori_loop(0, slabs, spread, 0, unroll=KEY_BUCKET // RED_ROWS)
    return jnp.minimum(lo, -0.5 * unit), hi


def _probe(nk, kf, idx_ref, state):
    lo, hi, c_lo, c_hi = state
    frac = (c_lo - kf) / jnp.maximum(c_lo - c_hi, 1.0)
    frac = jnp.minimum(jnp.maximum(frac, INTERP_MARGIN), 1.0 - INTERP_MARGIN)
    t = lo + (hi - lo) * frac
    acc = jnp.zeros((RED_ROWS, Q_BLOCK), F32)
    for r in range(nk // RED_ROWS):
        acc = acc + jnp.where(idx_ref[r * RED_ROWS:(r + 1) * RED_ROWS, :] >= t, 1.0, 0.0)
    c = _colsum(acc)
    ge = c >= kf
    return (jnp.where(ge, t, lo), jnp.where(ge, hi, t), jnp.where(ge, c, c_lo), jnp.where(ge, c_hi, c))


def _write_selection(nk, j_blk, lo, idx_ref, rhs_ref):
    r_k, c_q = _tile_iotas()
    later = 2.0 * jnp.maximum(r_k - c_q, 0).astype(F32)

    def write(r, carry):
        rows = pl.ds(pl.multiple_of(r * RED_ROWS, RED_ROWS), RED_ROWS)
        on = jnp.where(r == j_blk, -later, 0.0)
        rhs_ref[rows, LANES:2 * LANES] = jnp.where(idx_ref[rows, :] >= lo, on, -MASK_OFF).astype(BF16)
        return carry

    lax.fori_loop(0, nk // RED_ROWS, write, 0, unroll=KEY_BUCKET // RED_ROWS)


def _exact_fallback(slabs, kf, lo, hi, idx_ref, rhs_ref):
    r_k, c_q = _tile_iotas()
    later = 2.0 * jnp.maximum(r_k - c_q, 0).astype(F32)

    def slab(r):
        return idx_ref[pl.ds(pl.multiple_of(r * RED_ROWS, RED_ROWS), RED_ROWS), :]

    def count(pred):
        def body(r, acc):
            return acc + jnp.where(pred(slab(r), r), 1.0, 0.0)
        return _colsum(lax.fori_loop(0, slabs, body, jnp.zeros((RED_ROWS, Q_BLOCK), F32)))

    def kth(lo):
        def body(r, acc):
            s = slab(r)
            return jnp.minimum(acc, jnp.where(s >= lo, s, jnp.inf))
        acc = lax.fori_loop(0, slabs, body, jnp.full((RED_ROWS, Q_BLOCK), jnp.inf, F32))
        thr = jnp.min(acc, axis=0, keepdims=True)
        return thr, count(lambda s, r: s > thr)

    def slow_cond(carry):
        it, _, _, _, c_gt = carry
        return jnp.logical_and(it < SLOW_TRIPS, _any_lane(c_gt >= kf))

    def slow_body(carry):
        it, lo, hi, _, _ = carry
        for _ in range(4):
            mid = 0.5 * lo + 0.5 * hi
            ge = count(lambda s, r: s >= mid) >= kf
            lo, hi = jnp.where(ge, mid, lo), jnp.where(ge, hi, mid)
        thr, c_gt = kth(lo)
        return it + 1, lo, hi, thr, c_gt

    thr0, c_gt0 = kth(lo)
    _, _, _, thr, c_gt = lax.while_loop(slow_cond, slow_body, (jnp.int32(0), lo, hi, thr0, c_gt0))
    c_eq = count(lambda s, r: s == thr)
    need = (c_gt + c_eq) > kf

    def tie_step(_, carry):
        lo_i, hi_i = carry
        mid_i = jnp.floor((lo_i + hi_i) * 0.5)
        below = count(lambda s, r: jnp.logical_and(s == thr, (r * RED_ROWS + r_k).astype(F32) <= mid_i))
        ok = (c_gt + below) >= kf
        return jnp.where(ok, lo_i, mid_i), jnp.where(ok, mid_i, hi_i)

    last = jnp.asarray(slabs * RED_ROWS - 1, F32)
    lo_i = jnp.full((1, Q_BLOCK), -1.0, F32)
    hi_i = jnp.zeros((1, Q_BLOCK), F32) + last
    _, cut = lax.fori_loop(0, 13, tie_step, (lo_i, hi_i))
    cut = jnp.where(need, cut, last + 1.0).astype(jnp.int32)

    def write(r, carry):
        s = slab(r)
        krow = r * RED_ROWS + r_k
        sel = jnp.logical_or(s > thr, jnp.logical_and(s == thr, krow <= cut))
        on = jnp.where(r == slabs - 1, -later, 0.0)
        rhs_ref[pl.ds(pl.multiple_of(r * RED_ROWS, RED_ROWS), RED_ROWS), LANES:2 * LANES] = (
            jnp.where(sel, on, -MASK_OFF).astype(BF16))
        return carry

    lax.fori_loop(0, slabs, write, 0)


GROUP = HEADS // 2
UNITS = PAIR * (HEADS // GROUP)
VALUE_ROWS = HEAD_DIM + 16


def _probabilities(nk, qt_ref, qcols, rhs_ref, s_ref, p_ref, between):
    r_k, c_q = _tile_iotas()
    feat = lax.broadcasted_iota(jnp.int32, (HALF, Q_BLOCK), 0)
    group = GROUP
    width = group * Q_BLOCK
    units = [(u, g) for u in range(len(qcols)) for g in range(HEADS // group)]

    def score_operand(u, hh):
        slope = 2.0 ** (-(hh + 1))
        alibi = jnp.where(feat == 0, CHUNK * slope, jnp.where(feat == 1, slope, 0.0)).astype(BF16)
        scaled_ident = jnp.where(r_k == c_q, slope, 0.0).astype(BF16)
        return jnp.concatenate([qt_ref[hh * HEAD_DIM:(hh + 1) * HEAD_DIM, qcols[u]], alibi, scaled_ident], axis=0)

    def scores(n):
        u, g = units[n]
        lhs_t = jnp.concatenate([score_operand(u, g * group + i) for i in range(group)], axis=1)
        m_acc = jnp.full((RED_ROWS, width), -jnp.inf, F32)
        for rows in _row_blocks(nk, DOT_ROWS):
            blk = jnp.dot(rhs_ref[u, rows, :], lhs_t, preferred_element_type=F32)
            s_ref[n % 2, rows, :] = blk
            for sub in range((rows.stop - rows.start) // RED_ROWS):
                m_acc = jnp.maximum(m_acc, blk[sub * RED_ROWS:(sub + 1) * RED_ROWS, :])
        return jnp.max(m_acc, axis=0, keepdims=True)

    def probabilities(n, m):
        for rows in _row_blocks(nk, RED_ROWS):
            p_ref[n, rows, :] = jnp.exp((s_ref[n % 2, rows, :] - m).astype(BF16))

    m = scores(0)
    between()
    for n in range(len(units)):
        m_next = scores(n + 1) if n + 1 < len(units) else None
        probabilities(n, m)
        m = m_next


def _value_chunk(ch, vtb_ref, p_ref, part_ref):
    if isinstance(ch, int):
        n, blk = ch % UNITS, ch // UNITS
        rows = slice(blk * DOT_ROWS, (blk + 1) * DOT_ROWS)
    else:
        n = ch & (UNITS - 1)
        blk = lax.shift_right_logical(ch, UNITS.bit_length() - 1)
        rows = pl.ds(pl.multiple_of(blk * DOT_ROWS, DOT_ROWS), DOT_ROWS)
    part_ref[n, blk] = jnp.dot(vtb_ref[blk], p_ref[n, rows, :], preferred_element_type=F32)


def _head_tiles(part_ref, n_blocks):
    per_block = []
    for u in range(PAIR):
        outs = []
        for g in range(HEADS // GROUP):
            n = u * (HEADS // GROUP) + g
            o = part_ref[n, 0]
            for blk in range(1, part_ref.shape[1]):
                o = o + jnp.where(blk < n_blocks, part_ref[n, blk], 0.0)
            outs.append(o[0:HEAD_DIM, :] * (1.0 / o[HEAD_DIM:HEAD_DIM + 1, :]))
        tiles = []
        for t in range(HEADS // 2):
            o_g = outs[(2 * t) // GROUP]
            a = (2 * t) % GROUP
            tiles.append(jnp.concatenate([o_g[:, a * Q_BLOCK:(a + 1) * Q_BLOCK],
                                          o_g[:, (a + 1) * Q_BLOCK:(a + 2) * Q_BLOCK]], axis=0).T)
        per_block.append(tiles)
    return per_block


def _attn_kernel(qt_ref, qit_ref, wit_ref, kaug_ref, kk_ref, vt_ref, ma_ref, gate_ref, x_ref, gb_ref,
                 wo_ref, y_ref, idx_ref, rhs_ref, s_ref, p_ref, st_ref, vtb_ref, part_ref, done_ref,
                 *, seq, n_pairs, topk):
    g = pl.program_id(0)
    kf = float(topk)
    steps = seq // (PAIR * Q_BLOCK)
    key_blocks = seq // DOT_ROWS
    qcols = [slice(u * Q_BLOCK, (u + 1) * Q_BLOCK) for u in range(PAIR)]
    searching = g < n_pairs
    pending = g >= 1
    jj = lax.rem(g, steps)
    g_prev = jnp.maximum(g - 1, 0)
    slot = lax.div(g, steps) & 1
    slot_prev = lax.div(g_prev, steps) & 1

    @pl.when(g == 0)
    def _():
        p_ref[...] = jnp.zeros(p_ref.shape, BF16)
        vtb_ref[...] = jnp.zeros(vtb_ref.shape, BF16)
        part_ref[...] = jnp.zeros(part_ref.shape, F32)
        y_ref[...] = jnp.zeros(y_ref.shape, F32)

    @pl.when(jnp.logical_and(searching, jj == 0))
    def _():
        for u in range(PAIR):
            rhs_ref[u, :, 0:LANES] = kaug_ref[...]
        ones = jnp.ones((VALUE_ROWS - HEAD_DIM, DOT_ROWS), BF16)
        for blk in range(key_blocks):
            vtb_ref[slot, blk] = jnp.concatenate([vt_ref[:, blk * DOT_ROWS:(blk + 1) * DOT_ROWS], ones], axis=0)

    done_ref[0] = 0
    done_ref[1] = 0

    steps_per_bucket = KEY_BUCKET // (PAIR * Q_BLOCK)

    def buckets(step, active):
        return [((n + 1) * KEY_BUCKET,
                 jnp.logical_and(active, jnp.logical_and(step >= n * steps_per_bucket,
                                                         step < (n + 1) * steps_per_bucket)))
                for n in range(seq // KEY_BUCKET)]

    blocks = [PAIR * jj + u for u in range(PAIR)]
    total = UNITS * (lax.div(lax.rem(g_prev, steps), steps_per_bucket) + 1) * (KEY_BUCKET // DOT_ROWS)
    vtb_prev = vtb_ref.at[slot_prev]

    for nk, here in buckets(jj, searching):
        @pl.when(here)
        def _(nk=nk):
            lane_q = lax.broadcasted_iota(jnp.int32, (1, Q_BLOCK), 1)
            states = []
            for u in range(PAIR):
                n_adm = blocks[u] * Q_BLOCK + CHUNK + CHUNK * (lane_q >= CHUNK).astype(jnp.int32)
                lo, hi = _index_scores(nk, n_adm, qit_ref, qcols[u], wit_ref[:, qcols[u]], kk_ref, s_ref,
                                       idx_ref.at[u])
                n_adm_f = n_adm.astype(F32)
                c_lo = jnp.where(n_adm_f <= kf, kf, n_adm_f)
                states.append((lo, hi, c_lo, jnp.zeros((1, Q_BLOCK), F32)))

            def cond(carry):
                it, flat = carry[0], carry[1:]
                def tested():
                    still = jnp.logical_or(flat[2] != kf, flat[6] != kf)
                    return jnp.logical_and(it < FAST_TRIPS, _any_lane(still))

                return lax.cond(it < UNTESTED_TRIPS, lambda: jnp.bool_(True), tested)

            def body(carry):
                it, flat = carry[0], carry[1:]
                st = [flat[0:4], flat[4:8]]
                for _ in range(PROBES_PER_TRIP):
                    st = [_probe(nk, kf, idx_ref.at[u], st[u]) for u in range(PAIR)]
                for c in range(CHUNKS_PER_TRIP):
                    _value_chunk(jnp.minimum(it * CHUNKS_PER_TRIP + c, total - 1), vtb_prev, p_ref, part_ref)
                return (it + 1,) + tuple(st[0]) + tuple(st[1])

            final = lax.while_loop(cond, body, (jnp.int32(0),) + tuple(states[0]) + tuple(states[1]))
            done_ref[0] = jnp.minimum(final[0] * CHUNKS_PER_TRIP, total)
            done_ref[1] = final[0]
            for u in range(PAIR):
                lo, hi, c_lo, _ = final[1 + 4 * u:5 + 4 * u]
                _write_selection(nk, blocks[u], lo, idx_ref.at[u], rhs_ref.at[u])
                st_ref[4 * u + 0:4 * u + 1, :] = lo
                st_ref[4 * u + 1:4 * u + 2, :] = hi
                st_ref[4 * u + 2:4 * u + 3, :] = c_lo

    @pl.when(jnp.logical_and(pending, searching))
    def _():
        def chunk(ch, carry):
            _value_chunk(ch, vtb_prev, p_ref, part_ref)
            return carry

        lax.fori_loop(done_ref[0], total, chunk, 0)

    @pl.when(g == n_pairs)
    def _():
        for ch in range(UNITS * key_blocks):
            _value_chunk(ch, vtb_prev, p_ref, part_ref)

    lane = lax.broadcasted_iota(jnp.int32, (PAIR * Q_BLOCK, LANES), 1)
    lo_half = lane < HALF

    def finish_pending():
        per_block = _head_tiles(part_ref, lax.div(total, UNITS))
        mixed = [ma_ref[...]]
        for t in range(HEADS // 2):
            cols = slice(t * LANES, (t + 1) * LANES)
            o = jnp.concatenate([per_block[u][t] for u in range(PAIR)], axis=0)
            msq = _half_mean_sq(o * o, lo_half)
            ob = o * lax.rsqrt(msq + NORM_EPS) * gb_ref[:, cols] * gate_ref[:, cols].astype(F32)
            mixed.append(ob.astype(BF16))
        y_ref[...] = x_ref[...] + jnp.dot(jnp.concatenate(mixed, axis=1), wo_ref[...],
                                          preferred_element_type=F32)

    @pl.when(jnp.logical_and(searching, done_ref[1] >= FAST_TRIPS))
    def _():
        for u in range(PAIR):
            lo = st_ref[4 * u + 0:4 * u + 1, :]
            hi = st_ref[4 * u + 1:4 * u + 2, :]
            c_lo = st_ref[4 * u + 2:4 * u + 3, :]

            @pl.when(_any_lane(c_lo != kf))
            def _(u=u, lo=lo, hi=hi):
                _exact_fallback(blocks[u] + 1, kf, lo, hi, idx_ref.at[u], rhs_ref.at[u])

    for nk, here in buckets(jj, searching):
        @pl.when(here)
        def _(nk=nk):
            _probabilities(nk, qt_ref, qcols, rhs_ref, s_ref, p_ref, finish_pending)

    @pl.when(g == n_pairs)
    def _():
        finish_pending()


def kernel(x, norm_gain, w_in, sgu_norm_gain, sgu_w, sgu_b, q_norm_gain, k_norm_gain,
           idx_k_norm_gain, branch_norm_gain, w_out):
    bsz, seq, d_model = x.shape
    assert d_model == D_MODEL and norm_gain.shape[0] == 1
    assert seq % PROJ_ROWS == 0 and seq % KEY_BUCKET == 0 and KEY_BUCKET % (PAIR * Q_BLOCK) == 0
    tokens = bsz * seq
    topk = min(TOPK_MAX, seq // 4)
    idx_w_scale = (IDX_HEADS ** -0.5) * (IDX_DIM ** -0.5)

    w = w_in[0]
    a3 = 3 * A_WIDTH
    w_q = w[:, a3:a3 + B_WIDTH]
    w_k = w[:, a3 + B_WIDTH:a3 + B_WIDTH + HEAD_DIM]
    w_v = w[:, a3 + B_WIDTH + HEAD_DIM:a3 + B_WIDTH + 2 * HEAD_DIM]
    o_g = a3 + B_WIDTH + 2 * HEAD_DIM
    w_g = w[:, o_g:o_g + B_WIDTH]
    o_i = o_g + B_WIDTH
    w_iq = w[:, o_i:o_i + IDX_HEADS * IDX_DIM]
    w_ik = w[:, o_i + IDX_HEADS * IDX_DIM:o_i + IDX_HEADS * IDX_DIM + IDX_DIM]
    w_iw = w[:, o_i + IDX_HEADS * IDX_DIM + IDX_DIM:]
    w_main = jnp.concatenate([w[:, :a3], w_g, w_k, w_ik, w_v, w_iw,
                              jnp.zeros((D_MODEL, LANES - HEAD_DIM - IDX_HEADS), F32)], axis=1).astype(BF16)
    assert w_main.shape[1] == _PACKED_COLS
    w_t = jnp.concatenate([w_q.T, w_iq.T], axis=0).astype(BF16)
    assert w_t.shape[0] == _PACKED_ROWS
    x2 = x.reshape(tokens, D_MODEL)
    ng = norm_gain[0].reshape(1, D_MODEL)
    sgn = sgu_norm_gain[0].reshape(1, A_WIDTH)
    sw = sgu_w[0]
    sb = sgu_b[0].reshape(A_GROUPS, A_BLOCK, 1)
    qg = q_norm_gain[0].reshape(HEAD_DIM, 1)
    kg = jnp.concatenate([k_norm_gain[0], idx_k_norm_gain[0]]).reshape(1, LANES)
    ga = branch_norm_gain[0, :A_WIDTH].reshape(1, A_WIDTH)
    gb = branch_norm_gain[0, A_WIDTH:].reshape(1, B_WIDTH)
    wo = w_out[0].astype(BF16)

    tm = PROJ_ROWS
    full = lambda shape: pl.BlockSpec(shape, lambda i: (0,) * len(shape))
    rows = lambda width: pl.BlockSpec((tm, width), lambda i: (i, 0))
    colsT = lambda height: pl.BlockSpec((height, tm), lambda i: (0, i))
    outs = pl.pallas_call(
        functools.partial(_proj_kernel, tiles_per_seq=seq // tm, idx_w_scale=idx_w_scale),
        grid=(tokens // tm,),
        in_specs=[rows(D_MODEL), full((1, D_MODEL)), full((D_MODEL, _PACKED_COLS)),
                  full((_PACKED_ROWS, D_MODEL)), full((1, A_WIDTH)), full((A_GROUPS, A_BLOCK, A_BLOCK)),
                  full((A_GROUPS, A_BLOCK, 1)), full((HEAD_DIM, 1)), full((1, LANES)),
                  full((1, A_WIDTH))],
        out_specs=[rows(A_WIDTH), rows(B_WIDTH), rows(LANES), rows(LANES),
                   colsT(B_WIDTH), colsT(IDX_HEADS * IDX_DIM), colsT(HEAD_DIM), colsT(IDX_HEADS)],
        out_shape=[jax.ShapeDtypeStruct((tokens, A_WIDTH), BF16),
                   jax.ShapeDtypeStruct((tokens, B_WIDTH), BF16),
                   jax.ShapeDtypeStruct((tokens, LANES), BF16),
                   jax.ShapeDtypeStruct((tokens, LANES), BF16),
                   jax.ShapeDtypeStruct((B_WIDTH, tokens), BF16),
                   jax.ShapeDtypeStruct((IDX_HEADS * IDX_DIM, tokens), BF16),
                   jax.ShapeDtypeStruct((HEAD_DIM, tokens), BF16),
                   jax.ShapeDtypeStruct((IDX_HEADS, tokens), F32)],
        compiler_params=pltpu.CompilerParams(dimension_semantics=("arbitrary",),
                                             vmem_limit_bytes=VMEM_LIMIT),
        name="proj_sgu",
    )(x2, ng, w_main, w_t, sgn, sw, sb, qg, kg, ga)
    ma, gate, kaug, kk, qt, qit, vt, wit = outs

    qrows = PAIR * Q_BLOCK
    steps = seq // qrows
    n_pairs = bsz * steps
    cur = lambda g: jnp.minimum(g, n_pairs - 1)
    qblk = lambda width: pl.BlockSpec((qrows, width), lambda g: (jnp.maximum(g - 1, 0), 0))
    qblkT = lambda height: pl.BlockSpec((height, qrows), lambda g: (0, cur(g)))
    const = lambda shape: pl.BlockSpec(shape, lambda g: (0,) * len(shape))
    y = pl.pallas_call(
        functools.partial(_attn_kernel, seq=seq, n_pairs=n_pairs, topk=topk),
        grid=(n_pairs + 1,),
        in_specs=[qblkT(B_WIDTH), qblkT(IDX_HEADS * IDX_DIM), qblkT(IDX_HEADS),
                  pl.BlockSpec((seq, LANES), lambda g: (cur(g) // steps, 0)),
                  pl.BlockSpec((seq, LANES), lambda g: (cur(g) // steps, 0)),
                  pl.BlockSpec((HEAD_DIM, seq), lambda g: (0, cur(g) // steps)),
                  qblk(A_WIDTH), qblk(B_WIDTH), qblk(D_MODEL),
                  const((1, B_WIDTH)), const((D_MODEL, D_MODEL))],
        out_specs=qblk(D_MODEL),
        out_shape=jax.ShapeDtypeStruct((tokens, D_MODEL), F32),
        scratch_shapes=[pltpu.VMEM((PAIR, seq, Q_BLOCK), F32),
                        pltpu.VMEM((PAIR, seq, 2 * LANES), BF16),
                        pltpu.VMEM((2, seq, GROUP * Q_BLOCK), F32),
                        pltpu.VMEM((UNITS, seq, GROUP * Q_BLOCK), BF16),
                        pltpu.VMEM((8, Q_BLOCK), F32),
                        pltpu.VMEM((2, seq // DOT_ROWS, VALUE_ROWS, DOT_ROWS), BF16),
                        pltpu.VMEM((UNITS, seq // DOT_ROWS, VALUE_ROWS, GROUP * Q_BLOCK), F32),
                        pltpu.SMEM((2,), jnp.int32)],
        compiler_params=pltpu.CompilerParams(dimension_semantics=("arbitrary",),
                                             vmem_limit_bytes=VMEM_LIMIT),
        name="dsa_attn_out",
    )(qt, qit, wit, kaug, kk, vt, ma, gate, x2, gb, wo)
    return y.reshape(bsz, seq, D_MODEL)
```

```python
import functools

import jax
import jax.numpy as jnp
from jax import lax
from jax.experimental import pallas as pl
from jax.experimental.pallas import tpu as pltpu

F32 = jnp.float32
BF16 = jnp.bfloat16

D_MODEL = 1024
CHUNK = 64
A_WIDTH = 512
A_GROUPS = 4
A_BLOCK = 128
HEADS = 8
HEAD_DIM = 64
B_WIDTH = HEADS * HEAD_DIM
IDX_HEADS = 8
IDX_DIM = 64
TOPK_MAX = 256
Q_BLOCK = 128
PAIR = 2
NORM_EPS = 1e-6
MASK_OFF = 1e32
LANES = 128
HALF = LANES // 2
PROJ_ROWS = 1024
DOT_ROWS = 512
RED_ROWS = 128
KEY_BUCKET = 512
PROBES_PER_TRIP = 2
FAST_TRIPS = 24
UNTESTED_TRIPS = 8
INTERP_MARGIN = 0.02
SLOW_TRIPS = 70
CHUNKS_PER_TRIP = 2
VMEM_LIMIT = 48 * 1024 * 1024

_OFF_U, _OFF_V, _OFF_Z, _OFF_G, _OFF_K = 0, 512, 1024, 1536, 2048
_PACKED_COLS = 2304
_ROW_Q, _ROW_QI, _PACKED_ROWS = 0, 512, 1024

_NT = (((1,), (1,)), ((), ()))


def _gelu(x):
    c = 0.7978845608028654
    return 0.5 * x * (1.0 + jnp.tanh(c * (x + 0.044715 * (x * x * x))))


def _silu(x):
    return x / (1.0 + jnp.exp(-x))


def _row_blocks(total, size):
    return [slice(r, min(r + size, total)) for r in range(0, total, size)]


def _half_mean_sq(x2, lo_half):
    tot = jnp.sum(x2, axis=-1, keepdims=True)
    lo = jnp.sum(jnp.where(lo_half, x2, 0.0), axis=-1, keepdims=True)
    return jnp.where(lo_half, lo, tot - lo) * (1.0 / HALF)


def _proj_kernel(x_ref, ng_ref, w_ref, wt_ref, sgn_ref, sw_ref, sb_ref, qg_ref, kg_ref, ga_ref,
                 ma_ref, gate_ref, kaug_ref, kk_ref, qt_ref, qit_ref, vt_ref, wit_ref,
                 *, tiles_per_seq, idx_w_scale):
    tm = x_ref.shape[0]
    i = pl.program_id(0)
    x = x_ref[...]
    ms = jnp.mean(x * x, axis=-1, keepdims=True)
    h = (x * lax.rsqrt(ms + NORM_EPS) * ng_ref[...]).astype(BF16)

    lane = lax.broadcasted_iota(jnp.int32, (tm, LANES), 1)
    lo_half = lane < HALF

    def proj(off, width):
        return jnp.dot(h, w_ref[:, off:off + width], preferred_element_type=F32)

    gu = _gelu(proj(_OFF_U, A_WIDTH))
    gv = _gelu(proj(_OFF_V, A_WIDTH))
    pz = proj(_OFF_Z, A_WIDTH)
    r_i = lax.broadcasted_iota(jnp.int32, (A_BLOCK, A_BLOCK), 0)
    c_j = lax.broadcasted_iota(jnp.int32, (A_BLOCK, A_BLOCK), 1)
    causal = lax.shift_right_logical(c_j, 6) <= lax.shift_right_logical(r_i, 6)
    for g in range(A_GROUPS):
        cols = slice(g * LANES, (g + 1) * LANES)
        vg = gv[:, cols]
        mu = jnp.mean(vg, axis=-1, keepdims=True)
        d = vg - mu
        var = jnp.mean(d * d, axis=-1, keepdims=True)
        vn = (d * lax.rsqrt(var + NORM_EPS) * sgn_ref[:, cols]).astype(BF16)
        wg = jnp.where(causal, sw_ref[g], 0.0).astype(BF16)
        side = jnp.concatenate([vn[blk * A_BLOCK:(blk + 1) * A_BLOCK, :] for blk in range(tm // A_BLOCK)], axis=1)
        mixed = jnp.dot(wg, side, preferred_element_type=F32) + sb_ref[g]
        for blk in range(tm // A_BLOCK):
            rows = slice(blk * A_BLOCK, (blk + 1) * A_BLOCK)
            s = mixed[:, blk * LANES:(blk + 1) * LANES]
            ya = gu[rows, cols] * s
            oa = ya * lax.rsqrt(jnp.mean(ya * ya, axis=-1, keepdims=True) + NORM_EPS) * ga_ref[:, cols]
            ma_ref[rows, cols] = (oa * _silu(pz[rows, cols])).astype(BF16)

    gate_ref[...] = _silu(proj(_OFF_G, B_WIDTH)).astype(BF16)

    pkv = proj(_OFF_K, 2 * LANES)
    pk = pkv[:, 0:LANES]
    k_ms = jnp.sum(jnp.where(lo_half, pk * pk, 0.0), axis=-1, keepdims=True) * (1.0 / HALF)
    kn = pk * lax.rsqrt(k_ms + NORM_EPS)
    ik_mu = jnp.sum(jnp.where(lo_half, 0.0, pk), axis=-1, keepdims=True) * (1.0 / HALF)
    dk = pk - ik_mu
    ik_var = jnp.sum(jnp.where(lo_half, 0.0, dk * dk), axis=-1, keepdims=True) * (1.0 / HALF)
    kin = dk * lax.rsqrt(ik_var + NORM_EPS)
    tile = jnp.where(lo_half, kn, kin) * kg_ref[...]
    swapped = pltpu.roll(tile, HALF, axis=1)
    row = lax.broadcasted_iota(jnp.int32, (tm, LANES), 0)
    pos = (i % tiles_per_seq) * tm + row
    pos_hi = lax.shift_right_logical(pos, 6).astype(F32)
    pos_lo = (pos & (CHUNK - 1)).astype(F32)
    posfeat = jnp.where(lane == HALF, pos_hi, jnp.where(lane == HALF + 1, pos_lo, 0.0))
    kaug_ref[...] = jnp.where(lo_half, tile, posfeat).astype(BF16)
    kk_ref[...] = jnp.where(lo_half, swapped, tile).astype(BF16)

    pt = lax.dot_general(wt_ref[...], h, _NT, preferred_element_type=F32)
    for hh in range(HEADS):
        rows = slice(_ROW_Q + hh * HEAD_DIM, _ROW_Q + (hh + 1) * HEAD_DIM)
        xq = pt[rows, :]
        msq = jnp.mean(xq * xq, axis=0, keepdims=True)
        qt_ref[rows, :] = (xq * lax.rsqrt(msq + NORM_EPS) * qg_ref[...] * (HEAD_DIM ** -0.5)).astype(BF16)
    qit_ref[...] = pt[_ROW_QI:_ROW_QI + IDX_HEADS * IDX_DIM, :].astype(BF16)
    for blk in range(tm // LANES):
        cols = slice(blk * LANES, (blk + 1) * LANES)
        vw_t = pkv[cols, LANES:2 * LANES].T
        vt_ref[:, cols] = vw_t[0:HEAD_DIM, :].astype(BF16)
        wit_ref[:, cols] = vw_t[HEAD_DIM:HEAD_DIM + IDX_HEADS, :] * idx_w_scale


def _tile_iotas():
    r_k = lax.broadcasted_iota(jnp.int32, (Q_BLOCK, Q_BLOCK), 0)
    c_q = lax.broadcasted_iota(jnp.int32, (Q_BLOCK, Q_BLOCK), 1)
    return r_k, c_q


def _colsum(x):
    return jnp.sum(x, axis=0, keepdims=True)


def _any_lane(pred):
    return jnp.max(jnp.where(pred, 1.0, 0.0)) > 0.0


def _index_scores(nk, n_adm, qit_ref, qcols, wi, kk_ref, s_ref, idx_ref):
    r_k, c_q = _tile_iotas()
    top_rows = r_k < HALF
    per_head = []
    for t in range(IDX_HEADS // 2):
        qit = qit_ref[t * LANES:(t + 1) * LANES, qcols]
        zero = jnp.zeros_like(qit)
        per_head += [jnp.where(top_rows, qit, zero), jnp.where(top_rows, zero, qit)]
    half_heads = IDX_HEADS // 2
    for g in range(2):
        wg = jnp.concatenate(per_head[g * half_heads:(g + 1) * half_heads], axis=1)
        for rows in _row_blocks(nk, DOT_ROWS):
            s_ref[g, rows, :] = jnp.dot(kk_ref[rows, :], wg, preferred_element_type=F32)

    def weighted_relu(g, rows):
        acc = None
        for i in range(half_heads):
            hh = g * half_heads + i
            term = jnp.maximum(s_ref[g, rows, i * Q_BLOCK:(i + 1) * Q_BLOCK], 0.0) * wi[hh:hh + 1, :]
            acc = term if acc is None else acc + term
        return acc

    slabs = nk // RED_ROWS
    mn_acc = jnp.full((RED_ROWS, Q_BLOCK), jnp.inf, F32)
    mx_acc = jnp.full((RED_ROWS, Q_BLOCK), -jnp.inf, F32)
    tiny_acc = jnp.full((RED_ROWS, Q_BLOCK), jnp.inf, F32)
    for r in range(slabs):
        rows = slice(r * RED_ROWS, (r + 1) * RED_ROWS)
        acc = weighted_relu(0, rows) + weighted_relu(1, rows)
        if r >= slabs - KEY_BUCKET // RED_ROWS:
            adm = (r * RED_ROWS + r_k) < n_adm
            lo_fill = jnp.where(adm, acc, -jnp.inf)
            hi_fill = jnp.where(adm, acc, jnp.inf)
        else:
            lo_fill = hi_fill = acc
        mag = jnp.abs(hi_fill)
        idx_ref[rows, :] = lo_fill
        mn_acc = jnp.minimum(mn_acc, hi_fill)
        mx_acc = jnp.maximum(mx_acc, lo_fill)
        tiny_acc = jnp.minimum(tiny_acc, jnp.where(mag == 0.0, jnp.inf, mag))
    lo = jnp.min(mn_acc, axis=0, keepdims=True)
    hi = jnp.max(mx_acc, axis=0, keepdims=True)
    tiny = jnp.min(tiny_acc, axis=0, keepdims=True)

    unit = jnp.where(tiny < jnp.inf, tiny, 1.0)
    eps = unit * (0.5 / nk)
    rank0 = (1 + r_k).astype(F32)

    def spread(r, carry):
        rows = pl.ds(pl.multiple_of(r * RED_ROWS, RED_ROWS), RED_ROWS)
        s = idx_ref[rows, :]
        rank = rank0 + jnp.asarray(r * RED_ROWS, F32)
        idx_ref[rows, :] = jnp.where(s == 0.0, -(rank * eps), s)
        return carry

    lax.fori_loop(0, slabs, spread, 0, unroll=KEY_BUCKET // RED_ROWS)
    return jnp.minimum(lo, -0.5 * unit), hi


def _probe(nk, kf, idx_ref, state):
    lo, hi, c_lo, c_hi = state
    frac = (c_lo - kf) / jnp.maximum(c_lo - c_hi, 1.0)
    frac = jnp.minimum(jnp.maximum(frac, INTERP_MARGIN), 1.0 - INTERP_MARGIN)
    t = lo + (hi - lo) * frac
    acc = jnp.zeros((RED_ROWS, Q_BLOCK), F32)
    for r in range(nk // RED_ROWS):
        acc = acc + jnp.where(idx_ref[r * RED_ROWS:(r + 1) * RED_ROWS, :] >= t, 1.0, 0.0)
    c = _colsum(acc)
    ge = c >= kf
    return (jnp.where(ge, t, lo), jnp.where(ge, hi, t), jnp.where(ge, c, c_lo), jnp.where(ge, c_hi, c))


def _write_selection(nk, j_blk, lo, idx_ref, rhs_ref):
    r_k, c_q = _tile_iotas()
    later = 2.0 * jnp.maximum(r_k - c_q, 0).astype(F32)

    def write(r, carry):
        rows = pl.ds(pl.multiple_of(r * RED_ROWS, RED_ROWS), RED_ROWS)
        on = jnp.where(r == j_blk, -later, 0.0)
        rhs_ref[rows, LANES:2 * LANES] = jnp.where(idx_ref[rows, :] >= lo, on, -MASK_OFF).astype(BF16)
        return carry

    lax.fori_loop(0, nk // RED_ROWS, write, 0, unroll=KEY_BUCKET // RED_ROWS)


def _exact_fallback(slabs, kf, lo, hi, idx_ref, rhs_ref):
    r_k, c_q = _tile_iotas()
    later = 2.0 * jnp.maximum(r_k - c_q, 0).astype(F32)

    def slab(r):
        return idx_ref[pl.ds(pl.multiple_of(r * RED_ROWS, RED_ROWS), RED_ROWS), :]

    def count(pred):
        def body(r, acc):
            return acc + jnp.where(pred(slab(r), r), 1.0, 0.0)
        return _colsum(lax.fori_loop(0, slabs, body, jnp.zeros((RED_ROWS, Q_BLOCK), F32)))

    def kth(lo):
        def body(r, acc):
            s = slab(r)
            return jnp.minimum(acc, jnp.where(s >= lo, s, jnp.inf))
        acc = lax.fori_loop(0, slabs, body, jnp.full((RED_ROWS, Q_BLOCK), jnp.inf, F32))
        thr = jnp.min(acc, axis=0, keepdims=True)
        return thr, count(lambda s, r: s > thr)

    def slow_cond(carry):
        it, _, _, _, c_gt = carry
        return jnp.logical_and(it < SLOW_TRIPS, _any_lane(c_gt >= kf))

    def slow_body(carry):
        it, lo, hi, _, _ = carry
        for _ in range(4):
            mid = 0.5 * lo + 0.5 * hi
            ge = count(lambda s, r: s >= mid) >= kf
            lo, hi = jnp.where(ge, mid, lo), jnp.where(ge, hi, mid)
        thr, c_gt = kth(lo)
        return it + 1, lo, hi, thr, c_gt

    thr0, c_gt0 = kth(lo)
    _, _, _, thr, c_gt = lax.while_loop(slow_cond, slow_body, (jnp.int32(0), lo, hi, thr0, c_gt0))
    c_eq = count(lambda s, r: s == thr)
    need = (c_gt + c_eq) > kf

    def tie_step(_, carry):
        lo_i, hi_i = carry
        mid_i = jnp.floor((lo_i + hi_i) * 0.5)
        below = count(lambda s, r: jnp.logical_and(s == thr, (r * RED_ROWS + r_k).astype(F32) <= mid_i))
        ok = (c_gt + below) >= kf
        return jnp.where(ok, lo_i, mid_i), jnp.where(ok, mid_i, hi_i)

    last = jnp.asarray(slabs * RED_ROWS - 1, F32)
    lo_i = jnp.full((1, Q_BLOCK), -1.0, F32)
    hi_i = jnp.zeros((1, Q_BLOCK), F32) + last
    _, cut = lax.fori_loop(0, 13, tie_step, (lo_i, hi_i))
    cut = jnp.where(need, cut, last + 1.0).astype(jnp.int32)

    def write(r, carry):
        s = slab(r)
        krow = r * RED_ROWS + r_k
        sel = jnp.logical_or(s > thr, jnp.logical_and(s == thr, krow <= cut))
        on = jnp.where(r == slabs - 1, -later, 0.0)
        rhs_ref[pl.ds(pl.multiple_of(r * RED_ROWS, RED_ROWS), RED_ROWS), LANES:2 * LANES] = (
            jnp.where(sel, on, -MASK_OFF).astype(BF16))
        return carry

    lax.fori_loop(0, slabs, write, 0)


GROUP = HEADS // 2
UNITS = PAIR * (HEADS // GROUP)
VALUE_ROWS = HEAD_DIM + 16


def _probabilities(nk, qt_ref, qcols, rhs_ref, s_ref, p_ref, between):
    r_k, c_q = _tile_iotas()
    feat = lax.broadcasted_iota(jnp.int32, (HALF, Q_BLOCK), 0)
    group = GROUP
    width = group * Q_BLOCK
    units = [(u, g) for u in range(len(qcols)) for g in range(HEADS // group)]

    def score_operand(u, hh):
        slope = 2.0 ** (-(hh + 1))
        alibi = jnp.where(feat == 0, CHUNK * slope, jnp.where(feat == 1, slope, 0.0)).astype(BF16)
        scaled_ident = jnp.where(r_k == c_q, slope, 0.0).astype(BF16)
        return jnp.concatenate([qt_ref[hh * HEAD_DIM:(hh + 1) * HEAD_DIM, qcols[u]], alibi, scaled_ident], axis=0)

    def scores(n):
        u, g = units[n]
        lhs_t = jnp.concatenate([score_operand(u, g * group + i) for i in range(group)], axis=1)
        m_acc = jnp.full((RED_ROWS, width), -jnp.inf, F32)
        for rows in _row_blocks(nk, DOT_ROWS):
            blk = jnp.dot(rhs_ref[u, rows, :], lhs_t, preferred_element_type=F32)
            s_ref[n % 2, rows, :] = blk
            for sub in range((rows.stop - rows.start) // RED_ROWS):
                m_acc = jnp.maximum(m_acc, blk[sub * RED_ROWS:(sub + 1) * RED_ROWS, :])
        return jnp.max(m_acc, axis=0, keepdims=True)

    def probabilities(n, m):
        for rows in _row_blocks(nk, RED_ROWS):
            p_ref[n, rows, :] = jnp.exp((s_ref[n % 2, rows, :] - m).astype(BF16))

    m = scores(0)
    between()
    for n in range(len(units)):
        m_next = scores(n + 1) if n + 1 < len(units) else None
        probabilities(n, m)
        m = m_next


def _value_chunk(ch, vtb_ref, p_ref, part_ref):
    if isinstance(ch, int):
        n, blk = ch % UNITS, ch // UNITS
        rows = slice(blk * DOT_ROWS, (blk + 1) * DOT_ROWS)
    else:
        n = ch & (UNITS - 1)
        blk = lax.shift_right_logical(ch, UNITS.bit_length() - 1)
        rows = pl.ds(pl.multiple_of(blk * DOT_ROWS, DOT_ROWS), DOT_ROWS)
    part_ref[n, blk] = jnp.dot(vtb_ref[blk], p_ref[n, rows, :], preferred_element_type=F32)


def _head_tiles(part_ref, n_blocks):
    per_block = []
    for u in range(PAIR):
        outs = []
        for g in range(HEADS // GROUP):
            n = u * (HEADS // GROUP) + g
            o = part_ref[n, 0]
            for blk in range(1, part_ref.shape[1]):
                o = o + jnp.where(blk < n_blocks, part_ref[n, blk], 0.0)
            outs.append(o[0:HEAD_DIM, :] * (1.0 / o[HEAD_DIM:HEAD_DIM + 1, :]))
        tiles = []
        for t in range(HEADS // 2):
            o_g = outs[(2 * t) // GROUP]
            a = (2 * t) % GROUP
            tiles.append(jnp.concatenate([o_g[:, a * Q_BLOCK:(a + 1) * Q_BLOCK],
                                          o_g[:, (a + 1) * Q_BLOCK:(a + 2) * Q_BLOCK]], axis=0).T)
        per_block.append(tiles)
    return per_block


def _attn_kernel(qt_ref, qit_ref, wit_ref, kaug_ref, kk_ref, vt_ref, ma_ref, gate_ref, x_ref, gb_ref,
                 wo_ref, y_ref, idx_ref, rhs_ref, s_ref, p_ref, st_ref, vtb_ref, part_ref, done_ref,
                 *, seq, n_pairs, topk):
    g = pl.program_id(0)
    kf = float(topk)
    steps = seq // (PAIR * Q_BLOCK)
    key_blocks = seq // DOT_ROWS
    qcols = [slice(u * Q_BLOCK, (u + 1) * Q_BLOCK) for u in range(PAIR)]
    searching = g < n_pairs
    pending = g >= 1
    jj = lax.rem(g, steps)
    g_prev = jnp.maximum(g - 1, 0)
    slot = lax.div(g, steps) & 1
    slot_prev = lax.div(g_prev, steps) & 1

    @pl.when(g == 0)
    def _():
        p_ref[...] = jnp.zeros(p_ref.shape, BF16)
        vtb_ref[...] = jnp.zeros(vtb_ref.shape, BF16)
        part_ref[...] = jnp.zeros(part_ref.shape, F32)
        y_ref[...] = jnp.zeros(y_ref.shape, F32)

    @pl.when(jnp.logical_and(searching, jj == 0))
    def _():
        for u in range(PAIR):
            rhs_ref[u, :, 0:LANES] = kaug_ref[...]
        ones = jnp.ones((VALUE_ROWS - HEAD_DIM, DOT_ROWS), BF16)
        for blk in range(key_blocks):
            vtb_ref[slot, blk] = jnp.concatenate([vt_ref[:, blk * DOT_ROWS:(blk + 1) * DOT_ROWS], ones], axis=0)

    done_ref[0] = 0
    done_ref[1] = 0

    steps_per_bucket = KEY_BUCKET // (PAIR * Q_BLOCK)

    def buckets(step, active):
        return [((n + 1) * KEY_BUCKET,
                 jnp.logical_and(active, jnp.logical_and(step >= n * steps_per_bucket,
                                                         step < (n + 1) * steps_per_bucket)))
                for n in range(seq // KEY_BUCKET)]

    blocks = [PAIR * jj + u for u in range(PAIR)]
    total = UNITS * (lax.div(lax.rem(g_prev, steps), steps_per_bucket) + 1) * (KEY_BUCKET // DOT_ROWS)
    vtb_prev = vtb_ref.at[slot_prev]

    for nk, here in buckets(jj, searching):
        @pl.when(here)
        def _(nk=nk):
            lane_q = lax.broadcasted_iota(jnp.int32, (1, Q_BLOCK), 1)
            states = []
            for u in range(PAIR):
                n_adm = blocks[u] * Q_BLOCK + CHUNK + CHUNK * (lane_q >= CHUNK).astype(jnp.int32)
                lo, hi = _index_scores(nk, n_adm, qit_ref, qcols[u], wit_ref[:, qcols[u]], kk_ref, s_ref,
                                       idx_ref.at[u])
                n_adm_f = n_adm.astype(F32)
                c_lo = jnp.where(n_adm_f <= kf, kf, n_adm_f)
                states.append((lo, hi, c_lo, jnp.zeros((1, Q_BLOCK), F32)))

            def cond(carry):
                it, flat = carry[0], carry[1:]
                def tested():
                    still = jnp.logical_or(flat[2] != kf, flat[6] != kf)
                    return jnp.logical_and(it < FAST_TRIPS, _any_lane(still))

                return lax.cond(it < UNTESTED_TRIPS, lambda: jnp.bool_(True), tested)

            def body(carry):
                it, flat = carry[0], carry[1:]
                st = [flat[0:4], flat[4:8]]
                for _ in range(PROBES_PER_TRIP):
                    st = [_probe(nk, kf, idx_ref.at[u], st[u]) for u in range(PAIR)]
                for c in range(CHUNKS_PER_TRIP):
                    _value_chunk(jnp.minimum(it * CHUNKS_PER_TRIP + c, total - 1), vtb_prev, p_ref, part_ref)
                return (it + 1,) + tuple(st[0]) + tuple(st[1])

            final = lax.while_loop(cond, body, (jnp.int32(0),) + tuple(states[0]) + tuple(states[1]))
            done_ref[0] = jnp.minimum(final[0] * CHUNKS_PER_TRIP, total)
            done_ref[1] = final[0]
            for u in range(PAIR):
                lo, hi, c_lo, _ = final[1 + 4 * u:5 + 4 * u]
                _write_selection(nk, blocks[u], lo, idx_ref.at[u], rhs_ref.at[u])
                st_ref[4 * u + 0:4 * u + 1, :] = lo
                st_ref[4 * u + 1:4 * u + 2, :] = hi
                st_ref[4 * u + 2:4 * u + 3, :] = c_lo

    @pl.when(jnp.logical_and(pending, searching))
    def _():
        def chunk(ch, carry):
            _value_chunk(ch, vtb_prev, p_ref, part_ref)
            return carry

        lax.fori_loop(done_ref[0], total, chunk, 0)

    @pl.when(g == n_pairs)
    def _():
        for ch in range(UNITS * key_blocks):
            _value_chunk(ch, vtb_prev, p_ref, part_ref)

    lane = lax.broadcasted_iota(jnp.int32, (PAIR * Q_BLOCK, LANES), 1)
    lo_half = lane < HALF

    def finish_pending():
        per_block = _head_tiles(part_ref, lax.div(total, UNITS))
        mixed = [ma_ref[...]]
        for t in range(HEADS // 2):
            cols = slice(t * LANES, (t + 1) * LANES)
            o = jnp.concatenate([per_block[u][t] for u in range(PAIR)], axis=0)
            msq = _half_mean_sq(o * o, lo_half)
            ob = o * lax.rsqrt(msq + NORM_EPS) * gb_ref[:, cols] * gate_ref[:, cols].astype(F32)
            mixed.append(ob.astype(BF16))
        y_ref[...] = x_ref[...] + jnp.dot(jnp.concatenate(mixed, axis=1), wo_ref[...],
                                          preferred_element_type=F32)

    @pl.when(jnp.logical_and(searching, done_ref[1] >= FAST_TRIPS))
    def _():
        for u in range(PAIR):
            lo = st_ref[4 * u + 0:4 * u + 1, :]
            hi = st_ref[4 * u + 1:4 * u + 2, :]
            c_lo = st_ref[4 * u + 2:4 * u + 3, :]

            @pl.when(_any_lane(c_lo != kf))
            def _(u=u, lo=lo, hi=hi):
                _exact_fallback(blocks[u] + 1, kf, lo, hi, idx_ref.at[u], rhs_ref.at[u])

    for nk, here in buckets(jj, searching):
        @pl.when(here)
        def _(nk=nk):
            _probabilities(nk, qt_ref, qcols, rhs_ref, s_ref, p_ref, finish_pending)

    @pl.when(g == n_pairs)
    def _():
        finish_pending()


def kernel(x, norm_gain, w_in, sgu_norm_gain, sgu_w, sgu_b, q_norm_gain, k_norm_gain,
           idx_k_norm_gain, branch_norm_gain, w_out):
    bsz, seq, d_model = x.shape
    assert d_model == D_MODEL and norm_gain.shape[0] == 1
    assert seq % PROJ_ROWS == 0 and seq % KEY_BUCKET == 0 and KEY_BUCKET % (PAIR * Q_BLOCK) == 0
    tokens = bsz * seq
    topk = min(TOPK_MAX, seq // 4)
    idx_w_scale = (IDX_HEADS ** -0.5) * (IDX_DIM ** -0.5)

    w = w_in[0]
    a3 = 3 * A_WIDTH
    w_q = w[:, a3:a3 + B_WIDTH]
    w_k = w[:, a3 + B_WIDTH:a3 + B_WIDTH + HEAD_DIM]
    w_v = w[:, a3 + B_WIDTH + HEAD_DIM:a3 + B_WIDTH + 2 * HEAD_DIM]
    o_g = a3 + B_WIDTH + 2 * HEAD_DIM
    w_g = w[:, o_g:o_g + B_WIDTH]
    o_i = o_g + B_WIDTH
    w_iq = w[:, o_i:o_i + IDX_HEADS * IDX_DIM]
    w_ik = w[:, o_i + IDX_HEADS * IDX_DIM:o_i + IDX_HEADS * IDX_DIM + IDX_DIM]
    w_iw = w[:, o_i + IDX_HEADS * IDX_DIM + IDX_DIM:]
    w_main = jnp.concatenate([w[:, :a3], w_g, w_k, w_ik, w_v, w_iw,
                              jnp.zeros((D_MODEL, LANES - HEAD_DIM - IDX_HEADS), F32)], axis=1).astype(BF16)
    assert w_main.shape[1] == _PACKED_COLS
    w_t = jnp.concatenate([w_q.T, w_iq.T], axis=0).astype(BF16)
    assert w_t.shape[0] == _PACKED_ROWS
    x2 = x.reshape(tokens, D_MODEL)
    ng = norm_gain[0].reshape(1, D_MODEL)
    sgn = sgu_norm_gain[0].reshape(1, A_WIDTH)
    sw = sgu_w[0]
    sb = sgu_b[0].reshape(A_GROUPS, A_BLOCK, 1)
    qg = q_norm_gain[0].reshape(HEAD_DIM, 1)
    kg = jnp.concatenate([k_norm_gain[0], idx_k_norm_gain[0]]).reshape(1, LANES)
    ga = branch_norm_gain[0, :A_WIDTH].reshape(1, A_WIDTH)
    gb = branch_norm_gain[0, A_WIDTH:].reshape(1, B_WIDTH)
    wo = w_out[0].astype(BF16)

    tm = PROJ_ROWS
    full = lambda shape: pl.BlockSpec(shape, lambda i: (0,) * len(shape))
    rows = lambda width: pl.BlockSpec((tm, width), lambda i: (i, 0))
    colsT = lambda height: pl.BlockSpec((height, tm), lambda i: (0, i))
    outs = pl.pallas_call(
        functools.partial(_proj_kernel, tiles_per_seq=seq // tm, idx_w_scale=idx_w_scale),
        grid=(tokens // tm,),
        in_specs=[rows(D_MODEL), full((1, D_MODEL)), full((D_MODEL, _PACKED_COLS)),
                  full((_PACKED_ROWS, D_MODEL)), full((1, A_WIDTH)), full((A_GROUPS, A_BLOCK, A_BLOCK)),
                  full((A_GROUPS, A_BLOCK, 1)), full((HEAD_DIM, 1)), full((1, LANES)),
                  full((1, A_WIDTH))],
        out_specs=[rows(A_WIDTH), rows(B_WIDTH), rows(LANES), rows(LANES),
                   colsT(B_WIDTH), colsT(IDX_HEADS * IDX_DIM), colsT(HEAD_DIM), colsT(IDX_HEADS)],
        out_shape=[jax.ShapeDtypeStruct((tokens, A_WIDTH), BF16),
                   jax.ShapeDtypeStruct((tokens, B_WIDTH), BF16),
                   jax.ShapeDtypeStruct((tokens, LANES), BF16),
                   jax.ShapeDtypeStruct((tokens, LANES), BF16),
                   jax.ShapeDtypeStruct((B_WIDTH, tokens), BF16),
                   jax.ShapeDtypeStruct((IDX_HEADS * IDX_DIM, tokens), BF16),
                   jax.ShapeDtypeStruct((HEAD_DIM, tokens), BF16),
                   jax.ShapeDtypeStruct((IDX_HEADS, tokens), F32)],
        compiler_params=pltpu.CompilerParams(dimension_semantics=("arbitrary",),
                                             vmem_limit_bytes=VMEM_LIMIT),
        name="proj_sgu",
    )(x2, ng, w_main, w_t, sgn, sw, sb, qg, kg, ga)
    ma, gate, kaug, kk, qt, qit, vt, wit = outs

    qrows = PAIR * Q_BLOCK
    steps = seq // qrows
    n_pairs = bsz * steps
    cur = lambda g: jnp.minimum(g, n_pairs - 1)
    qblk = lambda width: pl.BlockSpec((qrows, width), lambda g: (jnp.maximum(g - 1, 0), 0))
    qblkT = lambda height: pl.BlockSpec((height, qrows), lambda g: (0, cur(g)))
    const = lambda shape: pl.BlockSpec(shape, lambda g: (0,) * len(shape))
    y = pl.pallas_call(
        functools.partial(_attn_kernel, seq=seq, n_pairs=n_pairs, topk=topk),
        grid=(n_pairs + 1,),
        in_specs=[qblkT(B_WIDTH), qblkT(IDX_HEADS * IDX_DIM), qblkT(IDX_HEADS),
                  pl.BlockSpec((seq, LANES), lambda g: (cur(g) // steps, 0)),
                  pl.BlockSpec((seq, LANES), lambda g: (cur(g) // steps, 0)),
                  pl.BlockSpec((HEAD_DIM, seq), lambda g: (0, cur(g) // steps)),
                  qblk(A_WIDTH), qblk(B_WIDTH), qblk(D_MODEL),
                  const((1, B_WIDTH)), const((D_MODEL, D_MODEL))],
        out_specs=qblk(D_MODEL),
        out_shape=jax.ShapeDtypeStruct((tokens, D_MODEL), F32),
        scratch_shapes=[pltpu.VMEM((PAIR, seq, Q_BLOCK), F32),
                        pltpu.VMEM((PAIR, seq, 2 * LANES), BF16),
                        pltpu.VMEM((2, seq, GROUP * Q_BLOCK), F32),
                        pltpu.VMEM((UNITS, seq, GROUP * Q_BLOCK), BF16),
                        pltpu.VMEM((8, Q_BLOCK), F32),
                        pltpu.VMEM((2, seq // DOT_ROWS, VALUE_ROWS, DOT_ROWS), BF16),
                        pltpu.VMEM((UNITS, seq // DOT_ROWS, VALUE_ROWS, GROUP * Q_BLOCK), F32),
                        pltpu.SMEM((2,), jnp.int32)],
        compiler_params=pltpu.CompilerParams(dimension_semantics=("arbitrary",),
                                             vmem_limit_bytes=VMEM_LIMIT),
        name="dsa_attn_out",
    )(qt, qit, wit, kaug, kk, vt, ma, gate, x2, gb, wo)
    return y.reshape(bsz, seq, D_MODEL)
```

```python
import functools

import jax
import jax.numpy as jnp
from jax import lax
from jax.experimental import pallas as pl
from jax.experimental.pallas import tpu as pltpu

F32 = jnp.float32
BF16 = jnp.bfloat16

D_MODEL = 1024
CHUNK = 64
A_WIDTH = 512
A_GROUPS = 4
A_BLOCK = 128
HEADS = 8
HEAD_DIM = 64
B_WIDTH = HEADS * HEAD_DIM
IDX_HEADS = 8
IDX_DIM = 64
TOPK_MAX = 256
Q_BLOCK = 128
PAIR = 2
NORM_EPS = 1e-6
MASK_OFF = 1e32
LANES = 128
HALF = LANES // 2
PROJ_ROWS = 1024
DOT_ROWS = 512
RED_ROWS = 128
KEY_BUCKET = 512
PROBES_PER_TRIP = 3
FAST_TRIPS = 16
UNTESTED_TRIPS = 5
INTERP_MARGIN = 0.02
SLOW_TRIPS = 70
CHUNKS_PER_TRIP = 3
VMEM_LIMIT = 48 * 1024 * 1024

_OFF_U, _OFF_V, _OFF_Z, _OFF_G, _OFF_K = 0, 512, 1024, 1536, 2048
_PACKED_COLS = 2304
_ROW_Q, _ROW_QI, _PACKED_ROWS = 0, 512, 1024

_NT = (((1,), (1,)), ((), ()))


def _gelu(x):
    c = 0.7978845608028654
    return 0.5 * x * (1.0 + jnp.tanh(c * (x + 0.044715 * (x * x * x))))


def _silu(x):
    return x / (1.0 + jnp.exp(-x))


def _row_blocks(total, size):
    return [slice(r, min(r + size, total)) for r in range(0, total, size)]


def _half_mean_sq(x2, lo_half):
    tot = jnp.sum(x2, axis=-1, keepdims=True)
    lo = jnp.sum(jnp.where(lo_half, x2, 0.0), axis=-1, keepdims=True)
    return jnp.where(lo_half, lo, tot - lo) * (1.0 / HALF)


def _proj_kernel(x_ref, ng_ref, w_ref, wt_ref, sgn_ref, sw_ref, sb_ref, qg_ref, kg_ref, ga_ref,
                 ma_ref, gate_ref, kaug_ref, kk_ref, qt_ref, qit_ref, vt_ref, wit_ref,
                 *, tiles_per_seq, idx_w_scale):
    tm = x_ref.shape[0]
    i = pl.program_id(0)
    x = x_ref[...]
    ms = jnp.mean(x * x, axis=-1, keepdims=True)
    h = (x * lax.rsqrt(ms + NORM_EPS) * ng_ref[...]).astype(BF16)

    lane = lax.broadcasted_iota(jnp.int32, (tm, LANES), 1)
    lo_half = lane < HALF

    def proj(off, width):
        return jnp.dot(h, w_ref[:, off:off + width], preferred_element_type=F32)

    gu = _gelu(proj(_OFF_U, A_WIDTH))
    gv = _gelu(proj(_OFF_V, A_WIDTH))
    pz = proj(_OFF_Z, A_WIDTH)
    r_i = lax.broadcasted_iota(jnp.int32, (A_BLOCK, A_BLOCK), 0)
    c_j = lax.broadcasted_iota(jnp.int32, (A_BLOCK, A_BLOCK), 1)
    causal = lax.shift_right_logical(c_j, 6) <= lax.shift_right_logical(r_i, 6)
    for g in range(A_GROUPS):
        cols = slice(g * LANES, (g + 1) * LANES)
        vg = gv[:, cols]
        mu = jnp.mean(vg, axis=-1, keepdims=True)
        d = vg - mu
        var = jnp.mean(d * d, axis=-1, keepdims=True)
        vn = (d * lax.rsqrt(var + NORM_EPS) * sgn_ref[:, cols]).astype(BF16)
        wg = jnp.where(causal, sw_ref[g], 0.0).astype(BF16)
        side = jnp.concatenate([vn[blk * A_BLOCK:(blk + 1) * A_BLOCK, :] for blk in range(tm // A_BLOCK)], axis=1)
        mixed = jnp.dot(wg, side, preferred_element_type=F32) + sb_ref[g]
        for blk in range(tm // A_BLOCK):
            rows = slice(blk * A_BLOCK, (blk + 1) * A_BLOCK)
            s = mixed[:, blk * LANES:(blk + 1) * LANES]
            ya = gu[rows, cols] * s
            oa = ya * lax.rsqrt(jnp.mean(ya * ya, axis=-1, keepdims=True) + NORM_EPS) * ga_ref[:, cols]
            ma_ref[rows, cols] = (oa * _silu(pz[rows, cols])).astype(BF16)

    gate_ref[...] = _silu(proj(_OFF_G, B_WIDTH)).astype(BF16)

    pkv = proj(_OFF_K, 2 * LANES)
    pk = pkv[:, 0:LANES]
    k_ms = jnp.sum(jnp.where(lo_half, pk * pk, 0.0), axis=-1, keepdims=True) * (1.0 / HALF)
    kn = pk * lax.rsqrt(k_ms + NORM_EPS)
    ik_mu = jnp.sum(jnp.where(lo_half, 0.0, pk), axis=-1, keepdims=True) * (1.0 / HALF)
    dk = pk - ik_mu
    ik_var = jnp.sum(jnp.where(lo_half, 0.0, dk * dk), axis=-1, keepdims=True) * (1.0 / HALF)
    kin = dk * lax.rsqrt(ik_var + NORM_EPS)
    tile = jnp.where(lo_half, kn, kin) * kg_ref[...]
    swapped = pltpu.roll(tile, HALF, axis=1)
    row = lax.broadcasted_iota(jnp.int32, (tm, LANES), 0)
    pos = (i % tiles_per_seq) * tm + row
    pos_hi = lax.shift_right_logical(pos, 6).astype(F32)
    pos_lo = (pos & (CHUNK - 1)).astype(F32)
    posfeat = jnp.where(lane == HALF, pos_hi, jnp.where(lane == HALF + 1, pos_lo, 0.0))
    kaug_ref[...] = jnp.where(lo_half, tile, posfeat).astype(BF16)
    kk_ref[...] = jnp.where(lo_half, swapped, tile).astype(BF16)

    pt = lax.dot_general(wt_ref[...], h, _NT, preferred_element_type=F32)
    for hh in range(HEADS):
        rows = slice(_ROW_Q + hh * HEAD_DIM, _ROW_Q + (hh + 1) * HEAD_DIM)
        xq = pt[rows, :]
        msq = jnp.mean(xq * xq, axis=0, keepdims=True)
        qt_ref[rows, :] = (xq * lax.rsqrt(msq + NORM_EPS) * qg_ref[...] * (HEAD_DIM ** -0.5)).astype(BF16)
    qit_ref[...] = pt[_ROW_QI:_ROW_QI + IDX_HEADS * IDX_DIM, :].astype(BF16)
    for blk in range(tm // LANES):
        cols = slice(blk * LANES, (blk + 1) * LANES)
        vw_t = pkv[cols, LANES:2 * LANES].T
        vt_ref[:, cols] = vw_t[0:HEAD_DIM, :].astype(BF16)
        wit_ref[:, cols] = vw_t[HEAD_DIM:HEAD_DIM + IDX_HEADS, :] * idx_w_scale


def _tile_iotas():
    r_k = lax.broadcasted_iota(jnp.int32, (Q_BLOCK, Q_BLOCK), 0)
    c_q = lax.broadcasted_iota(jnp.int32, (Q_BLOCK, Q_BLOCK), 1)
    return r_k, c_q


def _colsum(x):
    return jnp.sum(x, axis=0, keepdims=True)


def _any_lane(pred):
    return jnp.max(jnp.where(pred, 1.0, 0.0)) > 0.0


def _index_scores(nk, n_adm, qit_ref, qcols, wi, kk_ref, s_ref, idx_ref):
    r_k, c_q = _tile_iotas()
    top_rows = r_k < HALF
    per_head = []
    for t in range(IDX_HEADS // 2):
        qit = qit_ref[t * LANES:(t + 1) * LANES, qcols]
        zero = jnp.zeros_like(qit)
        per_head += [jnp.where(top_rows, qit, zero), jnp.where(top_rows, zero, qit)]
    half_heads = IDX_HEADS // 2
    for g in range(2):
        wg = jnp.concatenate(per_head[g * half_heads:(g + 1) * half_heads], axis=1)
        for rows in _row_blocks(nk, DOT_ROWS):
            s_ref[g, rows, :] = jnp.dot(kk_ref[rows, :], wg, preferred_element_type=F32)

    def weighted_relu(g, rows):
        acc = None
        for i in range(half_heads):
            hh = g * half_heads + i
            term = jnp.maximum(s_ref[g, rows, i * Q_BLOCK:(i + 1) * Q_BLOCK], 0.0) * wi[hh:hh + 1, :]
            acc = term if acc is None else acc + term
        return acc

    slabs = nk // RED_ROWS
    mn_acc = jnp.full((RED_ROWS, Q_BLOCK), jnp.inf, F32)
    mx_acc = jnp.full((RED_ROWS, Q_BLOCK), -jnp.inf, F32)
    tiny_acc = jnp.full((RED_ROWS, Q_BLOCK), jnp.inf, F32)
    for r in range(slabs):
        rows = slice(r * RED_ROWS, (r + 1) * RED_ROWS)
        acc = weighted_relu(0, rows) + weighted_relu(1, rows)
        if r >= slabs - KEY_BUCKET // RED_ROWS:
            adm = (r * RED_ROWS + r_k) < n_adm
            lo_fill = jnp.where(adm, acc, -jnp.inf)
            hi_fill = jnp.where(adm, acc, jnp.inf)
        else:
            lo_fill = hi_fill = acc
        mag = jnp.abs(hi_fill)
        idx_ref[rows, :] = lo_fill
        mn_acc = jnp.minimum(mn_acc, hi_fill)
        mx_acc = jnp.maximum(mx_acc, lo_fill)
        tiny_acc = jnp.minimum(tiny_acc, jnp.where(mag == 0.0, jnp.inf, mag))
    lo = jnp.min(mn_acc, axis=0, keepdims=True)
    hi = jnp.max(mx_acc, axis=0, keepdims=True)
    tiny = jnp.min(tiny_acc, axis=0, keepdims=True)

    unit = jnp.where(tiny < jnp.inf, tiny, 1.0)
    eps = unit * (0.5 / nk)
    rank0 = (1 + r_k).astype(F32)

    def spread(r, carry):
        rows = pl.ds(pl.multiple_of(r * RED_ROWS, RED_ROWS), RED_ROWS)
        s = idx_ref[rows, :]
        rank = rank0 + jnp.asarray(r * RED_ROWS, F32)
        idx_ref[rows, :] = jnp.where(s == 0.0, -(rank * eps), s)
        return carry

    lax.fori_loop(0, slabs, spread, 0, unroll=KEY_BUCKET // RED_ROWS)
    return jnp.minimum(lo, -0.5 * unit), hi


def _probe(nk, kf, idx_ref, state):
    lo, hi, c_lo, c_hi = state
    frac = (c_lo - kf) / jnp.maximum(c_lo - c_hi, 1.0)
    frac = jnp.minimum(jnp.maximum(frac, INTERP_MARGIN), 1.0 - INTERP_MARGIN)
    t = lo + (hi - lo) * frac
    acc = jnp.zeros((RED_ROWS, Q_BLOCK), F32)
    for r in range(nk // RED_ROWS):
        acc = acc + jnp.where(idx_ref[r * RED_ROWS:(r + 1) * RED_ROWS, :] >= t, 1.0, 0.0)
    c = _colsum(acc)
    ge = c >= kf
    return (jnp.where(ge, t, lo), jnp.where(ge, hi, t), jnp.where(ge, c, c_lo), jnp.where(ge, c_hi, c))


def _write_selection(nk, j_blk, lo, idx_ref, rhs_ref):
    r_k, c_q = _tile_iotas()
    later = 2.0 * jnp.maximum(r_k - c_q, 0).astype(F32)

    def write(r, carry):
        rows = pl.ds(pl.multiple_of(r * RED_ROWS, RED_ROWS), RED_ROWS)
        on = jnp.where(r == j_blk, -later, 0.0)
        rhs_ref[rows, LANES:2 * LANES] = jnp.where(idx_ref[rows, :] >= lo, on, -MASK_OFF).astype(BF16)
        return carry

    lax.fori_loop(0, nk // RED_ROWS, write, 0, unroll=KEY_BUCKET // RED_ROWS)


def _exact_fallback(slabs, kf, lo, hi, idx_ref, rhs_ref):
    r_k, c_q = _tile_iotas()
    later = 2.0 * jnp.maximum(r_k - c_q, 0).astype(F32)

    def slab(r):
        return idx_ref[pl.ds(pl.multiple_of(r * RED_ROWS, RED_ROWS), RED_ROWS), :]

    def count(pred):
        def body(r, acc):
            return acc + jnp.where(pred(slab(r), r), 1.0, 0.0)
        return _colsum(lax.fori_loop(0, slabs, body, jnp.zeros((RED_ROWS, Q_BLOCK), F32)))

    def kth(lo):
        def body(r, acc):
            s = slab(r)
            return jnp.minimum(acc, jnp.where(s >= lo, s, jnp.inf))
        acc = lax.fori_loop(0, slabs, body, jnp.full((RED_ROWS, Q_BLOCK), jnp.inf, F32))
        thr = jnp.min(acc, axis=0, keepdims=True)
        return thr, count(lambda s, r: s > thr)

    def slow_cond(carry):
        it, _, _, _, c_gt = carry
        return jnp.logical_and(it < SLOW_TRIPS, _any_lane(c_gt >= kf))

    def slow_body(carry):
        it, lo, hi, _, _ = carry
        for _ in range(4):
            mid = 0.5 * lo + 0.5 * hi
            ge = count(lambda s, r: s >= mid) >= kf
            lo, hi = jnp.where(ge, mid, lo), jnp.where(ge, hi, mid)
        thr, c_gt = kth(lo)
        return it + 1, lo, hi, thr, c_gt

    thr0, c_gt0 = kth(lo)
    _, _, _, thr, c_gt = lax.while_loop(slow_cond, slow_body, (jnp.int32(0), lo, hi, thr0, c_gt0))
    c_eq = count(lambda s, r: s == thr)
    need = (c_gt + c_eq) > kf

    def tie_step(_, carry):
        lo_i, hi_i = carry
        mid_i = jnp.floor((lo_i + hi_i) * 0.5)
        below = count(lambda s, r: jnp.logical_and(s == thr, (r * RED_ROWS + r_k).astype(F32) <= mid_i))
        ok = (c_gt + below) >= kf
        return jnp.where(ok, lo_i, mid_i), jnp.where(ok, mid_i, hi_i)

    last = jnp.asarray(slabs * RED_ROWS - 1, F32)
    lo_i = jnp.full((1, Q_BLOCK), -1.0, F32)
    hi_i = jnp.zeros((1, Q_BLOCK), F32) + last
    _, cut = lax.fori_loop(0, 13, tie_step, (lo_i, hi_i))
    cut = jnp.where(need, cut, last + 1.0).astype(jnp.int32)

    def write(r, carry):
        s = slab(r)
        krow = r * RED_ROWS + r_k
        sel = jnp.logical_or(s > thr, jnp.logical_and(s == thr, krow <= cut))
        on = jnp.where(r == slabs - 1, -later, 0.0)
        rhs_ref[pl.ds(pl.multiple_of(r * RED_ROWS, RED_ROWS), RED_ROWS), LANES:2 * LANES] = (
            jnp.where(sel, on, -MASK_OFF).astype(BF16))
        return carry

    lax.fori_loop(0, slabs, write, 0)


GROUP = HEADS // 2
UNITS = PAIR * (HEADS // GROUP)
VALUE_ROWS = HEAD_DIM + 16


def _probabilities(nk, qt_ref, qcols, rhs_ref, s_ref, p_ref, between):
    r_k, c_q = _tile_iotas()
    feat = lax.broadcasted_iota(jnp.int32, (HALF, Q_BLOCK), 0)
    group = GROUP
    width = group * Q_BLOCK
    units = [(u, g) for u in range(len(qcols)) for g in range(HEADS // group)]

    def score_operand(u, hh):
        slope = 2.0 ** (-(hh + 1))
        alibi = jnp.where(feat == 0, CHUNK * slope, jnp.where(feat == 1, slope, 0.0)).astype(BF16)
        scaled_ident = jnp.where(r_k == c_q, slope, 0.0).astype(BF16)
        return jnp.concatenate([qt_ref[hh * HEAD_DIM:(hh + 1) * HEAD_DIM, qcols[u]], alibi, scaled_ident], axis=0)

    def scores(n):
        u, g = units[n]
        lhs_t = jnp.concatenate([score_operand(u, g * group + i) for i in range(group)], axis=1)
        m_acc = jnp.full((RED_ROWS, width), -jnp.inf, F32)
        for rows in _row_blocks(nk, DOT_ROWS):
            blk = jnp.dot(rhs_ref[u, rows, :], lhs_t, preferred_element_type=F32)
            s_ref[n % 2, rows, :] = blk
            for sub in range((rows.stop - rows.start) // RED_ROWS):
                m_acc = jnp.maximum(m_acc, blk[sub * RED_ROWS:(sub + 1) * RED_ROWS, :])
        return jnp.max(m_acc, axis=0, keepdims=True)

    def probabilities(n, m):
        for rows in _row_blocks(nk, RED_ROWS):
            p_ref[n, rows, :] = jnp.exp((s_ref[n % 2, rows, :] - m).astype(BF16))

    m = scores(0)
    between()
    for n in range(len(units)):
        m_next = scores(n + 1) if n + 1 < len(units) else None
        probabilities(n, m)
        m = m_next


def _value_chunk(ch, vtb_ref, p_ref, part_ref):
    if isinstance(ch, int):
        n, blk = ch % UNITS, ch // UNITS
        rows = slice(blk * DOT_ROWS, (blk + 1) * DOT_ROWS)
    else:
        n = ch & (UNITS - 1)
        blk = lax.shift_right_logical(ch, UNITS.bit_length() - 1)
        rows = pl.ds(pl.multiple_of(blk * DOT_ROWS, DOT_ROWS), DOT_ROWS)
    part_ref[n, blk] = jnp.dot(vtb_ref[blk], p_ref[n, rows, :], preferred_element_type=F32)


def _head_tiles(part_ref, n_blocks):
    per_block = []
    for u in range(PAIR):
        outs = []
        for g in range(HEADS // GROUP):
            n = u * (HEADS // GROUP) + g
            o = part_ref[n, 0]
            for blk in range(1, part_ref.shape[1]):
                o = o + jnp.where(blk < n_blocks, part_ref[n, blk], 0.0)
            outs.append(o[0:HEAD_DIM, :] * (1.0 / o[HEAD_DIM:HEAD_DIM + 1, :]))
        tiles = []
        for t in range(HEADS // 2):
            o_g = outs[(2 * t) // GROUP]
            a = (2 * t) % GROUP
            tiles.append(jnp.concatenate([o_g[:, a * Q_BLOCK:(a + 1) * Q_BLOCK],
                                          o_g[:, (a + 1) * Q_BLOCK:(a + 2) * Q_BLOCK]], axis=0).T)
        per_block.append(tiles)
    return per_block


def _attn_kernel(qt_ref, qit_ref, wit_ref, kaug_ref, kk_ref, vt_ref, ma_ref, gate_ref, x_ref, gb_ref,
                 wo_ref, y_ref, idx_ref, rhs_ref, s_ref, p_ref, st_ref, vtb_ref, part_ref, done_ref,
                 *, seq, n_pairs, topk):
    g = pl.program_id(0)
    kf = float(topk)
    steps = seq // (PAIR * Q_BLOCK)
    key_blocks = seq // DOT_ROWS
    qcols = [slice(u * Q_BLOCK, (u + 1) * Q_BLOCK) for u in range(PAIR)]
    searching = g < n_pairs
    pending = g >= 1
    jj = lax.rem(g, steps)
    g_prev = jnp.maximum(g - 1, 0)
    slot = lax.div(g, steps) & 1
    slot_prev = lax.div(g_prev, steps) & 1

    @pl.when(g == 0)
    def _():
        p_ref[...] = jnp.zeros(p_ref.shape, BF16)
        vtb_ref[...] = jnp.zeros(vtb_ref.shape, BF16)
        part_ref[...] = jnp.zeros(part_ref.shape, F32)
        y_ref[...] = jnp.zeros(y_ref.shape, F32)

    @pl.when(jnp.logical_and(searching, jj == 0))
    def _():
        for u in range(PAIR):
            rhs_ref[u, :, 0:LANES] = kaug_ref[...]
        ones = jnp.ones((VALUE_ROWS - HEAD_DIM, DOT_ROWS), BF16)
        for blk in range(key_blocks):
            vtb_ref[slot, blk] = jnp.concatenate([vt_ref[:, blk * DOT_ROWS:(blk + 1) * DOT_ROWS], ones], axis=0)

    done_ref[0] = 0
    done_ref[1] = 0

    steps_per_bucket = KEY_BUCKET // (PAIR * Q_BLOCK)

    def buckets(step, active):
        return [((n + 1) * KEY_BUCKET,
                 jnp.logical_and(active, jnp.logical_and(step >= n * steps_per_bucket,
                                                         step < (n + 1) * steps_per_bucket)))
                for n in range(seq // KEY_BUCKET)]

    blocks = [PAIR * jj + u for u in range(PAIR)]
    total = UNITS * (lax.div(lax.rem(g_prev, steps), steps_per_bucket) + 1) * (KEY_BUCKET // DOT_ROWS)
    vtb_prev = vtb_ref.at[slot_prev]

    for nk, here in buckets(jj, searching):
        @pl.when(here)
        def _(nk=nk):
            lane_q = lax.broadcasted_iota(jnp.int32, (1, Q_BLOCK), 1)
            states = []
            for u in range(PAIR):
                n_adm = blocks[u] * Q_BLOCK + CHUNK + CHUNK * (lane_q >= CHUNK).astype(jnp.int32)
                lo, hi = _index_scores(nk, n_adm, qit_ref, qcols[u], wit_ref[:, qcols[u]], kk_ref, s_ref,
                                       idx_ref.at[u])
                n_adm_f = n_adm.astype(F32)
                c_lo = jnp.where(n_adm_f <= kf, kf, n_adm_f)
                states.append((lo, hi, c_lo, jnp.zeros((1, Q_BLOCK), F32)))

            def cond(carry):
                it, flat = carry[0], carry[1:]
                def tested():
                    still = jnp.logical_or(flat[2] != kf, flat[6] != kf)
                    return jnp.logical_and(it < FAST_TRIPS, _any_lane(still))

                return lax.cond(it < UNTESTED_TRIPS, lambda: jnp.bool_(True), tested)

            def body(carry):
                it, flat = carry[0], carry[1:]
                st = [flat[0:4], flat[4:8]]
                for _ in range(PROBES_PER_TRIP):
                    st = [_probe(nk, kf, idx_ref.at[u], st[u]) for u in range(PAIR)]
                for c in range(CHUNKS_PER_TRIP):
                    _value_chunk(jnp.minimum(it * CHUNKS_PER_TRIP + c, total - 1), vtb_prev, p_ref, part_ref)
                return (it + 1,) + tuple(st[0]) + tuple(st[1])

            final = lax.while_loop(cond, body, (jnp.int32(0),) + tuple(states[0]) + tuple(states[1]))
            done_ref[0] = jnp.minimum(final[0] * CHUNKS_PER_TRIP, total)
            done_ref[1] = final[0]
            for u in range(PAIR):
                lo, hi, c_lo, _ = final[1 + 4 * u:5 + 4 * u]
                _write_selection(nk, blocks[u], lo, idx_ref.at[u], rhs_ref.at[u])
                st_ref[4 * u + 0:4 * u + 1, :] = lo
                st_ref[4 * u + 1:4 * u + 2, :] = hi
                st_ref[4 * u + 2:4 * u + 3, :] = c_lo

    @pl.when(jnp.logical_and(pending, searching))
    def _():
        def chunk(ch, carry):
            _value_chunk(ch, vtb_prev, p_ref, part_ref)
            return carry

        lax.fori_loop(done_ref[0], total, chunk, 0)

    @pl.when(g == n_pairs)
    def _():
        for ch in range(UNITS * key_blocks):
            _value_chunk(ch, vtb_prev, p_ref, part_ref)

    lane = lax.broadcasted_iota(jnp.int32, (PAIR * Q_BLOCK, LANES), 1)
    lo_half = lane < HALF

    def finish_pending():
        per_block = _head_tiles(part_ref, lax.div(total, UNITS))
        mixed = [ma_ref[...]]
        for t in range(HEADS // 2):
            cols = slice(t * LANES, (t + 1) * LANES)
            o = jnp.concatenate([per_block[u][t] for u in range(PAIR)], axis=0)
            msq = _half_mean_sq(o * o, lo_half)
            ob = o * lax.rsqrt(msq + NORM_EPS) * gb_ref[:, cols] * gate_ref[:, cols].astype(F32)
            mixed.append(ob.astype(BF16))
        y_ref[...] = x_ref[...] + jnp.dot(jnp.concatenate(mixed, axis=1), wo_ref[...],
                                          preferred_element_type=F32)

    @pl.when(jnp.logical_and(searching, done_ref[1] >= FAST_TRIPS))
    def _():
        for u in range(PAIR):
            lo = st_ref[4 * u + 0:4 * u + 1, :]
            hi = st_ref[4 * u + 1:4 * u + 2, :]
            c_lo = st_ref[4 * u + 2:4 * u + 3, :]

            @pl.when(_any_lane(c_lo != kf))
            def _(u=u, lo=lo, hi=hi):
                _exact_fallback(blocks[u] + 1, kf, lo, hi, idx_ref.at[u], rhs_ref.at[u])

    @pl.when(pending)
    def _():
        finish_pending()

    for nk, here in buckets(jj, searching):
        @pl.when(here)
        def _(nk=nk):
            _probabilities(nk, qt_ref, qcols, rhs_ref, s_ref, p_ref, lambda: None)


def kernel(x, norm_gain, w_in, sgu_norm_gain, sgu_w, sgu_b, q_norm_gain, k_norm_gain,
           idx_k_norm_gain, branch_norm_gain, w_out):
    bsz, seq, d_model = x.shape
    assert d_model == D_MODEL and norm_gain.shape[0] == 1
    assert seq % PROJ_ROWS == 0 and seq % KEY_BUCKET == 0 and KEY_BUCKET % (PAIR * Q_BLOCK) == 0
    tokens = bsz * seq
    topk = min(TOPK_MAX, seq // 4)
    idx_w_scale = (IDX_HEADS ** -0.5) * (IDX_DIM ** -0.5)

    w = w_in[0]
    a3 = 3 * A_WIDTH
    w_q = w[:, a3:a3 + B_WIDTH]
    w_k = w[:, a3 + B_WIDTH:a3 + B_WIDTH + HEAD_DIM]
    w_v = w[:, a3 + B_WIDTH + HEAD_DIM:a3 + B_WIDTH + 2 * HEAD_DIM]
    o_g = a3 + B_WIDTH + 2 * HEAD_DIM
    w_g = w[:, o_g:o_g + B_WIDTH]
    o_i = o_g + B_WIDTH
    w_iq = w[:, o_i:o_i + IDX_HEADS * IDX_DIM]
    w_ik = w[:, o_i + IDX_HEADS * IDX_DIM:o_i + IDX_HEADS * IDX_DIM + IDX_DIM]
    w_iw = w[:, o_i + IDX_HEADS * IDX_DIM + IDX_DIM:]
    w_main = jnp.concatenate([w[:, :a3], w_g, w_k, w_ik, w_v, w_iw,
                              jnp.zeros((D_MODEL, LANES - HEAD_DIM - IDX_HEADS), F32)], axis=1).astype(BF16)
    assert w_main.shape[1] == _PACKED_COLS
    w_t = jnp.concatenate([w_q.T, w_iq.T], axis=0).astype(BF16)
    assert w_t.shape[0] == _PACKED_ROWS
    x2 = x.reshape(tokens, D_MODEL)
    ng = norm_gain[0].reshape(1, D_MODEL)
    sgn = sgu_norm_gain[0].reshape(1, A_WIDTH)
    sw = sgu_w[0]
    sb = sgu_b[0].reshape(A_GROUPS, A_BLOCK, 1)
    qg = q_norm_gain[0].reshape(HEAD_DIM, 1)
    kg = jnp.concatenate([k_norm_gain[0], idx_k_norm_gain[0]]).reshape(1, LANES)
    ga = branch_norm_gain[0, :A_WIDTH].reshape(1, A_WIDTH)
    gb = branch_norm_gain[0, A_WIDTH:].reshape(1, B_WIDTH)
    wo = w_out[0].astype(BF16)

    tm = PROJ_ROWS
    full = lambda shape: pl.BlockSpec(shape, lambda i: (0,) * len(shape))
    rows = lambda width: pl.BlockSpec((tm, width), lambda i: (i, 0))
    colsT = lambda height: pl.BlockSpec((height, tm), lambda i: (0, i))
    outs = pl.pallas_call(
        functools.partial(_proj_kernel, tiles_per_seq=seq // tm, idx_w_scale=idx_w_scale),
        grid=(tokens // tm,),
        in_specs=[rows(D_MODEL), full((1, D_MODEL)), full((D_MODEL, _PACKED_COLS)),
                  full((_PACKED_ROWS, D_MODEL)), full((1, A_WIDTH)), full((A_GROUPS, A_BLOCK, A_BLOCK)),
                  full((A_GROUPS, A_BLOCK, 1)), full((HEAD_DIM, 1)), full((1, LANES)),
                  full((1, A_WIDTH))],
        out_specs=[rows(A_WIDTH), rows(B_WIDTH), rows(LANES), rows(LANES),
                   colsT(B_WIDTH), colsT(IDX_HEADS * IDX_DIM), colsT(HEAD_DIM), colsT(IDX_HEADS)],
        out_shape=[jax.ShapeDtypeStruct((tokens, A_WIDTH), BF16),
                   jax.ShapeDtypeStruct((tokens, B_WIDTH), BF16),
                   jax.ShapeDtypeStruct((tokens, LANES), BF16),
                   jax.ShapeDtypeStruct((tokens, LANES), BF16),
                   jax.ShapeDtypeStruct((B_WIDTH, tokens), BF16),
                   jax.ShapeDtypeStruct((IDX_HEADS * IDX_DIM, tokens), BF16),
                   jax.ShapeDtypeStruct((HEAD_DIM, tokens), BF16),
                   jax.ShapeDtypeStruct((IDX_HEADS, tokens), F32)],
        compiler_params=pltpu.CompilerParams(dimension_semantics=("arbitrary",),
                                             vmem_limit_bytes=VMEM_LIMIT),
        name="proj_sgu",
    )(x2, ng, w_main, w_t, sgn, sw, sb, qg, kg, ga)
    ma, gate, kaug, kk, qt, qit, vt, wit = outs

    qrows = PAIR * Q_BLOCK
    steps = seq // qrows
    n_pairs = bsz * steps
    cur = lambda g: jnp.minimum(g, n_pairs - 1)
    qblk = lambda width: pl.BlockSpec((qrows, width), lambda g: (jnp.maximum(g - 1, 0), 0))
    qblkT = lambda height: pl.BlockSpec((height, qrows), lambda g: (0, cur(g)))
    const = lambda shape: pl.BlockSpec(shape, lambda g: (0,) * len(shape))
    y = pl.pallas_call(
        functools.partial(_attn_kernel, seq=seq, n_pairs=n_pairs, topk=topk),
        grid=(n_pairs + 1,),
        in_specs=[qblkT(B_WIDTH), qblkT(IDX_HEADS * IDX_DIM), qblkT(IDX_HEADS),
                  pl.BlockSpec((seq, LANES), lambda g: (cur(g) // steps, 0)),
                  pl.BlockSpec((seq, LANES), lambda g: (cur(g) // steps, 0)),
                  pl.BlockSpec((HEAD_DIM, seq), lambda g: (0, cur(g) // steps)),
                  qblk(A_WIDTH), qblk(B_WIDTH), qblk(D_MODEL),
                  const((1, B_WIDTH)), const((D_MODEL, D_MODEL))],
        out_specs=qblk(D_MODEL),
        out_shape=jax.ShapeDtypeStruct((tokens, D_MODEL), F32),
        scratch_shapes=[pltpu.VMEM((PAIR, seq, Q_BLOCK), F32),
                        pltpu.VMEM((PAIR, seq, 2 * LANES), BF16),
                        pltpu.VMEM((2, seq, GROUP * Q_BLOCK), F32),
                        pltpu.VMEM((UNITS, seq, GROUP * Q_BLOCK), BF16),
                        pltpu.VMEM((8, Q_BLOCK), F32),
                        pltpu.VMEM((2, seq // DOT_ROWS, VALUE_ROWS, DOT_ROWS), BF16),
                        pltpu.VMEM((UNITS, seq // DOT_ROWS, VALUE_ROWS, GROUP * Q_BLOCK), F32),
                        pltpu.SMEM((2,), jnp.int32)],
        compiler_params=pltpu.CompilerParams(dimension_semantics=("arbitrary",),
                                             vmem_limit_bytes=VMEM_LIMIT),
        name="dsa_attn_out",
    )(qt, qit, wit, kaug, kk, vt, ma, gate, x2, gb, wo)
    return y.reshape(bsz, seq, D_MODEL)
```
